```python
import math
import jax, jax.numpy as jnp
from jax import lax
import numpy as np

D_MODEL = 1024
BATCH = 8
SEQ = 4096
DEPTH = 1

MEM_LEN = 256
N_MLSTM_HEADS = 4
MLSTM_HEAD_DIM = 128
MLSTM_WIDTH = N_MLSTM_HEADS * MLSTM_HEAD_DIM
CONV_WIDTH = 4
MLSTM_CHUNK = 128
N_DIFF_HEADS = 4
DIFF_QK_DIM = 64
DIFF_V_DIM = 2 * DIFF_QK_DIM
DIFF_WIDTH = N_DIFF_HEADS * DIFF_V_DIM
Q_BLOCK = 128
N_MEM_HEADS = 4
MEM_HEAD_DIM = 128
MEM_WIDTH = N_MEM_HEADS * MEM_HEAD_DIM
MIX_WIDTH = MLSTM_WIDTH + DIFF_WIDTH + MEM_WIDTH
IN_WIDTHS = (MLSTM_WIDTH, MLSTM_WIDTH, MLSTM_WIDTH,
             2 * N_DIFF_HEADS * DIFF_QK_DIM, 2 * N_DIFF_HEADS * DIFF_QK_DIM, DIFF_WIDTH, DIFF_WIDTH,
             MEM_WIDTH, MEM_WIDTH)
IN_WIDTH = 4608
NORM_EPS = 1e-6

kernel_name = 'hybrid_mlstm_diffattn_memxattn'


def _split_cols(a, widths):
    offs = np.cumsum(widths)[:-1].tolist()
    return jnp.split(a, offs, axis=-1)


def rmsnorm(x, g):
    xf = x.astype(jnp.float32)
    y = xf * lax.rsqrt(jnp.mean(xf * xf, axis=-1, keepdims=True) + NORM_EPS)
    return (y * g.astype(jnp.float32)).astype(x.dtype)


def head_layernorm(h, g):
    mu = jnp.mean(h, axis=-1, keepdims=True)
    var = jnp.mean(jnp.square(h - mu), axis=-1, keepdims=True)
    y = (h - mu) * lax.rsqrt(var + NORM_EPS)
    y = y.reshape(h.shape[0], h.shape[1], -1)
    return y * g.astype(jnp.float32)


def causal_depthwise_conv(x, w, b):
    y = lax.conv_general_dilated(x, w[:, None, :], window_strides=(1,),
                                 padding=[(CONV_WIDTH - 1, 0)],
                                 dimension_numbers=('NWC', 'WIO', 'NWC'),
                                 feature_group_count=x.shape[-1])
    return y + b


def mlstm_chunkwise(q, k, v, i_pre, f_pre):
    B_, H_, S_, dh = q.shape
    L = MLSTM_CHUNK
    nc = S_ // L
    k = k * (dh ** -0.5)
    log_f = jax.nn.log_sigmoid(f_pre)

    def chunks(a):
        a = a.reshape((B_, H_, nc, L) + a.shape[3:])
        return jnp.moveaxis(a, 2, 0)

    qc, kc, vc, ic = chunks(q), chunks(k), chunks(v), chunks(i_pre)
    bc = jnp.cumsum(chunks(log_f), axis=-1)
    tri = jnp.tril(jnp.ones((L, L), dtype=bool))

    def step(carry, inp):
        C, n, m = carry
        qb, kb, vb, ib, bb = inp
        d = bb[..., :, None] - bb[..., None, :] + ib[..., None, :]
        d = jnp.where(tri, d, -jnp.inf)
        g = bb + m[..., None]
        m_t = jnp.maximum(g, jnp.max(d, axis=-1))
        s_qk = jnp.einsum('bhtk,bhsk->bhts', qb, kb) * jnp.exp(d - m_t[..., None])
        inter = jnp.exp(g - m_t)
        num = (jnp.einsum('bhts,bhsv->bhtv', s_qk, vb)
               + inter[..., None] * jnp.einsum('bhvk,bhtk->bhtv', C, qb))
        den = jnp.sum(s_qk, axis=-1) + inter * jnp.einsum('bhk,bhtk->bht', n, qb)
        h = num / jnp.maximum(jnp.abs(den), jnp.exp(-m_t))[..., None]
        b_end = bb[..., -1]
        a = b_end[..., None] - bb + ib
        m_new = jnp.maximum(b_end + m, jnp.max(a, axis=-1))
        wa = jnp.exp(a - m_new[..., None])
        decay = jnp.exp(b_end + m - m_new)
        C_new = decay[..., None, None] * C + jnp.einsum('bhs,bhsv,bhsk->bhvk', wa, vb, kb)
        n_new = decay[..., None] * n + jnp.einsum('bhs,bhsk->bhk', wa, kb)
        return (C_new, n_new, m_new), h

    init = (jnp.zeros((B_, H_, dh, dh), jnp.float32),
            jnp.zeros((B_, H_, dh), jnp.float32),
            jnp.zeros((B_, H_), jnp.float32))
    _, h = lax.scan(step, init, (qc, kc, vc, ic, bc))
    return jnp.moveaxis(h, 0, 2).reshape(B_, H_, S_, dh)


def diff_causal_attention(q, k, v, lam):
    S_ = q.shape[3]
    scale = DIFF_QK_DIM ** -0.5
    outs = []
    for j in range(S_ // Q_BLOCK):
        lo, hi = j * Q_BLOCK, (j + 1) * Q_BLOCK
        s = jnp.einsum('bhmqd,bhmkd->bhmqk', q[:, :, :, lo:hi], k[:, :, :, :hi]).astype(jnp.float32) * scale
        mask = jnp.arange(hi)[None, :] <= jnp.arange(lo, hi)[:, None]
        p = jax.nn.softmax(jnp.where(mask, s, -jnp.inf), axis=-1)
        pd = p[:, :, 0] - lam * p[:, :, 1]
        outs.append(jnp.einsum('bhqk,bhkd->bhqd', pd.astype(v.dtype), v[:, :, :hi]))
    return jnp.concatenate(outs, axis=2)


def setup_inputs(seed: int = 0) -> dict:
    key = jax.random.key(seed)
    ks = jax.random.split(key, 24)
    nrm = jax.random.normal
    f32 = jnp.float32
    H = N_MLSTM_HEADS
    b_i = 0.1 * nrm(ks[10], (DEPTH, H), f32)
    b_f = 3.0 + 3.0 * jax.random.uniform(ks[11], (DEPTH, H), f32)
    return {
        'x': nrm(ks[0], (BATCH, SEQ, D_MODEL), f32),
        'mem': nrm(ks[1], (BATCH, MEM_LEN, D_MODEL), f32),
        'norm_g': 1.0 + 0.02 * nrm(ks[2], (DEPTH, D_MODEL), f32),
        'w_in': nrm(ks[3], (DEPTH, D_MODEL, IN_WIDTH), f32) * D_MODEL ** -0.5,
        'conv_w': nrm(ks[4], (DEPTH, CONV_WIDTH, MLSTM_WIDTH), f32) * CONV_WIDTH ** -0.5,
        'conv_b': 0.01 * nrm(ks[5], (DEPTH, MLSTM_WIDTH), f32),
        'wq_m': nrm(ks[6], (DEPTH, H, MLSTM_HEAD_DIM, MLSTM_HEAD_DIM), f32) * MLSTM_HEAD_DIM ** -0.5,
        'wk_m': nrm(ks[7], (DEPTH, H, MLSTM_HEAD_DIM, MLSTM_HEAD_DIM), f32) * MLSTM_HEAD_DIM ** -0.5,
        'wv_m': nrm(ks[8], (DEPTH, H, MLSTM_HEAD_DIM, MLSTM_HEAD_DIM), f32) * MLSTM_HEAD_DIM ** -0.5,
        'w_if': nrm(ks[9], (DEPTH, 3 * MLSTM_WIDTH, 2 * H), f32) * (3 * MLSTM_WIDTH) ** -0.5,
        'b_if': jnp.concatenate([b_i, b_f], axis=-1),
        'mnorm_g': 1.0 + 0.02 * nrm(ks[12], (DEPTH, MLSTM_WIDTH), f32),
        'skip_m': 1.0 + 0.02 * nrm(ks[13], (DEPTH, MLSTM_WIDTH), f32),
        'lam_q1': 0.1 * nrm(ks[14], (DEPTH, DIFF_QK_DIM), f32),
        'lam_k1': 0.1 * nrm(ks[15], (DEPTH, DIFF_QK_DIM), f32),
        'lam_q2': 0.1 * nrm(ks[16], (DEPTH, DIFF_QK_DIM), f32),
        'lam_k2': 0.1 * nrm(ks[17], (DEPTH, DIFF_QK_DIM), f32),
        'dnorm_g': 1.0 + 0.02 * nrm(ks[18], (DEPTH, DIFF_WIDTH), f32),
        'mem_norm_g': 1.0 + 0.02 * nrm(ks[19], (DEPTH, D_MODEL), f32),
        'w_mem_kv': nrm(ks[20], (DEPTH, D_MODEL, 2 * MEM_WIDTH), f32) * D_MODEL ** -0.5,
        'w_out': nrm(ks[21], (DEPTH, MIX_WIDTH, D_MODEL), f32) * MIX_WIDTH ** -0.5,
        'final_g': 1.0 + 0.02 * nrm(ks[22], (D_MODEL,), f32),
    }


def reference(x, mem, norm_g, w_in, conv_w, conv_b, wq_m, wk_m, wv_m, w_if, b_if, mnorm_g, skip_m,
              lam_q1, lam_k1, lam_q2, lam_k2, dnorm_g, mem_norm_g, w_mem_kv, w_out, final_g):
    B_, S_, _ = x.shape
    Hm, dh = N_MLSTM_HEADS, MLSTM_HEAD_DIM
    Hd, Hc = N_DIFF_HEADS, N_MEM_HEADS
    for l in range(DEPTH):
        h = rmsnorm(x, norm_g[l])
        proj = h @ w_in[l]
        x_m, o_m, z_m, q_d, k_d, v_d, z_d, q_c, z_c = _split_cols(proj, IN_WIDTHS)

        x_cv = jax.nn.silu(causal_depthwise_conv(x_m, conv_w[l], conv_b[l]))
        xch = x_cv.reshape(B_, S_, Hm, dh)
        xmh = x_m.reshape(B_, S_, Hm, dh)
        q = jnp.einsum('bshd,hde->bshe', xch, wq_m[l])
        k = jnp.einsum('bshd,hde->bshe', xch, wk_m[l])
        v = jnp.einsum('bshd,hde->bshe', xmh, wv_m[l])
        qkv = jnp.concatenate([q, k, v], axis=2).reshape(B_, S_, 3 * MLSTM_WIDTH)
        gates = (qkv @ w_if[l] + b_if[l]).astype(jnp.float32)
        i_pre = jnp.transpose(gates[..., :Hm], (0, 2, 1))
        f_pre = jnp.transpose(gates[..., Hm:], (0, 2, 1))
        tr = lambda a: jnp.transpose(a.astype(jnp.float32), (0, 2, 1, 3))
        h_t = mlstm_chunkwise(tr(q), tr(k), tr(v), i_pre, f_pre)
        h_t = jnp.transpose(h_t, (0, 2, 1, 3))
        h_t = jax.nn.sigmoid(o_m.reshape(B_, S_, Hm, dh).astype(jnp.float32)) * h_t
        y_m = head_layernorm(h_t, mnorm_g[l]).astype(x.dtype)
        y_m = (y_m + skip_m[l] * x_cv) * jax.nn.silu(z_m)

        qd = jnp.transpose(q_d.reshape(B_, S_, Hd, 2, DIFF_QK_DIM), (0, 2, 3, 1, 4))
        kd = jnp.transpose(k_d.reshape(B_, S_, Hd, 2, DIFF_QK_DIM), (0, 2, 3, 1, 4))
        vd = jnp.transpose(v_d.reshape(B_, S_, Hd, DIFF_V_DIM), (0, 2, 1, 3))
        lam_init = 0.8 - 0.6 * math.exp(-0.3 * l)
        lam = (jnp.exp(jnp.sum(lam_q1[l].astype(jnp.float32) * lam_k1[l].astype(jnp.float32)))
               - jnp.exp(jnp.sum(lam_q2[l].astype(jnp.float32) * lam_k2[l].astype(jnp.float32)))
               + lam_init)
        od = jnp.transpose(diff_causal_attention(qd, kd, vd, lam), (0, 2, 1, 3))
        od = rmsnorm(od, dnorm_g[l].reshape(Hd, DIFF_V_DIM)) * (1.0 - lam_init)
        y_d = od.reshape(B_, S_, DIFF_WIDTH) * jax.nn.silu(z_d)

        kv = rmsnorm(mem, mem_norm_g[l]) @ w_mem_kv[l]
        km = kv[..., :MEM_WIDTH].reshape(B_, -1, Hc, MEM_HEAD_DIM)
        vm = kv[..., MEM_WIDTH:].reshape(B_, -1, Hc, MEM_HEAD_DIM)
        qc = q_c.reshape(B_, S_, Hc, MEM_HEAD_DIM)
        sc = jnp.einsum('bshd,bmhd->bhsm', qc, km).astype(jnp.float32) * MEM_HEAD_DIM ** -0.5
        pc = jax.nn.softmax(sc, axis=-1).astype(x.dtype)
        oc = jnp.einsum('bhsm,bmhd->bshd', pc, vm).reshape(B_, S_, MEM_WIDTH)
        y_c = oc * jax.nn.silu(z_c)

        y = jnp.concatenate([y_m, y_d, y_c], axis=-1) @ w_out[l]
        x = x + y
    return rmsnorm(x, final_g)
```

```python
import functools
import math

import jax
import jax.numpy as jnp
from jax import lax
from jax.experimental import pallas as pl
from jax.experimental.pallas import tpu as pltpu

F32 = jnp.float32
BF16 = jnp.bfloat16

D_MODEL = 1024
N_HEADS = 4
HEAD_DIM = 128
GROUP_WIDTH = N_HEADS * HEAD_DIM
DIFF_QK_DIM = 64
CONV_WIDTH = 4
MLSTM_CHUNK = 128
MEM_LEN = 256
IN_WIDTH = 9 * GROUP_WIDTH
NORM_EPS = 1e-6
LAM_INIT = 0.8 - 0.6 * math.exp(-0.3 * 0)

OFF_XM, OFF_OM, OFF_ZM, OFF_QD, OFF_KD, OFF_VD, OFF_ZD, OFF_QC, OFF_ZC = (
    i * GROUP_WIDTH for i in range(9))

PROJ_ROWS = 256
DIFF_Q_ROWS = 128
DIFF_KV_ROWS = 256
OUT_ROWS = 512
CONV_HALO = 8

NT_DIMS = (((1,), (1,)), ((), ()))
TN_DIMS = (((0,), (0,)), ((), ()))


def _dot(a, b):
    return jnp.dot(a, b, preferred_element_type=F32)


def _dot_nt(a, b):
    return lax.dot_general(a, b, NT_DIMS, preferred_element_type=F32)


def _silu(x):
    return x * jax.nn.sigmoid(x)


def _rms(x, g):
    return x * lax.rsqrt(jnp.mean(x * x, axis=-1, keepdims=True) + NORM_EPS) * g


def _head(h):
    return slice(h * HEAD_DIM, (h + 1) * HEAD_DIM)


def _memkv_kernel(mem_ref, g_ref, w_ref, kv_ref):
    y = _rms(mem_ref[0], g_ref[...])
    kv_ref[0] = _dot(y.astype(BF16), w_ref[...]).astype(BF16)


def _memkv_call(mem, g, w):
    b = mem.shape[0]
    return pl.pallas_call(
        _memkv_kernel,
        grid=(b,),
        in_specs=[
            pl.BlockSpec((1, MEM_LEN, D_MODEL), lambda i: (i, 0, 0)),
            pl.BlockSpec((1, D_MODEL), lambda i: (0, 0)),
            pl.BlockSpec((D_MODEL, 2 * GROUP_WIDTH), lambda i: (0, 0)),
        ],
        out_specs=pl.BlockSpec((1, MEM_LEN, 2 * GROUP_WIDTH), lambda i: (i, 0, 0)),
        out_shape=jax.ShapeDtypeStruct((b, MEM_LEN, 2 * GROUP_WIDTH), BF16),
        compiler_params=pltpu.CompilerParams(dimension_semantics=("arbitrary",)),
        name="memkv",
    )(mem, g, w)


def _proj_kernel(x_ref, kv_ref, ng_ref, win_ref, cw_ref, cb_ref, wq_ref, wk_ref, wv_ref, wif_ref, bif_ref,
                 xcv_ref, qm_ref, km_ref, vm_ref, gates_ref, om_ref, zm_ref,
                 qd_ref, kd_ref, vdt_ref, zd_ref, yc_ref, conv_ref):
    tm = PROJ_ROWS
    s = pl.program_id(1)
    h = _rms(x_ref[0], ng_ref[...]).astype(BF16)

    def proj(off):
        return _dot(h, win_ref[:, off:off + GROUP_WIDTH])

    x_m = proj(OFF_XM)

    @pl.when(s == 0)
    def _():
        conv_ref[0:CONV_HALO, :] = jnp.zeros((CONV_HALO, GROUP_WIDTH), F32)

    @pl.when(s != 0)
    def _():
        conv_ref[0:CONV_HALO, :] = conv_ref[tm:tm + CONV_HALO, :]

    conv_ref[CONV_HALO:CONV_HALO + tm, :] = x_m
    acc = jnp.broadcast_to(cb_ref[...], (tm, GROUP_WIDTH))
    for j in range(CONV_WIDTH):
        start = CONV_HALO - (CONV_WIDTH - 1) + j
        acc = acc + cw_ref[j:j + 1, :] * conv_ref[start:start + tm, :]
    x_cv = _silu(acc)
    xcv_ref[0] = x_cv.astype(BF16)
    xcv_b = x_cv.astype(BF16)
    xm_b = x_m.astype(BF16)

    gates = jnp.broadcast_to(bif_ref[...], (tm, HEAD_DIM))
    for hh in range(N_HEADS):
        sl = _head(hh)
        qb = _dot(xcv_b[:, sl], wq_ref[hh]).astype(BF16)
        kf = _dot(xcv_b[:, sl], wk_ref[hh])
        vb = _dot(xm_b[:, sl], wv_ref[hh]).astype(BF16)
        qm_ref[0, :, sl] = qb
        km_ref[0, :, sl] = (kf * HEAD_DIM ** -0.5).astype(BF16)
        vm_ref[0, :, sl] = vb
        gates = (gates
                 + _dot(qb, wif_ref[hh * HEAD_DIM:(hh + 1) * HEAD_DIM, :])
                 + _dot(kf.astype(BF16), wif_ref[GROUP_WIDTH + hh * HEAD_DIM:GROUP_WIDTH + (hh + 1) * HEAD_DIM, :])
                 + _dot(vb, wif_ref[2 * GROUP_WIDTH + hh * HEAD_DIM:2 * GROUP_WIDTH + (hh + 1) * HEAD_DIM, :]))
    gates_ref[0] = gates
    om_ref[0] = proj(OFF_OM).astype(BF16)
    zm_ref[0] = proj(OFF_ZM).astype(BF16)

    qd_ref[0] = (proj(OFF_QD) * DIFF_QK_DIM ** -0.5).astype(BF16)
    kd_ref[0] = proj(OFF_KD).astype(BF16)
    vdt_ref[0, 0] = proj(OFF_VD).T.astype(BF16)
    zd_ref[0] = proj(OFF_ZD).astype(BF16)

    q_c = proj(OFF_QC)
    z_c = proj(OFF_ZC)
    for hh in range(N_HEADS):
        sl = _head(hh)
        km = kv_ref[0, :, sl]
        vmem = kv_ref[0, :, GROUP_WIDTH + hh * HEAD_DIM:GROUP_WIDTH + (hh + 1) * HEAD_DIM]
        sc = _dot_nt(q_c[:, sl].astype(BF16), km) * HEAD_DIM ** -0.5
        p = jnp.exp(sc - jnp.max(sc, axis=-1, keepdims=True))
        pc = p / jnp.sum(p, axis=-1, keepdims=True)
        oc = _dot(pc.astype(BF16), vmem)
        yc_ref[0, :, sl] = (oc * _silu(z_c[:, sl])).astype(BF16)


def _proj_call(x, kv, norm_g, w_in, conv_w, conv_b, wq, wk, wv, w_if, b_if):
    b, s, _ = x.shape
    tm = PROJ_ROWS
    ns = s // tm
    tok = lambda w, dt: jax.ShapeDtypeStruct((b, s, w), dt)
    tok_spec = lambda w: pl.BlockSpec((1, tm, w), lambda i, j: (i, j, 0))
    const = lambda shape: pl.BlockSpec(shape, lambda i, j: (0,) * len(shape))
    out_shape = (
        tok(GROUP_WIDTH, BF16),
        tok(GROUP_WIDTH, BF16), tok(GROUP_WIDTH, BF16), tok(GROUP_WIDTH, BF16),
        tok(HEAD_DIM, F32),
        tok(GROUP_WIDTH, BF16), tok(GROUP_WIDTH, BF16),
        tok(GROUP_WIDTH, BF16), tok(GROUP_WIDTH, BF16),
        jax.ShapeDtypeStruct((b, ns, GROUP_WIDTH, tm), BF16),
        tok(GROUP_WIDTH, BF16),
        tok(GROUP_WIDTH, BF16),
    )
    out_specs = (
        tok_spec(GROUP_WIDTH), tok_spec(GROUP_WIDTH), tok_spec(GROUP_WIDTH), tok_spec(GROUP_WIDTH),
        tok_spec(HEAD_DIM), tok_spec(GROUP_WIDTH), tok_spec(GROUP_WIDTH),
        tok_spec(GROUP_WIDTH), tok_spec(GROUP_WIDTH),
        pl.BlockSpec((1, 1, GROUP_WIDTH, tm), lambda i, j: (i, j, 0, 0)),
        tok_spec(GROUP_WIDTH), tok_spec(GROUP_WIDTH),
    )
    in_specs = [
        pl.BlockSpec((1, tm, D_MODEL), lambda i, j: (i, j, 0)),
        pl.BlockSpec((1, MEM_LEN, 2 * GROUP_WIDTH), lambda i, j: (i, 0, 0)),
        const((1, D_MODEL)),
        const((D_MODEL, IN_WIDTH)),
        const((CONV_WIDTH, GROUP_WIDTH)),
        const((1, GROUP_WIDTH)),
        const((N_HEADS, HEAD_DIM, HEAD_DIM)),
        const((N_HEADS, HEAD_DIM, HEAD_DIM)),
        const((N_HEADS, HEAD_DIM, HEAD_DIM)),
        const((3 * GROUP_WIDTH, HEAD_DIM)),
        const((1, HEAD_DIM)),
    ]
    return pl.pallas_call(
        _proj_kernel,
        grid=(b, ns),
        in_specs=in_specs,
        out_specs=out_specs,
        out_shape=out_shape,
        scratch_shapes=[pltpu.VMEM((tm + CONV_HALO, GROUP_WIDTH), F32)],
        compiler_params=pltpu.CompilerParams(
            dimension_semantics=("arbitrary", "arbitrary"),
            vmem_limit_bytes=56 * 1024 * 1024),
        name="inproj",
    )(x, kv, norm_g, w_in, conv_w, conv_b, wq, wk, wv, w_if, b_if)


def _log_sigmoid(x):
    return jnp.minimum(x, 0.0) - jnp.log1p(jnp.exp(-jnp.abs(x)))


def _cumsum_rows(f, tril):
    hi = f.astype(BF16)
    r1 = f - hi.astype(F32)
    mid = r1.astype(BF16)
    lo = (r1 - mid.astype(F32)).astype(BF16)
    return _dot(tril, hi) + _dot(tril, mid) + _dot(tril, lo)


def _mlstm_kernel(q_ref, k_ref, v_ref, g_ref, om_ref, zm_ref, xcv_ref, mg_ref, sk_ref,
                  ym_ref, c_ref, n_ref, m_ref):
    L = MLSTM_CHUNK

    @pl.when(pl.program_id(1) == 0)
    def _():
        c_ref[...] = jnp.zeros_like(c_ref)
        n_ref[...] = jnp.zeros_like(n_ref)
        m_ref[...] = jnp.zeros_like(m_ref)

    row = lax.broadcasted_iota(jnp.int32, (L, L), 0)
    col = lax.broadcasted_iota(jnp.int32, (L, L), 1)
    causal = col <= row
    tril = jnp.where(causal, 1.0, 0.0).astype(BF16)

    g = g_ref[0]
    bb_all = _cumsum_rows(_log_sigmoid(g), tril)
    a_cols = jnp.where(col < N_HEADS, g, bb_all)
    a_rows = a_cols.T

    for hh in range(N_HEADS):
        sl = _head(hh)
        ib_col = a_cols[:, hh:hh + 1]
        bb_col = a_cols[:, N_HEADS + hh:N_HEADS + hh + 1]
        ib_row = a_rows[hh:hh + 1, :]
        bb_row = a_rows[N_HEADS + hh:N_HEADS + hh + 1, :]
        m_prev = m_ref[hh][:, 0:1]
        c_prev = c_ref[hh]
        n_prev = n_ref[hh]

        qh = q_ref[0, :, sl]
        kh = k_ref[0, :, sl]
        vh = v_ref[0, :, sl]

        d = jnp.where(causal, (bb_col - bb_row) + ib_row, -jnp.inf)
        g_col = bb_col + m_prev
        m_t = jnp.maximum(g_col, jnp.max(d, axis=-1, keepdims=True))
        s_qk = _dot_nt(qh, kh) * jnp.exp(d - m_t)
        inter = jnp.exp(g_col - m_t)
        num = _dot(s_qk.astype(BF16), vh) + inter * _dot_nt(qh, c_prev.astype(BF16))
        den = (jnp.sum(s_qk, axis=-1, keepdims=True)
               + inter * jnp.sum(qh.astype(F32) * n_prev, axis=-1, keepdims=True))
        h_t = num / jnp.maximum(jnp.abs(den), jnp.exp(-m_t))

        b_end = bb_col[L - 1:L, :]
        a = (b_end - bb_col) + ib_col
        m_new = jnp.maximum(b_end + m_prev, jnp.max(a, axis=0, keepdims=True))
        wa = jnp.exp(a - m_new)
        decay = jnp.exp(b_end + m_prev - m_new)
        vw = (vh.astype(F32) * wa).astype(BF16)
        c_ref[hh] = decay * c_prev + lax.dot_general(vw, kh, TN_DIMS, preferred_element_type=F32)
        n_ref[hh] = decay * n_prev + jnp.sum(kh.astype(F32) * wa, axis=0, keepdims=True)
        m_ref[hh] = jnp.broadcast_to(m_new, (1, HEAD_DIM))

        hg = jax.nn.sigmoid(om_ref[0, :, sl].astype(F32)) * h_t
        mu = jnp.mean(hg, axis=-1, keepdims=True)
        var = jnp.mean(jnp.square(hg - mu), axis=-1, keepdims=True)
        y = (hg - mu) * lax.rsqrt(var + NORM_EPS) * mg_ref[:, sl]
        y = (y + sk_ref[:, sl] * xcv_ref[0, :, sl].astype(F32)) * _silu(zm_ref[0, :, sl].astype(F32))
        ym_ref[0, :, sl] = y.astype(BF16)


def _mlstm_call(q, k, v, gates, om, zm, xcv, mnorm_g, skip_m):
    b, s, _ = q.shape
    L = MLSTM_CHUNK
    tok_spec = lambda w: pl.BlockSpec((1, L, w), lambda i, j: (i, j, 0))
    const = lambda shape: pl.BlockSpec(shape, lambda i, j: (0,) * len(shape))
    return pl.pallas_call(
        _mlstm_kernel,
        grid=(b, s // L),
        in_specs=[tok_spec(GROUP_WIDTH), tok_spec(GROUP_WIDTH), tok_spec(GROUP_WIDTH), tok_spec(HEAD_DIM),
                  tok_spec(GROUP_WIDTH), tok_spec(GROUP_WIDTH), tok_spec(GROUP_WIDTH),
                  const((1, GROUP_WIDTH)), const((1, GROUP_WIDTH))],
        out_specs=tok_spec(GROUP_WIDTH),
        out_shape=jax.ShapeDtypeStruct((b, s, GROUP_WIDTH), BF16),
        scratch_shapes=[pltpu.VMEM((N_HEADS, HEAD_DIM, HEAD_DIM), F32),
                        pltpu.VMEM((N_HEADS, 1, HEAD_DIM), F32),
                        pltpu.VMEM((N_HEADS, 1, HEAD_DIM), F32)],
        compiler_params=pltpu.CompilerParams(dimension_semantics=("arbitrary", "arbitrary")),
        name="mlstm",
    )(q, k, v, gates, om, zm, xcv, mnorm_g, skip_m)


def _diff_kernel(q_ref, k_ref, vt_ref, zd_ref, g_ref, lq1_ref, lk1_ref, lq2_ref, lk2_ref, yd_ref):
    tq, tk = DIFF_Q_ROWS, DIFF_KV_ROWS
    i = pl.program_id(2)
    q = q_ref[0]
    lane = lax.broadcasted_iota(jnp.int32, (tq, HEAD_DIM), 1)
    zero = jnp.zeros_like(q)
    qs = jnp.concatenate([jnp.where(lane < DIFF_QK_DIM, q, zero),
                          jnp.where(lane >= DIFF_QK_DIM, q, zero)], axis=0)

    def step(j, carry, masked):
        m, l, acc = carry
        kb = k_ref[0, pl.ds(pl.multiple_of(j * tk, tk), tk), :]
        s = _dot_nt(kb, qs)
        if masked:
            key_pos = j * tk + lax.broadcasted_iota(jnp.int32, (tk, 2 * tq), 0)
            q_pos = i * tq + (lax.broadcasted_iota(jnp.int32, (tk, 2 * tq), 1) & (tq - 1))
            s = jnp.where(key_pos <= q_pos, s, -jnp.inf)
        m_new = jnp.maximum(m, jnp.max(s, axis=0, keepdims=True))
        alpha = jnp.exp(m - m_new)
        p = jnp.exp(s - m_new)
        l = alpha * l + jnp.sum(p, axis=0, keepdims=True)
        acc = alpha * acc + _dot(vt_ref[0, j], p.astype(BF16))
        return m_new, l, acc

    init = (jnp.full((1, 2 * tq), -jnp.inf, F32), jnp.zeros((1, 2 * tq), F32),
            jnp.zeros((HEAD_DIM, 2 * tq), F32))
    n_full = (i * tq) // tk
    carry = lax.fori_loop(0, n_full, lambda j, c: step(j, c, False), init)
    _, l, acc = step(n_full, carry, True)

    lam = (jnp.exp(jnp.sum(lq1_ref[...] * lk1_ref[...], axis=-1, keepdims=True))
           - jnp.exp(jnp.sum(lq2_ref[...] * lk2_ref[...], axis=-1, keepdims=True)) + LAM_INIT)
    inv = 1.0 / l
    o_t = acc[:, :tq] * inv[:, :tq] - lam * (acc[:, tq:] * inv[:, tq:])
    o = _rms(o_t.T, g_ref[...]) * (1.0 - LAM_INIT)
    yd_ref[0] = (o * _silu(zd_ref[0].astype(F32))).astype(BF16)


def _diff_call(qd, kd, vdt, zd, dnorm_g, lq1, lk1, lq2, lk2):
    b, s, _ = qd.shape
    tq = DIFF_Q_ROWS
    nkv = vdt.shape[1]
    lam_spec = pl.BlockSpec((1, DIFF_QK_DIM), lambda i, h, j: (0, 0))
    return pl.pallas_call(
        _diff_kernel,
        grid=(b, N_HEADS, s // tq),
        in_specs=[
            pl.BlockSpec((1, tq, HEAD_DIM), lambda i, h, j: (i, j, h)),
            pl.BlockSpec((1, s, HEAD_DIM), lambda i, h, j: (i, 0, h)),
            pl.BlockSpec((1, nkv, HEAD_DIM, DIFF_KV_ROWS), lambda i, h, j: (i, 0, h, 0)),
            pl.BlockSpec((1, tq, HEAD_DIM), lambda i, h, j: (i, j, h)),
            pl.BlockSpec((1, HEAD_DIM), lambda i, h, j: (0, h)),
            lam_spec, lam_spec, lam_spec, lam_spec,
        ],
        out_specs=pl.BlockSpec((1, tq, HEAD_DIM), lambda i, h, j: (i, j, h)),
        out_shape=jax.ShapeDtypeStruct((b, s, GROUP_WIDTH), BF16),
        compiler_params=pltpu.CompilerParams(dimension_semantics=("arbitrary", "arbitrary", "arbitrary")),
        name="diffattn",
    )(qd, kd, vdt, zd, dnorm_g, lq1, lk1, lq2, lk2)


def _out_kernel(x_ref, ym_ref, yd_ref, yc_ref, wo_ref, fg_ref, o_ref):
    y = (_dot(ym_ref[...], wo_ref[0:GROUP_WIDTH, :])
         + _dot(yd_ref[...], wo_ref[GROUP_WIDTH:2 * GROUP_WIDTH, :])
         + _dot(yc_ref[...], wo_ref[2 * GROUP_WIDTH:3 * GROUP_WIDTH, :]))
    o_ref[...] = _rms(x_ref[...] + y, fg_ref[...])


def _out_call(x2, ym, yd, yc, w_out, final_g):
    t = x2.shape[0]
    tm = OUT_ROWS
    row_spec = lambda w: pl.BlockSpec((tm, w), lambda i: (i, 0))
    return pl.pallas_call(
        _out_kernel,
        grid=(t // tm,),
        in_specs=[row_spec(D_MODEL), row_spec(GROUP_WIDTH), row_spec(GROUP_WIDTH), row_spec(GROUP_WIDTH),
                  pl.BlockSpec((3 * GROUP_WIDTH, D_MODEL), lambda i: (0, 0)),
                  pl.BlockSpec((1, D_MODEL), lambda i: (0, 0))],
        out_specs=row_spec(D_MODEL),
        out_shape=jax.ShapeDtypeStruct((t, D_MODEL), F32),
        compiler_params=pltpu.CompilerParams(dimension_semantics=("arbitrary",)),
        name="outproj",
    )(x2, ym, yd, yc, w_out, final_g)


def kernel(x, mem, norm_g, w_in, conv_w, conv_b, wq_m, wk_m, wv_m, w_if, b_if, mnorm_g, skip_m,
           lam_q1, lam_k1, lam_q2, lam_k2, dnorm_g, mem_norm_g, w_mem_kv, w_out, final_g):
    b, s, d = x.shape
    assert (d, s % PROJ_ROWS, PROJ_ROWS, (b * s) % OUT_ROWS) == (D_MODEL, 0, DIFF_KV_ROWS, 0)
    assert norm_g.shape[0] == 1, "single-layer kernel"
    l = 0
    w_if_pad = jnp.pad(w_if[l], ((0, 0), (0, HEAD_DIM - w_if.shape[-1]))).astype(BF16)
    b_if_pad = jnp.pad(b_if[l], (0, HEAD_DIM - b_if.shape[-1]))[None, :]

    kv = _memkv_call(mem, mem_norm_g[l][None, :], w_mem_kv[l].astype(BF16))
    (xcv, qm, km, vm, gates, om, zm, qd, kd, vdt, zd, yc) = _proj_call(
        x, kv, norm_g[l][None, :], w_in[l].astype(BF16), conv_w[l], conv_b[l][None, :],
        wq_m[l].astype(BF16), wk_m[l].astype(BF16), wv_m[l].astype(BF16), w_if_pad, b_if_pad)
    ym = _mlstm_call(qm, km, vm, gates, om, zm, xcv, mnorm_g[l][None, :], skip_m[l][None, :])
    yd = _diff_call(qd, kd, vdt, zd, dnorm_g[l][None, :],
                    lam_q1[l][None, :], lam_k1[l][None, :], lam_q2[l][None, :], lam_k2[l][None, :])
    out = _out_call(x.reshape(b * s, d), ym.reshape(b * s, GROUP_WIDTH), yd.reshape(b * s, GROUP_WIDTH),
                    yc.reshape(b * s, GROUP_WIDTH), w_out[l].astype(BF16), final_g[None, :])
    return out.reshape(b, s, d)
```

```python
import functools
import math

import jax
import jax.numpy as jnp
from jax import lax
from jax.experimental import pallas as pl
from jax.experimental.pallas import tpu as pltpu

F32 = jnp.float32
BF16 = jnp.bfloat16

D_MODEL = 1024
N_HEADS = 4
HEAD_DIM = 128
GROUP_WIDTH = N_HEADS * HEAD_DIM
DIFF_QK_DIM = 64
CONV_WIDTH = 4
MLSTM_CHUNK = 128
MEM_LEN = 256
IN_WIDTH = 9 * GROUP_WIDTH
NORM_EPS = 1e-6
LAM_INIT = 0.8 - 0.6 * math.exp(-0.3 * 0)
LOG2E = math.log2(math.e)

OFF_XM, OFF_OM, OFF_ZM, OFF_QD, OFF_KD, OFF_VD, OFF_ZD, OFF_QC, OFF_ZC = (
    i * GROUP_WIDTH for i in range(9))

PROJ_ROWS = 256
DIFF_Q_ROWS = 128
DIFF_KV_ROWS = 512
OUT_ROWS = 512
CONV_HALO = 8

NT_DIMS = (((1,), (1,)), ((), ()))
TN_DIMS = (((0,), (0,)), ((), ()))


def _dot(a, b):
    return jnp.dot(a, b, preferred_element_type=F32)


def _dot_nt(a, b):
    return lax.dot_general(a, b, NT_DIMS, preferred_element_type=F32)


def _silu(x):
    return x * jax.nn.sigmoid(x)


def _rms(x, g):
    return x * lax.rsqrt(jnp.mean(x * x, axis=-1, keepdims=True) + NORM_EPS) * g


def _head(h):
    return slice(h * HEAD_DIM, (h + 1) * HEAD_DIM)


def _memkv_kernel(mem_ref, g_ref, w_ref, kv_ref):
    y = _rms(mem_ref[0], g_ref[...])
    kv_ref[0] = _dot(y.astype(BF16), w_ref[...]).astype(BF16)


def _memkv_call(mem, g, w):
    b = mem.shape[0]
    return pl.pallas_call(
        _memkv_kernel,
        grid=(b,),
        in_specs=[
            pl.BlockSpec((1, MEM_LEN, D_MODEL), lambda i: (i, 0, 0)),
            pl.BlockSpec((1, D_MODEL), lambda i: (0, 0)),
            pl.BlockSpec((D_MODEL, 2 * GROUP_WIDTH), lambda i: (0, 0)),
        ],
        out_specs=pl.BlockSpec((1, MEM_LEN, 2 * GROUP_WIDTH), lambda i: (i, 0, 0)),
        out_shape=jax.ShapeDtypeStruct((b, MEM_LEN, 2 * GROUP_WIDTH), BF16),
        compiler_params=pltpu.CompilerParams(dimension_semantics=("arbitrary",)),
        name="memkv",
    )(mem, g, w)


def _proj_kernel(x_ref, kv_ref, ng_ref, win_ref, cw_ref, cb_ref, wq_ref, wk_ref, wv_ref, wif_ref, bif_ref,
                 xcv_ref, qm_ref, km_ref, vm_ref, gates_ref, om_ref, zm_ref,
                 qd_ref, kd_ref, vdt_ref, zd_ref, yc_ref, conv_ref):
    tm = PROJ_ROWS
    s = pl.program_id(1)
    h = _rms(x_ref[0], ng_ref[...]).astype(BF16)

    def proj(off):
        return _dot(h, win_ref[:, off:off + GROUP_WIDTH])

    x_m = proj(OFF_XM)

    @pl.when(s == 0)
    def _():
        conv_ref[0:CONV_HALO, :] = jnp.zeros((CONV_HALO, GROUP_WIDTH), F32)

    @pl.when(s != 0)
    def _():
        conv_ref[0:CONV_HALO, :] = conv_ref[tm:tm + CONV_HALO, :]

    conv_ref[CONV_HALO:CONV_HALO + tm, :] = x_m
    acc = jnp.broadcast_to(cb_ref[...], (tm, GROUP_WIDTH))
    for j in range(CONV_WIDTH):
        start = CONV_HALO - (CONV_WIDTH - 1) + j
        acc = acc + cw_ref[j:j + 1, :] * conv_ref[start:start + tm, :]
    x_cv = _silu(acc)
    xcv_ref[0] = x_cv.astype(BF16)
    xcv_b = x_cv.astype(BF16)
    xm_b = x_m.astype(BF16)

    gates = jnp.broadcast_to(bif_ref[...], (tm, HEAD_DIM))
    for hh in range(N_HEADS):
        sl = _head(hh)
        qb = _dot(xcv_b[:, sl], wq_ref[hh]).astype(BF16)
        kf = _dot(xcv_b[:, sl], wk_ref[hh])
        vb = _dot(xm_b[:, sl], wv_ref[hh]).astype(BF16)
        qm_ref[0, :, sl] = qb
        km_ref[0, :, sl] = (kf * HEAD_DIM ** -0.5).astype(BF16)
        vm_ref[0, :, sl] = vb
        gates = (gates
                 + _dot(qb, wif_ref[hh * HEAD_DIM:(hh + 1) * HEAD_DIM, :])
                 + _dot(kf.astype(BF16), wif_ref[GROUP_WIDTH + hh * HEAD_DIM:GROUP_WIDTH + (hh + 1) * HEAD_DIM, :])
                 + _dot(vb, wif_ref[2 * GROUP_WIDTH + hh * HEAD_DIM:2 * GROUP_WIDTH + (hh + 1) * HEAD_DIM, :]))
    gates_ref[0] = gates
    om_ref[0] = proj(OFF_OM).astype(BF16)
    zm_ref[0] = proj(OFF_ZM).astype(BF16)

    qd_ref[0] = (proj(OFF_QD) * (DIFF_QK_DIM ** -0.5 * LOG2E)).astype(BF16)
    kd_ref[0] = proj(OFF_KD).astype(BF16)
    vdt_ref[0, 0] = proj(OFF_VD).T.astype(BF16)
    zd_ref[0] = proj(OFF_ZD).astype(BF16)

    q_c = proj(OFF_QC)
    z_c = proj(OFF_ZC)
    for hh in range(N_HEADS):
        sl = _head(hh)
        km = kv_ref[0, :, sl]
        vmem = kv_ref[0, :, GROUP_WIDTH + hh * HEAD_DIM:GROUP_WIDTH + (hh + 1) * HEAD_DIM]
        sc = _dot_nt(q_c[:, sl].astype(BF16), km) * HEAD_DIM ** -0.5
        p = jnp.exp(sc - jnp.max(sc, axis=-1, keepdims=True))
        pc = p / jnp.sum(p, axis=-1, keepdims=True)
        oc = _dot(pc.astype(BF16), vmem)
        yc_ref[0, :, sl] = (oc * _silu(z_c[:, sl])).astype(BF16)


def _proj_call(x, kv, norm_g, w_in, conv_w, conv_b, wq, wk, wv, w_if, b_if):
    b, s, _ = x.shape
    tm = PROJ_ROWS
    ns = s // tm
    tok = lambda w, dt: jax.ShapeDtypeStruct((b, s, w), dt)
    tok_spec = lambda w: pl.BlockSpec((1, tm, w), lambda i, j: (i, j, 0))
    const = lambda shape: pl.BlockSpec(shape, lambda i, j: (0,) * len(shape))
    out_shape = (
        tok(GROUP_WIDTH, BF16),
        tok(GROUP_WIDTH, BF16), tok(GROUP_WIDTH, BF16), tok(GROUP_WIDTH, BF16),
        tok(HEAD_DIM, F32),
        tok(GROUP_WIDTH, BF16), tok(GROUP_WIDTH, BF16),
        tok(GROUP_WIDTH, BF16), tok(GROUP_WIDTH, BF16),
        jax.ShapeDtypeStruct((b, ns, GROUP_WIDTH, tm), BF16),
        tok(GROUP_WIDTH, BF16),
        tok(GROUP_WIDTH, BF16),
    )
    out_specs = (
        tok_spec(GROUP_WIDTH), tok_spec(GROUP_WIDTH), tok_spec(GROUP_WIDTH), tok_spec(GROUP_WIDTH),
        tok_spec(HEAD_DIM), tok_spec(GROUP_WIDTH), tok_spec(GROUP_WIDTH),
        tok_spec(GROUP_WIDTH), tok_spec(GROUP_WIDTH),
        pl.BlockSpec((1, 1, GROUP_WIDTH, tm), lambda i, j: (i, j, 0, 0)),
        tok_spec(GROUP_WIDTH), tok_spec(GROUP_WIDTH),
    )
    in_specs = [
        pl.BlockSpec((1, tm, D_MODEL), lambda i, j: (i, j, 0)),
        pl.BlockSpec((1, MEM_LEN, 2 * GROUP_WIDTH), lambda i, j: (i, 0, 0)),
        const((1, D_MODEL)),
        const((D_MODEL, IN_WIDTH)),
        const((CONV_WIDTH, GROUP_WIDTH)),
        const((1, GROUP_WIDTH)),
        const((N_HEADS, HEAD_DIM, HEAD_DIM)),
        const((N_HEADS, HEAD_DIM, HEAD_DIM)),
        const((N_HEADS, HEAD_DIM, HEAD_DIM)),
        const((3 * GROUP_WIDTH, HEAD_DIM)),
        const((1, HEAD_DIM)),
    ]
    return pl.pallas_call(
        _proj_kernel,
        grid=(b, ns),
        in_specs=in_specs,
        out_specs=out_specs,
        out_shape=out_shape,
        scratch_shapes=[pltpu.VMEM((tm + CONV_HALO, GROUP_WIDTH), F32)],
        compiler_params=pltpu.CompilerParams(
            dimension_semantics=("arbitrary", "arbitrary"),
            vmem_limit_bytes=56 * 1024 * 1024),
        name="inproj",
    )(x, kv, norm_g, w_in, conv_w, conv_b, wq, wk, wv, w_if, b_if)


def _log_sigmoid(x):
    return jnp.minimum(x, 0.0) - jnp.log1p(jnp.exp(-jnp.abs(x)))


def _cumsum_rows(f, tril):
    hi = f.astype(BF16)
    r1 = f - hi.astype(F32)
    mid = r1.astype(BF16)
    lo = (r1 - mid.astype(F32)).astype(BF16)
    return _dot(tril, hi) + _dot(tril, mid) + _dot(tril, lo)


def _mlstm_kernel(q_ref, k_ref, v_ref, g_ref, om_ref, zm_ref, xcv_ref, mg_ref, sk_ref,
                  ym_ref, c_ref, n_ref, m_ref):
    L = MLSTM_CHUNK

    @pl.when(pl.program_id(1) == 0)
    def _():
        c_ref[...] = jnp.zeros_like(c_ref)
        n_ref[...] = jnp.zeros_like(n_ref)
        m_ref[...] = jnp.zeros_like(m_ref)

    row = lax.broadcasted_iota(jnp.int32, (L, L), 0)
    col = lax.broadcasted_iota(jnp.int32, (L, L), 1)
    causal = col <= row
    tril = jnp.where(causal, 1.0, 0.0).astype(BF16)

    g = g_ref[0]
    bb_all = _cumsum_rows(_log_sigmoid(g), tril)
    a_cols = jnp.where(col < N_HEADS, g, bb_all)
    a_rows = a_cols.T

    for hh in range(N_HEADS):
        sl = _head(hh)
        ib_col = a_cols[:, hh:hh + 1]
        bb_col = a_cols[:, N_HEADS + hh:N_HEADS + hh + 1]
        ib_row = a_rows[hh:hh + 1, :]
        bb_row = a_rows[N_HEADS + hh:N_HEADS + hh + 1, :]
        m_prev = m_ref[hh][:, 0:1]
        c_prev = c_ref[hh]
        n_prev = n_ref[hh]

        qh = q_ref[0, :, sl]
        kh = k_ref[0, :, sl]
        vh = v_ref[0, :, sl]

        d = jnp.where(causal, (bb_col - bb_row) + ib_row, -jnp.inf)
        g_col = bb_col + m_prev
        m_t = jnp.maximum(g_col, jnp.max(d, axis=-1, keepdims=True))
        s_qk = _dot_nt(qh, kh) * jnp.exp(d - m_t)
        inter = jnp.exp(g_col - m_t)
        num = _dot(s_qk.astype(BF16), vh) + inter * _dot_nt(qh, c_prev.astype(BF16))
        den = (jnp.sum(s_qk, axis=-1, keepdims=True)
               + inter * jnp.sum(qh.astype(F32) * n_prev, axis=-1, keepdims=True))
        h_t = num / jnp.maximum(jnp.abs(den), jnp.exp(-m_t))

        b_end = bb_col[L - 1:L, :]
        a = (b_end - bb_col) + ib_col
        m_new = jnp.maximum(b_end + m_prev, jnp.max(a, axis=0, keepdims=True))
        wa = jnp.exp(a - m_new)
        decay = jnp.exp(b_end + m_prev - m_new)
        vw = (vh.astype(F32) * wa).astype(BF16)
        c_ref[hh] = decay * c_prev + lax.dot_general(vw, kh, TN_DIMS, preferred_element_type=F32)
        n_ref[hh] = decay * n_prev + jnp.sum(kh.astype(F32) * wa, axis=0, keepdims=True)
        m_ref[hh] = jnp.broadcast_to(m_new, (1, HEAD_DIM))

        hg = jax.nn.sigmoid(om_ref[0, :, sl].astype(F32)) * h_t
        mu = jnp.mean(hg, axis=-1, keepdims=True)
        var = jnp.mean(jnp.square(hg - mu), axis=-1, keepdims=True)
        y = (hg - mu) * lax.rsqrt(var + NORM_EPS) * mg_ref[:, sl]
        y = (y + sk_ref[:, sl] * xcv_ref[0, :, sl].astype(F32)) * _silu(zm_ref[0, :, sl].astype(F32))
        ym_ref[0, :, sl] = y.astype(BF16)


def _mlstm_call(q, k, v, gates, om, zm, xcv, mnorm_g, skip_m):
    b, s, _ = q.shape
    L = MLSTM_CHUNK
    tok_spec = lambda w: pl.BlockSpec((1, L, w), lambda i, j: (i, j, 0))
    const = lambda shape: pl.BlockSpec(shape, lambda i, j: (0,) * len(shape))
    return pl.pallas_call(
        _mlstm_kernel,
        grid=(b, s // L),
        in_specs=[tok_spec(GROUP_WIDTH), tok_spec(GROUP_WIDTH), tok_spec(GROUP_WIDTH), tok_spec(HEAD_DIM),
                  tok_spec(GROUP_WIDTH), tok_spec(GROUP_WIDTH), tok_spec(GROUP_WIDTH),
                  const((1, GROUP_WIDTH)), const((1, GROUP_WIDTH))],
        out_specs=tok_spec(GROUP_WIDTH),
        out_shape=jax.ShapeDtypeStruct((b, s, GROUP_WIDTH), BF16),
        scratch_shapes=[pltpu.VMEM((N_HEADS, HEAD_DIM, HEAD_DIM), F32),
                        pltpu.VMEM((N_HEADS, 1, HEAD_DIM), F32),
                        pltpu.VMEM((N_HEADS, 1, HEAD_DIM), F32)],
        compiler_params=pltpu.CompilerParams(dimension_semantics=("arbitrary", "arbitrary")),
        name="mlstm",
    )(q, k, v, gates, om, zm, xcv, mnorm_g, skip_m)


def _diff_kernel(q_ref, k_ref, vt_ref, zd_ref, g_ref, lq1_ref, lk1_ref, lq2_ref, lk2_ref, yd_ref,
                 qs_ref, m_ref, l_ref, acc_ref, s_ref):
    tq, tk, tv = DIFF_Q_ROWS, DIFF_KV_ROWS, PROJ_ROWS
    i = pl.program_id(1)
    lane = lax.broadcasted_iota(jnp.int32, (tq, HEAD_DIM), 1)
    for hh in range(N_HEADS):
        q = q_ref[0, :, _head(hh)]
        zero = jnp.zeros_like(q)
        qs_ref[hh, 0:tq, :] = jnp.where(lane < DIFF_QK_DIM, q, zero)
        qs_ref[hh, tq:2 * tq, :] = jnp.where(lane >= DIFF_QK_DIM, q, zero)
    m_ref[...] = jnp.full_like(m_ref, -jnp.inf)
    l_ref[...] = jnp.zeros_like(l_ref)
    acc_ref[...] = jnp.zeros_like(acc_ref)

    def scores(j, masked):
        rows = pl.ds(pl.multiple_of(j * tk, tk), tk)
        if masked:
            key_pos = j * tk + lax.broadcasted_iota(jnp.int32, (tk, 2 * tq), 0)
            q_pos = i * tq + (lax.broadcasted_iota(jnp.int32, (tk, 2 * tq), 1) & (tq - 1))
            visible = key_pos <= q_pos
        for hh in range(N_HEADS):
            s = _dot_nt(k_ref[0, rows, _head(hh)], qs_ref[hh])
            if masked:
                s = jnp.where(visible, s, -jnp.inf)
            s_ref[hh, rows, :] = s
            m_ref[hh] = jnp.maximum(m_ref[hh], jnp.max(s, axis=0, keepdims=True))

    def values(j):
        rows = pl.ds(pl.multiple_of(j * tk, tk), tk)
        for hh in range(N_HEADS):
            sl = _head(hh)
            p = jnp.exp2(s_ref[hh, rows, :] - m_ref[hh])
            l_ref[hh] += jnp.sum(p, axis=0, keepdims=True)
            pb = p.astype(BF16)
            pv = _dot(vt_ref[0, j * (tk // tv), sl, :], pb[0:tv])
            for c in range(1, tk // tv):
                pv = pv + _dot(vt_ref[0, j * (tk // tv) + c, sl, :], pb[c * tv:(c + 1) * tv])
            acc_ref[hh] += pv

    n_full = (i * tq) // tk

    def scores_body(j, carry):
        scores(j, False)
        return carry

    def values_body(j, carry):
        values(j)
        return carry

    lax.fori_loop(0, n_full, scores_body, 0)
    scores(n_full, True)
    lax.fori_loop(0, n_full + 1, values_body, 0)

    lam = (jnp.exp(jnp.sum(lq1_ref[...] * lk1_ref[...], axis=-1, keepdims=True))
           - jnp.exp(jnp.sum(lq2_ref[...] * lk2_ref[...], axis=-1, keepdims=True)) + LAM_INIT)
    for hh in range(N_HEADS):
        sl = _head(hh)
        acc = acc_ref[hh]
        inv = 1.0 / l_ref[hh]
        o_t = acc[:, :tq] * inv[:, :tq] - lam * (acc[:, tq:] * inv[:, tq:])
        o = _rms(o_t.T, g_ref[:, sl]) * (1.0 - LAM_INIT)
        yd_ref[0, :, sl] = (o * _silu(zd_ref[0, :, sl].astype(F32))).astype(BF16)


def _diff_call(qd, kd, vdt, zd, dnorm_g, lq1, lk1, lq2, lk2):
    b, s, _ = qd.shape
    tq = DIFF_Q_ROWS
    nkv = vdt.shape[1]
    lam_spec = pl.BlockSpec((1, DIFF_QK_DIM), lambda i, j: (0, 0))
    tok_spec = pl.BlockSpec((1, tq, GROUP_WIDTH), lambda i, j: (i, j, 0))
    return pl.pallas_call(
        _diff_kernel,
        grid=(b, s // tq),
        in_specs=[
            tok_spec,
            pl.BlockSpec((1, s, GROUP_WIDTH), lambda i, j: (i, 0, 0)),
            pl.BlockSpec((1, nkv, GROUP_WIDTH, PROJ_ROWS), lambda i, j: (i, 0, 0, 0)),
            tok_spec,
            pl.BlockSpec((1, GROUP_WIDTH), lambda i, j: (0, 0)),
            lam_spec, lam_spec, lam_spec, lam_spec,
        ],
        out_specs=tok_spec,
        out_shape=jax.ShapeDtypeStruct((b, s, GROUP_WIDTH), BF16),
        scratch_shapes=[pltpu.VMEM((N_HEADS, 2 * tq, HEAD_DIM), BF16),
                        pltpu.VMEM((N_HEADS, 1, 2 * tq), F32),
                        pltpu.VMEM((N_HEADS, 1, 2 * tq), F32),
                        pltpu.VMEM((N_HEADS, HEAD_DIM, 2 * tq), F32),
                        pltpu.VMEM((N_HEADS, s, 2 * tq), F32)],
        compiler_params=pltpu.CompilerParams(
            dimension_semantics=("arbitrary", "arbitrary"),
            vmem_limit_bytes=56 * 1024 * 1024),
        name="diffattn",
    )(qd, kd, vdt, zd, dnorm_g, lq1, lk1, lq2, lk2)


def _out_kernel(x_ref, ym_ref, yd_ref, yc_ref, wo_ref, fg_ref, o_ref):
    y = (_dot(ym_ref[...], wo_ref[0:GROUP_WIDTH, :])
         + _dot(yd_ref[...], wo_ref[GROUP_WIDTH:2 * GROUP_WIDTH, :])
         + _dot(yc_ref[...], wo_ref[2 * GROUP_WIDTH:3 * GROUP_WIDTH, :]))
    o_ref[...] = _rms(x_ref[...] + y, fg_ref[...])


def _out_call(x2, ym, yd, yc, w_out, final_g):
    t = x2.shape[0]
    tm = OUT_ROWS
    row_spec = lambda w: pl.BlockSpec((tm, w), lambda i: (i, 0))
    return pl.pallas_call(
        _out_kernel,
        grid=(t // tm,),
        in_specs=[row_spec(D_MODEL), row_spec(GROUP_WIDTH), row_spec(GROUP_WIDTH), row_spec(GROUP_WIDTH),
                  pl.BlockSpec((3 * GROUP_WIDTH, D_MODEL), lambda i: (0, 0)),
                  pl.BlockSpec((1, D_MODEL), lambda i: (0, 0))],
        out_specs=row_spec(D_MODEL),
        out_shape=jax.ShapeDtypeStruct((t, D_MODEL), F32),
        compiler_params=pltpu.CompilerParams(dimension_semantics=("arbitrary",)),
        name="outproj",
    )(x2, ym, yd, yc, w_out, final_g)


def kernel(x, mem, norm_g, w_in, conv_w, conv_b, wq_m, wk_m, wv_m, w_if, b_if, mnorm_g, skip_m,
           lam_q1, lam_k1, lam_q2, lam_k2, dnorm_g, mem_norm_g, w_mem_kv, w_out, final_g):
    b, s, d = x.shape
    assert (d, s % DIFF_KV_ROWS, DIFF_KV_ROWS % PROJ_ROWS, (b * s) % OUT_ROWS) == (D_MODEL, 0, 0, 0)
    assert norm_g.shape[0] == 1, "single-layer kernel"
    l = 0
    w_if_pad = jnp.pad(w_if[l], ((0, 0), (0, HEAD_DIM - w_if.shape[-1]))).astype(BF16)
    b_if_pad = jnp.pad(b_if[l], (0, HEAD_DIM - b_if.shape[-1]))[None, :]

    kv = _memkv_call(mem, mem_norm_g[l][None, :], w_mem_kv[l].astype(BF16))
    (xcv, qm, km, vm, gates, om, zm, qd, kd, vdt, zd, yc) = _proj_call(
        x, kv, norm_g[l][None, :], w_in[l].astype(BF16), conv_w[l], conv_b[l][None, :],
        wq_m[l].astype(BF16), wk_m[l].astype(BF16), wv_m[l].astype(BF16), w_if_pad, b_if_pad)
    ym = _mlstm_call(qm, km, vm, gates, om, zm, xcv, mnorm_g[l][None, :], skip_m[l][None, :])
    yd = _diff_call(qd, kd, vdt, zd, dnorm_g[l][None, :],
                    lam_q1[l][None, :], lam_k1[l][None, :], lam_q2[l][None, :], lam_k2[l][None, :])
    out = _out_call(x.reshape(b * s, d), ym.reshape(b * s, GROUP_WIDTH), yd.reshape(b * s, GROUP_WIDTH),
                    yc.reshape(b * s, GROUP_WIDTH), w_out[l].astype(BF16), final_g[None, :])
    return out.reshape(b, s, d)
```

```python
import functools
import math

import jax
import jax.numpy as jnp
from jax import lax
from jax.experimental import pallas as pl
from jax.experimental.pallas import tpu as pltpu

F32 = jnp.float32
BF16 = jnp.bfloat16

D_MODEL = 1024
N_HEADS = 4
HEAD_DIM = 128
GROUP_WIDTH = N_HEADS * HEAD_DIM
DIFF_QK_DIM = 64
CONV_WIDTH = 4
MLSTM_CHUNK = 128
MEM_LEN = 256
IN_WIDTH = 9 * GROUP_WIDTH
NORM_EPS = 1e-6
LAM_INIT = 0.8 - 0.6 * math.exp(-0.3 * 0)
LOG2E = math.log2(math.e)

OFF_XM, OFF_OM, OFF_ZM, OFF_QD, OFF_KD, OFF_VD, OFF_ZD, OFF_QC, OFF_ZC = (
    i * GROUP_WIDTH for i in range(9))

PROJ_ROWS = 256
DIFF_Q_ROWS = 512
DIFF_KV_ROWS = 512
OUT_ROWS = 512
CONV_HALO = 8

NT_DIMS = (((1,), (1,)), ((), ()))
TN_DIMS = (((0,), (0,)), ((), ()))


def _dot(a, b):
    return jnp.dot(a, b, preferred_element_type=F32)


def _dot_nt(a, b):
    return lax.dot_general(a, b, NT_DIMS, preferred_element_type=F32)


def _silu(x):
    return x * jax.nn.sigmoid(x)


def _rms(x, g):
    return x * lax.rsqrt(jnp.mean(x * x, axis=-1, keepdims=True) + NORM_EPS) * g


def _head(h):
    return slice(h * HEAD_DIM, (h + 1) * HEAD_DIM)


def _memkv_kernel(mem_ref, g_ref, w_ref, kv_ref):
    y = _rms(mem_ref[0], g_ref[...])
    kv_ref[0] = _dot(y.astype(BF16), w_ref[...]).astype(BF16)


def _memkv_call(mem, g, w):
    b = mem.shape[0]
    return pl.pallas_call(
        _memkv_kernel,
        grid=(b,),
        in_specs=[
            pl.BlockSpec((1, MEM_LEN, D_MODEL), lambda i: (i, 0, 0)),
            pl.BlockSpec((1, D_MODEL), lambda i: (0, 0)),
            pl.BlockSpec((D_MODEL, 2 * GROUP_WIDTH), lambda i: (0, 0)),
        ],
        out_specs=pl.BlockSpec((1, MEM_LEN, 2 * GROUP_WIDTH), lambda i: (i, 0, 0)),
        out_shape=jax.ShapeDtypeStruct((b, MEM_LEN, 2 * GROUP_WIDTH), BF16),
        compiler_params=pltpu.CompilerParams(dimension_semantics=("arbitrary",)),
        name="memkv",
    )(mem, g, w)


def _proj_kernel(x_ref, kv_ref, ng_ref, win_ref, cw_ref, cb_ref, wq_ref, wk_ref, wv_ref, wif_ref, bif_ref,
                 xcv_ref, qm_ref, km_ref, vm_ref, gates_ref, om_ref, zm_ref,
                 qd_ref, kd_ref, vdt_ref, zd_ref, yc_ref, conv_ref):
    tm = PROJ_ROWS
    s = pl.program_id(1)
    h = _rms(x_ref[0], ng_ref[...]).astype(BF16)

    def proj(off):
        return _dot(h, win_ref[:, off:off + GROUP_WIDTH])

    x_m = proj(OFF_XM)

    @pl.when(s == 0)
    def _():
        conv_ref[0:CONV_HALO, :] = jnp.zeros((CONV_HALO, GROUP_WIDTH), F32)

    @pl.when(s != 0)
    def _():
        conv_ref[0:CONV_HALO, :] = conv_ref[tm:tm + CONV_HALO, :]

    conv_ref[CONV_HALO:CONV_HALO + tm, :] = x_m
    acc = jnp.broadcast_to(cb_ref[...], (tm, GROUP_WIDTH))
    for j in range(CONV_WIDTH):
        start = CONV_HALO - (CONV_WIDTH - 1) + j
        acc = acc + cw_ref[j:j + 1, :] * conv_ref[start:start + tm, :]
    x_cv = _silu(acc)
    xcv_ref[0] = x_cv.astype(BF16)
    xcv_b = x_cv.astype(BF16)
    xm_b = x_m.astype(BF16)

    gates = jnp.broadcast_to(bif_ref[...], (tm, HEAD_DIM))
    for hh in range(N_HEADS):
        sl = _head(hh)
        qb = _dot(xcv_b[:, sl], wq_ref[hh]).astype(BF16)
        kf = _dot(xcv_b[:, sl], wk_ref[hh])
        vb = _dot(xm_b[:, sl], wv_ref[hh]).astype(BF16)
        qm_ref[0, :, sl] = qb
        km_ref[0, :, sl] = (kf * HEAD_DIM ** -0.5).astype(BF16)
        vm_ref[0, :, sl] = vb
        gates = (gates
                 + _dot(qb, wif_ref[hh * HEAD_DIM:(hh + 1) * HEAD_DIM, :])
                 + _dot(kf.astype(BF16), wif_ref[GROUP_WIDTH + hh * HEAD_DIM:GROUP_WIDTH + (hh + 1) * HEAD_DIM, :])
                 + _dot(vb, wif_ref[2 * GROUP_WIDTH + hh * HEAD_DIM:2 * GROUP_WIDTH + (hh + 1) * HEAD_DIM, :]))
    gates_ref[0] = gates
    om_ref[0] = proj(OFF_OM).astype(BF16)
    zm_ref[0] = proj(OFF_ZM).astype(BF16)

    qd_ref[0] = (proj(OFF_QD) * (DIFF_QK_DIM ** -0.5 * LOG2E)).astype(BF16)
    kd_ref[0] = proj(OFF_KD).astype(BF16)
    vdt_ref[0, 0] = proj(OFF_VD).T.astype(BF16)
    zd_ref[0] = proj(OFF_ZD).astype(BF16)

    q_c = proj(OFF_QC)
    z_c = proj(OFF_ZC)
    for hh in range(N_HEADS):
        sl = _head(hh)
        km = kv_ref[0, :, sl]
        vmem = kv_ref[0, :, GROUP_WIDTH + hh * HEAD_DIM:GROUP_WIDTH + (hh + 1) * HEAD_DIM]
        sc = _dot_nt(q_c[:, sl].astype(BF16), km) * HEAD_DIM ** -0.5
        p = jnp.exp(sc - jnp.max(sc, axis=-1, keepdims=True))
        pc = p / jnp.sum(p, axis=-1, keepdims=True)
        oc = _dot(pc.astype(BF16), vmem)
        yc_ref[0, :, sl] = (oc * _silu(z_c[:, sl])).astype(BF16)


def _proj_call(x, kv, norm_g, w_in, conv_w, conv_b, wq, wk, wv, w_if, b_if):
    b, s, _ = x.shape
    tm = PROJ_ROWS
    ns = s // tm
    tok = lambda w, dt: jax.ShapeDtypeStruct((b, s, w), dt)
    tok_spec = lambda w: pl.BlockSpec((1, tm, w), lambda i, j: (i, j, 0))
    const = lambda shape: pl.BlockSpec(shape, lambda i, j: (0,) * len(shape))
    out_shape = (
        tok(GROUP_WIDTH, BF16),
        tok(GROUP_WIDTH, BF16), tok(GROUP_WIDTH, BF16), tok(GROUP_WIDTH, BF16),
        tok(HEAD_DIM, F32),
        tok(GROUP_WIDTH, BF16), tok(GROUP_WIDTH, BF16),
        tok(GROUP_WIDTH, BF16), tok(GROUP_WIDTH, BF16),
        jax.ShapeDtypeStruct((b, ns, GROUP_WIDTH, tm), BF16),
        tok(GROUP_WIDTH, BF16),
        tok(GROUP_WIDTH, BF16),
    )
    out_specs = (
        tok_spec(GROUP_WIDTH), tok_spec(GROUP_WIDTH), tok_spec(GROUP_WIDTH), tok_spec(GROUP_WIDTH),
        tok_spec(HEAD_DIM), tok_spec(GROUP_WIDTH), tok_spec(GROUP_WIDTH),
        tok_spec(GROUP_WIDTH), tok_spec(GROUP_WIDTH),
        pl.BlockSpec((1, 1, GROUP_WIDTH, tm), lambda i, j: (i, j, 0, 0)),
        tok_spec(GROUP_WIDTH), tok_spec(GROUP_WIDTH),
    )
    in_specs = [
        pl.BlockSpec((1, tm, D_MODEL), lambda i, j: (i, j, 0)),
        pl.BlockSpec((1, MEM_LEN, 2 * GROUP_WIDTH), lambda i, j: (i, 0, 0)),
        const((1, D_MODEL)),
        const((D_MODEL, IN_WIDTH)),
        const((CONV_WIDTH, GROUP_WIDTH)),
        const((1, GROUP_WIDTH)),
        const((N_HEADS, HEAD_DIM, HEAD_DIM)),
        const((N_HEADS, HEAD_DIM, HEAD_DIM)),
        const((N_HEADS, HEAD_DIM, HEAD_DIM)),
        const((3 * GROUP_WIDTH, HEAD_DIM)),
        const((1, HEAD_DIM)),
    ]
    return pl.pallas_call(
        _proj_kernel,
        grid=(b, ns),
        in_specs=in_specs,
        out_specs=out_specs,
        out_shape=out_shape,
        scratch_shapes=[pltpu.VMEM((tm + CONV_HALO, GROUP_WIDTH), F32)],
        compiler_params=pltpu.CompilerParams(
            dimension_semantics=("arbitrary", "arbitrary"),
            vmem_limit_bytes=56 * 1024 * 1024),
        name="inproj",
    )(x, kv, norm_g, w_in, conv_w, conv_b, wq, wk, wv, w_if, b_if)


def _log_sigmoid(x):
    return jnp.minimum(x, 0.0) - jnp.log1p(jnp.exp(-jnp.abs(x)))


def _cumsum_rows(f, tril):
    hi = f.astype(BF16)
    r1 = f - hi.astype(F32)
    mid = r1.astype(BF16)
    lo = (r1 - mid.astype(F32)).astype(BF16)
    return _dot(tril, hi) + _dot(tril, mid) + _dot(tril, lo)


def _mlstm_kernel(q_ref, k_ref, v_ref, g_ref, om_ref, zm_ref, xcv_ref, mg_ref, sk_ref,
                  ym_ref, c_ref, n_ref, m_ref):
    L = MLSTM_CHUNK

    @pl.when(pl.program_id(1) == 0)
    def _():
        c_ref[...] = jnp.zeros_like(c_ref)
        n_ref[...] = jnp.zeros_like(n_ref)
        m_ref[...] = jnp.zeros_like(m_ref)

    row = lax.broadcasted_iota(jnp.int32, (L, L), 0)
    col = lax.broadcasted_iota(jnp.int32, (L, L), 1)
    causal = col <= row
    tril = jnp.where(causal, 1.0, 0.0).astype(BF16)

    g = g_ref[0]
    bb_all = _cumsum_rows(_log_sigmoid(g), tril)
    a_cols = jnp.where(col < N_HEADS, g, bb_all)
    a_rows = a_cols.T

    for hh in range(N_HEADS):
        sl = _head(hh)
        ib_col = a_cols[:, hh:hh + 1]
        bb_col = a_cols[:, N_HEADS + hh:N_HEADS + hh + 1]
        ib_row = a_rows[hh:hh + 1, :]
        bb_row = a_rows[N_HEADS + hh:N_HEADS + hh + 1, :]
        m_prev = m_ref[hh][:, 0:1]
        c_prev = c_ref[hh]
        n_prev = n_ref[hh]

        qh = q_ref[0, :, sl]
        kh = k_ref[0, :, sl]
        vh = v_ref[0, :, sl]

        d = jnp.where(causal, (bb_col - bb_row) + ib_row, -jnp.inf)
        g_col = bb_col + m_prev
        m_t = jnp.maximum(g_col, jnp.max(d, axis=-1, keepdims=True))
        s_qk = _dot_nt(qh, kh) * jnp.exp(d - m_t)
        inter = jnp.exp(g_col - m_t)
        num = _dot(s_qk.astype(BF16), vh) + inter * _dot_nt(qh, c_prev.astype(BF16))
        den = (jnp.sum(s_qk, axis=-1, keepdims=True)
               + inter * jnp.sum(qh.astype(F32) * n_prev, axis=-1, keepdims=True))
        h_t = num / jnp.maximum(jnp.abs(den), jnp.exp(-m_t))

        b_end = bb_col[L - 1:L, :]
        a = (b_end - bb_col) + ib_col
        m_new = jnp.maximum(b_end + m_prev, jnp.max(a, axis=0, keepdims=True))
        wa = jnp.exp(a - m_new)
        decay = jnp.exp(b_end + m_prev - m_new)
        vw = (vh.astype(F32) * wa).astype(BF16)
        c_ref[hh] = decay * c_prev + lax.dot_general(vw, kh, TN_DIMS, preferred_element_type=F32)
        n_ref[hh] = decay * n_prev + jnp.sum(kh.astype(F32) * wa, axis=0, keepdims=True)
        m_ref[hh] = jnp.broadcast_to(m_new, (1, HEAD_DIM))

        hg = jax.nn.sigmoid(om_ref[0, :, sl].astype(F32)) * h_t
        mu = jnp.mean(hg, axis=-1, keepdims=True)
        var = jnp.mean(jnp.square(hg - mu), axis=-1, keepdims=True)
        y = (hg - mu) * lax.rsqrt(var + NORM_EPS) * mg_ref[:, sl]
        y = (y + sk_ref[:, sl] * xcv_ref[0, :, sl].astype(F32)) * _silu(zm_ref[0, :, sl].astype(F32))
        ym_ref[0, :, sl] = y.astype(BF16)


def _mlstm_call(q, k, v, gates, om, zm, xcv, mnorm_g, skip_m):
    b, s, _ = q.shape
    L = MLSTM_CHUNK
    tok_spec = lambda w: pl.BlockSpec((1, L, w), lambda i, j: (i, j, 0))
    const = lambda shape: pl.BlockSpec(shape, lambda i, j: (0,) * len(shape))
    return pl.pallas_call(
        _mlstm_kernel,
        grid=(b, s // L),
        in_specs=[tok_spec(GROUP_WIDTH), tok_spec(GROUP_WIDTH), tok_spec(GROUP_WIDTH), tok_spec(HEAD_DIM),
                  tok_spec(GROUP_WIDTH), tok_spec(GROUP_WIDTH), tok_spec(GROUP_WIDTH),
                  const((1, GROUP_WIDTH)), const((1, GROUP_WIDTH))],
        out_specs=tok_spec(GROUP_WIDTH),
        out_shape=jax.ShapeDtypeStruct((b, s, GROUP_WIDTH), BF16),
        scratch_shapes=[pltpu.VMEM((N_HEADS, HEAD_DIM, HEAD_DIM), F32),
                        pltpu.VMEM((N_HEADS, 1, HEAD_DIM), F32),
                        pltpu.VMEM((N_HEADS, 1, HEAD_DIM), F32)],
        compiler_params=pltpu.CompilerParams(dimension_semantics=("arbitrary", "arbitrary")),
        name="mlstm",
    )(q, k, v, gates, om, zm, xcv, mnorm_g, skip_m)


def _diff_kernel(q_ref, k_ref, vt_ref, zd_ref, g_ref, lq1_ref, lk1_ref, lq2_ref, lk2_ref, yd_ref,
                 qs_ref, m_ref, l_ref, acc_ref, s0_ref, s1_ref, mx0_ref, mx1_ref):
    tq, tk, tv = DIFF_Q_ROWS, DIFF_KV_ROWS, PROJ_ROWS
    i = pl.program_id(1)
    lane = lax.broadcasted_iota(jnp.int32, (tq, HEAD_DIM), 1)
    for hh in range(N_HEADS):
        q = q_ref[0, :, _head(hh)]
        zero = jnp.zeros_like(q)
        qs_ref[hh, 0:tq, :] = jnp.where(lane < DIFF_QK_DIM, q, zero)
        qs_ref[hh, tq:2 * tq, :] = jnp.where(lane >= DIFF_QK_DIM, q, zero)
    m_ref[...] = jnp.full_like(m_ref, -jnp.inf)
    l_ref[...] = jnp.zeros_like(l_ref)
    acc_ref[...] = jnp.zeros_like(acc_ref)

    def scores(t, s_ref, mx_ref, masked):
        rows = pl.ds(pl.multiple_of(t * tk, tk), tk)
        if masked:
            key_pos = t * tk + lax.broadcasted_iota(jnp.int32, (tk, 2 * tq), 0)
            q_pos = i * tq + (lax.broadcasted_iota(jnp.int32, (tk, 2 * tq), 1) & (tq - 1))
            visible = key_pos <= q_pos
        for hh in range(N_HEADS):
            s = _dot_nt(k_ref[0, rows, _head(hh)], qs_ref[hh])
            if masked:
                s = jnp.where(visible, s, -jnp.inf)
            s_ref[hh] = s
            mx_ref[hh] = jnp.max(s, axis=0, keepdims=True)

    def consume(t, s_ref, mx_ref):
        for hh in range(N_HEADS):
            sl = _head(hh)
            m_old = m_ref[hh]
            m_new = jnp.maximum(m_old, mx_ref[hh])
            alpha = jnp.exp2(m_old - m_new)
            p = jnp.exp2(s_ref[hh] - m_new)
            l_ref[hh] = alpha * l_ref[hh] + jnp.sum(p, axis=0, keepdims=True)
            pb = p.astype(BF16)
            pv = _dot(vt_ref[0, t * (tk // tv), sl, :], pb[0:tv])
            for c in range(1, tk // tv):
                pv = pv + _dot(vt_ref[0, t * (tk // tv) + c, sl, :], pb[c * tv:(c + 1) * tv])
            acc_ref[hh] = alpha * acc_ref[hh] + pv
            m_ref[hh] = m_new

    def pair(u, carry):
        consume(2 * u, s0_ref, mx0_ref)
        scores(2 * u + 1, s1_ref, mx1_ref, masked=False)
        consume(2 * u + 1, s1_ref, mx1_ref)
        scores(2 * u + 2, s0_ref, mx0_ref, masked=True)
        return carry

    scores(0, s0_ref, mx0_ref, masked=True)
    lax.fori_loop(0, i // 2, pair, 0)

    @pl.when(i % 2 == 0)
    def _():
        consume(i, s0_ref, mx0_ref)

    @pl.when(i % 2 == 1)
    def _():
        consume(i - 1, s0_ref, mx0_ref)
        scores(i, s1_ref, mx1_ref, masked=True)
        consume(i, s1_ref, mx1_ref)

    lam = (jnp.exp(jnp.sum(lq1_ref[...] * lk1_ref[...], axis=-1, keepdims=True))
           - jnp.exp(jnp.sum(lq2_ref[...] * lk2_ref[...], axis=-1, keepdims=True)) + LAM_INIT)
    for hh in range(N_HEADS):
        sl = _head(hh)
        acc = acc_ref[hh]
        inv = 1.0 / l_ref[hh]
        o_t = acc[:, :tq] * inv[:, :tq] - lam * (acc[:, tq:] * inv[:, tq:])
        o = _rms(o_t.T, g_ref[:, sl]) * (1.0 - LAM_INIT)
        yd_ref[0, :, sl] = (o * _silu(zd_ref[0, :, sl].astype(F32))).astype(BF16)


def _diff_call(qd, kd, vdt, zd, dnorm_g, lq1, lk1, lq2, lk2):
    b, s, _ = qd.shape
    tq = DIFF_Q_ROWS
    nkv = vdt.shape[1]
    lam_spec = pl.BlockSpec((1, DIFF_QK_DIM), lambda i, j: (0, 0))
    tok_spec = pl.BlockSpec((1, tq, GROUP_WIDTH), lambda i, j: (i, j, 0))
    return pl.pallas_call(
        _diff_kernel,
        grid=(b, s // tq),
        in_specs=[
            tok_spec,
            pl.BlockSpec((1, s, GROUP_WIDTH), lambda i, j: (i, 0, 0)),
            pl.BlockSpec((1, nkv, GROUP_WIDTH, PROJ_ROWS), lambda i, j: (i, 0, 0, 0)),
            tok_spec,
            pl.BlockSpec((1, GROUP_WIDTH), lambda i, j: (0, 0)),
            lam_spec, lam_spec, lam_spec, lam_spec,
        ],
        out_specs=tok_spec,
        out_shape=jax.ShapeDtypeStruct((b, s, GROUP_WIDTH), BF16),
        scratch_shapes=[pltpu.VMEM((N_HEADS, 2 * tq, HEAD_DIM), BF16),
                        pltpu.VMEM((N_HEADS, 1, 2 * tq), F32),
                        pltpu.VMEM((N_HEADS, 1, 2 * tq), F32),
                        pltpu.VMEM((N_HEADS, HEAD_DIM, 2 * tq), F32),
                        pltpu.VMEM((N_HEADS, DIFF_KV_ROWS, 2 * tq), F32),
                        pltpu.VMEM((N_HEADS, DIFF_KV_ROWS, 2 * tq), F32),
                        pltpu.VMEM((N_HEADS, 1, 2 * tq), F32),
                        pltpu.VMEM((N_HEADS, 1, 2 * tq), F32)],
        compiler_params=pltpu.CompilerParams(
            dimension_semantics=("arbitrary", "arbitrary"),
            vmem_limit_bytes=52 * 1024 * 1024),
        name="diffattn",
    )(qd, kd, vdt, zd, dnorm_g, lq1, lk1, lq2, lk2)


def _out_kernel(x_ref, ym_ref, yd_ref, yc_ref, wo_ref, fg_ref, o_ref):
    y = (_dot(ym_ref[...], wo_ref[0:GROUP_WIDTH, :])
         + _dot(yd_ref[...], wo_ref[GROUP_WIDTH:2 * GROUP_WIDTH, :])
         + _dot(yc_ref[...], wo_ref[2 * GROUP_WIDTH:3 * GROUP_WIDTH, :]))
    o_ref[...] = _rms(x_ref[...] + y, fg_ref[...])


def _out_call(x2, ym, yd, yc, w_out, final_g):
    t = x2.shape[0]
    tm = OUT_ROWS
    row_spec = lambda w: pl.BlockSpec((tm, w), lambda i: (i, 0))
    return pl.pallas_call(
        _out_kernel,
        grid=(t // tm,),
        in_specs=[row_spec(D_MODEL), row_spec(GROUP_WIDTH), row_spec(GROUP_WIDTH), row_spec(GROUP_WIDTH),
                  pl.BlockSpec((3 * GROUP_WIDTH, D_MODEL), lambda i: (0, 0)),
                  pl.BlockSpec((1, D_MODEL), lambda i: (0, 0))],
        out_specs=row_spec(D_MODEL),
        out_shape=jax.ShapeDtypeStruct((t, D_MODEL), F32),
        compiler_params=pltpu.CompilerParams(dimension_semantics=("arbitrary",)),
        name="outproj",
    )(x2, ym, yd, yc, w_out, final_g)


def kernel(x, mem, norm_g, w_in, conv_w, conv_b, wq_m, wk_m, wv_m, w_if, b_if, mnorm_g, skip_m,
           lam_q1, lam_k1, lam_q2, lam_k2, dnorm_g, mem_norm_g, w_mem_kv, w_out, final_g):
    b, s, d = x.shape
    assert (d, s % DIFF_KV_ROWS, DIFF_KV_ROWS % PROJ_ROWS, (b * s) % OUT_ROWS) == (D_MODEL, 0, 0, 0)
    assert DIFF_Q_ROWS == DIFF_KV_ROWS
    assert norm_g.shape[0] == 1, "single-layer kernel"
    l = 0
    w_if_pad = jnp.pad(w_if[l], ((0, 0), (0, HEAD_DIM - w_if.shape[-1]))).astype(BF16)
    b_if_pad = jnp.pad(b_if[l], (0, HEAD_DIM - b_if.shape[-1]))[None, :]

    kv = _memkv_call(mem, mem_norm_g[l][None, :], w_mem_kv[l].astype(BF16))
    (xcv, qm, km, vm, gates, om, zm, qd, kd, vdt, zd, yc) = _proj_call(
        x, kv, norm_g[l][None, :], w_in[l].astype(BF16), conv_w[l], conv_b[l][None, :],
        wq_m[l].astype(BF16), wk_m[l].astype(BF16), wv_m[l].astype(BF16), w_if_pad, b_if_pad)
    ym = _mlstm_call(qm, km, vm, gates, om, zm, xcv, mnorm_g[l][None, :], skip_m[l][None, :])
    yd = _diff_call(qd, kd, vdt, zd, dnorm_g[l][None, :],
                    lam_q1[l][None, :], lam_k1[l][None, :], lam_q2[l][None, :], lam_k2[l][None, :])
    out = _out_call(x.reshape(b * s, d), ym.reshape(b * s, GROUP_WIDTH), yd.reshape(b * s, GROUP_WIDTH),
                    yc.reshape(b * s, GROUP_WIDTH), w_out[l].astype(BF16), final_g[None, :])
    return out.reshape(b, s, d)
```

```python
import functools
import math

import jax
import jax.numpy as jnp
from jax import lax
from jax.experimental import pallas as pl
from jax.experimental.pallas import tpu as pltpu

F32 = jnp.float32
BF16 = jnp.bfloat16

D_MODEL = 1024
N_HEADS = 4
HEAD_DIM = 128
GROUP_WIDTH = N_HEADS * HEAD_DIM
DIFF_QK_DIM = 64
CONV_WIDTH = 4
MLSTM_CHUNK = 128
MLSTM_STEP_CHUNKS = 4
MEM_LEN = 256
IN_WIDTH = 9 * GROUP_WIDTH
NORM_EPS = 1e-6
LAM_INIT = 0.8 - 0.6 * math.exp(-0.3 * 0)
LOG2E = math.log2(math.e)

OFF_XM, OFF_OM, OFF_ZM, OFF_QD, OFF_KD, OFF_VD, OFF_ZD, OFF_QC, OFF_ZC = (
    i * GROUP_WIDTH for i in range(9))

PROJ_ROWS = 256
DIFF_Q_ROWS = 512
DIFF_KV_ROWS = 512
OUT_ROWS = 512
CONV_HALO = 8

NT_DIMS = (((1,), (1,)), ((), ()))
TN_DIMS = (((0,), (0,)), ((), ()))


def _dot(a, b):
    return jnp.dot(a, b, preferred_element_type=F32)


def _dot_nt(a, b):
    return lax.dot_general(a, b, NT_DIMS, preferred_element_type=F32)


def _silu(x):
    return x * jax.nn.sigmoid(x)


def _rms(x, g):
    return x * lax.rsqrt(jnp.mean(x * x, axis=-1, keepdims=True) + NORM_EPS) * g


def _head(h):
    return slice(h * HEAD_DIM, (h + 1) * HEAD_DIM)


def _memkv_kernel(mem_ref, g_ref, w_ref, kv_ref):
    y = _rms(mem_ref[0], g_ref[...])
    kv_ref[0] = _dot(y.astype(BF16), w_ref[...]).astype(BF16)


def _memkv_call(mem, g, w):
    b = mem.shape[0]
    return pl.pallas_call(
        _memkv_kernel,
        grid=(b,),
        in_specs=[
            pl.BlockSpec((1, MEM_LEN, D_MODEL), lambda i: (i, 0, 0)),
            pl.BlockSpec((1, D_MODEL), lambda i: (0, 0)),
            pl.BlockSpec((D_MODEL, 2 * GROUP_WIDTH), lambda i: (0, 0)),
        ],
        out_specs=pl.BlockSpec((1, MEM_LEN, 2 * GROUP_WIDTH), lambda i: (i, 0, 0)),
        out_shape=jax.ShapeDtypeStruct((b, MEM_LEN, 2 * GROUP_WIDTH), BF16),
        compiler_params=pltpu.CompilerParams(dimension_semantics=("arbitrary",)),
        name="memkv",
    )(mem, g, w)


def _proj_kernel(x_ref, kv_ref, ng_ref, win_ref, cw_ref, cb_ref, wq_ref, wk_ref, wv_ref, wif_ref, bif_ref,
                 xcv_ref, qm_ref, km_ref, vm_ref, gates_ref, om_ref, zm_ref,
                 qd_ref, kd_ref, vdt_ref, zd_ref, yc_ref, conv_ref):
    tm = PROJ_ROWS
    s = pl.program_id(1)
    h = _rms(x_ref[0], ng_ref[...]).astype(BF16)

    def proj(off):
        return _dot(h, win_ref[:, off:off + GROUP_WIDTH])

    x_m = proj(OFF_XM)

    @pl.when(s == 0)
    def _():
        conv_ref[0:CONV_HALO, :] = jnp.zeros((CONV_HALO, GROUP_WIDTH), F32)

    @pl.when(s != 0)
    def _():
        conv_ref[0:CONV_HALO, :] = conv_ref[tm:tm + CONV_HALO, :]

    conv_ref[CONV_HALO:CONV_HALO + tm, :] = x_m
    acc = jnp.broadcast_to(cb_ref[...], (tm, GROUP_WIDTH))
    for j in range(CONV_WIDTH):
        start = CONV_HALO - (CONV_WIDTH - 1) + j
        acc = acc + cw_ref[j:j + 1, :] * conv_ref[start:start + tm, :]
    x_cv = _silu(acc)
    xcv_ref[0] = x_cv.astype(BF16)
    xcv_b = x_cv.astype(BF16)
    xm_b = x_m.astype(BF16)

    gates = jnp.broadcast_to(bif_ref[...], (tm, HEAD_DIM))
    for hh in range(N_HEADS):
        sl = _head(hh)
        qb = _dot(xcv_b[:, sl], wq_ref[hh]).astype(BF16)
        kf = _dot(xcv_b[:, sl], wk_ref[hh])
        vb = _dot(xm_b[:, sl], wv_ref[hh]).astype(BF16)
        qm_ref[0, :, sl] = qb
        km_ref[0, :, sl] = (kf * HEAD_DIM ** -0.5).astype(BF16)
        vm_ref[0, :, sl] = vb
        gates = (gates
                 + _dot(qb, wif_ref[hh * HEAD_DIM:(hh + 1) * HEAD_DIM, :])
                 + _dot(kf.astype(BF16), wif_ref[GROUP_WIDTH + hh * HEAD_DIM:GROUP_WIDTH + (hh + 1) * HEAD_DIM, :])
                 + _dot(vb, wif_ref[2 * GROUP_WIDTH + hh * HEAD_DIM:2 * GROUP_WIDTH + (hh + 1) * HEAD_DIM, :]))
    gates_ref[0] = gates
    om_ref[0] = proj(OFF_OM).astype(BF16)
    zm_ref[0] = proj(OFF_ZM).astype(BF16)

    qd_ref[0] = (proj(OFF_QD) * (DIFF_QK_DIM ** -0.5 * LOG2E)).astype(BF16)
    kd_ref[0] = proj(OFF_KD).astype(BF16)
    vdt_ref[0, 0] = proj(OFF_VD).T.astype(BF16)
    zd_ref[0] = proj(OFF_ZD).astype(BF16)

    q_c = proj(OFF_QC)
    z_c = proj(OFF_ZC)
    for hh in range(N_HEADS):
        sl = _head(hh)
        km = kv_ref[0, :, sl]
        vmem = kv_ref[0, :, GROUP_WIDTH + hh * HEAD_DIM:GROUP_WIDTH + (hh + 1) * HEAD_DIM]
        sc = _dot_nt(q_c[:, sl].astype(BF16), km) * HEAD_DIM ** -0.5
        p = jnp.exp(sc - jnp.max(sc, axis=-1, keepdims=True))
        pc = p / jnp.sum(p, axis=-1, keepdims=True)
        oc = _dot(pc.astype(BF16), vmem)
        yc_ref[0, :, sl] = (oc * _silu(z_c[:, sl])).astype(BF16)


def _proj_call(x, kv, norm_g, w_in, conv_w, conv_b, wq, wk, wv, w_if, b_if):
    b, s, _ = x.shape
    tm = PROJ_ROWS
    ns = s // tm
    tok = lambda w, dt: jax.ShapeDtypeStruct((b, s, w), dt)
    tok_spec = lambda w: pl.BlockSpec((1, tm, w), lambda i, j: (i, j, 0))
    const = lambda shape: pl.BlockSpec(shape, lambda i, j: (0,) * len(shape))
    out_shape = (
        tok(GROUP_WIDTH, BF16),
        tok(GROUP_WIDTH, BF16), tok(GROUP_WIDTH, BF16), tok(GROUP_WIDTH, BF16),
        tok(HEAD_DIM, F32),
        tok(GROUP_WIDTH, BF16), tok(GROUP_WIDTH, BF16),
        tok(GROUP_WIDTH, BF16), tok(GROUP_WIDTH, BF16),
        jax.ShapeDtypeStruct((b, ns, GROUP_WIDTH, tm), BF16),
        tok(GROUP_WIDTH, BF16),
        tok(GROUP_WIDTH, BF16),
    )
    out_specs = (
        tok_spec(GROUP_WIDTH), tok_spec(GROUP_WIDTH), tok_spec(GROUP_WIDTH), tok_spec(GROUP_WIDTH),
        tok_spec(HEAD_DIM), tok_spec(GROUP_WIDTH), tok_spec(GROUP_WIDTH),
        tok_spec(GROUP_WIDTH), tok_spec(GROUP_WIDTH),
        pl.BlockSpec((1, 1, GROUP_WIDTH, tm), lambda i, j: (i, j, 0, 0)),
        tok_spec(GROUP_WIDTH), tok_spec(GROUP_WIDTH),
    )
    in_specs = [
        pl.BlockSpec((1, tm, D_MODEL), lambda i, j: (i, j, 0)),
        pl.BlockSpec((1, MEM_LEN, 2 * GROUP_WIDTH), lambda i, j: (i, 0, 0)),
        const((1, D_MODEL)),
        const((D_MODEL, IN_WIDTH)),
        const((CONV_WIDTH, GROUP_WIDTH)),
        const((1, GROUP_WIDTH)),
        const((N_HEADS, HEAD_DIM, HEAD_DIM)),
        const((N_HEADS, HEAD_DIM, HEAD_DIM)),
        const((N_HEADS, HEAD_DIM, HEAD_DIM)),
        const((3 * GROUP_WIDTH, HEAD_DIM)),
        const((1, HEAD_DIM)),
    ]
    return pl.pallas_call(
        _proj_kernel,
        grid=(b, ns),
        in_specs=in_specs,
        out_specs=out_specs,
        out_shape=out_shape,
        scratch_shapes=[pltpu.VMEM((tm + CONV_HALO, GROUP_WIDTH), F32)],
        compiler_params=pltpu.CompilerParams(
            dimension_semantics=("arbitrary", "arbitrary"),
            vmem_limit_bytes=56 * 1024 * 1024),
        name="inproj",
    )(x, kv, norm_g, w_in, conv_w, conv_b, wq, wk, wv, w_if, b_if)


def _log_sigmoid(x):
    return jnp.minimum(x, 0.0) - jnp.log1p(jnp.exp(-jnp.abs(x)))


def _cumsum_rows(f, tril):
    hi = f.astype(BF16)
    r1 = f - hi.astype(F32)
    mid = r1.astype(BF16)
    lo = (r1 - mid.astype(F32)).astype(BF16)
    return _dot(tril, hi) + _dot(tril, mid) + _dot(tril, lo)


def _mlstm_kernel(q_ref, k_ref, v_ref, g_ref, om_ref, zm_ref, xcv_ref, mg_ref, sk_ref,
                  ym_ref, c_ref, n_ref, m_ref):
    L = MLSTM_CHUNK

    @pl.when(pl.program_id(1) == 0)
    def _():
        c_ref[...] = jnp.zeros_like(c_ref)
        n_ref[...] = jnp.zeros_like(n_ref)
        m_ref[...] = jnp.zeros_like(m_ref)

    row = lax.broadcasted_iota(jnp.int32, (L, L), 0)
    col = lax.broadcasted_iota(jnp.int32, (L, L), 1)
    causal = col <= row
    tril = jnp.where(causal, 1.0, 0.0).astype(BF16)

    chunks = range(MLSTM_STEP_CHUNKS)
    heads = range(N_HEADS)
    probs = [(c, hh) for c in chunks for hh in heads]
    rows = lambda c: slice(c * L, (c + 1) * L)
    q_of = lambda p: q_ref[0, rows(p[0]), _head(p[1])]
    k_of = lambda p: k_ref[0, rows(p[0]), _head(p[1])]
    v_of = lambda p: v_ref[0, rows(p[0]), _head(p[1])]

    a_cols, a_rows = [], []
    for c in chunks:
        g = g_ref[0, rows(c), :]
        bb_all = _cumsum_rows(_log_sigmoid(g), tril)
        ac = jnp.where(col < N_HEADS, g, bb_all)
        a_cols.append(ac)
        a_rows.append(ac.T)
    ib_col = {(c, hh): a_cols[c][:, hh:hh + 1] for c, hh in probs}
    bb_col = {(c, hh): a_cols[c][:, N_HEADS + hh:N_HEADS + hh + 1] for c, hh in probs}
    ib_row = {(c, hh): a_rows[c][hh:hh + 1, :] for c, hh in probs}
    bb_row = {(c, hh): a_rows[c][N_HEADS + hh:N_HEADS + hh + 1, :] for c, hh in probs}
    b_end = {p: bb_col[p][L - 1:L, :] for p in probs}

    qk = {p: _dot_nt(q_of(p), k_of(p)) for p in probs}

    each = lambda f: {p: f(p) for p in probs}
    d = each(lambda p: jnp.where(causal, (bb_col[p] - bb_row[p]) + ib_row[p], -jnp.inf))
    r = each(lambda p: jnp.max(d[p], axis=-1, keepdims=True))
    s_loc = each(lambda p: qk[p] * jnp.exp(d[p] - r[p]))
    rs = each(lambda p: jnp.sum(s_loc[p], axis=-1, keepdims=True))
    s_bf = each(lambda p: s_loc[p].astype(BF16))
    a = each(lambda p: (b_end[p] - bb_col[p]) + ib_col[p])
    a_max = each(lambda p: jnp.max(a[p], axis=0, keepdims=True))
    w_loc = each(lambda p: jnp.exp(a[p] - a_max[p]))
    vw = each(lambda p: (v_of(p).astype(F32) * w_loc[p]).astype(BF16))
    nu = each(lambda p: jnp.sum(k_of(p).astype(F32) * w_loc[p], axis=0, keepdims=True))

    sv = {p: _dot(s_bf[p], v_of(p)) for p in probs}
    u = {p: lax.dot_general(vw[p], k_of(p), TN_DIMS, preferred_element_type=F32) for p in probs}

    m_in, c_in, n_in = {}, {}, {}
    for hh in heads:
        m_cur = m_ref[hh][:, 0:1]
        c_cur = c_ref[hh]
        n_cur = n_ref[hh]
        for c in chunks:
            p = (c, hh)
            m_in[p], c_in[p], n_in[p] = m_cur, c_cur.astype(BF16), n_cur
            m_new = jnp.maximum(b_end[p] + m_cur, a_max[p])
            decay = jnp.exp(b_end[p] + m_cur - m_new)
            e_upd = jnp.exp(a_max[p] - m_new)
            c_cur = decay * c_cur + e_upd * u[p]
            n_cur = decay * n_cur + e_upd * nu[p]
            m_cur = m_new
        c_ref[hh] = c_cur
        n_ref[hh] = n_cur
        m_ref[hh] = jnp.broadcast_to(m_cur, (1, HEAD_DIM))

    q_c = {p: _dot_nt(q_of(p), c_in[p]) for p in probs}

    tile_of = lambda ref, p: ref[0, rows(p[0]), _head(p[1])].astype(F32)
    g_col = each(lambda p: bb_col[p] + m_in[p])
    m_t = each(lambda p: jnp.maximum(g_col[p], r[p]))
    e_loc = each(lambda p: jnp.exp(r[p] - m_t[p]))
    inter = each(lambda p: jnp.exp(g_col[p] - m_t[p]))
    q_n = each(lambda p: jnp.sum(q_of(p).astype(F32) * n_in[p], axis=-1, keepdims=True))
    den = each(lambda p: e_loc[p] * rs[p] + inter[p] * q_n[p])
    scale = each(lambda p: 1.0 / jnp.maximum(jnp.abs(den[p]), jnp.exp(-m_t[p])))
    hg = each(lambda p: jax.nn.sigmoid(tile_of(om_ref, p))
              * ((e_loc[p] * sv[p] + inter[p] * q_c[p]) * scale[p]))
    mu = each(lambda p: jnp.mean(hg[p], axis=-1, keepdims=True))
    cen = each(lambda p: hg[p] - mu[p])
    var = each(lambda p: jnp.mean(jnp.square(cen[p]), axis=-1, keepdims=True))
    for p in probs:
        sl = _head(p[1])
        y = cen[p] * lax.rsqrt(var[p] + NORM_EPS) * mg_ref[:, sl]
        y = (y + sk_ref[:, sl] * tile_of(xcv_ref, p)) * _silu(tile_of(zm_ref, p))
        ym_ref[0, rows(p[0]), sl] = y.astype(BF16)


def _mlstm_call(q, k, v, gates, om, zm, xcv, mnorm_g, skip_m):
    b, s, _ = q.shape
    rows = MLSTM_STEP_CHUNKS * MLSTM_CHUNK
    tok_spec = lambda w: pl.BlockSpec((1, rows, w), lambda i, j: (i, j, 0))
    const = lambda shape: pl.BlockSpec(shape, lambda i, j: (0,) * len(shape))
    return pl.pallas_call(
        _mlstm_kernel,
        grid=(b, s // rows),
        in_specs=[tok_spec(GROUP_WIDTH), tok_spec(GROUP_WIDTH), tok_spec(GROUP_WIDTH), tok_spec(HEAD_DIM),
                  tok_spec(GROUP_WIDTH), tok_spec(GROUP_WIDTH), tok_spec(GROUP_WIDTH),
                  const((1, GROUP_WIDTH)), const((1, GROUP_WIDTH))],
        out_specs=tok_spec(GROUP_WIDTH),
        out_shape=jax.ShapeDtypeStruct((b, s, GROUP_WIDTH), BF16),
        scratch_shapes=[pltpu.VMEM((N_HEADS, HEAD_DIM, HEAD_DIM), F32),
                        pltpu.VMEM((N_HEADS, 1, HEAD_DIM), F32),
                        pltpu.VMEM((N_HEADS, 1, HEAD_DIM), F32)],
        compiler_params=pltpu.CompilerParams(dimension_semantics=("arbitrary", "arbitrary")),
        name="mlstm",
    )(q, k, v, gates, om, zm, xcv, mnorm_g, skip_m)


def _diff_kernel(q_ref, k_ref, vt_ref, zd_ref, g_ref, lq1_ref, lk1_ref, lq2_ref, lk2_ref, yd_ref,
                 qs_ref, m_ref, l_ref, acc_ref, s0_ref, s1_ref, mx0_ref, mx1_ref):
    tq, tk, tv = DIFF_Q_ROWS, DIFF_KV_ROWS, PROJ_ROWS
    i = pl.program_id(1)
    lane = lax.broadcasted_iota(jnp.int32, (tq, HEAD_DIM), 1)
    for hh in range(N_HEADS):
        q = q_ref[0, :, _head(hh)]
        zero = jnp.zeros_like(q)
        qs_ref[hh, 0:tq, :] = jnp.where(lane < DIFF_QK_DIM, q, zero)
        qs_ref[hh, tq:2 * tq, :] = jnp.where(lane >= DIFF_QK_DIM, q, zero)
    m_ref[...] = jnp.full_like(m_ref, -jnp.inf)
    l_ref[...] = jnp.zeros_like(l_ref)
    acc_ref[...] = jnp.zeros_like(acc_ref)

    def scores(t, s_ref, mx_ref, masked):
        rows = pl.ds(pl.multiple_of(t * tk, tk), tk)
        if masked:
            key_pos = t * tk + lax.broadcasted_iota(jnp.int32, (tk, 2 * tq), 0)
            q_pos = i * tq + (lax.broadcasted_iota(jnp.int32, (tk, 2 * tq), 1) & (tq - 1))
            visible = key_pos <= q_pos
        for hh in range(N_HEADS):
            s = _dot_nt(k_ref[0, rows, _head(hh)], qs_ref[hh])
            if masked:
                s = jnp.where(visible, s, -jnp.inf)
            s_ref[hh] = s
            mx_ref[hh] = jnp.max(s, axis=0, keepdims=True)

    def consume(t, s_ref, mx_ref):
        for hh in range(N_HEADS):
            sl = _head(hh)
            m_old = m_ref[hh]
            m_new = jnp.maximum(m_old, mx_ref[hh])
            alpha = jnp.exp2(m_old - m_new)
            p = jnp.exp2(s_ref[hh] - m_new)
            l_ref[hh] = alpha * l_ref[hh] + jnp.sum(p, axis=0, keepdims=True)
            pb = p.astype(BF16)
            pv = _dot(vt_ref[0, t * (tk // tv), sl, :], pb[0:tv])
            for c in range(1, tk // tv):
                pv = pv + _dot(vt_ref[0, t * (tk // tv) + c, sl, :], pb[c * tv:(c + 1) * tv])
            acc_ref[hh] = alpha * acc_ref[hh] + pv
            m_ref[hh] = m_new

    def pair(u, carry):
        consume(2 * u, s0_ref, mx0_ref)
        scores(2 * u + 1, s1_ref, mx1_ref, masked=False)
        consume(2 * u + 1, s1_ref, mx1_ref)
        scores(2 * u + 2, s0_ref, mx0_ref, masked=True)
        return carry

    scores(0, s0_ref, mx0_ref, masked=True)
    lax.fori_loop(0, i // 2, pair, 0)

    @pl.when(i % 2 == 0)
    def _():
        consume(i, s0_ref, mx0_ref)

    @pl.when(i % 2 == 1)
    def _():
        consume(i - 1, s0_ref, mx0_ref)
        scores(i, s1_ref, mx1_ref, masked=True)
        consume(i, s1_ref, mx1_ref)

    lam = (jnp.exp(jnp.sum(lq1_ref[...] * lk1_ref[...], axis=-1, keepdims=True))
           - jnp.exp(jnp.sum(lq2_ref[...] * lk2_ref[...], axis=-1, keepdims=True)) + LAM_INIT)
    for hh in range(N_HEADS):
        sl = _head(hh)
        acc = acc_ref[hh]
        inv = 1.0 / l_ref[hh]
        o_t = acc[:, :tq] * inv[:, :tq] - lam * (acc[:, tq:] * inv[:, tq:])
        o = _rms(o_t.T, g_ref[:, sl]) * (1.0 - LAM_INIT)
        yd_ref[0, :, sl] = (o * _silu(zd_ref[0, :, sl].astype(F32))).astype(BF16)


def _diff_call(qd, kd, vdt, zd, dnorm_g, lq1, lk1, lq2, lk2):
    b, s, _ = qd.shape
    tq = DIFF_Q_ROWS
    nkv = vdt.shape[1]
    lam_spec = pl.BlockSpec((1, DIFF_QK_DIM), lambda i, j: (0, 0))
    tok_spec = pl.BlockSpec((1, tq, GROUP_WIDTH), lambda i, j: (i, j, 0))
    return pl.pallas_call(
        _diff_kernel,
        grid=(b, s // tq),
        in_specs=[
            tok_spec,
            pl.BlockSpec((1, s, GROUP_WIDTH), lambda i, j: (i, 0, 0)),
            pl.BlockSpec((1, nkv, GROUP_WIDTH, PROJ_ROWS), lambda i, j: (i, 0, 0, 0)),
            tok_spec,
            pl.BlockSpec((1, GROUP_WIDTH), lambda i, j: (0, 0)),
            lam_spec, lam_spec, lam_spec, lam_spec,
        ],
        out_specs=tok_spec,
        out_shape=jax.ShapeDtypeStruct((b, s, GROUP_WIDTH), BF16),
        scratch_shapes=[pltpu.VMEM((N_HEADS, 2 * tq, HEAD_DIM), BF16),
                        pltpu.VMEM((N_HEADS, 1, 2 * tq), F32),
                        pltpu.VMEM((N_HEADS, 1, 2 * tq), F32),
                        pltpu.VMEM((N_HEADS, HEAD_DIM, 2 * tq), F32),
                        pltpu.VMEM((N_HEADS, DIFF_KV_ROWS, 2 * tq), F32),
                        pltpu.VMEM((N_HEADS, DIFF_KV_ROWS, 2 * tq), F32),
                        pltpu.VMEM((N_HEADS, 1, 2 * tq), F32),
                        pltpu.VMEM((N_HEADS, 1, 2 * tq), F32)],
        compiler_params=pltpu.CompilerParams(
            dimension_semantics=("arbitrary", "arbitrary"),
            vmem_limit_bytes=52 * 1024 * 1024),
        name="diffattn",
    )(qd, kd, vdt, zd, dnorm_g, lq1, lk1, lq2, lk2)


def _out_kernel(x_ref, ym_ref, yd_ref, yc_ref, wo_ref, fg_ref, o_ref):
    y = (_dot(ym_ref[...], wo_ref[0:GROUP_WIDTH, :])
         + _dot(yd_ref[...], wo_ref[GROUP_WIDTH:2 * GROUP_WIDTH, :])
         + _dot(yc_ref[...], wo_ref[2 * GROUP_WIDTH:3 * GROUP_WIDTH, :]))
    o_ref[...] = _rms(x_ref[...] + y, fg_ref[...])


def _out_call(x2, ym, yd, yc, w_out, final_g):
    t = x2.shape[0]
    tm = OUT_ROWS
    row_spec = lambda w: pl.BlockSpec((tm, w), lambda i: (i, 0))
    return pl.pallas_call(
        _out_kernel,
        grid=(t // tm,),
        in_specs=[row_spec(D_MODEL), row_spec(GROUP_WIDTH), row_spec(GROUP_WIDTH), row_spec(GROUP_WIDTH),
                  pl.BlockSpec((3 * GROUP_WIDTH, D_MODEL), lambda i: (0, 0)),
                  pl.BlockSpec((1, D_MODEL), lambda i: (0, 0))],
        out_specs=row_spec(D_MODEL),
        out_shape=jax.ShapeDtypeStruct((t, D_MODEL), F32),
        compiler_params=pltpu.CompilerParams(dimension_semantics=("arbitrary",)),
        name="outproj",
    )(x2, ym, yd, yc, w_out, final_g)


def kernel(x, mem, norm_g, w_in, conv_w, conv_b, wq_m, wk_m, wv_m, w_if, b_if, mnorm_g, skip_m,
           lam_q1, lam_k1, lam_q2, lam_k2, dnorm_g, mem_norm_g, w_mem_kv, w_out, final_g):
    b, s, d = x.shape
    assert (d, s % DIFF_KV_ROWS, DIFF_KV_ROWS % PROJ_ROWS, (b * s) % OUT_ROWS) == (D_MODEL, 0, 0, 0)
    assert DIFF_Q_ROWS == DIFF_KV_ROWS
    assert norm_g.shape[0] == 1, "single-layer kernel"
    l = 0
    w_if_pad = jnp.pad(w_if[l], ((0, 0), (0, HEAD_DIM - w_if.shape[-1]))).astype(BF16)
    b_if_pad = jnp.pad(b_if[l], (0, HEAD_DIM - b_if.shape[-1]))[None, :]

    kv = _memkv_call(mem, mem_norm_g[l][None, :], w_mem_kv[l].astype(BF16))
    (xcv, qm, km, vm, gates, om, zm, qd, kd, vdt, zd, yc) = _proj_call(
        x, kv, norm_g[l][None, :], w_in[l].astype(BF16), conv_w[l], conv_b[l][None, :],
        wq_m[l].astype(BF16), wk_m[l].astype(BF16), wv_m[l].astype(BF16), w_if_pad, b_if_pad)
    ym = _mlstm_call(qm, km, vm, gates, om, zm, xcv, mnorm_g[l][None, :], skip_m[l][None, :])
    yd = _diff_call(qd, kd, vdt, zd, dnorm_g[l][None, :],
                    lam_q1[l][None, :], lam_k1[l][None, :], lam_q2[l][None, :], lam_k2[l][None, :])
    out = _out_call(x.reshape(b * s, d), ym.reshape(b * s, GROUP_WIDTH), yd.reshape(b * s, GROUP_WIDTH),
                    yc.reshape(b * s, GROUP_WIDTH), w_out[l].astype(BF16), final_g[None, :])
    return out.reshape(b, s, d)
```

```python
import functools
import math

import jax
import jax.numpy as jnp
from jax import lax
from jax.experimental import pallas as pl
from jax.experimental.pallas import tpu as pltpu

F32 = jnp.float32
BF16 = jnp.bfloat16

D_MODEL = 1024
N_HEADS = 4
HEAD_DIM = 128
GROUP_WIDTH = N_HEADS * HEAD_DIM
DIFF_QK_DIM = 64
CONV_WIDTH = 4
MLSTM_CHUNK = 128
MLSTM_STEP_CHUNKS = 4
MEM_LEN = 256
IN_WIDTH = 9 * GROUP_WIDTH
NORM_EPS = 1e-6
LAM_INIT = 0.8 - 0.6 * math.exp(-0.3 * 0)
LOG2E = math.log2(math.e)

OFF_XM, OFF_OM, OFF_ZM, OFF_QD, OFF_KD, OFF_VD, OFF_ZD, OFF_QC, OFF_ZC = (
    i * GROUP_WIDTH for i in range(9))

PROJ_ROWS = 256
DIFF_Q_ROWS = 512
DIFF_KV_ROWS = 512
OUT_ROWS = 512
CONV_HALO = 8

NT_DIMS = (((1,), (1,)), ((), ()))
TN_DIMS = (((0,), (0,)), ((), ()))


def _dot(a, b):
    return jnp.dot(a, b, preferred_element_type=F32)


def _dot_nt(a, b):
    return lax.dot_general(a, b, NT_DIMS, preferred_element_type=F32)


def _silu(x):
    return x * jax.nn.sigmoid(x)


def _rms(x, g):
    return x * lax.rsqrt(jnp.mean(x * x, axis=-1, keepdims=True) + NORM_EPS) * g


def _head(h):
    return slice(h * HEAD_DIM, (h + 1) * HEAD_DIM)


def _memkv_kernel(mem_ref, g_ref, w_ref, kv_ref):
    y = _rms(mem_ref[0], g_ref[...])
    kv_ref[0] = _dot(y.astype(BF16), w_ref[...]).astype(BF16)


def _memkv_call(mem, g, w):
    b = mem.shape[0]
    return pl.pallas_call(
        _memkv_kernel,
        grid=(b,),
        in_specs=[
            pl.BlockSpec((1, MEM_LEN, D_MODEL), lambda i: (i, 0, 0)),
            pl.BlockSpec((1, D_MODEL), lambda i: (0, 0)),
            pl.BlockSpec((D_MODEL, 2 * GROUP_WIDTH), lambda i: (0, 0)),
        ],
        out_specs=pl.BlockSpec((1, MEM_LEN, 2 * GROUP_WIDTH), lambda i: (i, 0, 0)),
        out_shape=jax.ShapeDtypeStruct((b, MEM_LEN, 2 * GROUP_WIDTH), BF16),
        compiler_params=pltpu.CompilerParams(dimension_semantics=("arbitrary",)),
        name="memkv",
    )(mem, g, w)


def _proj_kernel(x_ref, kv_ref, ng_ref, win_ref, cw_ref, cb_ref, wqk_ref, wv_ref, wif_ref, bif_ref,
                 xcv_ref, qm_ref, km_ref, vm_ref, gates_ref, om_ref, zm_ref,
                 qd_ref, kd_ref, vdt_ref, zd_ref, yc_ref, conv_ref):
    tm = PROJ_ROWS

    @pl.when(pl.program_id(1) == 0)
    def _():
        conv_ref[tm:tm + CONV_HALO, :] = jnp.zeros((CONV_HALO, GROUP_WIDTH), F32)

    h = _rms(x_ref[0], ng_ref[...]).astype(BF16)

    def proj(off):
        return _dot(h, win_ref[:, off:off + GROUP_WIDTH])

    heads = range(N_HEADS)
    x_m = proj(OFF_XM)
    q_cb = proj(OFF_QC).astype(BF16)
    z_c = proj(OFF_ZC)
    om_f = proj(OFF_OM)

    conv_ref[0:CONV_HALO, :] = conv_ref[tm:tm + CONV_HALO, :]
    conv_ref[CONV_HALO:CONV_HALO + tm, :] = x_m
    acc = jnp.broadcast_to(cb_ref[...], (tm, GROUP_WIDTH))
    for j in range(CONV_WIDTH):
        start = CONV_HALO - (CONV_WIDTH - 1) + j
        acc = acc + cw_ref[j:j + 1, :] * conv_ref[start:start + tm, :]
    sc = [_dot_nt(q_cb[:, _head(hh)], kv_ref[0, :, _head(hh)]) * HEAD_DIM ** -0.5 for hh in heads]
    zm_f = proj(OFF_ZM)
    x_cv = _silu(acc)
    xcv_b = x_cv.astype(BF16)
    xcv_ref[0] = xcv_b
    xm_b = x_m.astype(BF16)
    om_ref[0] = om_f.astype(BF16)

    qk = [_dot(xcv_b[:, _head(hh)], wqk_ref[hh]) for hh in heads]
    vv = [_dot(xm_b[:, _head(hh)], wv_ref[hh]) for hh in heads]
    qd_f = proj(OFF_QD)
    kd_f = proj(OFF_KD)

    mx = [jnp.max(sc[hh], axis=-1, keepdims=True) for hh in heads]
    p = [jnp.exp(sc[hh] - mx[hh]) for hh in heads]
    inv = [1.0 / jnp.sum(p[hh], axis=-1, keepdims=True) for hh in heads]
    pc = [(p[hh] * inv[hh]).astype(BF16) for hh in heads]
    zm_ref[0] = zm_f.astype(BF16)
    qd_ref[0] = (qd_f * (DIFF_QK_DIM ** -0.5 * LOG2E)).astype(BF16)

    q_all = jnp.concatenate([qk[hh][:, :HEAD_DIM] for hh in heads], axis=1).astype(BF16)
    k_all = jnp.concatenate([qk[hh][:, HEAD_DIM:] for hh in heads], axis=1)
    v_all = jnp.concatenate(vv, axis=1).astype(BF16)
    oc = [_dot(pc[hh], kv_ref[0, :, GROUP_WIDTH + hh * HEAD_DIM:GROUP_WIDTH + (hh + 1) * HEAD_DIM]) for hh in heads]
    vd_f = proj(OFF_VD)
    zd_f = proj(OFF_ZD)
    qm_ref[0] = q_all
    km_ref[0] = (k_all * HEAD_DIM ** -0.5).astype(BF16)
    vm_ref[0] = v_all
    gates_ref[0] = (jnp.broadcast_to(bif_ref[...], (tm, HEAD_DIM))
                    + _dot(q_all, wif_ref[0:GROUP_WIDTH, :])
                    + _dot(k_all.astype(BF16), wif_ref[GROUP_WIDTH:2 * GROUP_WIDTH, :])
                    + _dot(v_all, wif_ref[2 * GROUP_WIDTH:3 * GROUP_WIDTH, :]))
    kd_ref[0] = kd_f.astype(BF16)
    for hh in heads:
        yc_ref[0, :, _head(hh)] = (oc[hh] * _silu(z_c[:, _head(hh)])).astype(BF16)
    vdt_ref[0, 0] = vd_f.T.astype(BF16)
    zd_ref[0] = zd_f.astype(BF16)


def _proj_call(x, kv, norm_g, w_in, conv_w, conv_b, wqk, wv, w_if, b_if):
    b, s, _ = x.shape
    tm = PROJ_ROWS
    ns = s // tm
    tok = lambda w, dt: jax.ShapeDtypeStruct((b, s, w), dt)
    tok_spec = lambda w: pl.BlockSpec((1, tm, w), lambda i, j: (i, j, 0))
    const = lambda shape: pl.BlockSpec(shape, lambda i, j: (0,) * len(shape))
    out_shape = (
        tok(GROUP_WIDTH, BF16),
        tok(GROUP_WIDTH, BF16), tok(GROUP_WIDTH, BF16), tok(GROUP_WIDTH, BF16),
        tok(HEAD_DIM, F32),
        tok(GROUP_WIDTH, BF16), tok(GROUP_WIDTH, BF16),
        tok(GROUP_WIDTH, BF16), tok(GROUP_WIDTH, BF16),
        jax.ShapeDtypeStruct((b, ns, GROUP_WIDTH, tm), BF16),
        tok(GROUP_WIDTH, BF16),
        tok(GROUP_WIDTH, BF16),
    )
    out_specs = (
        tok_spec(GROUP_WIDTH), tok_spec(GROUP_WIDTH), tok_spec(GROUP_WIDTH), tok_spec(GROUP_WIDTH),
        tok_spec(HEAD_DIM), tok_spec(GROUP_WIDTH), tok_spec(GROUP_WIDTH),
        tok_spec(GROUP_WIDTH), tok_spec(GROUP_WIDTH),
        pl.BlockSpec((1, 1, GROUP_WIDTH, tm), lambda i, j: (i, j, 0, 0)),
        tok_spec(GROUP_WIDTH), tok_spec(GROUP_WIDTH),
    )
    in_specs = [
        pl.BlockSpec((1, tm, D_MODEL), lambda i, j: (i, j, 0)),
        pl.BlockSpec((1, MEM_LEN, 2 * GROUP_WIDTH), lambda i, j: (i, 0, 0)),
        const((1, D_MODEL)),
        const((D_MODEL, IN_WIDTH)),
        const((CONV_WIDTH, GROUP_WIDTH)),
        const((1, GROUP_WIDTH)),
        const((N_HEADS, HEAD_DIM, 2 * HEAD_DIM)),
        const((N_HEADS, HEAD_DIM, HEAD_DIM)),
        const((3 * GROUP_WIDTH, HEAD_DIM)),
        const((1, HEAD_DIM)),
    ]
    return pl.pallas_call(
        _proj_kernel,
        grid=(b, ns),
        in_specs=in_specs,
        out_specs=out_specs,
        out_shape=out_shape,
        scratch_shapes=[pltpu.VMEM((tm + CONV_HALO, GROUP_WIDTH), F32)],
        compiler_params=pltpu.CompilerParams(
            dimension_semantics=("arbitrary", "arbitrary"),
            vmem_limit_bytes=56 * 1024 * 1024),
        name="inproj",
    )(x, kv, norm_g, w_in, conv_w, conv_b, wqk, wv, w_if, b_if)


def _log_sigmoid(x):
    return jnp.minimum(x, 0.0) - jnp.log1p(jnp.exp(-jnp.abs(x)))


def _cumsum_rows(f, tril):
    hi = f.astype(BF16)
    r1 = f - hi.astype(F32)
    mid = r1.astype(BF16)
    lo = (r1 - mid.astype(F32)).astype(BF16)
    return _dot(tril, hi) + _dot(tril, mid) + _dot(tril, lo)


def _mlstm_kernel(q_ref, k_ref, v_ref, g_ref, om_ref, zm_ref, xcv_ref, mg_ref, sk_ref,
                  x_ref, yd_ref, yc_ref, wo_ref, fg_ref,
                  o_ref, c_ref, n_ref, m_ref, ym_ref, part_ref):
    L = MLSTM_CHUNK

    def project_piece(c, first):
        r_c = slice(c * L, (c + 1) * L)
        if first:
            part_ref[r_c, :] = x_ref[0, r_c, :] + _dot(yd_ref[0, r_c, :], wo_ref[GROUP_WIDTH:2 * GROUP_WIDTH, :])
        else:
            part_ref[r_c, :] += _dot(yc_ref[0, r_c, :], wo_ref[2 * GROUP_WIDTH:3 * GROUP_WIDTH, :])

    side_pieces = [(c, first) for c in range(MLSTM_STEP_CHUNKS) for first in (True, False)]

    def side_work():
        if side_pieces:
            project_piece(*side_pieces.pop(0))

    @pl.when(pl.program_id(1) == 0)
    def _():
        c_ref[...] = jnp.zeros_like(c_ref)
        n_ref[...] = jnp.zeros_like(n_ref)
        m_ref[...] = jnp.zeros_like(m_ref)

    row = lax.broadcasted_iota(jnp.int32, (L, L), 0)
    col = lax.broadcasted_iota(jnp.int32, (L, L), 1)
    causal = col <= row
    tril = jnp.where(causal, 1.0, 0.0).astype(BF16)

    chunks = range(MLSTM_STEP_CHUNKS)
    heads = range(N_HEADS)
    probs = [(c, hh) for c in chunks for hh in heads]
    rows = lambda c: slice(c * L, (c + 1) * L)
    q_of = lambda p: q_ref[0, rows(p[0]), _head(p[1])]
    k_of = lambda p: k_ref[0, rows(p[0]), _head(p[1])]
    v_of = lambda p: v_ref[0, rows(p[0]), _head(p[1])]

    a_cols, a_rows = [], []
    for c in chunks:
        g = g_ref[0, rows(c), :]
        bb_all = _cumsum_rows(_log_sigmoid(g), tril)
        ac = jnp.where(col < N_HEADS, g, bb_all)
        a_cols.append(ac)
        a_rows.append(ac.T)
    ib_col = {(c, hh): a_cols[c][:, hh:hh + 1] for c, hh in probs}
    bb_col = {(c, hh): a_cols[c][:, N_HEADS + hh:N_HEADS + hh + 1] for c, hh in probs}
    ib_row = {(c, hh): a_rows[c][hh:hh + 1, :] for c, hh in probs}
    bb_row = {(c, hh): a_rows[c][N_HEADS + hh:N_HEADS + hh + 1, :] for c, hh in probs}
    b_end = {p: bb_col[p][L - 1:L, :] for p in probs}

    qk = {p: _dot_nt(q_of(p), k_of(p)) for p in probs}

    def each(f):
        out = {p: f(p) for p in probs}
        side_work()
        return out

    d = each(lambda p: jnp.where(causal, (bb_col[p] - bb_row[p]) + ib_row[p], -jnp.inf))
    r = each(lambda p: jnp.max(d[p], axis=-1, keepdims=True))
    s_loc = each(lambda p: qk[p] * jnp.exp(d[p] - r[p]))
    rs = each(lambda p: jnp.sum(s_loc[p], axis=-1, keepdims=True))
    s_bf = each(lambda p: s_loc[p].astype(BF16))
    a = each(lambda p: (b_end[p] - bb_col[p]) + ib_col[p])
    a_max = each(lambda p: jnp.max(a[p], axis=0, keepdims=True))
    w_loc = each(lambda p: jnp.exp(a[p] - a_max[p]))
    vw = each(lambda p: (v_of(p).astype(F32) * w_loc[p]).astype(BF16))
    nu = each(lambda p: jnp.sum(k_of(p).astype(F32) * w_loc[p], axis=0, keepdims=True))

    sv = {p: _dot(s_bf[p], v_of(p)) for p in probs}
    u = {p: lax.dot_general(vw[p], k_of(p), TN_DIMS, preferred_element_type=F32) for p in probs}

    m_in, c_in, n_in = {}, {}, {}
    for hh in heads:
        m_cur = m_ref[hh][:, 0:1]
        c_cur = c_ref[hh]
        n_cur = n_ref[hh]
        for c in chunks:
            p = (c, hh)
            m_in[p], c_in[p], n_in[p] = m_cur, c_cur.astype(BF16), n_cur
            m_new = jnp.maximum(b_end[p] + m_cur, a_max[p])
            decay = jnp.exp(b_end[p] + m_cur - m_new)
            e_upd = jnp.exp(a_max[p] - m_new)
            c_cur = decay * c_cur + e_upd * u[p]
            n_cur = decay * n_cur + e_upd * nu[p]
            m_cur = m_new
        c_ref[hh] = c_cur
        n_ref[hh] = n_cur
        m_ref[hh] = jnp.broadcast_to(m_cur, (1, HEAD_DIM))

    q_c = {p: _dot_nt(q_of(p), c_in[p]) for p in probs}

    tile_of = lambda ref, p: ref[0, rows(p[0]), _head(p[1])].astype(F32)
    g_col = each(lambda p: bb_col[p] + m_in[p])
    m_t = each(lambda p: jnp.maximum(g_col[p], r[p]))
    e_loc = each(lambda p: jnp.exp(r[p] - m_t[p]))
    inter = each(lambda p: jnp.exp(g_col[p] - m_t[p]))
    q_n = each(lambda p: jnp.sum(q_of(p).astype(F32) * n_in[p], axis=-1, keepdims=True))
    den = each(lambda p: e_loc[p] * rs[p] + inter[p] * q_n[p])
    scale = each(lambda p: 1.0 / jnp.maximum(jnp.abs(den[p]), jnp.exp(-m_t[p])))
    hg = each(lambda p: jax.nn.sigmoid(tile_of(om_ref, p))
              * ((e_loc[p] * sv[p] + inter[p] * q_c[p]) * scale[p]))
    mu = each(lambda p: jnp.mean(hg[p], axis=-1, keepdims=True))
    cen = each(lambda p: hg[p] - mu[p])
    var = each(lambda p: jnp.mean(jnp.square(cen[p]), axis=-1, keepdims=True))
    for p in probs:
        sl = _head(p[1])
        y = cen[p] * lax.rsqrt(var[p] + NORM_EPS) * mg_ref[:, sl]
        y = (y + sk_ref[:, sl] * tile_of(xcv_ref, p)) * _silu(tile_of(zm_ref, p))
        ym_ref[rows(p[0]), sl] = y.astype(BF16)
        if p[1] == N_HEADS - 1:
            r_c = rows(p[0])
            o_ref[0, r_c, :] = _rms(part_ref[r_c, :] + _dot(ym_ref[r_c, :], wo_ref[0:GROUP_WIDTH, :]), fg_ref[...])


def _mlstm_out_call(q, k, v, gates, om, zm, xcv, mnorm_g, skip_m, x, yd, yc, w_out, final_g):
    b, s, _ = q.shape
    rows = MLSTM_STEP_CHUNKS * MLSTM_CHUNK
    tok_spec = lambda w: pl.BlockSpec((1, rows, w), lambda i, j: (i, j, 0))
    const = lambda shape: pl.BlockSpec(shape, lambda i, j: (0,) * len(shape))
    return pl.pallas_call(
        _mlstm_kernel,
        grid=(b, s // rows),
        in_specs=[tok_spec(GROUP_WIDTH), tok_spec(GROUP_WIDTH), tok_spec(GROUP_WIDTH), tok_spec(HEAD_DIM),
                  tok_spec(GROUP_WIDTH), tok_spec(GROUP_WIDTH), tok_spec(GROUP_WIDTH),
                  const((1, GROUP_WIDTH)), const((1, GROUP_WIDTH)),
                  tok_spec(D_MODEL), tok_spec(GROUP_WIDTH), tok_spec(GROUP_WIDTH),
                  const((3 * GROUP_WIDTH, D_MODEL)), const((1, D_MODEL))],
        out_specs=tok_spec(D_MODEL),
        out_shape=jax.ShapeDtypeStruct((b, s, D_MODEL), F32),
        scratch_shapes=[pltpu.VMEM((N_HEADS, HEAD_DIM, HEAD_DIM), F32),
                        pltpu.VMEM((N_HEADS, 1, HEAD_DIM), F32),
                        pltpu.VMEM((N_HEADS, 1, HEAD_DIM), F32),
                        pltpu.VMEM((rows, GROUP_WIDTH), BF16),
                        pltpu.VMEM((rows, D_MODEL), F32)],
        compiler_params=pltpu.CompilerParams(
            dimension_semantics=("arbitrary", "arbitrary"),
            vmem_limit_bytes=40 * 1024 * 1024),
        name="mlstm_out",
    )(q, k, v, gates, om, zm, xcv, mnorm_g, skip_m, x, yd, yc, w_out, final_g)


def _diff_kernel(q_ref, k_ref, vt_ref, zd_ref, g_ref, lq1_ref, lk1_ref, lq2_ref, lk2_ref, yd_ref,
                 qs_ref, m_ref, l_ref, acc_ref, s0_ref, s1_ref, mx0_ref, mx1_ref):
    tq, tk, tv = DIFF_Q_ROWS, DIFF_KV_ROWS, PROJ_ROWS
    i = pl.program_id(1)
    lane = lax.broadcasted_iota(jnp.int32, (tq, HEAD_DIM), 1)
    for hh in range(N_HEADS):
        q = q_ref[0, :, _head(hh)]
        zero = jnp.zeros_like(q)
        qs_ref[hh, 0:tq, :] = jnp.where(lane < DIFF_QK_DIM, q, zero)
        qs_ref[hh, tq:2 * tq, :] = jnp.where(lane >= DIFF_QK_DIM, q, zero)
    m_ref[...] = jnp.full_like(m_ref, -jnp.inf)
    l_ref[...] = jnp.zeros_like(l_ref)
    acc_ref[...] = jnp.zeros_like(acc_ref)

    def scores(t, s_ref, mx_ref, masked):
        rows = pl.ds(pl.multiple_of(t * tk, tk), tk)
        if masked:
            key_pos = t * tk + lax.broadcasted_iota(jnp.int32, (tk, 2 * tq), 0)
            q_pos = i * tq + (lax.broadcasted_iota(jnp.int32, (tk, 2 * tq), 1) & (tq - 1))
            visible = key_pos <= q_pos
        for hh in range(N_HEADS):
            s = _dot_nt(k_ref[0, rows, _head(hh)], qs_ref[hh])
            if masked:
                s = jnp.where(visible, s, -jnp.inf)
            s_ref[hh] = s
            mx_ref[hh] = jnp.max(s, axis=0, keepdims=True)

    def consume(t, s_ref, mx_ref):
        for hh in range(N_HEADS):
            sl = _head(hh)
            m_old = m_ref[hh]
            m_new = jnp.maximum(m_old, mx_ref[hh])
            alpha = jnp.exp2(m_old - m_new)
            p = jnp.exp2(s_ref[hh] - m_new)
            l_ref[hh] = alpha * l_ref[hh] + jnp.sum(p, axis=0, keepdims=True)
            pb = p.astype(BF16)
            pv = _dot(vt_ref[0, t * (tk // tv), sl, :], pb[0:tv])
            for c in range(1, tk // tv):
                pv = pv + _dot(vt_ref[0, t * (tk // tv) + c, sl, :], pb[c * tv:(c + 1) * tv])
            acc_ref[hh] = alpha * acc_ref[hh] + pv
            m_ref[hh] = m_new

    def pair(u, carry):
        consume(2 * u, s0_ref, mx0_ref)
        scores(2 * u + 1, s1_ref, mx1_ref, masked=False)
        consume(2 * u + 1, s1_ref, mx1_ref)
        scores(2 * u + 2, s0_ref, mx0_ref, masked=True)
        return carry

    scores(0, s0_ref, mx0_ref, masked=True)
    lax.fori_loop(0, i // 2, pair, 0)

    @pl.when(i % 2 == 0)
    def _():
        consume(i, s0_ref, mx0_ref)

    @pl.when(i % 2 == 1)
    def _():
        consume(i - 1, s0_ref, mx0_ref)
        scores(i, s1_ref, mx1_ref, masked=True)
        consume(i, s1_ref, mx1_ref)

    lam = (jnp.exp(jnp.sum(lq1_ref[...] * lk1_ref[...], axis=-1, keepdims=True))
           - jnp.exp(jnp.sum(lq2_ref[...] * lk2_ref[...], axis=-1, keepdims=True)) + LAM_INIT)
    for hh in range(N_HEADS):
        sl = _head(hh)
        acc = acc_ref[hh]
        inv = 1.0 / l_ref[hh]
        o_t = acc[:, :tq] * inv[:, :tq] - lam * (acc[:, tq:] * inv[:, tq:])
        o = _rms(o_t.T, g_ref[:, sl]) * (1.0 - LAM_INIT)
        yd_ref[0, :, sl] = (o * _silu(zd_ref[0, :, sl].astype(F32))).astype(BF16)


def _diff_call(qd, kd, vdt, zd, dnorm_g, lq1, lk1, lq2, lk2):
    b, s, _ = qd.shape
    tq = DIFF_Q_ROWS
    nkv = vdt.shape[1]
    lam_spec = pl.BlockSpec((1, DIFF_QK_DIM), lambda i, j: (0, 0))
    tok_spec = pl.BlockSpec((1, tq, GROUP_WIDTH), lambda i, j: (i, j, 0))
    return pl.pallas_call(
        _diff_kernel,
        grid=(b, s // tq),
        in_specs=[
            tok_spec,
            pl.BlockSpec((1, s, GROUP_WIDTH), lambda i, j: (i, 0, 0)),
            pl.BlockSpec((1, nkv, GROUP_WIDTH, PROJ_ROWS), lambda i, j: (i, 0, 0, 0)),
            tok_spec,
            pl.BlockSpec((1, GROUP_WIDTH), lambda i, j: (0, 0)),
            lam_spec, lam_spec, lam_spec, lam_spec,
        ],
        out_specs=tok_spec,
        out_shape=jax.ShapeDtypeStruct((b, s, GROUP_WIDTH), BF16),
        scratch_shapes=[pltpu.VMEM((N_HEADS, 2 * tq, HEAD_DIM), BF16),
                        pltpu.VMEM((N_HEADS, 1, 2 * tq), F32),
                        pltpu.VMEM((N_HEADS, 1, 2 * tq), F32),
                        pltpu.VMEM((N_HEADS, HEAD_DIM, 2 * tq), F32),
                        pltpu.VMEM((N_HEADS, DIFF_KV_ROWS, 2 * tq), F32),
                        pltpu.VMEM((N_HEADS, DIFF_KV_ROWS, 2 * tq), F32),
                        pltpu.VMEM((N_HEADS, 1, 2 * tq), F32),
                        pltpu.VMEM((N_HEADS, 1, 2 * tq), F32)],
        compiler_params=pltpu.CompilerParams(
            dimension_semantics=("arbitrary", "arbitrary"),
            vmem_limit_bytes=52 * 1024 * 1024),
        name="diffattn",
    )(qd, kd, vdt, zd, dnorm_g, lq1, lk1, lq2, lk2)


def kernel(x, mem, norm_g, w_in, conv_w, conv_b, wq_m, wk_m, wv_m, w_if, b_if, mnorm_g, skip_m,
           lam_q1, lam_k1, lam_q2, lam_k2, dnorm_g, mem_norm_g, w_mem_kv, w_out, final_g):
    b, s, d = x.shape
    assert (d, s % DIFF_KV_ROWS, DIFF_KV_ROWS % PROJ_ROWS) == (D_MODEL, 0, 0)
    assert s % (MLSTM_STEP_CHUNKS * MLSTM_CHUNK) == 0
    assert DIFF_Q_ROWS == DIFF_KV_ROWS
    assert norm_g.shape[0] == 1, "single-layer kernel"
    l = 0
    w_if_pad = jnp.pad(w_if[l], ((0, 0), (0, HEAD_DIM - w_if.shape[-1]))).astype(BF16)
    b_if_pad = jnp.pad(b_if[l], (0, HEAD_DIM - b_if.shape[-1]))[None, :]

    kv = _memkv_call(mem, mem_norm_g[l][None, :], w_mem_kv[l].astype(BF16))
    (xcv, qm, km, vm, gates, om, zm, qd, kd, vdt, zd, yc) = _proj_call(
        x, kv, norm_g[l][None, :], w_in[l].astype(BF16), conv_w[l], conv_b[l][None, :],
        jnp.concatenate([wq_m[l], wk_m[l]], axis=-1).astype(BF16), wv_m[l].astype(BF16), w_if_pad, b_if_pad)
    yd = _diff_call(qd, kd, vdt, zd, dnorm_g[l][None, :],
                    lam_q1[l][None, :], lam_k1[l][None, :], lam_q2[l][None, :], lam_k2[l][None, :])
    return _mlstm_out_call(qm, km, vm, gates, om, zm, xcv, mnorm_g[l][None, :], skip_m[l][None, :],
                           x, yd, yc, w_out[l].astype(BF16), final_g[None, :])
```

```python
import functools
import math

import jax
import jax.numpy as jnp
from jax import lax
from jax.experimental import pallas as pl
from jax.experimental.pallas import tpu as pltpu

F32 = jnp.float32
BF16 = jnp.bfloat16

D_MODEL = 1024
N_HEADS = 4
HEAD_DIM = 128
GROUP_WIDTH = N_HEADS * HEAD_DIM
DIFF_QK_DIM = 64
CONV_WIDTH = 4
MLSTM_CHUNK = 128
MLSTM_STEP_CHUNKS = 4
MEM_LEN = 256
IN_WIDTH = 9 * GROUP_WIDTH
NORM_EPS = 1e-6
LAM_INIT = 0.8 - 0.6 * math.exp(-0.3 * 0)
LOG2E = math.log2(math.e)

OFF_XM, OFF_OM, OFF_ZM, OFF_QD, OFF_KD, OFF_VD, OFF_ZD, OFF_QC, OFF_ZC = (
    i * GROUP_WIDTH for i in range(9))

PROJ_ROWS = 256
DIFF_Q_ROWS = 512
DIFF_KV_ROWS = 512
DIFF_KEY_CHUNK = 128
CONV_HALO = 8

NT_DIMS = (((1,), (1,)), ((), ()))
TN_DIMS = (((0,), (0,)), ((), ()))


def _dot(a, b):
    return jnp.dot(a, b, preferred_element_type=F32)


def _dot_nt(a, b):
    return lax.dot_general(a, b, NT_DIMS, preferred_element_type=F32)


def _silu(x):
    return x * jax.nn.sigmoid(x)


def _rms(x, g):
    return x * lax.rsqrt(jnp.mean(x * x, axis=-1, keepdims=True) + NORM_EPS) * g


def _head(h):
    return slice(h * HEAD_DIM, (h + 1) * HEAD_DIM)


def _memkv_kernel(mem_ref, g_ref, w_ref, kv_ref):
    y = _rms(mem_ref[0], g_ref[...])
    kv_ref[0] = _dot(y.astype(BF16), w_ref[...]).astype(BF16)


def _memkv_call(mem, g, w):
    b = mem.shape[0]
    return pl.pallas_call(
        _memkv_kernel,
        grid=(b,),
        in_specs=[
            pl.BlockSpec((1, MEM_LEN, D_MODEL), lambda i: (i, 0, 0)),
            pl.BlockSpec((1, D_MODEL), lambda i: (0, 0)),
            pl.BlockSpec((D_MODEL, 2 * GROUP_WIDTH), lambda i: (0, 0)),
        ],
        out_specs=pl.BlockSpec((1, MEM_LEN, 2 * GROUP_WIDTH), lambda i: (i, 0, 0)),
        out_shape=jax.ShapeDtypeStruct((b, MEM_LEN, 2 * GROUP_WIDTH), BF16),
        compiler_params=pltpu.CompilerParams(dimension_semantics=("arbitrary",)),
        name="memkv",
    )(mem, g, w)


def _proj_kernel(x_ref, kv_ref, ng_ref, win_ref, cw_ref, cb_ref, wqk_ref, wv_ref, wif_ref, bif_ref,
                 xcv_ref, qm_ref, km_ref, vm_ref, gates_ref, om_ref, zm_ref,
                 qd_ref, kd_ref, vdt_ref, zd_ref, yc_ref, conv_ref):
    tm = PROJ_ROWS

    @pl.when(pl.program_id(1) == 0)
    def _():
        conv_ref[tm:tm + CONV_HALO, :] = jnp.zeros((CONV_HALO, GROUP_WIDTH), F32)

    h = _rms(x_ref[0], ng_ref[...]).astype(BF16)

    def proj(off):
        return _dot(h, win_ref[:, off:off + GROUP_WIDTH])

    heads = range(N_HEADS)
    x_m = proj(OFF_XM)
    q_cb = proj(OFF_QC).astype(BF16)
    z_c = proj(OFF_ZC)
    om_f = proj(OFF_OM)

    conv_ref[0:CONV_HALO, :] = conv_ref[tm:tm + CONV_HALO, :]
    conv_ref[CONV_HALO:CONV_HALO + tm, :] = x_m
    acc = jnp.broadcast_to(cb_ref[...], (tm, GROUP_WIDTH))
    for j in range(CONV_WIDTH):
        start = CONV_HALO - (CONV_WIDTH - 1) + j
        acc = acc + cw_ref[j:j + 1, :] * conv_ref[start:start + tm, :]
    sc = [_dot_nt(q_cb[:, _head(hh)], kv_ref[0, :, _head(hh)]) * HEAD_DIM ** -0.5 for hh in heads]
    zm_f = proj(OFF_ZM)
    x_cv = _silu(acc)
    xcv_b = x_cv.astype(BF16)
    xcv_ref[0] = xcv_b
    xm_b = x_m.astype(BF16)
    om_ref[0] = om_f.astype(BF16)

    qk = [_dot(xcv_b[:, _head(hh)], wqk_ref[hh]) for hh in heads]
    vv = [_dot(xm_b[:, _head(hh)], wv_ref[hh]) for hh in heads]
    qd_f = proj(OFF_QD)
    kd_f = proj(OFF_KD)

    mx = [jnp.max(sc[hh], axis=-1, keepdims=True) for hh in heads]
    p = [jnp.exp(sc[hh] - mx[hh]) for hh in heads]
    inv = [1.0 / jnp.sum(p[hh], axis=-1, keepdims=True) for hh in heads]
    pc = [(p[hh] * inv[hh]).astype(BF16) for hh in heads]
    zm_ref[0] = zm_f.astype(BF16)
    qd_ref[0] = (qd_f * (DIFF_QK_DIM ** -0.5 * LOG2E)).astype(BF16)

    q_all = jnp.concatenate([qk[hh][:, :HEAD_DIM] for hh in heads], axis=1).astype(BF16)
    k_all = jnp.concatenate([qk[hh][:, HEAD_DIM:] for hh in heads], axis=1)
    v_all = jnp.concatenate(vv, axis=1).astype(BF16)
    oc = [_dot(pc[hh], kv_ref[0, :, GROUP_WIDTH + hh * HEAD_DIM:GROUP_WIDTH + (hh + 1) * HEAD_DIM]) for hh in heads]
    vd_f = proj(OFF_VD)
    zd_f = proj(OFF_ZD)
    qm_ref[0] = q_all
    km_ref[0] = (k_all * HEAD_DIM ** -0.5).astype(BF16)
    vm_ref[0] = v_all
    gates_ref[0] = (jnp.broadcast_to(bif_ref[...], (tm, HEAD_DIM))
                    + _dot(q_all, wif_ref[0:GROUP_WIDTH, :])
                    + _dot(k_all.astype(BF16), wif_ref[GROUP_WIDTH:2 * GROUP_WIDTH, :])
                    + _dot(v_all, wif_ref[2 * GROUP_WIDTH:3 * GROUP_WIDTH, :]))
    kd_ref[0] = kd_f.astype(BF16)
    for hh in heads:
        yc_ref[0, :, _head(hh)] = (oc[hh] * _silu(z_c[:, _head(hh)])).astype(BF16)
    vdt_ref[0, 0] = vd_f.T.astype(BF16)
    zd_ref[0] = zd_f.astype(BF16)


def _proj_call(x, kv, norm_g, w_in, conv_w, conv_b, wqk, wv, w_if, b_if):
    b, s, _ = x.shape
    tm = PROJ_ROWS
    ns = s // tm
    tok = lambda w, dt: jax.ShapeDtypeStruct((b, s, w), dt)
    tok_spec = lambda w: pl.BlockSpec((1, tm, w), lambda i, j: (i, j, 0))
    const = lambda shape: pl.BlockSpec(shape, lambda i, j: (0,) * len(shape))
    out_shape = (
        tok(GROUP_WIDTH, BF16),
        tok(GROUP_WIDTH, BF16), tok(GROUP_WIDTH, BF16), tok(GROUP_WIDTH, BF16),
        tok(HEAD_DIM, F32),
        tok(GROUP_WIDTH, BF16), tok(GROUP_WIDTH, BF16),
        tok(GROUP_WIDTH, BF16), tok(GROUP_WIDTH, BF16),
        jax.ShapeDtypeStruct((b, ns, GROUP_WIDTH, tm), BF16),
        tok(GROUP_WIDTH, BF16),
        tok(GROUP_WIDTH, BF16),
    )
    out_specs = (
        tok_spec(GROUP_WIDTH), tok_spec(GROUP_WIDTH), tok_spec(GROUP_WIDTH), tok_spec(GROUP_WIDTH),
        tok_spec(HEAD_DIM), tok_spec(GROUP_WIDTH), tok_spec(GROUP_WIDTH),
        tok_spec(GROUP_WIDTH), tok_spec(GROUP_WIDTH),
        pl.BlockSpec((1, 1, GROUP_WIDTH, tm), lambda i, j: (i, j, 0, 0)),
        tok_spec(GROUP_WIDTH), tok_spec(GROUP_WIDTH),
    )
    in_specs = [
        pl.BlockSpec((1, tm, D_MODEL), lambda i, j: (i, j, 0)),
        pl.BlockSpec((1, MEM_LEN, 2 * GROUP_WIDTH), lambda i, j: (i, 0, 0)),
        const((1, D_MODEL)),
        const((D_MODEL, IN_WIDTH)),
        const((CONV_WIDTH, GROUP_WIDTH)),
        const((1, GROUP_WIDTH)),
        const((N_HEADS, HEAD_DIM, 2 * HEAD_DIM)),
        const((N_HEADS, HEAD_DIM, HEAD_DIM)),
        const((3 * GROUP_WIDTH, HEAD_DIM)),
        const((1, HEAD_DIM)),
    ]
    return pl.pallas_call(
        _proj_kernel,
        grid=(b, ns),
        in_specs=in_specs,
        out_specs=out_specs,
        out_shape=out_shape,
        scratch_shapes=[pltpu.VMEM((tm + CONV_HALO, GROUP_WIDTH), F32)],
        compiler_params=pltpu.CompilerParams(
            dimension_semantics=("arbitrary", "arbitrary"),
            vmem_limit_bytes=56 * 1024 * 1024),
        name="inproj",
    )(x, kv, norm_g, w_in, conv_w, conv_b, wqk, wv, w_if, b_if)


def _log_sigmoid(x):
    return jnp.minimum(x, 0.0) - jnp.log1p(jnp.exp(-jnp.abs(x)))


def _cumsum_rows(f, tril):
    hi = f.astype(BF16)
    r1 = f - hi.astype(F32)
    mid = r1.astype(BF16)
    lo = (r1 - mid.astype(F32)).astype(BF16)
    return _dot(tril, hi) + _dot(tril, mid) + _dot(tril, lo)


def _mlstm_kernel(q_ref, k_ref, v_ref, g_ref, om_ref, zm_ref, xcv_ref, mg_ref, sk_ref,
                  x_ref, yd_ref, yc_ref, wo_ref, fg_ref,
                  o_ref, c_ref, n_ref, m_ref, ym_ref, part_ref):
    L = MLSTM_CHUNK

    def project_piece(c, first):
        r_c = slice(c * L, (c + 1) * L)
        if first:
            part_ref[r_c, :] = x_ref[0, r_c, :] + _dot(yd_ref[0, r_c, :], wo_ref[GROUP_WIDTH:2 * GROUP_WIDTH, :])
        else:
            part_ref[r_c, :] += _dot(yc_ref[0, r_c, :], wo_ref[2 * GROUP_WIDTH:3 * GROUP_WIDTH, :])

    side_pieces = [(c, first) for c in range(MLSTM_STEP_CHUNKS) for first in (True, False)]

    def side_work():
        if side_pieces:
            project_piece(*side_pieces.pop(0))

    @pl.when(pl.program_id(1) == 0)
    def _():
        c_ref[...] = jnp.zeros_like(c_ref)
        n_ref[...] = jnp.zeros_like(n_ref)
        m_ref[...] = jnp.zeros_like(m_ref)

    row = lax.broadcasted_iota(jnp.int32, (L, L), 0)
    col = lax.broadcasted_iota(jnp.int32, (L, L), 1)
    causal = col <= row
    tril = jnp.where(causal, 1.0, 0.0).astype(BF16)

    chunks = range(MLSTM_STEP_CHUNKS)
    heads = range(N_HEADS)
    probs = [(c, hh) for c in chunks for hh in heads]
    rows = lambda c: slice(c * L, (c + 1) * L)
    q_of = lambda p: q_ref[0, rows(p[0]), _head(p[1])]
    k_of = lambda p: k_ref[0, rows(p[0]), _head(p[1])]
    v_of = lambda p: v_ref[0, rows(p[0]), _head(p[1])]

    a_cols, a_rows = [], []
    for c in chunks:
        g = g_ref[0, rows(c), :]
        bb_all = _cumsum_rows(_log_sigmoid(g), tril)
        ac = jnp.where(col < N_HEADS, g, bb_all)
        a_cols.append(ac)
        a_rows.append(ac.T)
    ib_col = {(c, hh): a_cols[c][:, hh:hh + 1] for c, hh in probs}
    bb_col = {(c, hh): a_cols[c][:, N_HEADS + hh:N_HEADS + hh + 1] for c, hh in probs}
    ib_row = {(c, hh): a_rows[c][hh:hh + 1, :] for c, hh in probs}
    bb_row = {(c, hh): a_rows[c][N_HEADS + hh:N_HEADS + hh + 1, :] for c, hh in probs}
    b_end = {p: bb_col[p][L - 1:L, :] for p in probs}

    qk = {p: _dot_nt(q_of(p), k_of(p)) for p in probs}

    def each(f):
        out = {p: f(p) for p in probs}
        side_work()
        return out

    d = each(lambda p: jnp.where(causal, (bb_col[p] - bb_row[p]) + ib_row[p], -jnp.inf))
    r = each(lambda p: jnp.max(d[p], axis=-1, keepdims=True))
    s_loc = each(lambda p: qk[p] * jnp.exp(d[p] - r[p]))
    rs = each(lambda p: jnp.sum(s_loc[p], axis=-1, keepdims=True))
    s_bf = each(lambda p: s_loc[p].astype(BF16))
    a = each(lambda p: (b_end[p] - bb_col[p]) + ib_col[p])
    a_max = each(lambda p: jnp.max(a[p], axis=0, keepdims=True))
    w_loc = each(lambda p: jnp.exp(a[p] - a_max[p]))
    vw = each(lambda p: (v_of(p).astype(F32) * w_loc[p]).astype(BF16))
    nu = each(lambda p: jnp.sum(k_of(p).astype(F32) * w_loc[p], axis=0, keepdims=True))

    sv = {p: _dot(s_bf[p], v_of(p)) for p in probs}
    u = {p: lax.dot_general(vw[p], k_of(p), TN_DIMS, preferred_element_type=F32) for p in probs}

    m_in, c_in, n_in = {}, {}, {}
    for hh in heads:
        m_cur = m_ref[hh][:, 0:1]
        c_cur = c_ref[hh]
        n_cur = n_ref[hh]
        for c in chunks:
            p = (c, hh)
            m_in[p], c_in[p], n_in[p] = m_cur, c_cur.astype(BF16), n_cur
            m_new = jnp.maximum(b_end[p] + m_cur, a_max[p])
            decay = jnp.exp(b_end[p] + m_cur - m_new)
            e_upd = jnp.exp(a_max[p] - m_new)
            c_cur = decay * c_cur + e_upd * u[p]
            n_cur = decay * n_cur + e_upd * nu[p]
            m_cur = m_new
        c_ref[hh] = c_cur
        n_ref[hh] = n_cur
        m_ref[hh] = jnp.broadcast_to(m_cur, (1, HEAD_DIM))

    q_c = {p: _dot_nt(q_of(p), c_in[p]) for p in probs}

    tile_of = lambda ref, p: ref[0, rows(p[0]), _head(p[1])].astype(F32)
    g_col = each(lambda p: bb_col[p] + m_in[p])
    m_t = each(lambda p: jnp.maximum(g_col[p], r[p]))
    e_loc = each(lambda p: jnp.exp(r[p] - m_t[p]))
    inter = each(lambda p: jnp.exp(g_col[p] - m_t[p]))
    q_n = each(lambda p: jnp.sum(q_of(p).astype(F32) * n_in[p], axis=-1, keepdims=True))
    den = each(lambda p: e_loc[p] * rs[p] + inter[p] * q_n[p])
    scale = each(lambda p: 1.0 / jnp.maximum(jnp.abs(den[p]), jnp.exp(-m_t[p])))
    hg = each(lambda p: jax.nn.sigmoid(tile_of(om_ref, p))
              * ((e_loc[p] * sv[p] + inter[p] * q_c[p]) * scale[p]))
    mu = each(lambda p: jnp.mean(hg[p], axis=-1, keepdims=True))
    cen = each(lambda p: hg[p] - mu[p])
    var = each(lambda p: jnp.mean(jnp.square(cen[p]), axis=-1, keepdims=True))
    for p in probs:
        sl = _head(p[1])
        y = cen[p] * lax.rsqrt(var[p] + NORM_EPS) * mg_ref[:, sl]
        y = (y + sk_ref[:, sl] * tile_of(xcv_ref, p)) * _silu(tile_of(zm_ref, p))
        ym_ref[rows(p[0]), sl] = y.astype(BF16)
        if p[1] == N_HEADS - 1:
            r_c = rows(p[0])
            o_ref[0, r_c, :] = _rms(part_ref[r_c, :] + _dot(ym_ref[r_c, :], wo_ref[0:GROUP_WIDTH, :]), fg_ref[...])


def _mlstm_out_call(q, k, v, gates, om, zm, xcv, mnorm_g, skip_m, x, yd, yc, w_out, final_g):
    b, s, _ = q.shape
    rows = MLSTM_STEP_CHUNKS * MLSTM_CHUNK
    tok_spec = lambda w: pl.BlockSpec((1, rows, w), lambda i, j: (i, j, 0))
    const = lambda shape: pl.BlockSpec(shape, lambda i, j: (0,) * len(shape))
    return pl.pallas_call(
        _mlstm_kernel,
        grid=(b, s // rows),
        in_specs=[tok_spec(GROUP_WIDTH), tok_spec(GROUP_WIDTH), tok_spec(GROUP_WIDTH), tok_spec(HEAD_DIM),
                  tok_spec(GROUP_WIDTH), tok_spec(GROUP_WIDTH), tok_spec(GROUP_WIDTH),
                  const((1, GROUP_WIDTH)), const((1, GROUP_WIDTH)),
                  tok_spec(D_MODEL), tok_spec(GROUP_WIDTH), tok_spec(GROUP_WIDTH),
                  const((3 * GROUP_WIDTH, D_MODEL)), const((1, D_MODEL))],
        out_specs=tok_spec(D_MODEL),
        out_shape=jax.ShapeDtypeStruct((b, s, D_MODEL), F32),
        scratch_shapes=[pltpu.VMEM((N_HEADS, HEAD_DIM, HEAD_DIM), F32),
                        pltpu.VMEM((N_HEADS, 1, HEAD_DIM), F32),
                        pltpu.VMEM((N_HEADS, 1, HEAD_DIM), F32),
                        pltpu.VMEM((rows, GROUP_WIDTH), BF16),
                        pltpu.VMEM((rows, D_MODEL), F32)],
        compiler_params=pltpu.CompilerParams(
            dimension_semantics=("arbitrary", "arbitrary"),
            vmem_limit_bytes=40 * 1024 * 1024),
        name="mlstm_out",
    )(q, k, v, gates, om, zm, xcv, mnorm_g, skip_m, x, yd, yc, w_out, final_g)


def _diff_kernel(q_ref, k_ref, vt_ref, zd_ref, g_ref, lq1_ref, lk1_ref, lq2_ref, lk2_ref, yd_ref,
                 qs_ref, m_ref, l_ref, acc_ref, s0_ref, s1_ref, mx0_ref, mx1_ref):
    tq, tk, tv = DIFF_Q_ROWS, DIFF_KV_ROWS, PROJ_ROWS
    i = pl.program_id(1)
    lane = lax.broadcasted_iota(jnp.int32, (tq, HEAD_DIM), 1)
    for hh in range(N_HEADS):
        q = q_ref[0, :, _head(hh)]
        zero = jnp.zeros_like(q)
        qs_ref[hh, 0:tq, :] = jnp.where(lane < DIFF_QK_DIM, q, zero)
        qs_ref[hh, tq:2 * tq, :] = jnp.where(lane >= DIFF_QK_DIM, q, zero)
    m_ref[...] = jnp.full_like(m_ref, -jnp.inf)
    l_ref[...] = jnp.zeros_like(l_ref)
    acc_ref[...] = jnp.zeros_like(acc_ref)

    kc = DIFF_KEY_CHUNK
    sub = HEAD_DIM
    tri = (lax.broadcasted_iota(jnp.int32, (kc, sub), 0) <= lax.broadcasted_iota(jnp.int32, (kc, sub), 1))

    def mask_diagonal(s_c, c):
        blocks, maxes = [], []
        for b8 in range(2 * tq // sub):
            b = b8 % (tq // sub)
            blk = s_c[:, b8 * sub:(b8 + 1) * sub]
            if c > b:
                blocks.append(jnp.full((kc, sub), -jnp.inf, F32))
                maxes.append(jnp.full((1, sub), -jnp.inf, F32))
                continue
            if c == b:
                blk = jnp.where(tri, blk, -jnp.inf)
            blocks.append(blk)
            maxes.append(jnp.max(blk, axis=0, keepdims=True))
        return jnp.concatenate(blocks, axis=1), jnp.concatenate(maxes, axis=1)

    group_a = (tuple(range(0, N_HEADS // 2)), s0_ref, mx0_ref)
    group_b = (tuple(range(N_HEADS // 2, N_HEADS)), s1_ref, mx1_ref)

    def step(consume=None, produce=None):
        n_slots = N_HEADS // 2
        for slot in range(n_slots):
            if consume is not None:
                t_c, (heads_c, sc_ref, mxc_ref) = consume
                h_c = heads_c[slot]
                m_old = m_ref[h_c]
                m_new = jnp.maximum(m_old, mxc_ref[slot])
                alpha = jnp.exp2(m_old - m_new)
                l_sum, pv, pb_prev = None, None, None
            if produce is not None:
                t_p, (heads_p, sp_ref, mxp_ref), diagonal = produce
                h_p = heads_p[slot]
                mx = None
            for c in range(tk // kc):
                chunk = slice(c * kc, (c + 1) * kc)
                if produce is not None:
                    rows = pl.ds(pl.multiple_of(t_p * tk + c * kc, kc), kc)
                    s_c = _dot_nt(k_ref[0, rows, _head(h_p)], qs_ref[h_p])
                if consume is not None:
                    p_c = jnp.exp2(sc_ref[slot, chunk, :] - m_new)
                    p_sum = jnp.sum(p_c, axis=0, keepdims=True)
                    l_sum = p_sum if l_sum is None else l_sum + p_sum
                    pb_c = p_c.astype(BF16)
                    if (c + 1) % (tv // kc) == 0:
                        pb = pb_c if pb_prev is None else jnp.concatenate([pb_prev, pb_c], axis=0)
                        d = _dot(vt_ref[0, t_c * (tk // tv) + (c * kc) // tv, _head(h_c), :], pb)
                        pv = d if pv is None else pv + d
                        pb_prev = None
                    else:
                        pb_prev = pb_c if pb_prev is None else jnp.concatenate([pb_prev, pb_c], axis=0)
                if produce is not None:
                    if diagonal:
                        s_c, c_max = mask_diagonal(s_c, c)
                    else:
                        c_max = jnp.max(s_c, axis=0, keepdims=True)
                    sp_ref[slot, chunk, :] = s_c
                    mx = c_max if mx is None else jnp.maximum(mx, c_max)
            if consume is not None:
                l_ref[h_c] = alpha * l_ref[h_c] + l_sum
                acc_ref[h_c] = alpha * acc_ref[h_c] + pv
                m_ref[h_c] = m_new
            if produce is not None:
                mxp_ref[slot] = mx

    def tile(t, next_diagonal):
        step(consume=(t, group_a), produce=(t, group_b, False))
        step(consume=(t, group_b), produce=(t + 1, group_a, next_diagonal))

    def body(t, carry):
        tile(t, False)
        return carry

    @pl.when(i == 0)
    def _():
        step(produce=(0, group_a, True))

    @pl.when(i > 0)
    def _():
        step(produce=(0, group_a, False))
        lax.fori_loop(0, i - 1, body, 0)
        tile(i - 1, True)

    lam = (jnp.exp(jnp.sum(lq1_ref[...] * lk1_ref[...], axis=-1, keepdims=True))
           - jnp.exp(jnp.sum(lq2_ref[...] * lk2_ref[...], axis=-1, keepdims=True)) + LAM_INIT)

    def finish(heads):
        for hh in heads:
            sl = _head(hh)
            acc = acc_ref[hh]
            inv = 1.0 / l_ref[hh]
            o_t = acc[:, :tq] * inv[:, :tq] - lam * (acc[:, tq:] * inv[:, tq:])
            o = _rms(o_t.T, g_ref[:, sl]) * (1.0 - LAM_INIT)
            yd_ref[0, :, sl] = (o * _silu(zd_ref[0, :, sl].astype(F32))).astype(BF16)

    step(consume=(i, group_a), produce=(i, group_b, True))
    finish(group_a[0])
    step(consume=(i, group_b))
    finish(group_b[0])


def _diff_call(qd, kd, vdt, zd, dnorm_g, lq1, lk1, lq2, lk2):
    b, s, _ = qd.shape
    tq = DIFF_Q_ROWS
    nkv = vdt.shape[1]
    lam_spec = pl.BlockSpec((1, DIFF_QK_DIM), lambda i, j: (0, 0))
    tok_spec = pl.BlockSpec((1, tq, GROUP_WIDTH), lambda i, j: (i, j, 0))
    return pl.pallas_call(
        _diff_kernel,
        grid=(b, s // tq),
        in_specs=[
            tok_spec,
            pl.BlockSpec((1, s, GROUP_WIDTH), lambda i, j: (i, 0, 0)),
            pl.BlockSpec((1, nkv, GROUP_WIDTH, PROJ_ROWS), lambda i, j: (i, 0, 0, 0)),
            tok_spec,
            pl.BlockSpec((1, GROUP_WIDTH), lambda i, j: (0, 0)),
            lam_spec, lam_spec, lam_spec, lam_spec,
        ],
        out_specs=tok_spec,
        out_shape=jax.ShapeDtypeStruct((b, s, GROUP_WIDTH), BF16),
        scratch_shapes=[pltpu.VMEM((N_HEADS, 2 * tq, HEAD_DIM), BF16),
                        pltpu.VMEM((N_HEADS, 1, 2 * tq), F32),
                        pltpu.VMEM((N_HEADS, 1, 2 * tq), F32),
                        pltpu.VMEM((N_HEADS, HEAD_DIM, 2 * tq), F32),
                        pltpu.VMEM((N_HEADS // 2, DIFF_KV_ROWS, 2 * tq), F32),
                        pltpu.VMEM((N_HEADS // 2, DIFF_KV_ROWS, 2 * tq), F32),
                        pltpu.VMEM((N_HEADS // 2, 1, 2 * tq), F32),
                        pltpu.VMEM((N_HEADS // 2, 1, 2 * tq), F32)],
        compiler_params=pltpu.CompilerParams(
            dimension_semantics=("arbitrary", "arbitrary"),
            vmem_limit_bytes=52 * 1024 * 1024),
        name="diffattn",
    )(qd, kd, vdt, zd, dnorm_g, lq1, lk1, lq2, lk2)


def kernel(x, mem, norm_g, w_in, conv_w, conv_b, wq_m, wk_m, wv_m, w_if, b_if, mnorm_g, skip_m,
           lam_q1, lam_k1, lam_q2, lam_k2, dnorm_g, mem_norm_g, w_mem_kv, w_out, final_g):
    b, s, d = x.shape
    assert (d, s % DIFF_KV_ROWS, DIFF_KV_ROWS % PROJ_ROWS) == (D_MODEL, 0, 0)
    assert s % (MLSTM_STEP_CHUNKS * MLSTM_CHUNK) == 0
    assert DIFF_Q_ROWS == DIFF_KV_ROWS
    assert norm_g.shape[0] == 1, "single-layer kernel"
    l = 0
    w_if_pad = jnp.pad(w_if[l], ((0, 0), (0, HEAD_DIM - w_if.shape[-1]))).astype(BF16)
    b_if_pad = jnp.pad(b_if[l], (0, HEAD_DIM - b_if.shape[-1]))[None, :]

    kv = _memkv_call(mem, mem_norm_g[l][None, :], w_mem_kv[l].astype(BF16))
    (xcv, qm, km, vm, gates, om, zm, qd, kd, vdt, zd, yc) = _proj_call(
        x, kv, norm_g[l][None, :], w_in[l].astype(BF16), conv_w[l], conv_b[l][None, :],
        jnp.concatenate([wq_m[l], wk_m[l]], axis=-1).astype(BF16), wv_m[l].astype(BF16), w_if_pad, b_if_pad)
    yd = _diff_call(qd, kd, vdt, zd, dnorm_g[l][None, :],
                    lam_q1[l][None, :], lam_k1[l][None, :], lam_q2[l][None, :], lam_k2[l][None, :])
    return _mlstm_out_call(qm, km, vm, gates, om, zm, xcv, mnorm_g[l][None, :], skip_m[l][None, :],
                           x, yd, yc, w_out[l].astype(BF16), final_g[None, :])
```

```python
import math

import jax
import jax.numpy as jnp
from jax import lax
from jax.experimental import pallas as pl
from jax.experimental.pallas import tpu as pltpu

F32 = jnp.float32
BF16 = jnp.bfloat16

D_MODEL = 1024
N_HEADS = 4
HEAD_DIM = 128
SUBLANES = 8
GROUP_WIDTH = N_HEADS * HEAD_DIM
DIFF_QK_DIM = 64
CONV_WIDTH = 4
MLSTM_CHUNK = 128
MLSTM_STEP_CHUNKS = 4
MEM_LEN = 256
IN_WIDTH = 9 * GROUP_WIDTH
NORM_EPS = 1e-6
LAM_INIT = 0.8 - 0.6 * math.exp(-0.3 * 0)
LOG2E = math.log2(math.e)

OFF_XM, OFF_OM, OFF_ZM, OFF_QD, OFF_KD, OFF_VD, OFF_ZD, OFF_QC, OFF_ZC = (
    i * GROUP_WIDTH for i in range(9))

FRONT_ROWS = MLSTM_STEP_CHUNKS * MLSTM_CHUNK
V_TILE = 256
DIFF_Q_ROWS = 512
DIFF_KV_ROWS = 512
DIFF_KEY_CHUNK = 128
OUT_ROWS = 512
CONV_HALO = 8

NT_DIMS = (((1,), (1,)), ((), ()))
TN_DIMS = (((0,), (0,)), ((), ()))


def _dot(a, b):
    return jnp.dot(a, b, preferred_element_type=F32)


def _dot_nt(a, b):
    return lax.dot_general(a, b, NT_DIMS, preferred_element_type=F32)


def _silu(x):
    return x * jax.nn.sigmoid(x)


def _rms(x, g):
    return x * lax.rsqrt(jnp.mean(x * x, axis=-1, keepdims=True) + NORM_EPS) * g


def _head(h):
    return slice(h * HEAD_DIM, (h + 1) * HEAD_DIM)


def _fold_rows(x, op):
    r, n = x.shape
    return op(x.reshape(r // SUBLANES, SUBLANES, n), axis=0)


def _log_sigmoid(x):
    return jnp.minimum(x, 0.0) - jnp.log1p(jnp.exp(-jnp.abs(x)))


def _cumsum_rows(f, tril):
    hi = f.astype(BF16)
    r1 = f - hi.astype(F32)
    mid = r1.astype(BF16)
    lo = (r1 - mid.astype(F32)).astype(BF16)
    return _dot(tril, hi) + _dot(tril, mid) + _dot(tril, lo)


def _memkv_kernel(mem_ref, g_ref, w_ref, kv_ref):
    y = _rms(mem_ref[0], g_ref[...])
    kv_ref[0] = _dot(y.astype(BF16), w_ref[...]).astype(BF16)


def _memkv_call(mem, g, w):
    b = mem.shape[0]
    return pl.pallas_call(
        _memkv_kernel,
        grid=(b,),
        in_specs=[
            pl.BlockSpec((1, MEM_LEN, D_MODEL), lambda i: (i, 0, 0)),
            pl.BlockSpec((1, D_MODEL), lambda i: (0, 0)),
            pl.BlockSpec((D_MODEL, 2 * GROUP_WIDTH), lambda i: (0, 0)),
        ],
        out_specs=pl.BlockSpec((1, MEM_LEN, 2 * GROUP_WIDTH), lambda i: (i, 0, 0)),
        out_shape=jax.ShapeDtypeStruct((b, MEM_LEN, 2 * GROUP_WIDTH), BF16),
        compiler_params=pltpu.CompilerParams(dimension_semantics=("arbitrary",)),
        name="memkv",
    )(mem, g, w)


def _front_kernel(x_ref, kv_ref, ng_ref, win_ref, cw_ref, cb_ref, wqk_ref, wv_ref, wif_ref, bif_ref,
                  mg_ref, sk_ref,
                  ym_ref, qd_ref, kd_ref, vdt_ref, zd_ref, yc_ref,
                  conv_ref, c_ref, n_ref, m_ref, q_s, k_s, v_s, g_s, om_s, zm_s, xcv_s):
    tm = FRONT_ROWS
    L = MLSTM_CHUNK
    heads = range(N_HEADS)

    @pl.when(pl.program_id(1) == 0)
    def _():
        conv_ref[tm:tm + CONV_HALO, :] = jnp.zeros((CONV_HALO, GROUP_WIDTH), F32)
        c_ref[...] = jnp.zeros_like(c_ref)
        n_ref[...] = jnp.zeros_like(n_ref)
        m_ref[...] = jnp.zeros_like(m_ref)

    h = _rms(x_ref[0], ng_ref[...]).astype(BF16)

    def proj(off):
        return _dot(h, win_ref[:, off:off + GROUP_WIDTH])

    def put_vdt(val):
        v_t = val.T.astype(BF16)
        for j in range(tm // V_TILE):
            vdt_ref[0, j] = v_t[:, j * V_TILE:(j + 1) * V_TILE]

    def put(ref, scale=None):
        def sink(val):
            ref[...] = (val if scale is None else val * scale).astype(BF16).reshape(ref.shape)
        return sink

    side_pieces = [(OFF_OM, put(om_s)), (OFF_ZM, put(zm_s)),
                   (OFF_QD, put(qd_ref, DIFF_QK_DIM ** -0.5 * LOG2E)), (OFF_KD, put(kd_ref)),
                   (OFF_VD, put_vdt), (OFF_ZD, put(zd_ref))]
    pending = []

    def side_start():
        if side_pieces:
            off, sink = side_pieces.pop(0)
            pending.append((proj(off), sink))

    def side_finish():
        if pending:
            val, sink = pending.pop(0)
            sink(val)

    x_m = proj(OFF_XM)
    q_cb = proj(OFF_QC).astype(BF16)
    z_c = proj(OFF_ZC)

    conv_ref[0:CONV_HALO, :] = conv_ref[tm:tm + CONV_HALO, :]
    conv_ref[CONV_HALO:CONV_HALO + tm, :] = x_m
    acc = jnp.broadcast_to(cb_ref[...], (tm, GROUP_WIDTH))
    for j in range(CONV_WIDTH):
        start = CONV_HALO - (CONV_WIDTH - 1) + j
        acc = acc + cw_ref[j:j + 1, :] * conv_ref[start:start + tm, :]
    sc = [_dot_nt(q_cb[:, _head(hh)], kv_ref[0, :, _head(hh)]) * HEAD_DIM ** -0.5 for hh in heads]
    x_cv = _silu(acc)
    xcv_b = x_cv.astype(BF16)
    xcv_s[...] = xcv_b
    xm_b = x_m.astype(BF16)

    qk_p = [_dot(xcv_b[:, _head(hh)], wqk_ref[hh]) for hh in heads]
    vv_p = [_dot(xm_b[:, _head(hh)], wv_ref[hh]) for hh in heads]

    mx = [jnp.max(sc[hh], axis=-1, keepdims=True) for hh in heads]
    pm = [jnp.exp(sc[hh] - mx[hh]) for hh in heads]
    inv = [1.0 / jnp.sum(pm[hh], axis=-1, keepdims=True) for hh in heads]
    pc = [(pm[hh] * inv[hh]).astype(BF16) for hh in heads]

    q_all = jnp.concatenate([qk_p[hh][:, :HEAD_DIM] for hh in heads], axis=1).astype(BF16)
    k_all = jnp.concatenate([qk_p[hh][:, HEAD_DIM:] for hh in heads], axis=1)
    v_all = jnp.concatenate(vv_p, axis=1).astype(BF16)
    oc = [_dot(pc[hh], kv_ref[0, :, GROUP_WIDTH + hh * HEAD_DIM:GROUP_WIDTH + (hh + 1) * HEAD_DIM]) for hh in heads]
    q_s[...] = q_all
    k_s[...] = (k_all * HEAD_DIM ** -0.5).astype(BF16)
    v_s[...] = v_all
    g_s[...] = (jnp.broadcast_to(bif_ref[...], (tm, HEAD_DIM))
                + _dot(q_all, wif_ref[0:GROUP_WIDTH, :])
                + _dot(k_all.astype(BF16), wif_ref[GROUP_WIDTH:2 * GROUP_WIDTH, :])
                + _dot(v_all, wif_ref[2 * GROUP_WIDTH:3 * GROUP_WIDTH, :]))
    for hh in heads:
        yc_ref[0, :, _head(hh)] = (oc[hh] * _silu(z_c[:, _head(hh)])).astype(BF16)

    row = lax.broadcasted_iota(jnp.int32, (L, L), 0)
    col = lax.broadcasted_iota(jnp.int32, (L, L), 1)
    causal = col <= row
    tril = jnp.where(causal, 1.0, 0.0).astype(BF16)
    chunks = range(MLSTM_STEP_CHUNKS)
    probs = [(c, hh) for c in chunks for hh in heads]
    rows = lambda c: slice(c * L, (c + 1) * L)
    q_of = lambda p: q_s[rows(p[0]), _head(p[1])]
    k_of = lambda p: k_s[rows(p[0]), _head(p[1])]
    v_of = lambda p: v_s[rows(p[0]), _head(p[1])]

    def each(f, side=False):
        if side:
            side_start()
        out = {p: f(p) for p in probs}
        if side:
            side_finish()
        return out

    a_cols, a_rows = [], []
    for c in chunks:
        g = g_s[rows(c), :]
        bb_all = _cumsum_rows(_log_sigmoid(g), tril)
        ac = jnp.where(col < N_HEADS, g, bb_all)
        a_cols.append(ac)
        a_rows.append(ac.T)
    ib_col = {(c, hh): a_cols[c][:, hh:hh + 1] for c, hh in probs}
    bb_col = {(c, hh): a_cols[c][:, N_HEADS + hh:N_HEADS + hh + 1] for c, hh in probs}
    ib_row = {(c, hh): a_rows[c][hh:hh + 1, :] for c, hh in probs}
    bb_row = {(c, hh): a_rows[c][N_HEADS + hh:N_HEADS + hh + 1, :] for c, hh in probs}
    b_end = {p: bb_col[p][L - 1:L, :] for p in probs}

    qk = {p: _dot_nt(q_of(p), k_of(p)) for p in probs}

    d = each(lambda p: jnp.where(causal, (bb_col[p] - bb_row[p]) + ib_row[p], -jnp.inf), side=True)
    r = each(lambda p: jnp.max(d[p], axis=-1, keepdims=True))
    s_loc = each(lambda p: qk[p] * jnp.exp(d[p] - r[p]), side=True)
    rs = each(lambda p: jnp.sum(s_loc[p], axis=-1, keepdims=True))
    s_bf = each(lambda p: s_loc[p].astype(BF16))
    a = each(lambda p: (b_end[p] - bb_col[p]) + ib_col[p], side=True)
    a_max = each(lambda p: jnp.max(a[p], axis=0, keepdims=True))
    w_loc = each(lambda p: jnp.exp(a[p] - a_max[p]))
    vw = each(lambda p: (v_of(p).astype(F32) * w_loc[p]).astype(BF16), side=True)
    nu = each(lambda p: jnp.sum(k_of(p).astype(F32) * w_loc[p], axis=0, keepdims=True))

    sv = {p: _dot(s_bf[p], v_of(p)) for p in probs}
    u = {p: lax.dot_general(vw[p], k_of(p), TN_DIMS, preferred_element_type=F32) for p in probs}

    m_in, c_in, n_in = {}, {}, {}
    for hh in heads:
        m_cur = m_ref[hh][:, 0:1]
        c_cur = c_ref[hh]
        n_cur = n_ref[hh]
        for c in chunks:
            p = (c, hh)
            m_in[p], c_in[p], n_in[p] = m_cur, c_cur.astype(BF16), n_cur
            m_new = jnp.maximum(b_end[p] + m_cur, a_max[p])
            decay = jnp.exp(b_end[p] + m_cur - m_new)
            e_upd = jnp.exp(a_max[p] - m_new)
            c_cur = decay * c_cur + e_upd * u[p]
            n_cur = decay * n_cur + e_upd * nu[p]
            m_cur = m_new
        c_ref[hh] = c_cur
        n_ref[hh] = n_cur
        m_ref[hh] = jnp.broadcast_to(m_cur, (1, HEAD_DIM))

    q_c = {p: _dot_nt(q_of(p), c_in[p]) for p in probs}

    tile_of = lambda ref, p: ref[rows(p[0]), _head(p[1])].astype(F32)
    g_col = each(lambda p: bb_col[p] + m_in[p])
    m_t = each(lambda p: jnp.maximum(g_col[p], r[p]))
    e_loc = each(lambda p: jnp.exp(r[p] - m_t[p]))
    inter = each(lambda p: jnp.exp(g_col[p] - m_t[p]))
    q_n = each(lambda p: jnp.sum(q_of(p).astype(F32) * n_in[p], axis=-1, keepdims=True))
    den = each(lambda p: e_loc[p] * rs[p] + inter[p] * q_n[p])
    scale = each(lambda p: 1.0 / jnp.maximum(jnp.abs(den[p]), jnp.exp(-m_t[p])))
    hg = each(lambda p: jax.nn.sigmoid(tile_of(om_s, p))
              * ((e_loc[p] * sv[p] + inter[p] * q_c[p]) * scale[p]), side=True)
    mu = each(lambda p: jnp.mean(hg[p], axis=-1, keepdims=True))
    cen = each(lambda p: hg[p] - mu[p], side=True)
    var = each(lambda p: jnp.mean(jnp.square(cen[p]), axis=-1, keepdims=True))
    assert not side_pieces and not pending
    for p in probs:
        sl = _head(p[1])
        y = cen[p] * lax.rsqrt(var[p] + NORM_EPS) * mg_ref[:, sl]
        y = (y + sk_ref[:, sl] * tile_of(xcv_s, p)) * _silu(tile_of(zm_s, p))
        ym_ref[0, rows(p[0]), sl] = y.astype(BF16)


def _front_call(x, kv, norm_g, w_in, conv_w, conv_b, wqk, wv, w_if, b_if, mnorm_g, skip_m):
    b, s, _ = x.shape
    tm = FRONT_ROWS
    ns = s // tm
    tok = lambda w: jax.ShapeDtypeStruct((b, s, w), BF16)
    tok_spec = lambda w: pl.BlockSpec((1, tm, w), lambda i, j: (i, j, 0))
    const = lambda shape: pl.BlockSpec(shape, lambda i, j: (0,) * len(shape))
    vdt_tiles = tm // V_TILE
    out_shape = (
        tok(GROUP_WIDTH),
        tok(GROUP_WIDTH), tok(GROUP_WIDTH),
        jax.ShapeDtypeStruct((b, ns * vdt_tiles, GROUP_WIDTH, V_TILE), BF16),
        tok(GROUP_WIDTH),
        tok(GROUP_WIDTH),
    )
    out_specs = (
        tok_spec(GROUP_WIDTH), tok_spec(GROUP_WIDTH), tok_spec(GROUP_WIDTH),
        pl.BlockSpec((1, vdt_tiles, GROUP_WIDTH, V_TILE), lambda i, j: (i, j, 0, 0)),
        tok_spec(GROUP_WIDTH), tok_spec(GROUP_WIDTH),
    )
    in_specs = [
        pl.BlockSpec((1, tm, D_MODEL), lambda i, j: (i, j, 0)),
        pl.BlockSpec((1, MEM_LEN, 2 * GROUP_WIDTH), lambda i, j: (i, 0, 0)),
        const((1, D_MODEL)),
        const((D_MODEL, IN_WIDTH)),
        const((CONV_WIDTH, GROUP_WIDTH)),
        const((1, GROUP_WIDTH)),
        const((N_HEADS, HEAD_DIM, 2 * HEAD_DIM)),
        const((N_HEADS, HEAD_DIM, HEAD_DIM)),
        const((3 * GROUP_WIDTH, HEAD_DIM)),
        const((1, HEAD_DIM)),
        const((1, GROUP_WIDTH)),
        const((1, GROUP_WIDTH)),
    ]
    tile_bf16 = pltpu.VMEM((tm, GROUP_WIDTH), BF16)
    return pl.pallas_call(
        _front_kernel,
        grid=(b, ns),
        in_specs=in_specs,
        out_specs=out_specs,
        out_shape=out_shape,
        scratch_shapes=[pltpu.VMEM((tm + CONV_HALO, GROUP_WIDTH), F32),
                        pltpu.VMEM((N_HEADS, HEAD_DIM, HEAD_DIM), F32),
                        pltpu.VMEM((N_HEADS, 1, HEAD_DIM), F32),
                        pltpu.VMEM((N_HEADS, 1, HEAD_DIM), F32),
                        tile_bf16, tile_bf16, tile_bf16,
                        pltpu.VMEM((tm, HEAD_DIM), F32),
                        tile_bf16, tile_bf16, tile_bf16],
        compiler_params=pltpu.CompilerParams(
            dimension_semantics=("arbitrary", "arbitrary"),
            vmem_limit_bytes=56 * 1024 * 1024),
        name="front",
    )(x, kv, norm_g, w_in, conv_w, conv_b, wqk, wv, w_if, b_if, mnorm_g, skip_m)


def _diff_kernel(q_ref, k_ref, vt_ref, zd_ref, g_ref, lq1_ref, lk1_ref, lq2_ref, lk2_ref, yd_ref,
                 qs_ref, m_ref, l_ref, acc_ref, s0_ref, s1_ref, mx0_ref, mx1_ref):
    tq, tk, tv = DIFF_Q_ROWS, DIFF_KV_ROWS, V_TILE
    i = pl.program_id(1)
    lane = lax.broadcasted_iota(jnp.int32, (tq, HEAD_DIM), 1)
    for hh in range(N_HEADS):
        q = q_ref[0, :, _head(hh)]
        zero = jnp.zeros_like(q)
        qs_ref[hh, 0:tq, :] = jnp.where(lane < DIFF_QK_DIM, q, zero)
        qs_ref[hh, tq:2 * tq, :] = jnp.where(lane >= DIFF_QK_DIM, q, zero)
    m_ref[...] = jnp.full_like(m_ref, -jnp.inf)
    l_ref[...] = jnp.zeros_like(l_ref)
    acc_ref[...] = jnp.zeros_like(acc_ref)

    kc = DIFF_KEY_CHUNK
    sub = HEAD_DIM
    tri = (lax.broadcasted_iota(jnp.int32, (kc, sub), 0) <= lax.broadcasted_iota(jnp.int32, (kc, sub), 1))

    def mask_diagonal(s_c, c):
        blocks, maxes = [], []
        for b8 in range(2 * tq // sub):
            b = b8 % (tq // sub)
            blk = s_c[:, b8 * sub:(b8 + 1) * sub]
            if c > b:
                blocks.append(jnp.full((kc, sub), -jnp.inf, F32))
                maxes.append(jnp.full((SUBLANES, sub), -jnp.inf, F32))
                continue
            if c == b:
                blk = jnp.where(tri, blk, -jnp.inf)
            blocks.append(blk)
            maxes.append(_fold_rows(blk, jnp.max))
        return jnp.concatenate(blocks, axis=1), jnp.concatenate(maxes, axis=1)

    group_a = (tuple(range(0, N_HEADS // 2)), s0_ref, mx0_ref)
    group_b = (tuple(range(N_HEADS // 2, N_HEADS)), s1_ref, mx1_ref)

    def step(consume=None, produce=None):
        n_slots = N_HEADS // 2
        for slot in range(n_slots):
            if consume is not None:
                t_c, (heads_c, sc_ref, mxc_ref) = consume
                h_c = heads_c[slot]
                m_old = m_ref[h_c]
                m_new = jnp.maximum(m_old, mxc_ref[slot])
                alpha = jnp.exp2(m_old - m_new)
                l_sum, pv, pb_prev = None, None, None
            if produce is not None:
                t_p, (heads_p, sp_ref, mxp_ref), diagonal = produce
                h_p = heads_p[slot]
                mx = None
            for c in range(tk // kc):
                chunk = slice(c * kc, (c + 1) * kc)
                if produce is not None:
                    rows = pl.ds(pl.multiple_of(t_p * tk + c * kc, kc), kc)
                    s_c = _dot_nt(k_ref[0, rows, _head(h_p)], qs_ref[h_p])
                if consume is not None:
                    p_c = jnp.exp2(sc_ref[slot, chunk, :] - m_new)
                    p_sum = _fold_rows(p_c, jnp.sum)
                    l_sum = p_sum if l_sum is None else l_sum + p_sum
                    pb_c = p_c.astype(BF16)
                    pb_prev = pb_c if pb_prev is None else jnp.concatenate([pb_prev, pb_c], axis=0)
                    if c == tk // kc - 1:
                        vt = jnp.concatenate([vt_ref[0, t_c * (tk // tv) + j, _head(h_c), :]
                                              for j in range(tk // tv)], axis=1)
                        pv = _dot(vt, pb_prev)
                if produce is not None:
                    if diagonal:
                        s_c, c_max = mask_diagonal(s_c, c)
                    else:
                        c_max = _fold_rows(s_c, jnp.max)
                    sp_ref[slot, chunk, :] = s_c
                    mx = c_max if mx is None else jnp.maximum(mx, c_max)
            if consume is not None:
                l_ref[h_c] = alpha * l_ref[h_c] + l_sum
                acc_ref[h_c] = alpha * acc_ref[h_c] + pv
                m_ref[h_c] = m_new
            if produce is not None:
                mxp_ref[slot] = jnp.max(mx, axis=0, keepdims=True)

    def tile(t, next_diagonal):
        step(consume=(t, group_a), produce=(t, group_b, False))
        step(consume=(t, group_b), produce=(t + 1, group_a, next_diagonal))

    def body(t, carry):
        tile(t, False)
        return carry

    @pl.when(i == 0)
    def _():
        step(produce=(0, group_a, True))

    @pl.when(i > 0)
    def _():
        step(produce=(0, group_a, False))
        lax.fori_loop(0, i - 1, body, 0)
        tile(i - 1, True)

    lam = (jnp.exp(jnp.sum(lq1_ref[...] * lk1_ref[...], axis=-1, keepdims=True))
           - jnp.exp(jnp.sum(lq2_ref[...] * lk2_ref[...], axis=-1, keepdims=True)) + LAM_INIT)

    def finish(heads):
        for hh in heads:
            sl = _head(hh)
            acc = acc_ref[hh]
            inv = 1.0 / jnp.sum(l_ref[hh], axis=0, keepdims=True)
            o_t = acc[:, :tq] * inv[:, :tq] - lam * (acc[:, tq:] * inv[:, tq:])
            o = _rms(o_t.T, g_ref[:, sl]) * (1.0 - LAM_INIT)
            yd_ref[0, :, sl] = (o * _silu(zd_ref[0, :, sl].astype(F32))).astype(BF16)

    step(consume=(i, group_a), produce=(i, group_b, True))
    finish(group_a[0])
    step(consume=(i, group_b))
    finish(group_b[0])


def _diff_call(qd, kd, vdt, zd, dnorm_g, lq1, lk1, lq2, lk2):
    b, s, _ = qd.shape
    tq = DIFF_Q_ROWS
    nkv = vdt.shape[1]
    lam_spec = pl.BlockSpec((1, DIFF_QK_DIM), lambda i, j: (0, 0))
    tok_spec = pl.BlockSpec((1, tq, GROUP_WIDTH), lambda i, j: (i, j, 0))
    return pl.pallas_call(
        _diff_kernel,
        grid=(b, s // tq),
        in_specs=[
            tok_spec,
            pl.BlockSpec((1, s, GROUP_WIDTH), lambda i, j: (i, 0, 0)),
            pl.BlockSpec((1, nkv, GROUP_WIDTH, V_TILE), lambda i, j: (i, 0, 0, 0)),
            tok_spec,
            pl.BlockSpec((1, GROUP_WIDTH), lambda i, j: (0, 0)),
            lam_spec, lam_spec, lam_spec, lam_spec,
        ],
        out_specs=tok_spec,
        out_shape=jax.ShapeDtypeStruct((b, s, GROUP_WIDTH), BF16),
        scratch_shapes=[pltpu.VMEM((N_HEADS, 2 * tq, HEAD_DIM), BF16),
                        pltpu.VMEM((N_HEADS, 1, 2 * tq), F32),
                        pltpu.VMEM((N_HEADS, SUBLANES, 2 * tq), F32),
                        pltpu.VMEM((N_HEADS, HEAD_DIM, 2 * tq), F32),
                        pltpu.VMEM((N_HEADS // 2, DIFF_KV_ROWS, 2 * tq), F32),
                        pltpu.VMEM((N_HEADS // 2, DIFF_KV_ROWS, 2 * tq), F32),
                        pltpu.VMEM((N_HEADS // 2, 1, 2 * tq), F32),
                        pltpu.VMEM((N_HEADS // 2, 1, 2 * tq), F32)],
        compiler_params=pltpu.CompilerParams(
            dimension_semantics=("arbitrary", "arbitrary"),
            vmem_limit_bytes=52 * 1024 * 1024),
        name="diffattn",
    )(qd, kd, vdt, zd, dnorm_g, lq1, lk1, lq2, lk2)


def _out_kernel(x_ref, ym_ref, yd_ref, yc_ref, wo_ref, fg_ref, o_ref):
    y = (_dot(ym_ref[...], wo_ref[0:GROUP_WIDTH, :])
         + _dot(yd_ref[...], wo_ref[GROUP_WIDTH:2 * GROUP_WIDTH, :])
         + _dot(yc_ref[...], wo_ref[2 * GROUP_WIDTH:3 * GROUP_WIDTH, :]))
    o_ref[...] = _rms(x_ref[...] + y, fg_ref[...])


def _out_call(x2, ym, yd, yc, w_out, final_g):
    t = x2.shape[0]
    tm = OUT_ROWS
    row_spec = lambda w: pl.BlockSpec((tm, w), lambda i: (i, 0))
    return pl.pallas_call(
        _out_kernel,
        grid=(t // tm,),
        in_specs=[row_spec(D_MODEL), row_spec(GROUP_WIDTH), row_spec(GROUP_WIDTH), row_spec(GROUP_WIDTH),
                  pl.BlockSpec((3 * GROUP_WIDTH, D_MODEL), lambda i: (0, 0)),
                  pl.BlockSpec((1, D_MODEL), lambda i: (0, 0))],
        out_specs=row_spec(D_MODEL),
        out_shape=jax.ShapeDtypeStruct((t, D_MODEL), F32),
        compiler_params=pltpu.CompilerParams(dimension_semantics=("arbitrary",)),
        name="outproj",
    )(x2, ym, yd, yc, w_out, final_g)


def kernel(x, mem, norm_g, w_in, conv_w, conv_b, wq_m, wk_m, wv_m, w_if, b_if, mnorm_g, skip_m,
           lam_q1, lam_k1, lam_q2, lam_k2, dnorm_g, mem_norm_g, w_mem_kv, w_out, final_g):
    b, s, d = x.shape
    assert (d, s % DIFF_KV_ROWS, DIFF_KV_ROWS % V_TILE, FRONT_ROWS % V_TILE) == (D_MODEL, 0, 0, 0)
    assert (s % FRONT_ROWS, (b * s) % OUT_ROWS, V_TILE % DIFF_KEY_CHUNK) == (0, 0, 0)
    assert DIFF_Q_ROWS == DIFF_KV_ROWS and DIFF_KEY_CHUNK == HEAD_DIM
    assert norm_g.shape[0] == 1, "single-layer kernel"
    l = 0
    w_if_pad = jnp.pad(w_if[l], ((0, 0), (0, HEAD_DIM - w_if.shape[-1]))).astype(BF16)
    b_if_pad = jnp.pad(b_if[l], (0, HEAD_DIM - b_if.shape[-1]))[None, :]

    kv = _memkv_call(mem, mem_norm_g[l][None, :], w_mem_kv[l].astype(BF16))
    ym, qd, kd, vdt, zd, yc = _front_call(
        x, kv, norm_g[l][None, :], w_in[l].astype(BF16), conv_w[l], conv_b[l][None, :],
        jnp.concatenate([wq_m[l], wk_m[l]], axis=-1).astype(BF16), wv_m[l].astype(BF16), w_if_pad, b_if_pad,
        mnorm_g[l][None, :], skip_m[l][None, :])
    yd = _diff_call(qd, kd, vdt, zd, dnorm_g[l][None, :],
                    lam_q1[l][None, :], lam_k1[l][None, :], lam_q2[l][None, :], lam_k2[l][None, :])
    out = _out_call(x.reshape(b * s, d), ym.reshape(b * s, GROUP_WIDTH), yd.reshape(b * s, GROUP_WIDTH),
                    yc.reshape(b * s, GROUP_WIDTH), w_out[l].astype(BF16), final_g[None, :])
    return out.reshape(b, s, d)
```

```python
import math

import jax
import jax.numpy as jnp
from jax import lax
from jax.experimental import pallas as pl
from jax.experimental.pallas import tpu as pltpu

F32 = jnp.float32
BF16 = jnp.bfloat16

D_MODEL = 1024
N_HEADS = 4
HEAD_DIM = 128
SUBLANES = 8
GROUP_WIDTH = N_HEADS * HEAD_DIM
DIFF_QK_DIM = 64
CONV_WIDTH = 4
MLSTM_CHUNK = 128
MLSTM_STEP_CHUNKS = 4
MEM_LEN = 256
IN_WIDTH = 9 * GROUP_WIDTH
NORM_EPS = 1e-6
LAM_INIT = 0.8 - 0.6 * math.exp(-0.3 * 0)
LOG2E = math.log2(math.e)

OFF_XM, OFF_OM, OFF_ZM, OFF_QD, OFF_KD, OFF_VD, OFF_ZD, OFF_QC, OFF_ZC = (
    i * GROUP_WIDTH for i in range(9))

FRONT_ROWS = MLSTM_STEP_CHUNKS * MLSTM_CHUNK
V_TILE = 256
DIFF_Q_ROWS = 512
DIFF_KV_ROWS = 512
DIFF_KEY_CHUNK = 128
CONV_HALO = 8

NT_DIMS = (((1,), (1,)), ((), ()))
TN_DIMS = (((0,), (0,)), ((), ()))


def _dot(a, b):
    return jnp.dot(a, b, preferred_element_type=F32)


def _dot_nt(a, b):
    return lax.dot_general(a, b, NT_DIMS, preferred_element_type=F32)


def _silu(x):
    return x * jax.nn.sigmoid(x)


def _rms(x, g):
    return x * lax.rsqrt(jnp.mean(x * x, axis=-1, keepdims=True) + NORM_EPS) * g


def _head(h):
    return slice(h * HEAD_DIM, (h + 1) * HEAD_DIM)


def _fold_rows(x, op):
    r, n = x.shape
    return op(x.reshape(r // SUBLANES, SUBLANES, n), axis=0)


def _log_sigmoid(x):
    return jnp.minimum(x, 0.0) - jnp.log1p(jnp.exp(-jnp.abs(x)))


def _cumsum_rows(f, tril):
    hi = f.astype(BF16)
    r1 = f - hi.astype(F32)
    mid = r1.astype(BF16)
    lo = (r1 - mid.astype(F32)).astype(BF16)
    return _dot(tril, hi) + _dot(tril, mid) + _dot(tril, lo)


def _memkv_kernel(mem_ref, g_ref, w_ref, kv_ref):
    y = _rms(mem_ref[0], g_ref[...])
    kv_ref[0] = _dot(y.astype(BF16), w_ref[...]).astype(BF16)


def _memkv_call(mem, g, w):
    b = mem.shape[0]
    return pl.pallas_call(
        _memkv_kernel,
        grid=(b,),
        in_specs=[
            pl.BlockSpec((1, MEM_LEN, D_MODEL), lambda i: (i, 0, 0)),
            pl.BlockSpec((1, D_MODEL), lambda i: (0, 0)),
            pl.BlockSpec((D_MODEL, 2 * GROUP_WIDTH), lambda i: (0, 0)),
        ],
        out_specs=pl.BlockSpec((1, MEM_LEN, 2 * GROUP_WIDTH), lambda i: (i, 0, 0)),
        out_shape=jax.ShapeDtypeStruct((b, MEM_LEN, 2 * GROUP_WIDTH), BF16),
        compiler_params=pltpu.CompilerParams(dimension_semantics=("arbitrary",)),
        name="memkv",
    )(mem, g, w)


def _front_kernel(x_ref, kv_ref, ng_ref, win_ref, cw_ref, cb_ref, wqk_ref, wv_ref, wif_ref, bif_ref,
                  mg_ref, sk_ref,
                  ym_ref, qd_ref, kd_ref, vdt_ref, zd_ref, yc_ref,
                  conv_ref, c_ref, n_ref, m_ref, q_s, k_s, v_s, g_s, om_s, zm_s, xcv_s):
    tm = FRONT_ROWS
    L = MLSTM_CHUNK
    heads = range(N_HEADS)

    @pl.when(pl.program_id(1) == 0)
    def _():
        conv_ref[tm:tm + CONV_HALO, :] = jnp.zeros((CONV_HALO, GROUP_WIDTH), F32)
        c_ref[...] = jnp.zeros_like(c_ref)
        n_ref[...] = jnp.zeros_like(n_ref)
        m_ref[...] = jnp.zeros_like(m_ref)

    h = _rms(x_ref[0], ng_ref[...]).astype(BF16)

    def proj(off):
        return _dot(h, win_ref[:, off:off + GROUP_WIDTH])

    def put_vdt(val):
        v_t = val.T.astype(BF16)
        for j in range(tm // V_TILE):
            vdt_ref[0, j] = v_t[:, j * V_TILE:(j + 1) * V_TILE]

    def put(ref, scale=None):
        def sink(val):
            ref[...] = (val if scale is None else val * scale).astype(BF16).reshape(ref.shape)
        return sink

    side_pieces = [(OFF_OM, put(om_s)), (OFF_ZM, put(zm_s)),
                   (OFF_QD, put(qd_ref, DIFF_QK_DIM ** -0.5 * LOG2E)), (OFF_KD, put(kd_ref)),
                   (OFF_VD, put_vdt), (OFF_ZD, put(zd_ref))]
    pending = []

    def side_start():
        if side_pieces:
            off, sink = side_pieces.pop(0)
            pending.append((proj(off), sink))

    def side_finish():
        if pending:
            val, sink = pending.pop(0)
            sink(val)

    x_m = proj(OFF_XM)
    q_cb = proj(OFF_QC).astype(BF16)
    z_c = proj(OFF_ZC)

    conv_ref[0:CONV_HALO, :] = conv_ref[tm:tm + CONV_HALO, :]
    conv_ref[CONV_HALO:CONV_HALO + tm, :] = x_m
    acc = jnp.broadcast_to(cb_ref[...], (tm, GROUP_WIDTH))
    for j in range(CONV_WIDTH):
        start = CONV_HALO - (CONV_WIDTH - 1) + j
        acc = acc + cw_ref[j:j + 1, :] * conv_ref[start:start + tm, :]
    sc = [_dot_nt(q_cb[:, _head(hh)], kv_ref[0, :, _head(hh)]) * HEAD_DIM ** -0.5 for hh in heads]
    x_cv = _silu(acc)
    xcv_b = x_cv.astype(BF16)
    xcv_s[...] = xcv_b
    xm_b = x_m.astype(BF16)

    qk_p = [_dot(xcv_b[:, _head(hh)], wqk_ref[hh]) for hh in heads]
    vv_p = [_dot(xm_b[:, _head(hh)], wv_ref[hh]) for hh in heads]

    mx = [jnp.max(sc[hh], axis=-1, keepdims=True) for hh in heads]
    pm = [jnp.exp(sc[hh] - mx[hh]) for hh in heads]
    inv = [1.0 / jnp.sum(pm[hh], axis=-1, keepdims=True) for hh in heads]
    pc = [(pm[hh] * inv[hh]).astype(BF16) for hh in heads]

    q_all = jnp.concatenate([qk_p[hh][:, :HEAD_DIM] for hh in heads], axis=1).astype(BF16)
    k_all = jnp.concatenate([qk_p[hh][:, HEAD_DIM:] for hh in heads], axis=1)
    v_all = jnp.concatenate(vv_p, axis=1).astype(BF16)
    oc = [_dot(pc[hh], kv_ref[0, :, GROUP_WIDTH + hh * HEAD_DIM:GROUP_WIDTH + (hh + 1) * HEAD_DIM]) for hh in heads]
    q_s[...] = q_all
    k_s[...] = (k_all * HEAD_DIM ** -0.5).astype(BF16)
    v_s[...] = v_all
    g_s[...] = (jnp.broadcast_to(bif_ref[...], (tm, HEAD_DIM))
                + _dot(q_all, wif_ref[0:GROUP_WIDTH, :])
                + _dot(k_all.astype(BF16), wif_ref[GROUP_WIDTH:2 * GROUP_WIDTH, :])
                + _dot(v_all, wif_ref[2 * GROUP_WIDTH:3 * GROUP_WIDTH, :]))
    for hh in heads:
        yc_ref[0, :, _head(hh)] = (oc[hh] * _silu(z_c[:, _head(hh)])).astype(BF16)

    row = lax.broadcasted_iota(jnp.int32, (L, L), 0)
    col = lax.broadcasted_iota(jnp.int32, (L, L), 1)
    causal = col <= row
    tril = jnp.where(causal, 1.0, 0.0).astype(BF16)
    chunks = range(MLSTM_STEP_CHUNKS)
    probs = [(c, hh) for c in chunks for hh in heads]
    rows = lambda c: slice(c * L, (c + 1) * L)
    q_of = lambda p: q_s[rows(p[0]), _head(p[1])]
    k_of = lambda p: k_s[rows(p[0]), _head(p[1])]
    v_of = lambda p: v_s[rows(p[0]), _head(p[1])]

    def each(f, side=False):
        if side:
            side_start()
        out = {p: f(p) for p in probs}
        if side:
            side_finish()
        return out

    a_cols, a_rows = [], []
    for c in chunks:
        g = g_s[rows(c), :]
        bb_all = _cumsum_rows(_log_sigmoid(g), tril)
        ac = jnp.where(col < N_HEADS, g, bb_all)
        a_cols.append(ac)
        a_rows.append(ac.T)
    ib_col = {(c, hh): a_cols[c][:, hh:hh + 1] for c, hh in probs}
    bb_col = {(c, hh): a_cols[c][:, N_HEADS + hh:N_HEADS + hh + 1] for c, hh in probs}
    ib_row = {(c, hh): a_rows[c][hh:hh + 1, :] for c, hh in probs}
    bb_row = {(c, hh): a_rows[c][N_HEADS + hh:N_HEADS + hh + 1, :] for c, hh in probs}
    b_end = {p: bb_col[p][L - 1:L, :] for p in probs}

    qk = {p: _dot_nt(q_of(p), k_of(p)) for p in probs}

    d = each(lambda p: jnp.where(causal, (bb_col[p] - bb_row[p]) + ib_row[p], -jnp.inf), side=True)
    r = each(lambda p: jnp.max(d[p], axis=-1, keepdims=True))
    s_loc = each(lambda p: qk[p] * jnp.exp(d[p] - r[p]), side=True)
    rs = each(lambda p: jnp.sum(s_loc[p], axis=-1, keepdims=True))
    s_bf = each(lambda p: s_loc[p].astype(BF16))
    a = each(lambda p: (b_end[p] - bb_col[p]) + ib_col[p], side=True)
    a_max = each(lambda p: jnp.max(a[p], axis=0, keepdims=True))
    w_loc = each(lambda p: jnp.exp(a[p] - a_max[p]))
    vw = each(lambda p: (v_of(p).astype(F32) * w_loc[p]).astype(BF16), side=True)
    nu = each(lambda p: jnp.sum(k_of(p).astype(F32) * w_loc[p], axis=0, keepdims=True))

    sv = {p: _dot(s_bf[p], v_of(p)) for p in probs}
    u = {p: lax.dot_general(vw[p], k_of(p), TN_DIMS, preferred_element_type=F32) for p in probs}

    m_in, c_in, n_in = {}, {}, {}
    for hh in heads:
        m_cur = m_ref[hh][:, 0:1]
        c_cur = c_ref[hh]
        n_cur = n_ref[hh]
        for c in chunks:
            p = (c, hh)
            m_in[p], c_in[p], n_in[p] = m_cur, c_cur.astype(BF16), n_cur
            m_new = jnp.maximum(b_end[p] + m_cur, a_max[p])
            decay = jnp.exp(b_end[p] + m_cur - m_new)
            e_upd = jnp.exp(a_max[p] - m_new)
            c_cur = decay * c_cur + e_upd * u[p]
            n_cur = decay * n_cur + e_upd * nu[p]
            m_cur = m_new
        c_ref[hh] = c_cur
        n_ref[hh] = n_cur
        m_ref[hh] = jnp.broadcast_to(m_cur, (1, HEAD_DIM))

    q_c = {p: _dot_nt(q_of(p), c_in[p]) for p in probs}

    tile_of = lambda ref, p: ref[rows(p[0]), _head(p[1])].astype(F32)
    g_col = each(lambda p: bb_col[p] + m_in[p])
    m_t = each(lambda p: jnp.maximum(g_col[p], r[p]))
    e_loc = each(lambda p: jnp.exp(r[p] - m_t[p]))
    inter = each(lambda p: jnp.exp(g_col[p] - m_t[p]))
    q_n = each(lambda p: jnp.sum(q_of(p).astype(F32) * n_in[p], axis=-1, keepdims=True))
    den = each(lambda p: e_loc[p] * rs[p] + inter[p] * q_n[p])
    scale = each(lambda p: 1.0 / jnp.maximum(jnp.abs(den[p]), jnp.exp(-m_t[p])))
    hg = each(lambda p: jax.nn.sigmoid(tile_of(om_s, p))
              * ((e_loc[p] * sv[p] + inter[p] * q_c[p]) * scale[p]), side=True)
    mu = each(lambda p: jnp.mean(hg[p], axis=-1, keepdims=True))
    cen = each(lambda p: hg[p] - mu[p], side=True)
    var = each(lambda p: jnp.mean(jnp.square(cen[p]), axis=-1, keepdims=True))
    assert not side_pieces and not pending
    for p in probs:
        sl = _head(p[1])
        y = cen[p] * lax.rsqrt(var[p] + NORM_EPS) * mg_ref[:, sl]
        y = (y + sk_ref[:, sl] * tile_of(xcv_s, p)) * _silu(tile_of(zm_s, p))
        ym_ref[0, rows(p[0]), sl] = y.astype(BF16)


def _front_call(x, kv, norm_g, w_in, conv_w, conv_b, wqk, wv, w_if, b_if, mnorm_g, skip_m):
    b, s, _ = x.shape
    tm = FRONT_ROWS
    ns = s // tm
    tok = lambda w: jax.ShapeDtypeStruct((b, s, w), BF16)
    tok_spec = lambda w: pl.BlockSpec((1, tm, w), lambda i, j: (i, j, 0))
    const = lambda shape: pl.BlockSpec(shape, lambda i, j: (0,) * len(shape))
    vdt_tiles = tm // V_TILE
    out_shape = (
        tok(GROUP_WIDTH),
        tok(GROUP_WIDTH), tok(GROUP_WIDTH),
        jax.ShapeDtypeStruct((b, ns * vdt_tiles, GROUP_WIDTH, V_TILE), BF16),
        tok(GROUP_WIDTH),
        tok(GROUP_WIDTH),
    )
    out_specs = (
        tok_spec(GROUP_WIDTH), tok_spec(GROUP_WIDTH), tok_spec(GROUP_WIDTH),
        pl.BlockSpec((1, vdt_tiles, GROUP_WIDTH, V_TILE), lambda i, j: (i, j, 0, 0)),
        tok_spec(GROUP_WIDTH), tok_spec(GROUP_WIDTH),
    )
    in_specs = [
        pl.BlockSpec((1, tm, D_MODEL), lambda i, j: (i, j, 0)),
        pl.BlockSpec((1, MEM_LEN, 2 * GROUP_WIDTH), lambda i, j: (i, 0, 0)),
        const((1, D_MODEL)),
        const((D_MODEL, IN_WIDTH)),
        const((CONV_WIDTH, GROUP_WIDTH)),
        const((1, GROUP_WIDTH)),
        const((N_HEADS, HEAD_DIM, 2 * HEAD_DIM)),
        const((N_HEADS, HEAD_DIM, HEAD_DIM)),
        const((3 * GROUP_WIDTH, HEAD_DIM)),
        const((1, HEAD_DIM)),
        const((1, GROUP_WIDTH)),
        const((1, GROUP_WIDTH)),
    ]
    tile_bf16 = pltpu.VMEM((tm, GROUP_WIDTH), BF16)
    return pl.pallas_call(
        _front_kernel,
        grid=(b, ns),
        in_specs=in_specs,
        out_specs=out_specs,
        out_shape=out_shape,
        scratch_shapes=[pltpu.VMEM((tm + CONV_HALO, GROUP_WIDTH), F32),
                        pltpu.VMEM((N_HEADS, HEAD_DIM, HEAD_DIM), F32),
                        pltpu.VMEM((N_HEADS, 1, HEAD_DIM), F32),
                        pltpu.VMEM((N_HEADS, 1, HEAD_DIM), F32),
                        tile_bf16, tile_bf16, tile_bf16,
                        pltpu.VMEM((tm, HEAD_DIM), F32),
                        tile_bf16, tile_bf16, tile_bf16],
        compiler_params=pltpu.CompilerParams(
            dimension_semantics=("arbitrary", "arbitrary"),
            vmem_limit_bytes=56 * 1024 * 1024),
        name="front",
    )(x, kv, norm_g, w_in, conv_w, conv_b, wqk, wv, w_if, b_if, mnorm_g, skip_m)


def _diff_kernel(q_ref, k_ref, vt_ref, zd_ref, g_ref, lq1_ref, lk1_ref, lq2_ref, lk2_ref,
                 x_ref, ym_ref, yc_ref, wo_ref, fg_ref, o_ref,
                 qs_ref, m_ref, l_ref, acc_ref, s0_ref, s1_ref, mx0_ref, mx1_ref, yd_ref, part_ref):
    tq, tk, tv = DIFF_Q_ROWS, DIFF_KV_ROWS, V_TILE
    i = pl.program_id(1)
    lane = lax.broadcasted_iota(jnp.int32, (tq, HEAD_DIM), 1)
    for hh in range(N_HEADS):
        q = q_ref[0, :, _head(hh)]
        zero = jnp.zeros_like(q)
        qs_ref[hh, 0:tq, :] = jnp.where(lane < DIFF_QK_DIM, q, zero)
        qs_ref[hh, tq:2 * tq, :] = jnp.where(lane >= DIFF_QK_DIM, q, zero)
    m_ref[...] = jnp.full_like(m_ref, -jnp.inf)
    l_ref[...] = jnp.zeros_like(l_ref)
    acc_ref[...] = jnp.zeros_like(acc_ref)

    kc = DIFF_KEY_CHUNK
    sub = HEAD_DIM
    tri = (lax.broadcasted_iota(jnp.int32, (kc, sub), 0) <= lax.broadcasted_iota(jnp.int32, (kc, sub), 1))

    def mask_diagonal(s_c, c):
        blocks, maxes = [], []
        for b8 in range(2 * tq // sub):
            b = b8 % (tq // sub)
            blk = s_c[:, b8 * sub:(b8 + 1) * sub]
            if c > b:
                blocks.append(jnp.full((kc, sub), -jnp.inf, F32))
                maxes.append(jnp.full((SUBLANES, sub), -jnp.inf, F32))
                continue
            if c == b:
                blk = jnp.where(tri, blk, -jnp.inf)
            blocks.append(blk)
            maxes.append(_fold_rows(blk, jnp.max))
        return jnp.concatenate(blocks, axis=1), jnp.concatenate(maxes, axis=1)

    group_a = (tuple(range(0, N_HEADS // 2)), s0_ref, mx0_ref)
    group_b = (tuple(range(N_HEADS // 2, N_HEADS)), s1_ref, mx1_ref)

    def step(consume=None, produce=None):
        n_slots = N_HEADS // 2
        for slot in range(n_slots):
            if consume is not None:
                t_c, (heads_c, sc_ref, mxc_ref) = consume
                h_c = heads_c[slot]
                m_old = m_ref[h_c]
                m_new = jnp.maximum(m_old, mxc_ref[slot])
                alpha = jnp.exp2(m_old - m_new)
                l_sum, pv, pb_prev = None, None, None
            if produce is not None:
                t_p, (heads_p, sp_ref, mxp_ref), diagonal = produce
                h_p = heads_p[slot]
                mx = None
            for c in range(tk // kc):
                chunk = slice(c * kc, (c + 1) * kc)
                if produce is not None:
                    rows = pl.ds(pl.multiple_of(t_p * tk + c * kc, kc), kc)
                    s_c = _dot_nt(k_ref[0, rows, _head(h_p)], qs_ref[h_p])
                if consume is not None:
                    p_c = jnp.exp2(sc_ref[slot, chunk, :] - m_new)
                    p_sum = _fold_rows(p_c, jnp.sum)
                    l_sum = p_sum if l_sum is None else l_sum + p_sum
                    pb_c = p_c.astype(BF16)
                    pb_prev = pb_c if pb_prev is None else jnp.concatenate([pb_prev, pb_c], axis=0)
                    if c == tk // kc - 1:
                        vt = jnp.concatenate([vt_ref[0, t_c * (tk // tv) + j, _head(h_c), :]
                                              for j in range(tk // tv)], axis=1)
                        pv = _dot(vt, pb_prev)
                if produce is not None:
                    if diagonal:
                        s_c, c_max = mask_diagonal(s_c, c)
                    else:
                        c_max = _fold_rows(s_c, jnp.max)
                    sp_ref[slot, chunk, :] = s_c
                    mx = c_max if mx is None else jnp.maximum(mx, c_max)
            if consume is not None:
                l_ref[h_c] = alpha * l_ref[h_c] + l_sum
                acc_ref[h_c] = alpha * acc_ref[h_c] + pv
                m_ref[h_c] = m_new
            if produce is not None:
                mxp_ref[slot] = jnp.max(mx, axis=0, keepdims=True)

    def tile(t, next_diagonal):
        step(consume=(t, group_a), produce=(t, group_b, False))
        step(consume=(t, group_b), produce=(t + 1, group_a, next_diagonal))

    def body(t, carry):
        tile(t, False)
        return carry

    @pl.when(i == 0)
    def _():
        step(produce=(0, group_a, True))

    @pl.when(i > 0)
    def _():
        step(produce=(0, group_a, False))
        lax.fori_loop(0, i - 1, body, 0)
        tile(i - 1, True)

    lam = (jnp.exp(jnp.sum(lq1_ref[...] * lk1_ref[...], axis=-1, keepdims=True))
           - jnp.exp(jnp.sum(lq2_ref[...] * lk2_ref[...], axis=-1, keepdims=True)) + LAM_INIT)

    def finish(heads):
        for hh in heads:
            sl = _head(hh)
            acc = acc_ref[hh]
            inv = 1.0 / jnp.sum(l_ref[hh], axis=0, keepdims=True)
            o_t = acc[:, :tq] * inv[:, :tq] - lam * (acc[:, tq:] * inv[:, tq:])
            o = _rms(o_t.T, g_ref[:, sl]) * (1.0 - LAM_INIT)
            yd_ref[:, sl] = (o * _silu(zd_ref[0, :, sl].astype(F32))).astype(BF16)

    def project_rows(c, src_ref, w_rows, first):
        r_c = slice(c * HEAD_DIM, (c + 1) * HEAD_DIM)
        y = _dot(src_ref[0, r_c, :], wo_ref[w_rows, :])
        part_ref[r_c, :] = (x_ref[0, r_c, :] + y) if first else (part_ref[r_c, :] + y)

    def project_heads(heads):
        cols = slice(heads[0] * HEAD_DIM, (heads[-1] + 1) * HEAD_DIM)
        w_rows = slice(GROUP_WIDTH + cols.start, GROUP_WIDTH + cols.stop)
        return _dot(yd_ref[:, cols], wo_ref[w_rows, :])

    step(consume=(i, group_a), produce=(i, group_b, True))
    for c in range(tq // HEAD_DIM):
        project_rows(c, ym_ref, slice(0, GROUP_WIDTH), True)
    finish(group_a[0])
    for c in range(tq // HEAD_DIM):
        project_rows(c, yc_ref, slice(2 * GROUP_WIDTH, 3 * GROUP_WIDTH), False)
    y_a = project_heads(group_a[0])
    step(consume=(i, group_b))
    finish(group_b[0])
    o_ref[0] = _rms(part_ref[...] + y_a + project_heads(group_b[0]), fg_ref[...])


def _diff_out_call(qd, kd, vdt, zd, dnorm_g, lq1, lk1, lq2, lk2, x, ym, yc, w_out, final_g):
    b, s, _ = qd.shape
    tq = DIFF_Q_ROWS
    nkv = vdt.shape[1]
    const = lambda shape: pl.BlockSpec(shape, lambda i, j: (0,) * len(shape))
    tok_spec = lambda w: pl.BlockSpec((1, tq, w), lambda i, j: (i, j, 0))
    return pl.pallas_call(
        _diff_kernel,
        grid=(b, s // tq),
        in_specs=[
            tok_spec(GROUP_WIDTH),
            pl.BlockSpec((1, s, GROUP_WIDTH), lambda i, j: (i, 0, 0)),
            pl.BlockSpec((1, nkv, GROUP_WIDTH, V_TILE), lambda i, j: (i, 0, 0, 0)),
            tok_spec(GROUP_WIDTH),
            const((1, GROUP_WIDTH)),
            const((1, DIFF_QK_DIM)), const((1, DIFF_QK_DIM)), const((1, DIFF_QK_DIM)), const((1, DIFF_QK_DIM)),
            tok_spec(D_MODEL), tok_spec(GROUP_WIDTH), tok_spec(GROUP_WIDTH),
            const((3 * GROUP_WIDTH, D_MODEL)), const((1, D_MODEL)),
        ],
        out_specs=tok_spec(D_MODEL),
        out_shape=jax.ShapeDtypeStruct((b, s, D_MODEL), F32),
        scratch_shapes=[pltpu.VMEM((N_HEADS, 2 * tq, HEAD_DIM), BF16),
                        pltpu.VMEM((N_HEADS, 1, 2 * tq), F32),
                        pltpu.VMEM((N_HEADS, SUBLANES, 2 * tq), F32),
                        pltpu.VMEM((N_HEADS, HEAD_DIM, 2 * tq), F32),
                        pltpu.VMEM((N_HEADS // 2, DIFF_KV_ROWS, 2 * tq), F32),
                        pltpu.VMEM((N_HEADS // 2, DIFF_KV_ROWS, 2 * tq), F32),
                        pltpu.VMEM((N_HEADS // 2, 1, 2 * tq), F32),
                        pltpu.VMEM((N_HEADS // 2, 1, 2 * tq), F32),
                        pltpu.VMEM((tq, GROUP_WIDTH), BF16),
                        pltpu.VMEM((tq, D_MODEL), F32)],
        compiler_params=pltpu.CompilerParams(
            dimension_semantics=("arbitrary", "arbitrary"),
            vmem_limit_bytes=58 * 1024 * 1024),
        name="diffattn_out",
    )(qd, kd, vdt, zd, dnorm_g, lq1, lk1, lq2, lk2, x, ym, yc, w_out, final_g)


def kernel(x, mem, norm_g, w_in, conv_w, conv_b, wq_m, wk_m, wv_m, w_if, b_if, mnorm_g, skip_m,
           lam_q1, lam_k1, lam_q2, lam_k2, dnorm_g, mem_norm_g, w_mem_kv, w_out, final_g):
    b, s, d = x.shape
    assert (d, s % DIFF_KV_ROWS, DIFF_KV_ROWS % V_TILE, FRONT_ROWS % V_TILE) == (D_MODEL, 0, 0, 0)
    assert (s % FRONT_ROWS, V_TILE % DIFF_KEY_CHUNK) == (0, 0)
    assert DIFF_Q_ROWS == DIFF_KV_ROWS and DIFF_KEY_CHUNK == HEAD_DIM
    assert norm_g.shape[0] == 1, "single-layer kernel"
    l = 0
    w_if_pad = jnp.pad(w_if[l], ((0, 0), (0, HEAD_DIM - w_if.shape[-1]))).astype(BF16)
    b_if_pad = jnp.pad(b_if[l], (0, HEAD_DIM - b_if.shape[-1]))[None, :]

    kv = _memkv_call(mem, mem_norm_g[l][None, :], w_mem_kv[l].astype(BF16))
    ym, qd, kd, vdt, zd, yc = _front_call(
        x, kv, norm_g[l][None, :], w_in[l].astype(BF16), conv_w[l], conv_b[l][None, :],
        jnp.concatenate([wq_m[l], wk_m[l]], axis=-1).astype(BF16), wv_m[l].astype(BF16), w_if_pad, b_if_pad,
        mnorm_g[l][None, :], skip_m[l][None, :])
    return _diff_out_call(qd, kd, vdt, zd, dnorm_g[l][None, :],
                          lam_q1[l][None, :], lam_k1[l][None, :], lam_q2[l][None, :], lam_k2[l][None, :],
                          x, ym, yc, w_out[l].astype(BF16), final_g[None, :])
```

```python
import math

import jax
import jax.numpy as jnp
from jax import lax
from jax.experimental import pallas as pl
from jax.experimental.pallas import tpu as pltpu

F32 = jnp.float32
BF16 = jnp.bfloat16

D_MODEL = 1024
N_HEADS = 4
HEAD_DIM = 128
SUBLANES = 8
GROUP_WIDTH = N_HEADS * HEAD_DIM
DIFF_QK_DIM = 64
CONV_WIDTH = 4
MLSTM_CHUNK = 128
MLSTM_STEP_CHUNKS = 4
MEM_LEN = 256
IN_WIDTH = 9 * GROUP_WIDTH
NORM_EPS = 1e-6
LAM_INIT = 0.8 - 0.6 * math.exp(-0.3 * 0)
LOG2E = math.log2(math.e)

OFF_XM, OFF_OM, OFF_ZM, OFF_QD, OFF_KD, OFF_VD, OFF_ZD, OFF_QC, OFF_ZC = (
    i * GROUP_WIDTH for i in range(9))

FRONT_ROWS = MLSTM_STEP_CHUNKS * MLSTM_CHUNK
V_TILE = 256
VT_PAD = 16
VT_ROWS = HEAD_DIM + VT_PAD
DIFF_Q_ROWS = 512
DIFF_KV_ROWS = 512
DIFF_KEY_CHUNK = 128
CONV_HALO = 8

NT_DIMS = (((1,), (1,)), ((), ()))
TN_DIMS = (((0,), (0,)), ((), ()))


def _dot(a, b):
    return jnp.dot(a, b, preferred_element_type=F32)


def _dot_nt(a, b):
    return lax.dot_general(a, b, NT_DIMS, preferred_element_type=F32)


def _silu(x):
    return x * jax.nn.sigmoid(x)


def _rms(x, g):
    return x * lax.rsqrt(jnp.mean(x * x, axis=-1, keepdims=True) + NORM_EPS) * g


def _head(h):
    return slice(h * HEAD_DIM, (h + 1) * HEAD_DIM)


def _fold_rows(x, op):
    r, n = x.shape
    return op(x.reshape(r // SUBLANES, SUBLANES, n), axis=0)


def _log_sigmoid(x):
    return jnp.minimum(x, 0.0) - jnp.log1p(jnp.exp(-jnp.abs(x)))


def _cumsum_rows(f, tril):
    hi = f.astype(BF16)
    r1 = f - hi.astype(F32)
    mid = r1.astype(BF16)
    lo = (r1 - mid.astype(F32)).astype(BF16)
    return _dot(tril, hi) + _dot(tril, mid) + _dot(tril, lo)


def _memkv_kernel(mem_ref, g_ref, w_ref, kv_ref):
    y = _rms(mem_ref[0], g_ref[...])
    kv_ref[0] = _dot(y.astype(BF16), w_ref[...]).astype(BF16)


def _memkv_call(mem, g, w):
    b = mem.shape[0]
    return pl.pallas_call(
        _memkv_kernel,
        grid=(b,),
        in_specs=[
            pl.BlockSpec((1, MEM_LEN, D_MODEL), lambda i: (i, 0, 0)),
            pl.BlockSpec((1, D_MODEL), lambda i: (0, 0)),
            pl.BlockSpec((D_MODEL, 2 * GROUP_WIDTH), lambda i: (0, 0)),
        ],
        out_specs=pl.BlockSpec((1, MEM_LEN, 2 * GROUP_WIDTH), lambda i: (i, 0, 0)),
        out_shape=jax.ShapeDtypeStruct((b, MEM_LEN, 2 * GROUP_WIDTH), BF16),
        compiler_params=pltpu.CompilerParams(dimension_semantics=("arbitrary",)),
        name="memkv",
    )(mem, g, w)


def _front_kernel(x_ref, kv_ref, ng_ref, win_ref, cw_ref, cb_ref, wqk_ref, wv_ref, wif_ref, bif_ref,
                  mg_ref, sk_ref,
                  ym_ref, qd_ref, kd_ref, vdt_ref, zd_ref, yc_ref,
                  conv_ref, c_ref, n_ref, m_ref, q_s, k_s, v_s, g_s, om_s, zm_s, xcv_s):
    tm = FRONT_ROWS
    L = MLSTM_CHUNK
    heads = range(N_HEADS)

    @pl.when(pl.program_id(1) == 0)
    def _():
        conv_ref[tm:tm + CONV_HALO, :] = jnp.zeros((CONV_HALO, GROUP_WIDTH), F32)
        c_ref[...] = jnp.zeros_like(c_ref)
        n_ref[...] = jnp.zeros_like(n_ref)
        m_ref[...] = jnp.zeros_like(m_ref)

    h = _rms(x_ref[0], ng_ref[...]).astype(BF16)

    def proj(off):
        return _dot(h, win_ref[:, off:off + GROUP_WIDTH])

    def put_vdt(val):
        v_t = val.T.astype(BF16)
        pad_row = lax.broadcasted_iota(jnp.int32, (VT_PAD, tm), 0)
        pad = jnp.where(pad_row == 0, 1.0, 0.0).astype(BF16)
        v_aug = jnp.concatenate([blk for hh in heads for blk in (v_t[_head(hh), :], pad)], axis=0)
        for j in range(tm // V_TILE):
            vdt_ref[0, j] = v_aug[:, j * V_TILE:(j + 1) * V_TILE]

    def put(ref, scale=None):
        def sink(val):
            ref[...] = (val if scale is None else val * scale).astype(BF16).reshape(ref.shape)
        return sink

    side_pieces = [(OFF_OM, put(om_s)), (OFF_ZM, put(zm_s)),
                   (OFF_QD, put(qd_ref, DIFF_QK_DIM ** -0.5 * LOG2E)), (OFF_KD, put(kd_ref)),
                   (OFF_VD, put_vdt), (OFF_ZD, put(zd_ref))]
    pending = []

    def side_start():
        if side_pieces:
            off, sink = side_pieces.pop(0)
            pending.append((proj(off), sink))

    def side_finish():
        if pending:
            val, sink = pending.pop(0)
            sink(val)

    x_m = proj(OFF_XM)
    q_cb = proj(OFF_QC).astype(BF16)
    z_c = proj(OFF_ZC)

    conv_ref[0:CONV_HALO, :] = conv_ref[tm:tm + CONV_HALO, :]
    conv_ref[CONV_HALO:CONV_HALO + tm, :] = x_m
    acc = jnp.broadcast_to(cb_ref[...], (tm, GROUP_WIDTH))
    for j in range(CONV_WIDTH):
        start = CONV_HALO - (CONV_WIDTH - 1) + j
        acc = acc + cw_ref[j:j + 1, :] * conv_ref[start:start + tm, :]
    sc = [_dot_nt(q_cb[:, _head(hh)], kv_ref[0, :, _head(hh)]) * HEAD_DIM ** -0.5 for hh in heads]
    x_cv = _silu(acc)
    xcv_b = x_cv.astype(BF16)
    xcv_s[...] = xcv_b
    xm_b = x_m.astype(BF16)

    qk_p = [_dot(xcv_b[:, _head(hh)], wqk_ref[hh]) for hh in heads]
    vv_p = [_dot(xm_b[:, _head(hh)], wv_ref[hh]) for hh in heads]

    mx = [jnp.max(sc[hh], axis=-1, keepdims=True) for hh in heads]
    pm = [jnp.exp(sc[hh] - mx[hh]) for hh in heads]
    inv = [1.0 / jnp.sum(pm[hh], axis=-1, keepdims=True) for hh in heads]
    pc = [(pm[hh] * inv[hh]).astype(BF16) for hh in heads]

    q_all = jnp.concatenate([qk_p[hh][:, :HEAD_DIM] for hh in heads], axis=1).astype(BF16)
    k_all = jnp.concatenate([qk_p[hh][:, HEAD_DIM:] for hh in heads], axis=1)
    v_all = jnp.concatenate(vv_p, axis=1).astype(BF16)
    oc = [_dot(pc[hh], kv_ref[0, :, GROUP_WIDTH + hh * HEAD_DIM:GROUP_WIDTH + (hh + 1) * HEAD_DIM]) for hh in heads]
    q_s[...] = q_all
    k_s[...] = (k_all * HEAD_DIM ** -0.5).astype(BF16)
    v_s[...] = v_all
    g_s[...] = (jnp.broadcast_to(bif_ref[...], (tm, HEAD_DIM))
                + _dot(q_all, wif_ref[0:GROUP_WIDTH, :])
                + _dot(k_all.astype(BF16), wif_ref[GROUP_WIDTH:2 * GROUP_WIDTH, :])
                + _dot(v_all, wif_ref[2 * GROUP_WIDTH:3 * GROUP_WIDTH, :]))
    for hh in heads:
        yc_ref[0, :, _head(hh)] = (oc[hh] * _silu(z_c[:, _head(hh)])).astype(BF16)

    row = lax.broadcasted_iota(jnp.int32, (L, L), 0)
    col = lax.broadcasted_iota(jnp.int32, (L, L), 1)
    causal = col <= row
    tril = jnp.where(causal, 1.0, 0.0).astype(BF16)
    chunks = range(MLSTM_STEP_CHUNKS)
    probs = [(c, hh) for c in chunks for hh in heads]
    rows = lambda c: slice(c * L, (c + 1) * L)
    q_of = lambda p: q_s[rows(p[0]), _head(p[1])]
    k_of = lambda p: k_s[rows(p[0]), _head(p[1])]
    v_of = lambda p: v_s[rows(p[0]), _head(p[1])]

    def each(f, side=False):
        if side:
            side_start()
        out = {p: f(p) for p in probs}
        if side:
            side_finish()
        return out

    a_cols, a_rows = [], []
    for c in chunks:
        g = g_s[rows(c), :]
        bb_all = _cumsum_rows(_log_sigmoid(g), tril)
        ac = jnp.where(col < N_HEADS, g, bb_all)
        a_cols.append(ac)
        a_rows.append(ac.T)
    ib_col = {(c, hh): a_cols[c][:, hh:hh + 1] for c, hh in probs}
    bb_col = {(c, hh): a_cols[c][:, N_HEADS + hh:N_HEADS + hh + 1] for c, hh in probs}
    ib_row = {(c, hh): a_rows[c][hh:hh + 1, :] for c, hh in probs}
    bb_row = {(c, hh): a_rows[c][N_HEADS + hh:N_HEADS + hh + 1, :] for c, hh in probs}
    b_end = {p: bb_col[p][L - 1:L, :] for p in probs}

    qk = {p: _dot_nt(q_of(p), k_of(p)) for p in probs}

    d = each(lambda p: jnp.where(causal, (bb_col[p] - bb_row[p]) + ib_row[p], -jnp.inf), side=True)
    r = each(lambda p: jnp.max(d[p], axis=-1, keepdims=True))
    s_loc = each(lambda p: qk[p] * jnp.exp(d[p] - r[p]), side=True)
    rs = each(lambda p: jnp.sum(s_loc[p], axis=-1, keepdims=True))
    s_bf = each(lambda p: s_loc[p].astype(BF16))
    a = each(lambda p: (b_end[p] - bb_col[p]) + ib_col[p], side=True)
    a_max = each(lambda p: jnp.max(a[p], axis=0, keepdims=True))
    w_loc = each(lambda p: jnp.exp(a[p] - a_max[p]))
    vw = each(lambda p: (v_of(p).astype(F32) * w_loc[p]).astype(BF16), side=True)
    nu = each(lambda p: jnp.sum(k_of(p).astype(F32) * w_loc[p], axis=0, keepdims=True))

    sv = {p: _dot(s_bf[p], v_of(p)) for p in probs}
    u = {p: lax.dot_general(vw[p], k_of(p), TN_DIMS, preferred_element_type=F32) for p in probs}

    m_in, c_in, n_in = {}, {}, {}
    for hh in heads:
        m_cur = m_ref[hh][:, 0:1]
        c_cur = c_ref[hh]
        n_cur = n_ref[hh]
        for c in chunks:
            p = (c, hh)
            m_in[p], c_in[p], n_in[p] = m_cur, c_cur.astype(BF16), n_cur
            m_new = jnp.maximum(b_end[p] + m_cur, a_max[p])
            decay = jnp.exp(b_end[p] + m_cur - m_new)
            e_upd = jnp.exp(a_max[p] - m_new)
            c_cur = decay * c_cur + e_upd * u[p]
            n_cur = decay * n_cur + e_upd * nu[p]
            m_cur = m_new
        c_ref[hh] = c_cur
        n_ref[hh] = n_cur
        m_ref[hh] = jnp.broadcast_to(m_cur, (1, HEAD_DIM))

    q_c = {p: _dot_nt(q_of(p), c_in[p]) for p in probs}

    tile_of = lambda ref, p: ref[rows(p[0]), _head(p[1])].astype(F32)
    g_col = each(lambda p: bb_col[p] + m_in[p])
    m_t = each(lambda p: jnp.maximum(g_col[p], r[p]))
    e_loc = each(lambda p: jnp.exp(r[p] - m_t[p]))
    inter = each(lambda p: jnp.exp(g_col[p] - m_t[p]))
    q_n = each(lambda p: jnp.sum(q_of(p).astype(F32) * n_in[p], axis=-1, keepdims=True))
    den = each(lambda p: e_loc[p] * rs[p] + inter[p] * q_n[p])
    scale = each(lambda p: 1.0 / jnp.maximum(jnp.abs(den[p]), jnp.exp(-m_t[p])))
    hg = each(lambda p: jax.nn.sigmoid(tile_of(om_s, p))
              * ((e_loc[p] * sv[p] + inter[p] * q_c[p]) * scale[p]), side=True)
    mu = each(lambda p: jnp.mean(hg[p], axis=-1, keepdims=True))
    cen = each(lambda p: hg[p] - mu[p], side=True)
    var = each(lambda p: jnp.mean(jnp.square(cen[p]), axis=-1, keepdims=True))
    assert not side_pieces and not pending
    for p in probs:
        sl = _head(p[1])
        y = cen[p] * lax.rsqrt(var[p] + NORM_EPS) * mg_ref[:, sl]
        y = (y + sk_ref[:, sl] * tile_of(xcv_s, p)) * _silu(tile_of(zm_s, p))
        ym_ref[0, rows(p[0]), sl] = y.astype(BF16)


def _front_call(x, kv, norm_g, w_in, conv_w, conv_b, wqk, wv, w_if, b_if, mnorm_g, skip_m):
    b, s, _ = x.shape
    tm = FRONT_ROWS
    ns = s // tm
    tok = lambda w: jax.ShapeDtypeStruct((b, s, w), BF16)
    tok_spec = lambda w: pl.BlockSpec((1, tm, w), lambda i, j: (i, j, 0))
    const = lambda shape: pl.BlockSpec(shape, lambda i, j: (0,) * len(shape))
    vdt_tiles = tm // V_TILE
    out_shape = (
        tok(GROUP_WIDTH),
        tok(GROUP_WIDTH), tok(GROUP_WIDTH),
        jax.ShapeDtypeStruct((b, ns * vdt_tiles, N_HEADS * VT_ROWS, V_TILE), BF16),
        tok(GROUP_WIDTH),
        tok(GROUP_WIDTH),
    )
    out_specs = (
        tok_spec(GROUP_WIDTH), tok_spec(GROUP_WIDTH), tok_spec(GROUP_WIDTH),
        pl.BlockSpec((1, vdt_tiles, N_HEADS * VT_ROWS, V_TILE), lambda i, j: (i, j, 0, 0)),
        tok_spec(GROUP_WIDTH), tok_spec(GROUP_WIDTH),
    )
    in_specs = [
        pl.BlockSpec((1, tm, D_MODEL), lambda i, j: (i, j, 0)),
        pl.BlockSpec((1, MEM_LEN, 2 * GROUP_WIDTH), lambda i, j: (i, 0, 0)),
        const((1, D_MODEL)),
        const((D_MODEL, IN_WIDTH)),
        const((CONV_WIDTH, GROUP_WIDTH)),
        const((1, GROUP_WIDTH)),
        const((N_HEADS, HEAD_DIM, 2 * HEAD_DIM)),
        const((N_HEADS, HEAD_DIM, HEAD_DIM)),
        const((3 * GROUP_WIDTH, HEAD_DIM)),
        const((1, HEAD_DIM)),
        const((1, GROUP_WIDTH)),
        const((1, GROUP_WIDTH)),
    ]
    tile_bf16 = pltpu.VMEM((tm, GROUP_WIDTH), BF16)
    return pl.pallas_call(
        _front_kernel,
        grid=(b, ns),
        in_specs=in_specs,
        out_specs=out_specs,
        out_shape=out_shape,
        scratch_shapes=[pltpu.VMEM((tm + CONV_HALO, GROUP_WIDTH), F32),
                        pltpu.VMEM((N_HEADS, HEAD_DIM, HEAD_DIM), F32),
                        pltpu.VMEM((N_HEADS, 1, HEAD_DIM), F32),
                        pltpu.VMEM((N_HEADS, 1, HEAD_DIM), F32),
                        tile_bf16, tile_bf16, tile_bf16,
                        pltpu.VMEM((tm, HEAD_DIM), F32),
                        tile_bf16, tile_bf16, tile_bf16],
        compiler_params=pltpu.CompilerParams(
            dimension_semantics=("arbitrary", "arbitrary"),
            vmem_limit_bytes=56 * 1024 * 1024),
        name="front",
    )(x, kv, norm_g, w_in, conv_w, conv_b, wqk, wv, w_if, b_if, mnorm_g, skip_m)


def _diff_kernel(q_ref, k_ref, vt_ref, zd_ref, g_ref, lq1_ref, lk1_ref, lq2_ref, lk2_ref,
                 x_ref, ym_ref, yc_ref, wo_ref, fg_ref, o_ref,
                 qs_ref, m_ref, l_ref, acc_ref, s0_ref, s1_ref, mx0_ref, mx1_ref, yd_ref, part_ref):
    tq, tk, tv = DIFF_Q_ROWS, DIFF_KV_ROWS, V_TILE
    i = pl.program_id(1)
    lane = lax.broadcasted_iota(jnp.int32, (tq, HEAD_DIM), 1)
    for hh in range(N_HEADS):
        q = q_ref[0, :, _head(hh)]
        zero = jnp.zeros_like(q)
        qs_ref[hh, 0:tq, :] = jnp.where(lane < DIFF_QK_DIM, q, zero)
        qs_ref[hh, tq:2 * tq, :] = jnp.where(lane >= DIFF_QK_DIM, q, zero)
    m_ref[...] = jnp.full_like(m_ref, -jnp.inf)
    l_ref[...] = jnp.zeros_like(l_ref)
    acc_ref[...] = jnp.zeros_like(acc_ref)

    kc = DIFF_KEY_CHUNK
    sub = HEAD_DIM
    tri = (lax.broadcasted_iota(jnp.int32, (kc, sub), 0) <= lax.broadcasted_iota(jnp.int32, (kc, sub), 1))

    def mask_diagonal(s_c, c):
        blocks, maxes = [], []
        for b8 in range(2 * tq // sub):
            b = b8 % (tq // sub)
            blk = s_c[:, b8 * sub:(b8 + 1) * sub]
            if c > b:
                blocks.append(jnp.full((kc, sub), -jnp.inf, F32))
                maxes.append(jnp.full((SUBLANES, sub), -jnp.inf, F32))
                continue
            if c == b:
                blk = jnp.where(tri, blk, -jnp.inf)
            blocks.append(blk)
            maxes.append(_fold_rows(blk, jnp.max))
        return jnp.concatenate(blocks, axis=1), jnp.concatenate(maxes, axis=1)

    group_a = (tuple(range(0, N_HEADS // 2)), s0_ref, mx0_ref)
    group_b = (tuple(range(N_HEADS // 2, N_HEADS)), s1_ref, mx1_ref)

    def step(consume=None, produce=None):
        n_slots = N_HEADS // 2
        for slot in range(n_slots):
            if consume is not None:
                t_c, (heads_c, sc_ref, mxc_ref) = consume
                h_c = heads_c[slot]
                m_old = m_ref[h_c]
                m_new = jnp.maximum(m_old, mxc_ref[slot])
                alpha = jnp.exp2(m_old - m_new)
                pv, pb_prev = None, None
            if produce is not None:
                t_p, (heads_p, sp_ref, mxp_ref), diagonal = produce
                h_p = heads_p[slot]
                mx = None
            for c in range(tk // kc):
                chunk = slice(c * kc, (c + 1) * kc)
                if produce is not None:
                    rows = pl.ds(pl.multiple_of(t_p * tk + c * kc, kc), kc)
                    s_c = _dot_nt(k_ref[0, rows, _head(h_p)], qs_ref[h_p])
                if consume is not None:
                    pb_c = jnp.exp2(sc_ref[slot, chunk, :] - m_new).astype(BF16)
                    pb_prev = pb_c if pb_prev is None else jnp.concatenate([pb_prev, pb_c], axis=0)
                    if c == tk // kc - 1:
                        v_rows = slice(h_c * VT_ROWS, (h_c + 1) * VT_ROWS)
                        vt = jnp.concatenate([vt_ref[0, t_c * (tk // tv) + j, v_rows, :]
                                              for j in range(tk // tv)], axis=1)
                        pv = _dot(vt, pb_prev)
                if produce is not None:
                    if diagonal:
                        s_c, c_max = mask_diagonal(s_c, c)
                    else:
                        c_max = _fold_rows(s_c, jnp.max)
                    sp_ref[slot, chunk, :] = s_c
                    mx = c_max if mx is None else jnp.maximum(mx, c_max)
            if consume is not None:
                l_ref[h_c] = alpha * l_ref[h_c] + pv[HEAD_DIM:HEAD_DIM + 1, :]
                acc_ref[h_c] = alpha * acc_ref[h_c] + pv[:HEAD_DIM, :]
                m_ref[h_c] = m_new
            if produce is not None:
                mxp_ref[slot] = jnp.max(mx, axis=0, keepdims=True)

    def tile(t, next_diagonal):
        step(consume=(t, group_a), produce=(t, group_b, False))
        step(consume=(t, group_b), produce=(t + 1, group_a, next_diagonal))

    def body(t, carry):
        tile(t, False)
        return carry

    @pl.when(i == 0)
    def _():
        step(produce=(0, group_a, True))

    @pl.when(i > 0)
    def _():
        step(produce=(0, group_a, False))
        lax.fori_loop(0, i - 1, body, 0)
        tile(i - 1, True)

    lam = (jnp.exp(jnp.sum(lq1_ref[...] * lk1_ref[...], axis=-1, keepdims=True))
           - jnp.exp(jnp.sum(lq2_ref[...] * lk2_ref[...], axis=-1, keepdims=True)) + LAM_INIT)

    def finish(heads):
        for hh in heads:
            sl = _head(hh)
            acc = acc_ref[hh]
            inv = 1.0 / l_ref[hh]
            o_t = acc[:, :tq] * inv[:, :tq] - lam * (acc[:, tq:] * inv[:, tq:])
            o = _rms(o_t.T, g_ref[:, sl]) * (1.0 - LAM_INIT)
            yd_ref[:, sl] = (o * _silu(zd_ref[0, :, sl].astype(F32))).astype(BF16)

    def project_rows(c, src_ref, w_rows, first):
        r_c = slice(c * HEAD_DIM, (c + 1) * HEAD_DIM)
        y = _dot(src_ref[0, r_c, :], wo_ref[w_rows, :])
        part_ref[r_c, :] = (x_ref[0, r_c, :] + y) if first else (part_ref[r_c, :] + y)

    def project_heads(heads):
        cols = slice(heads[0] * HEAD_DIM, (heads[-1] + 1) * HEAD_DIM)
        w_rows = slice(GROUP_WIDTH + cols.start, GROUP_WIDTH + cols.stop)
        return _dot(yd_ref[:, cols], wo_ref[w_rows, :])

    step(consume=(i, group_a), produce=(i, group_b, True))
    for c in range(tq // HEAD_DIM):
        project_rows(c, ym_ref, slice(0, GROUP_WIDTH), True)
    finish(group_a[0])
    for c in range(tq // HEAD_DIM):
        project_rows(c, yc_ref, slice(2 * GROUP_WIDTH, 3 * GROUP_WIDTH), False)
    y_a = project_heads(group_a[0])
    step(consume=(i, group_b))
    finish(group_b[0])
    o_ref[0] = _rms(part_ref[...] + y_a + project_heads(group_b[0]), fg_ref[...])


def _diff_out_call(qd, kd, vdt, zd, dnorm_g, lq1, lk1, lq2, lk2, x, ym, yc, w_out, final_g):
    b, s, _ = qd.shape
    tq = DIFF_Q_ROWS
    nkv = vdt.shape[1]
    const = lambda shape: pl.BlockSpec(shape, lambda i, j: (0,) * len(shape))
    tok_spec = lambda w: pl.BlockSpec((1, tq, w), lambda i, j: (i, j, 0))
    return pl.pallas_call(
        _diff_kernel,
        grid=(b, s // tq),
        in_specs=[
            tok_spec(GROUP_WIDTH),
            pl.BlockSpec((1, s, GROUP_WIDTH), lambda i, j: (i, 0, 0)),
            pl.BlockSpec((1, nkv, N_HEADS * VT_ROWS, V_TILE), lambda i, j: (i, 0, 0, 0)),
            tok_spec(GROUP_WIDTH),
            const((1, GROUP_WIDTH)),
            const((1, DIFF_QK_DIM)), const((1, DIFF_QK_DIM)), const((1, DIFF_QK_DIM)), const((1, DIFF_QK_DIM)),
            tok_spec(D_MODEL), tok_spec(GROUP_WIDTH), tok_spec(GROUP_WIDTH),
            const((3 * GROUP_WIDTH, D_MODEL)), const((1, D_MODEL)),
        ],
        out_specs=tok_spec(D_MODEL),
        out_shape=jax.ShapeDtypeStruct((b, s, D_MODEL), F32),
        scratch_shapes=[pltpu.VMEM((N_HEADS, 2 * tq, HEAD_DIM), BF16),
                        pltpu.VMEM((N_HEADS, 1, 2 * tq), F32),
                        pltpu.VMEM((N_HEADS, 1, 2 * tq), F32),
                        pltpu.VMEM((N_HEADS, HEAD_DIM, 2 * tq), F32),
                        pltpu.VMEM((N_HEADS // 2, DIFF_KV_ROWS, 2 * tq), F32),
                        pltpu.VMEM((N_HEADS // 2, DIFF_KV_ROWS, 2 * tq), F32),
                        pltpu.VMEM((N_HEADS // 2, 1, 2 * tq), F32),
                        pltpu.VMEM((N_HEADS // 2, 1, 2 * tq), F32),
                        pltpu.VMEM((tq, GROUP_WIDTH), BF16),
                        pltpu.VMEM((tq, D_MODEL), F32)],
        compiler_params=pltpu.CompilerParams(
            dimension_semantics=("arbitrary", "arbitrary"),
            vmem_limit_bytes=58 * 1024 * 1024),
        name="diffattn_out",
    )(qd, kd, vdt, zd, dnorm_g, lq1, lk1, lq2, lk2, x, ym, yc, w_out, final_g)


def kernel(x, mem, norm_g, w_in, conv_w, conv_b, wq_m, wk_m, wv_m, w_if, b_if, mnorm_g, skip_m,
           lam_q1, lam_k1, lam_q2, lam_k2, dnorm_g, mem_norm_g, w_mem_kv, w_out, final_g):
    b, s, d = x.shape
    assert (d, s % DIFF_KV_ROWS, DIFF_KV_ROWS % V_TILE, FRONT_ROWS % V_TILE) == (D_MODEL, 0, 0, 0)
    assert (s % FRONT_ROWS, V_TILE % DIFF_KEY_CHUNK) == (0, 0)
    assert DIFF_Q_ROWS == DIFF_KV_ROWS and DIFF_KEY_CHUNK == HEAD_DIM
    assert norm_g.shape[0] == 1, "single-layer kernel"
    l = 0
    w_if_pad = jnp.pad(w_if[l], ((0, 0), (0, HEAD_DIM - w_if.shape[-1]))).astype(BF16)
    b_if_pad = jnp.pad(b_if[l], (0, HEAD_DIM - b_if.shape[-1]))[None, :]

    kv = _memkv_call(mem, mem_norm_g[l][None, :], w_mem_kv[l].astype(BF16))
    ym, qd, kd, vdt, zd, yc = _front_call(
        x, kv, norm_g[l][None, :], w_in[l].astype(BF16), conv_w[l], conv_b[l][None, :],
        jnp.concatenate([wq_m[l], wk_m[l]], axis=-1).astype(BF16), wv_m[l].astype(BF16), w_if_pad, b_if_pad,
        mnorm_g[l][None, :], skip_m[l][None, :])
    return _diff_out_call(qd, kd, vdt, zd, dnorm_g[l][None, :],
                          lam_q1[l][None, :], lam_k1[l][None, :], lam_q2[l][None, :], lam_k2[l][None, :],
                          x, ym, yc, w_out[l].astype(BF16), final_g[None, :])
```

```python
import math

import jax
import jax.numpy as jnp
from jax import lax
from jax.experimental import pallas as pl
from jax.experimental.pallas import tpu as pltpu

F32 = jnp.float32
BF16 = jnp.bfloat16

D_MODEL = 1024
N_HEADS = 4
HEAD_DIM = 128
SUBLANES = 8
GROUP_WIDTH = N_HEADS * HEAD_DIM
DIFF_QK_DIM = 64
CONV_WIDTH = 4
MLSTM_CHUNK = 128
MLSTM_STEP_CHUNKS = 4
MEM_LEN = 256
IN_WIDTH = 9 * GROUP_WIDTH
NORM_EPS = 1e-6
LAM_INIT = 0.8 - 0.6 * math.exp(-0.3 * 0)
LOG2E = math.log2(math.e)

OFF_XM, OFF_OM, OFF_ZM, OFF_QD, OFF_KD, OFF_VD, OFF_ZD, OFF_QC, OFF_ZC = (
    i * GROUP_WIDTH for i in range(9))

FRONT_ROWS = MLSTM_STEP_CHUNKS * MLSTM_CHUNK
V_TILE = 256
VT_PAD = 16
VT_ROWS = HEAD_DIM + VT_PAD
DIFF_Q_ROWS = 512
DIFF_KV_ROWS = 512
DIFF_KEY_CHUNK = 128
CONV_HALO = 8

NT_DIMS = (((1,), (1,)), ((), ()))
TN_DIMS = (((0,), (0,)), ((), ()))


def _dot(a, b):
    return jnp.dot(a, b, preferred_element_type=F32)


def _dot_nt(a, b):
    return lax.dot_general(a, b, NT_DIMS, preferred_element_type=F32)


def _silu(x):
    return x * jax.nn.sigmoid(x)


def _rms(x, g):
    return x * lax.rsqrt(jnp.mean(x * x, axis=-1, keepdims=True) + NORM_EPS) * g


def _head(h):
    return slice(h * HEAD_DIM, (h + 1) * HEAD_DIM)


def _fold_rows(x, op):
    r, n = x.shape
    return op(x.reshape(r // SUBLANES, SUBLANES, n), axis=0)


def _log_sigmoid(x):
    return jnp.minimum(x, 0.0) - jnp.log1p(jnp.exp(-jnp.abs(x)))


def _cumsum_rows(f, tril):
    hi = f.astype(BF16)
    r1 = f - hi.astype(F32)
    mid = r1.astype(BF16)
    lo = (r1 - mid.astype(F32)).astype(BF16)
    return _dot(tril, hi) + _dot(tril, mid) + _dot(tril, lo)


def _memkv_kernel(mem_ref, g_ref, w_ref, kv_ref):
    y = _rms(mem_ref[0], g_ref[...])
    kv_ref[0] = _dot(y.astype(BF16), w_ref[...]).astype(BF16)


def _memkv_call(mem, g, w):
    b = mem.shape[0]
    return pl.pallas_call(
        _memkv_kernel,
        grid=(b,),
        in_specs=[
            pl.BlockSpec((1, MEM_LEN, D_MODEL), lambda i: (i, 0, 0)),
            pl.BlockSpec((1, D_MODEL), lambda i: (0, 0)),
            pl.BlockSpec((D_MODEL, 2 * GROUP_WIDTH), lambda i: (0, 0)),
        ],
        out_specs=pl.BlockSpec((1, MEM_LEN, 2 * GROUP_WIDTH), lambda i: (i, 0, 0)),
        out_shape=jax.ShapeDtypeStruct((b, MEM_LEN, 2 * GROUP_WIDTH), BF16),
        compiler_params=pltpu.CompilerParams(dimension_semantics=("arbitrary",)),
        name="memkv",
    )(mem, g, w)


def _front_kernel(x_ref, kv_ref, ng_ref, win_ref, cw_ref, cb_ref, wqk_ref, wv_ref, wif_ref, bif_ref,
                  mg_ref, sk_ref,
                  ym_ref, qd_ref, kd_ref, vdt_ref, zd_ref, yc_ref,
                  conv_ref, c_ref, n_ref, m_ref, q_s, k_s, v_s, g_s, om_s, zm_s, xcv_s):
    tm = FRONT_ROWS
    L = MLSTM_CHUNK
    heads = range(N_HEADS)

    @pl.when(pl.program_id(1) == 0)
    def _():
        conv_ref[tm:tm + CONV_HALO, :] = jnp.zeros((CONV_HALO, GROUP_WIDTH), F32)
        c_ref[...] = jnp.zeros_like(c_ref)
        n_ref[...] = jnp.zeros_like(n_ref)
        m_ref[...] = jnp.zeros_like(m_ref)

    h = _rms(x_ref[0], ng_ref[...]).astype(BF16)

    def proj(off):
        return _dot(h, win_ref[:, off:off + GROUP_WIDTH])

    def put_vdt(val):
        v_t = val.T.astype(BF16)
        pad_row = lax.broadcasted_iota(jnp.int32, (VT_PAD, tm), 0)
        pad = jnp.where(pad_row == 0, 1.0, 0.0).astype(BF16)
        v_aug = jnp.concatenate([blk for hh in heads for blk in (v_t[_head(hh), :], pad)], axis=0)
        for j in range(tm // V_TILE):
            vdt_ref[0, j] = v_aug[:, j * V_TILE:(j + 1) * V_TILE]

    def put(ref, scale=None):
        def sink(val):
            ref[...] = (val if scale is None else val * scale).astype(BF16).reshape(ref.shape)
        return sink

    side_pieces = [(OFF_OM, put(om_s)), (OFF_ZM, put(zm_s)),
                   (OFF_QD, put(qd_ref, DIFF_QK_DIM ** -0.5 * LOG2E)), (OFF_KD, put(kd_ref)),
                   (OFF_VD, put_vdt), (OFF_ZD, put(zd_ref))]
    pending = []

    def side_start():
        if side_pieces:
            off, sink = side_pieces.pop(0)
            pending.append((proj(off), sink))

    def side_finish():
        if pending:
            val, sink = pending.pop(0)
            sink(val)

    x_m = proj(OFF_XM)
    q_cb = proj(OFF_QC).astype(BF16)
    z_c = proj(OFF_ZC)

    conv_ref[0:CONV_HALO, :] = conv_ref[tm:tm + CONV_HALO, :]
    conv_ref[CONV_HALO:CONV_HALO + tm, :] = x_m
    acc = jnp.broadcast_to(cb_ref[...], (tm, GROUP_WIDTH))
    for j in range(CONV_WIDTH):
        start = CONV_HALO - (CONV_WIDTH - 1) + j
        acc = acc + cw_ref[j:j + 1, :] * conv_ref[start:start + tm, :]
    sc = [_dot_nt(q_cb[:, _head(hh)], kv_ref[0, :, _head(hh)]) * HEAD_DIM ** -0.5 for hh in heads]
    x_cv = _silu(acc)
    xcv_b = x_cv.astype(BF16)
    xcv_s[...] = xcv_b
    xm_b = x_m.astype(BF16)

    qk_p = [_dot(xcv_b[:, _head(hh)], wqk_ref[hh]) for hh in heads]
    vv_p = [_dot(xm_b[:, _head(hh)], wv_ref[hh]) for hh in heads]

    q_all = jnp.concatenate([qk_p[hh][:, :HEAD_DIM] for hh in heads], axis=1).astype(BF16)
    k_all = jnp.concatenate([qk_p[hh][:, HEAD_DIM:] for hh in heads], axis=1)
    v_all = jnp.concatenate(vv_p, axis=1).astype(BF16)
    q_s[...] = q_all
    k_s[...] = (k_all * HEAD_DIM ** -0.5).astype(BF16)
    v_s[...] = v_all
    g_s[...] = (jnp.broadcast_to(bif_ref[...], (tm, HEAD_DIM))
                + _dot(q_all, wif_ref[0:GROUP_WIDTH, :])
                + _dot(k_all.astype(BF16), wif_ref[GROUP_WIDTH:2 * GROUP_WIDTH, :])
                + _dot(v_all, wif_ref[2 * GROUP_WIDTH:3 * GROUP_WIDTH, :]))

    mx = [jnp.max(sc[hh], axis=-1, keepdims=True) for hh in heads]
    pm = [jnp.exp(sc[hh] - mx[hh]) for hh in heads]
    inv = [1.0 / jnp.sum(pm[hh], axis=-1, keepdims=True) for hh in heads]
    pc = [(pm[hh] * inv[hh]).astype(BF16) for hh in heads]
    oc = [_dot(pc[hh], kv_ref[0, :, GROUP_WIDTH + hh * HEAD_DIM:GROUP_WIDTH + (hh + 1) * HEAD_DIM]) for hh in heads]
    for hh in heads:
        yc_ref[0, :, _head(hh)] = (oc[hh] * _silu(z_c[:, _head(hh)])).astype(BF16)

    row = lax.broadcasted_iota(jnp.int32, (L, L), 0)
    col = lax.broadcasted_iota(jnp.int32, (L, L), 1)
    causal = col <= row
    tril = jnp.where(causal, 1.0, 0.0).astype(BF16)
    chunks = range(MLSTM_STEP_CHUNKS)
    probs = [(c, hh) for c in chunks for hh in heads]
    rows = lambda c: slice(c * L, (c + 1) * L)
    q_of = lambda p: q_s[rows(p[0]), _head(p[1])]
    k_of = lambda p: k_s[rows(p[0]), _head(p[1])]
    v_of = lambda p: v_s[rows(p[0]), _head(p[1])]

    def each(f, side=False):
        if side:
            side_start()
        out = {p: f(p) for p in probs}
        if side:
            side_finish()
        return out

    a_cols, a_rows = [], []
    for c in chunks:
        g = g_s[rows(c), :]
        bb_all = _cumsum_rows(_log_sigmoid(g), tril)
        ac = jnp.where(col < N_HEADS, g, bb_all)
        a_cols.append(ac)
        a_rows.append(ac.T)
    ib_col = {(c, hh): a_cols[c][:, hh:hh + 1] for c, hh in probs}
    bb_col = {(c, hh): a_cols[c][:, N_HEADS + hh:N_HEADS + hh + 1] for c, hh in probs}
    ib_row = {(c, hh): a_rows[c][hh:hh + 1, :] for c, hh in probs}
    bb_row = {(c, hh): a_rows[c][N_HEADS + hh:N_HEADS + hh + 1, :] for c, hh in probs}
    b_end = {p: bb_col[p][L - 1:L, :] for p in probs}

    qk = {p: _dot_nt(q_of(p), k_of(p)) for p in probs}

    d = each(lambda p: jnp.where(causal, (bb_col[p] - bb_row[p]) + ib_row[p], -jnp.inf), side=True)
    r = each(lambda p: jnp.max(d[p], axis=-1, keepdims=True))
    s_loc = each(lambda p: qk[p] * jnp.exp(d[p] - r[p]), side=True)
    rs = each(lambda p: jnp.sum(s_loc[p], axis=-1, keepdims=True))
    s_bf = each(lambda p: s_loc[p].astype(BF16))
    a = each(lambda p: (b_end[p] - bb_col[p]) + ib_col[p], side=True)
    a_max = each(lambda p: jnp.max(a[p], axis=0, keepdims=True))
    w_loc = each(lambda p: jnp.exp(a[p] - a_max[p]))
    vw = each(lambda p: (v_of(p).astype(F32) * w_loc[p]).astype(BF16), side=True)
    nu = each(lambda p: jnp.sum(k_of(p).astype(F32) * w_loc[p], axis=0, keepdims=True))

    sv = {p: _dot(s_bf[p], v_of(p)) for p in probs}
    u = {p: lax.dot_general(vw[p], k_of(p), TN_DIMS, preferred_element_type=F32) for p in probs}

    m_in, c_in, n_in = {}, {}, {}
    for hh in heads:
        m_cur = m_ref[hh][:, 0:1]
        c_cur = c_ref[hh]
        n_cur = n_ref[hh]
        for c in chunks:
            p = (c, hh)
            m_in[p], c_in[p], n_in[p] = m_cur, c_cur.astype(BF16), n_cur
            m_new = jnp.maximum(b_end[p] + m_cur, a_max[p])
            decay = jnp.exp(b_end[p] + m_cur - m_new)
            e_upd = jnp.exp(a_max[p] - m_new)
            c_cur = decay * c_cur + e_upd * u[p]
            n_cur = decay * n_cur + e_upd * nu[p]
            m_cur = m_new
        c_ref[hh] = c_cur
        n_ref[hh] = n_cur
        m_ref[hh] = jnp.broadcast_to(m_cur, (1, HEAD_DIM))

    q_c = {p: _dot_nt(q_of(p), c_in[p]) for p in probs}

    tile_of = lambda ref, p: ref[rows(p[0]), _head(p[1])].astype(F32)
    g_col = each(lambda p: bb_col[p] + m_in[p])
    m_t = each(lambda p: jnp.maximum(g_col[p], r[p]))
    e_loc = each(lambda p: jnp.exp(r[p] - m_t[p]))
    inter = each(lambda p: jnp.exp(g_col[p] - m_t[p]))
    q_n = each(lambda p: jnp.sum(q_of(p).astype(F32) * n_in[p], axis=-1, keepdims=True))
    den = each(lambda p: e_loc[p] * rs[p] + inter[p] * q_n[p])
    scale = each(lambda p: 1.0 / jnp.maximum(jnp.abs(den[p]), jnp.exp(-m_t[p])))
    hg = each(lambda p: jax.nn.sigmoid(tile_of(om_s, p))
              * ((e_loc[p] * sv[p] + inter[p] * q_c[p]) * scale[p]), side=True)
    mu = each(lambda p: jnp.mean(hg[p], axis=-1, keepdims=True))
    cen = each(lambda p: hg[p] - mu[p], side=True)
    var = each(lambda p: jnp.mean(jnp.square(cen[p]), axis=-1, keepdims=True))
    assert not side_pieces and not pending
    for p in probs:
        sl = _head(p[1])
        y = cen[p] * lax.rsqrt(var[p] + NORM_EPS) * mg_ref[:, sl]
        y = (y + sk_ref[:, sl] * tile_of(xcv_s, p)) * _silu(tile_of(zm_s, p))
        ym_ref[0, rows(p[0]), sl] = y.astype(BF16)


def _front_call(x, kv, norm_g, w_in, conv_w, conv_b, wqk, wv, w_if, b_if, mnorm_g, skip_m):
    b, s, _ = x.shape
    tm = FRONT_ROWS
    ns = s // tm
    tok = lambda w: jax.ShapeDtypeStruct((b, s, w), BF16)
    tok_spec = lambda w: pl.BlockSpec((1, tm, w), lambda i, j: (i, j, 0))
    const = lambda shape: pl.BlockSpec(shape, lambda i, j: (0,) * len(shape))
    vdt_tiles = tm // V_TILE
    out_shape = (
        tok(GROUP_WIDTH),
        tok(GROUP_WIDTH), tok(GROUP_WIDTH),
        jax.ShapeDtypeStruct((b, ns * vdt_tiles, N_HEADS * VT_ROWS, V_TILE), BF16),
        tok(GROUP_WIDTH),
        tok(GROUP_WIDTH),
    )
    out_specs = (
        tok_spec(GROUP_WIDTH), tok_spec(GROUP_WIDTH), tok_spec(GROUP_WIDTH),
        pl.BlockSpec((1, vdt_tiles, N_HEADS * VT_ROWS, V_TILE), lambda i, j: (i, j, 0, 0)),
        tok_spec(GROUP_WIDTH), tok_spec(GROUP_WIDTH),
    )
    in_specs = [
        pl.BlockSpec((1, tm, D_MODEL), lambda i, j: (i, j, 0)),
        pl.BlockSpec((1, MEM_LEN, 2 * GROUP_WIDTH), lambda i, j: (i, 0, 0)),
        const((1, D_MODEL)),
        const((D_MODEL, IN_WIDTH)),
        const((CONV_WIDTH, GROUP_WIDTH)),
        const((1, GROUP_WIDTH)),
        const((N_HEADS, HEAD_DIM, 2 * HEAD_DIM)),
        const((N_HEADS, HEAD_DIM, HEAD_DIM)),
        const((3 * GROUP_WIDTH, HEAD_DIM)),
        const((1, HEAD_DIM)),
        const((1, GROUP_WIDTH)),
        const((1, GROUP_WIDTH)),
    ]
    tile_bf16 = pltpu.VMEM((tm, GROUP_WIDTH), BF16)
    return pl.pallas_call(
        _front_kernel,
        grid=(b, ns),
        in_specs=in_specs,
        out_specs=out_specs,
        out_shape=out_shape,
        scratch_shapes=[pltpu.VMEM((tm + CONV_HALO, GROUP_WIDTH), F32),
                        pltpu.VMEM((N_HEADS, HEAD_DIM, HEAD_DIM), F32),
                        pltpu.VMEM((N_HEADS, 1, HEAD_DIM), F32),
                        pltpu.VMEM((N_HEADS, 1, HEAD_DIM), F32),
                        tile_bf16, tile_bf16, tile_bf16,
                        pltpu.VMEM((tm, HEAD_DIM), F32),
                        tile_bf16, tile_bf16, tile_bf16],
        compiler_params=pltpu.CompilerParams(
            dimension_semantics=("arbitrary", "arbitrary"),
            vmem_limit_bytes=56 * 1024 * 1024),
        name="front",
    )(x, kv, norm_g, w_in, conv_w, conv_b, wqk, wv, w_if, b_if, mnorm_g, skip_m)


def _diff_kernel(q_ref, k_ref, vt_ref, zd_ref, g_ref, lq1_ref, lk1_ref, lq2_ref, lk2_ref,
                 x_ref, ym_ref, yc_ref, wo_ref, fg_ref, o_ref,
                 qs_ref, m_ref, l_ref, acc_ref, s0_ref, s1_ref, mx0_ref, mx1_ref, yd_ref, part_ref):
    tq, tk, tv = DIFF_Q_ROWS, DIFF_KV_ROWS, V_TILE
    i = pl.program_id(1)
    lane = lax.broadcasted_iota(jnp.int32, (tq, HEAD_DIM), 1)
    for hh in range(N_HEADS):
        q = q_ref[0, :, _head(hh)]
        zero = jnp.zeros_like(q)
        qs_ref[hh, 0:tq, :] = jnp.where(lane < DIFF_QK_DIM, q, zero)
        qs_ref[hh, tq:2 * tq, :] = jnp.where(lane >= DIFF_QK_DIM, q, zero)
    m_ref[...] = jnp.full_like(m_ref, -jnp.inf)
    l_ref[...] = jnp.zeros_like(l_ref)
    acc_ref[...] = jnp.zeros_like(acc_ref)

    kc = DIFF_KEY_CHUNK
    sub = HEAD_DIM
    tri = (lax.broadcasted_iota(jnp.int32, (kc, sub), 0) <= lax.broadcasted_iota(jnp.int32, (kc, sub), 1))

    def diagonal_dots(k_rows, h, c):
        return [_dot_nt(k_rows, qs_ref[h, m * tq + c * sub:(m + 1) * tq, :]) for m in range(2)]

    def diagonal_scores(seen_maps, c):
        blocks, maxes = [], []
        for seen in seen_maps:
            for b in range(tq // sub):
                if b < c:
                    blocks.append(jnp.full((kc, sub), -jnp.inf, F32))
                    maxes.append(jnp.full((SUBLANES, sub), -jnp.inf, F32))
                    continue
                blk = seen[:, (b - c) * sub:(b - c + 1) * sub]
                if b == c:
                    blk = jnp.where(tri, blk, -jnp.inf)
                blocks.append(blk)
                maxes.append(_fold_rows(blk, jnp.max))
        return jnp.concatenate(blocks, axis=1), jnp.concatenate(maxes, axis=1)

    group_a = (tuple(range(0, N_HEADS // 2)), s0_ref, mx0_ref)
    group_b = (tuple(range(N_HEADS // 2, N_HEADS)), s1_ref, mx1_ref)

    def step(consume=None, produce=None):
        n_slots = N_HEADS // 2
        for slot in range(n_slots):
            if consume is not None:
                t_c, (heads_c, sc_ref, mxc_ref) = consume
                h_c = heads_c[slot]
                m_old = m_ref[h_c]
                m_new = jnp.maximum(m_old, mxc_ref[slot])
                alpha = jnp.exp2(m_old - m_new)
                pv, pb_prev = None, None
            if produce is not None:
                t_p, (heads_p, sp_ref, mxp_ref), diagonal = produce
                h_p = heads_p[slot]
                mx = None
            for c in range(tk // kc):
                chunk = slice(c * kc, (c + 1) * kc)
                if produce is not None:
                    rows = pl.ds(pl.multiple_of(t_p * tk + c * kc, kc), kc)
                    k_rows = k_ref[0, rows, _head(h_p)]
                    if diagonal:
                        seen = diagonal_dots(k_rows, h_p, c)
                    else:
                        s_c = _dot_nt(k_rows, qs_ref[h_p])
                if consume is not None:
                    pb_c = jnp.exp2(sc_ref[slot, chunk, :] - m_new).astype(BF16)
                    pb_prev = pb_c if pb_prev is None else jnp.concatenate([pb_prev, pb_c], axis=0)
                    if c == tk // kc - 1:
                        v_rows = slice(h_c * VT_ROWS, (h_c + 1) * VT_ROWS)
                        vt = jnp.concatenate([vt_ref[0, t_c * (tk // tv) + j, v_rows, :]
                                              for j in range(tk // tv)], axis=1)
                        pv = _dot(vt, pb_prev)
                if produce is not None:
                    if diagonal:
                        s_c, c_max = diagonal_scores(seen, c)
                    else:
                        c_max = _fold_rows(s_c, jnp.max)
                    sp_ref[slot, chunk, :] = s_c
                    mx = c_max if mx is None else jnp.maximum(mx, c_max)
            if consume is not None:
                l_ref[h_c] = alpha * l_ref[h_c] + pv[HEAD_DIM:HEAD_DIM + 1, :]
                acc_ref[h_c] = alpha * acc_ref[h_c] + pv[:HEAD_DIM, :]
                m_ref[h_c] = m_new
            if produce is not None:
                mxp_ref[slot] = jnp.max(mx, axis=0, keepdims=True)

    def tile(t, next_diagonal):
        step(consume=(t, group_a), produce=(t, group_b, False))
        step(consume=(t, group_b), produce=(t + 1, group_a, next_diagonal))

    def body(t, carry):
        tile(t, False)
        return carry

    @pl.when(i == 0)
    def _():
        step(produce=(0, group_a, True))

    @pl.when(i > 0)
    def _():
        step(produce=(0, group_a, False))
        lax.fori_loop(0, i - 1, body, 0)
        tile(i - 1, True)

    lam = (jnp.exp(jnp.sum(lq1_ref[...] * lk1_ref[...], axis=-1, keepdims=True))
           - jnp.exp(jnp.sum(lq2_ref[...] * lk2_ref[...], axis=-1, keepdims=True)) + LAM_INIT)

    def finish(heads):
        for hh in heads:
            sl = _head(hh)
            acc = acc_ref[hh]
            inv = 1.0 / l_ref[hh]
            o_t = acc[:, :tq] * inv[:, :tq] - lam * (acc[:, tq:] * inv[:, tq:])
            o = _rms(o_t.T, g_ref[:, sl]) * (1.0 - LAM_INIT)
            yd_ref[:, sl] = (o * _silu(zd_ref[0, :, sl].astype(F32))).astype(BF16)

    def project_rows(c, src_ref, w_rows, first):
        r_c = slice(c * HEAD_DIM, (c + 1) * HEAD_DIM)
        y = _dot(src_ref[0, r_c, :], wo_ref[w_rows, :])
        part_ref[r_c, :] = (x_ref[0, r_c, :] + y) if first else (part_ref[r_c, :] + y)

    def project_heads(heads):
        cols = slice(heads[0] * HEAD_DIM, (heads[-1] + 1) * HEAD_DIM)
        w_rows = slice(GROUP_WIDTH + cols.start, GROUP_WIDTH + cols.stop)
        return _dot(yd_ref[:, cols], wo_ref[w_rows, :])

    step(consume=(i, group_a), produce=(i, group_b, True))
    for c in range(tq // HEAD_DIM):
        project_rows(c, ym_ref, slice(0, GROUP_WIDTH), True)
    finish(group_a[0])
    for c in range(tq // HEAD_DIM):
        project_rows(c, yc_ref, slice(2 * GROUP_WIDTH, 3 * GROUP_WIDTH), False)
    y_a = project_heads(group_a[0])
    step(consume=(i, group_b))
    finish(group_b[0])
    o_ref[0] = _rms(part_ref[...] + y_a + project_heads(group_b[0]), fg_ref[...])


def _diff_out_call(qd, kd, vdt, zd, dnorm_g, lq1, lk1, lq2, lk2, x, ym, yc, w_out, final_g):
    b, s, _ = qd.shape
    tq = DIFF_Q_ROWS
    nkv = vdt.shape[1]
    const = lambda shape: pl.BlockSpec(shape, lambda i, j: (0,) * len(shape))
    tok_spec = lambda w: pl.BlockSpec((1, tq, w), lambda i, j: (i, j, 0))
    return pl.pallas_call(
        _diff_kernel,
        grid=(b, s // tq),
        in_specs=[
            tok_spec(GROUP_WIDTH),
            pl.BlockSpec((1, s, GROUP_WIDTH), lambda i, j: (i, 0, 0)),
            pl.BlockSpec((1, nkv, N_HEADS * VT_ROWS, V_TILE), lambda i, j: (i, 0, 0, 0)),
            tok_spec(GROUP_WIDTH),
            const((1, GROUP_WIDTH)),
            const((1, DIFF_QK_DIM)), const((1, DIFF_QK_DIM)), const((1, DIFF_QK_DIM)), const((1, DIFF_QK_DIM)),
            tok_spec(D_MODEL), tok_spec(GROUP_WIDTH), tok_spec(GROUP_WIDTH),
            const((3 * GROUP_WIDTH, D_MODEL)), const((1, D_MODEL)),
        ],
        out_specs=tok_spec(D_MODEL),
        out_shape=jax.ShapeDtypeStruct((b, s, D_MODEL), F32),
        scratch_shapes=[pltpu.VMEM((N_HEADS, 2 * tq, HEAD_DIM), BF16),
                        pltpu.VMEM((N_HEADS, 1, 2 * tq), F32),
                        pltpu.VMEM((N_HEADS, 1, 2 * tq), F32),
                        pltpu.VMEM((N_HEADS, HEAD_DIM, 2 * tq), F32),
                        pltpu.VMEM((N_HEADS // 2, DIFF_KV_ROWS, 2 * tq), F32),
                        pltpu.VMEM((N_HEADS // 2, DIFF_KV_ROWS, 2 * tq), F32),
                        pltpu.VMEM((N_HEADS // 2, 1, 2 * tq), F32),
                        pltpu.VMEM((N_HEADS // 2, 1, 2 * tq), F32),
                        pltpu.VMEM((tq, GROUP_WIDTH), BF16),
                        pltpu.VMEM((tq, D_MODEL), F32)],
        compiler_params=pltpu.CompilerParams(
            dimension_semantics=("arbitrary", "arbitrary"),
            vmem_limit_bytes=58 * 1024 * 1024),
        name="diffattn_out",
    )(qd, kd, vdt, zd, dnorm_g, lq1, lk1, lq2, lk2, x, ym, yc, w_out, final_g)


def kernel(x, mem, norm_g, w_in, conv_w, conv_b, wq_m, wk_m, wv_m, w_if, b_if, mnorm_g, skip_m,
           lam_q1, lam_k1, lam_q2, lam_k2, dnorm_g, mem_norm_g, w_mem_kv, w_out, final_g):
    b, s, d = x.shape
    assert (d, s % DIFF_KV_ROWS, DIFF_KV_ROWS % V_TILE, FRONT_ROWS % V_TILE) == (D_MODEL, 0, 0, 0)
    assert (s % FRONT_ROWS, V_TILE % DIFF_KEY_CHUNK) == (0, 0)
    assert DIFF_Q_ROWS == DIFF_KV_ROWS and DIFF_KEY_CHUNK == HEAD_DIM
    assert norm_g.shape[0] == 1, "single-layer kernel"
    l = 0
    w_if_pad = jnp.pad(w_if[l], ((0, 0), (0, HEAD_DIM - w_if.shape[-1]))).astype(BF16)
    b_if_pad = jnp.pad(b_if[l], (0, HEAD_DIM - b_if.shape[-1]))[None, :]

    kv = _memkv_call(mem, mem_norm_g[l][None, :], w_mem_kv[l].astype(BF16))
    ym, qd, kd, vdt, zd, yc = _front_call(
        x, kv, norm_g[l][None, :], w_in[l].astype(BF16), conv_w[l], conv_b[l][None, :],
        jnp.concatenate([wq_m[l], wk_m[l]], axis=-1).astype(BF16), wv_m[l].astype(BF16), w_if_pad, b_if_pad,
        mnorm_g[l][None, :], skip_m[l][None, :])
    return _diff_out_call(qd, kd, vdt, zd, dnorm_g[l][None, :],
                          lam_q1[l][None, :], lam_k1[l][None, :], lam_q2[l][None, :], lam_k2[l][None, :],
                          x, ym, yc, w_out[l].astype(BF16), final_g[None, :])
```

```python
import math

import jax
import jax.numpy as jnp
from jax import lax
from jax.experimental import pallas as pl
from jax.experimental.pallas import tpu as pltpu

F32 = jnp.float32
BF16 = jnp.bfloat16

D_MODEL = 1024
N_HEADS = 4
HEAD_DIM = 128
SUBLANES = 8
GROUP_WIDTH = N_HEADS * HEAD_DIM
DIFF_QK_DIM = 64
CONV_WIDTH = 4
MLSTM_CHUNK = 128
MLSTM_STEP_CHUNKS = 4
MEM_LEN = 256
IN_WIDTH = 9 * GROUP_WIDTH
NORM_EPS = 1e-6
LAM_INIT = 0.8 - 0.6 * math.exp(-0.3 * 0)
LOG2E = math.log2(math.e)

OFF_XM, OFF_OM, OFF_ZM, OFF_QD, OFF_KD, OFF_VD, OFF_ZD, OFF_QC, OFF_ZC = (
    i * GROUP_WIDTH for i in range(9))

FRONT_ROWS = MLSTM_STEP_CHUNKS * MLSTM_CHUNK
V_TILE = 256
VT_PAD = 16
VT_ROWS = HEAD_DIM + VT_PAD
DIFF_Q_ROWS = 512
DIFF_KV_ROWS = 512
DIFF_KEY_CHUNK = 128
CONV_HALO = 8

NT_DIMS = (((1,), (1,)), ((), ()))
TN_DIMS = (((0,), (0,)), ((), ()))


def _dot(a, b):
    return jnp.dot(a, b, preferred_element_type=F32)


def _dot_nt(a, b):
    return lax.dot_general(a, b, NT_DIMS, preferred_element_type=F32)


def _silu(x):
    return x * jax.nn.sigmoid(x)


def _rms(x, g):
    return x * lax.rsqrt(jnp.mean(x * x, axis=-1, keepdims=True) + NORM_EPS) * g


def _head(h):
    return slice(h * HEAD_DIM, (h + 1) * HEAD_DIM)


def _fold_rows(x, op):
    r, n = x.shape
    return op(x.reshape(r // SUBLANES, SUBLANES, n), axis=0)


def _log_sigmoid(x):
    return jnp.minimum(x, 0.0) - jnp.log1p(jnp.exp(-jnp.abs(x)))


def _cumsum_rows(f, tril):
    hi = f.astype(BF16)
    r1 = f - hi.astype(F32)
    mid = r1.astype(BF16)
    lo = (r1 - mid.astype(F32)).astype(BF16)
    return _dot(tril, hi) + _dot(tril, mid) + _dot(tril, lo)


def _memkv_kernel(mem_ref, g_ref, w_ref, kv_ref):
    y = _rms(mem_ref[0], g_ref[...])
    kv_ref[0] = _dot(y.astype(BF16), w_ref[...]).astype(BF16)


def _memkv_call(mem, g, w):
    b = mem.shape[0]
    return pl.pallas_call(
        _memkv_kernel,
        grid=(b,),
        in_specs=[
            pl.BlockSpec((1, MEM_LEN, D_MODEL), lambda i: (i, 0, 0)),
            pl.BlockSpec((1, D_MODEL), lambda i: (0, 0)),
            pl.BlockSpec((D_MODEL, 2 * GROUP_WIDTH), lambda i: (0, 0)),
        ],
        out_specs=pl.BlockSpec((1, MEM_LEN, 2 * GROUP_WIDTH), lambda i: (i, 0, 0)),
        out_shape=jax.ShapeDtypeStruct((b, MEM_LEN, 2 * GROUP_WIDTH), BF16),
        compiler_params=pltpu.CompilerParams(dimension_semantics=("arbitrary",)),
        name="memkv",
    )(mem, g, w)


def _front_kernel(x_ref, kv_ref, ng_ref, win_ref, cw_ref, cb_ref, wqk_ref, wv_ref, wif_ref, bif_ref,
                  mg_ref, sk_ref,
                  ym_ref, qd_ref, kd_ref, vdt_ref, zd_ref, yc_ref,
                  conv_ref, c_ref, n_ref, m_ref, q_s, k_s, v_s, g_s, om_s, zm_s, xcv_s):
    tm = FRONT_ROWS
    L = MLSTM_CHUNK
    heads = range(N_HEADS)

    @pl.when(pl.program_id(1) == 0)
    def _():
        conv_ref[tm:tm + CONV_HALO, :] = jnp.zeros((CONV_HALO, GROUP_WIDTH), F32)
        c_ref[...] = jnp.zeros_like(c_ref)
        n_ref[...] = jnp.zeros_like(n_ref)
        m_ref[...] = jnp.zeros_like(m_ref)

    h = _rms(x_ref[0], ng_ref[...]).astype(BF16)

    def proj(off):
        return _dot(h, win_ref[:, off:off + GROUP_WIDTH])

    def put_vdt(val):
        v_t = val.T.astype(BF16)
        pad_row = lax.broadcasted_iota(jnp.int32, (VT_PAD, tm), 0)
        pad = jnp.where(pad_row == 0, 1.0, 0.0).astype(BF16)
        v_aug = jnp.concatenate([blk for hh in heads for blk in (v_t[_head(hh), :], pad)], axis=0)
        for j in range(tm // V_TILE):
            vdt_ref[0, j] = v_aug[:, j * V_TILE:(j + 1) * V_TILE]

    def put(ref, scale=None):
        def sink(val):
            ref[...] = (val if scale is None else val * scale).astype(BF16).reshape(ref.shape)
        return sink

    side_pieces = [(OFF_OM, put(om_s)), (OFF_ZM, put(zm_s)),
                   (OFF_QD, put(qd_ref, DIFF_QK_DIM ** -0.5 * LOG2E)), (OFF_KD, put(kd_ref)),
                   (OFF_VD, put_vdt), (OFF_ZD, put(zd_ref))]
    pending = []

    def side_start():
        if side_pieces:
            off, sink = side_pieces.pop(0)
            pending.append((proj(off), sink))

    def side_finish():
        if pending:
            val, sink = pending.pop(0)
            sink(val)

    x_m = proj(OFF_XM)
    q_cb = proj(OFF_QC).astype(BF16)
    z_c = proj(OFF_ZC)

    conv_ref[0:CONV_HALO, :] = conv_ref[tm:tm + CONV_HALO, :]
    conv_ref[CONV_HALO:CONV_HALO + tm, :] = x_m
    acc = jnp.broadcast_to(cb_ref[...], (tm, GROUP_WIDTH))
    for j in range(CONV_WIDTH):
        start = CONV_HALO - (CONV_WIDTH - 1) + j
        acc = acc + cw_ref[j:j + 1, :] * conv_ref[start:start + tm, :]
    sc = [_dot_nt(q_cb[:, _head(hh)], kv_ref[0, :, _head(hh)]) * HEAD_DIM ** -0.5 for hh in heads]
    x_cv = _silu(acc)
    xcv_b = x_cv.astype(BF16)
    xcv_s[...] = xcv_b
    xm_b = x_m.astype(BF16)

    qk_p = [_dot(xcv_b[:, _head(hh)], wqk_ref[hh]) for hh in heads]
    vv_p = [_dot(xm_b[:, _head(hh)], wv_ref[hh]) for hh in heads]

    q_all = jnp.concatenate([qk_p[hh][:, :HEAD_DIM] for hh in heads], axis=1).astype(BF16)
    k_all = jnp.concatenate([qk_p[hh][:, HEAD_DIM:] for hh in heads], axis=1)
    v_all = jnp.concatenate(vv_p, axis=1).astype(BF16)
    q_s[...] = q_all
    k_s[...] = (k_all * HEAD_DIM ** -0.5).astype(BF16)
    v_s[...] = v_all
    g_s[...] = (jnp.broadcast_to(bif_ref[...], (tm, HEAD_DIM))
                + _dot(q_all, wif_ref[0:GROUP_WIDTH, :])
                + _dot(k_all.astype(BF16), wif_ref[GROUP_WIDTH:2 * GROUP_WIDTH, :])
                + _dot(v_all, wif_ref[2 * GROUP_WIDTH:3 * GROUP_WIDTH, :]))

    mx = [jnp.max(sc[hh], axis=-1, keepdims=True) for hh in heads]
    pm = [jnp.exp(sc[hh] - mx[hh]) for hh in heads]
    inv = [1.0 / jnp.sum(pm[hh], axis=-1, keepdims=True) for hh in heads]
    pc = [(pm[hh] * inv[hh]).astype(BF16) for hh in heads]
    oc = [_dot(pc[hh], kv_ref[0, :, GROUP_WIDTH + hh * HEAD_DIM:GROUP_WIDTH + (hh + 1) * HEAD_DIM]) for hh in heads]
    for hh in heads:
        yc_ref[0, :, _head(hh)] = (oc[hh] * _silu(z_c[:, _head(hh)])).astype(BF16)

    row = lax.broadcasted_iota(jnp.int32, (L, L), 0)
    col = lax.broadcasted_iota(jnp.int32, (L, L), 1)
    causal = col <= row
    tril = jnp.where(causal, 1.0, 0.0).astype(BF16)
    chunks = range(MLSTM_STEP_CHUNKS)
    probs = [(c, hh) for c in chunks for hh in heads]
    rows = lambda c: slice(c * L, (c + 1) * L)
    q_of = lambda p: q_s[rows(p[0]), _head(p[1])]
    k_of = lambda p: k_s[rows(p[0]), _head(p[1])]
    v_of = lambda p: v_s[rows(p[0]), _head(p[1])]

    def each(f, side=False):
        if side:
            side_start()
        out = {p: f(p) for p in probs}
        if side:
            side_finish()
        return out

    a_cols, a_rows = [], []
    for c in chunks:
        g = g_s[rows(c), :]
        bb_all = _cumsum_rows(_log_sigmoid(g), tril)
        ac = jnp.where(col < N_HEADS, g, bb_all)
        a_cols.append(ac)
        a_rows.append(ac.T)
    ib_col = {(c, hh): a_cols[c][:, hh:hh + 1] for c, hh in probs}
    bb_col = {(c, hh): a_cols[c][:, N_HEADS + hh:N_HEADS + hh + 1] for c, hh in probs}
    ib_row = {(c, hh): a_rows[c][hh:hh + 1, :] for c, hh in probs}
    bb_row = {(c, hh): a_rows[c][N_HEADS + hh:N_HEADS + hh + 1, :] for c, hh in probs}
    b_end = {p: bb_col[p][L - 1:L, :] for p in probs}

    qk = {p: _dot_nt(q_of(p), k_of(p)) for p in probs}

    d = each(lambda p: jnp.where(causal, (bb_col[p] - bb_row[p]) + ib_row[p], -jnp.inf), side=True)
    r = each(lambda p: jnp.max(d[p], axis=-1, keepdims=True))
    s_loc = each(lambda p: qk[p] * jnp.exp(d[p] - r[p]), side=True)
    rs = each(lambda p: jnp.sum(s_loc[p], axis=-1, keepdims=True))
    s_bf = each(lambda p: s_loc[p].astype(BF16))
    a = each(lambda p: (b_end[p] - bb_col[p]) + ib_col[p], side=True)
    a_max = each(lambda p: jnp.max(a[p], axis=0, keepdims=True))
    w_loc = each(lambda p: jnp.exp(a[p] - a_max[p]))
    vw = each(lambda p: (v_of(p).astype(F32) * w_loc[p]).astype(BF16), side=True)
    nu = each(lambda p: jnp.sum(k_of(p).astype(F32) * w_loc[p], axis=0, keepdims=True))

    sv = {p: _dot(s_bf[p], v_of(p)) for p in probs}
    u = {p: lax.dot_general(vw[p], k_of(p), TN_DIMS, preferred_element_type=F32) for p in probs}

    m_in, c_in, n_in = {}, {}, {}
    for hh in heads:
        m_cur = m_ref[hh][:, 0:1]
        c_cur = c_ref[hh]
        n_cur = n_ref[hh]
        for c in chunks:
            p = (c, hh)
            m_in[p], c_in[p], n_in[p] = m_cur, c_cur.astype(BF16), n_cur
            m_new = jnp.maximum(b_end[p] + m_cur, a_max[p])
            decay = jnp.exp(b_end[p] + m_cur - m_new)
            e_upd = jnp.exp(a_max[p] - m_new)
            c_cur = decay * c_cur + e_upd * u[p]
            n_cur = decay * n_cur + e_upd * nu[p]
            m_cur = m_new
        c_ref[hh] = c_cur
        n_ref[hh] = n_cur
        m_ref[hh] = jnp.broadcast_to(m_cur, (1, HEAD_DIM))

    q_c = {p: _dot_nt(q_of(p), c_in[p]) for p in probs}

    tile_of = lambda ref, p: ref[rows(p[0]), _head(p[1])].astype(F32)
    g_col = each(lambda p: bb_col[p] + m_in[p])
    m_t = each(lambda p: jnp.maximum(g_col[p], r[p]))
    e_loc = each(lambda p: jnp.exp(r[p] - m_t[p]))
    inter = each(lambda p: jnp.exp(g_col[p] - m_t[p]))
    q_n = each(lambda p: jnp.sum(q_of(p).astype(F32) * n_in[p], axis=-1, keepdims=True))
    den = each(lambda p: e_loc[p] * rs[p] + inter[p] * q_n[p])
    scale = each(lambda p: 1.0 / jnp.maximum(jnp.abs(den[p]), jnp.exp(-m_t[p])))
    hg = each(lambda p: jax.nn.sigmoid(tile_of(om_s, p))
              * ((e_loc[p] * sv[p] + inter[p] * q_c[p]) * scale[p]), side=True)
    mu = each(lambda p: jnp.mean(hg[p], axis=-1, keepdims=True))
    cen = each(lambda p: hg[p] - mu[p], side=True)
    var = each(lambda p: jnp.mean(jnp.square(cen[p]), axis=-1, keepdims=True))
    assert not side_pieces and not pending
    for p in probs:
        sl = _head(p[1])
        y = cen[p] * lax.rsqrt(var[p] + NORM_EPS) * mg_ref[:, sl]
        y = (y + sk_ref[:, sl] * tile_of(xcv_s, p)) * _silu(tile_of(zm_s, p))
        ym_ref[0, rows(p[0]), sl] = y.astype(BF16)


def _front_call(x, kv, norm_g, w_in, conv_w, conv_b, wqk, wv, w_if, b_if, mnorm_g, skip_m):
    b, s, _ = x.shape
    tm = FRONT_ROWS
    ns = s // tm
    tok = lambda w: jax.ShapeDtypeStruct((b, s, w), BF16)
    tok_spec = lambda w: pl.BlockSpec((1, tm, w), lambda i, j: (i, j, 0))
    const = lambda shape: pl.BlockSpec(shape, lambda i, j: (0,) * len(shape))
    vdt_tiles = tm // V_TILE
    out_shape = (
        tok(GROUP_WIDTH),
        tok(GROUP_WIDTH), tok(GROUP_WIDTH),
        jax.ShapeDtypeStruct((b, ns * vdt_tiles, N_HEADS * VT_ROWS, V_TILE), BF16),
        tok(GROUP_WIDTH),
        tok(GROUP_WIDTH),
    )
    out_specs = (
        tok_spec(GROUP_WIDTH), tok_spec(GROUP_WIDTH), tok_spec(GROUP_WIDTH),
        pl.BlockSpec((1, vdt_tiles, N_HEADS * VT_ROWS, V_TILE), lambda i, j: (i, j, 0, 0)),
        tok_spec(GROUP_WIDTH), tok_spec(GROUP_WIDTH),
    )
    in_specs = [
        pl.BlockSpec((1, tm, D_MODEL), lambda i, j: (i, j, 0)),
        pl.BlockSpec((1, MEM_LEN, 2 * GROUP_WIDTH), lambda i, j: (i, 0, 0)),
        const((1, D_MODEL)),
        const((D_MODEL, IN_WIDTH)),
        const((CONV_WIDTH, GROUP_WIDTH)),
        const((1, GROUP_WIDTH)),
        const((N_HEADS, HEAD_DIM, 2 * HEAD_DIM)),
        const((N_HEADS, HEAD_DIM, HEAD_DIM)),
        const((3 * GROUP_WIDTH, HEAD_DIM)),
        const((1, HEAD_DIM)),
        const((1, GROUP_WIDTH)),
        const((1, GROUP_WIDTH)),
    ]
    tile_bf16 = pltpu.VMEM((tm, GROUP_WIDTH), BF16)
    return pl.pallas_call(
        _front_kernel,
        grid=(b, ns),
        in_specs=in_specs,
        out_specs=out_specs,
        out_shape=out_shape,
        scratch_shapes=[pltpu.VMEM((tm + CONV_HALO, GROUP_WIDTH), F32),
                        pltpu.VMEM((N_HEADS, HEAD_DIM, HEAD_DIM), F32),
                        pltpu.VMEM((N_HEADS, 1, HEAD_DIM), F32),
                        pltpu.VMEM((N_HEADS, 1, HEAD_DIM), F32),
                        tile_bf16, tile_bf16, tile_bf16,
                        pltpu.VMEM((tm, HEAD_DIM), F32),
                        tile_bf16, tile_bf16, tile_bf16],
        compiler_params=pltpu.CompilerParams(
            dimension_semantics=("arbitrary", "arbitrary"),
            vmem_limit_bytes=56 * 1024 * 1024),
        name="front",
    )(x, kv, norm_g, w_in, conv_w, conv_b, wqk, wv, w_if, b_if, mnorm_g, skip_m)


def _diff_kernel(q_ref, k_ref, vt_ref, zd_ref, g_ref, lq1_ref, lk1_ref, lq2_ref, lk2_ref,
                 x_ref, ym_ref, yc_ref, wo_ref, fg_ref, o_ref,
                 qs_ref, m_ref, l_ref, acc_ref, s0_ref, s1_ref, mx0_ref, mx1_ref, yd_ref, part_ref):
    tq, tk, tv = DIFF_Q_ROWS, DIFF_KV_ROWS, V_TILE
    i = pl.program_id(1)
    feat = lax.broadcasted_iota(jnp.int32, (HEAD_DIM, tq), 0)
    for hh in range(N_HEADS):
        q_t = q_ref[0, :, _head(hh)].astype(F32).T
        qs_ref[hh, :, 0:tq] = jnp.where(feat < DIFF_QK_DIM, q_t, 0.0).astype(BF16)
        qs_ref[hh, :, tq:2 * tq] = jnp.where(feat >= DIFF_QK_DIM, q_t, 0.0).astype(BF16)
    m_ref[...] = jnp.full_like(m_ref, -jnp.inf)
    l_ref[...] = jnp.zeros_like(l_ref)
    acc_ref[...] = jnp.zeros_like(acc_ref)

    kc = DIFF_KEY_CHUNK
    sub = HEAD_DIM
    tri = (lax.broadcasted_iota(jnp.int32, (kc, sub), 0) <= lax.broadcasted_iota(jnp.int32, (kc, sub), 1))

    def mask_diagonal(s_c, c):
        blocks, maxes = [], []
        for b8 in range(2 * tq // sub):
            b = b8 % (tq // sub)
            blk = s_c[:, b8 * sub:(b8 + 1) * sub]
            if c > b:
                blocks.append(jnp.full((kc, sub), -jnp.inf, F32))
                maxes.append(jnp.full((SUBLANES, sub), -jnp.inf, F32))
                continue
            if c == b:
                blk = jnp.where(tri, blk, -jnp.inf)
            blocks.append(blk)
            maxes.append(_fold_rows(blk, jnp.max))
        return jnp.concatenate(blocks, axis=1), jnp.concatenate(maxes, axis=1)

    group_a = (tuple(range(0, N_HEADS // 2)), s0_ref, mx0_ref)
    group_b = (tuple(range(N_HEADS // 2, N_HEADS)), s1_ref, mx1_ref)

    def step(consume=None, produce=None):
        n_slots = N_HEADS // 2
        for slot in range(n_slots):
            if consume is not None:
                t_c, (heads_c, sc_ref, mxc_ref) = consume
                h_c = heads_c[slot]
                m_old = m_ref[h_c]
                m_new = jnp.maximum(m_old, mxc_ref[slot])
                alpha = jnp.exp2(m_old - m_new)
                pv, pb_prev = None, None
            if produce is not None:
                t_p, (heads_p, sp_ref, mxp_ref), diagonal = produce
                h_p = heads_p[slot]
                mx = None
            for c in range(tk // kc):
                chunk = slice(c * kc, (c + 1) * kc)
                if produce is not None:
                    rows = pl.ds(pl.multiple_of(t_p * tk + c * kc, kc), kc)
                    s_c = _dot(k_ref[0, rows, _head(h_p)], qs_ref[h_p])
                if consume is not None:
                    pb_c = jnp.exp2(sc_ref[slot, chunk, :] - m_new).astype(BF16)
                    pb_prev = pb_c if pb_prev is None else jnp.concatenate([pb_prev, pb_c], axis=0)
                    if c == tk // kc - 1:
                        v_rows = slice(h_c * VT_ROWS, (h_c + 1) * VT_ROWS)
                        vt = jnp.concatenate([vt_ref[0, t_c * (tk // tv) + j, v_rows, :]
                                              for j in range(tk // tv)], axis=1)
                        pv = _dot(vt, pb_prev)
                if produce is not None:
                    if diagonal:
                        s_c, c_max = mask_diagonal(s_c, c)
                    else:
                        c_max = _fold_rows(s_c, jnp.max)
                    sp_ref[slot, chunk, :] = s_c
                    mx = c_max if mx is None else jnp.maximum(mx, c_max)
            if consume is not None:
                l_ref[h_c] = alpha * l_ref[h_c] + pv[HEAD_DIM:HEAD_DIM + 1, :]
                acc_ref[h_c] = alpha * acc_ref[h_c] + pv[:HEAD_DIM, :]
                m_ref[h_c] = m_new
            if produce is not None:
                mxp_ref[slot] = jnp.max(mx, axis=0, keepdims=True)

    def tile(t, next_diagonal):
        step(consume=(t, group_a), produce=(t, group_b, False))
        step(consume=(t, group_b), produce=(t + 1, group_a, next_diagonal))

    def body(t, carry):
        tile(t, False)
        return carry

    @pl.when(i == 0)
    def _():
        step(produce=(0, group_a, True))

    @pl.when(i > 0)
    def _():
        step(produce=(0, group_a, False))
        lax.fori_loop(0, i - 1, body, 0)
        tile(i - 1, True)

    lam = (jnp.exp(jnp.sum(lq1_ref[...] * lk1_ref[...], axis=-1, keepdims=True))
           - jnp.exp(jnp.sum(lq2_ref[...] * lk2_ref[...], axis=-1, keepdims=True)) + LAM_INIT)

    def finish(heads):
        for hh in heads:
            sl = _head(hh)
            acc = acc_ref[hh]
            inv = 1.0 / l_ref[hh]
            o_t = acc[:, :tq] * inv[:, :tq] - lam * (acc[:, tq:] * inv[:, tq:])
            o = _rms(o_t.T, g_ref[:, sl]) * (1.0 - LAM_INIT)
            yd_ref[:, sl] = (o * _silu(zd_ref[0, :, sl].astype(F32))).astype(BF16)

    def project_rows(c, src_ref, w_rows, first):
        r_c = slice(c * HEAD_DIM, (c + 1) * HEAD_DIM)
        y = _dot(src_ref[0, r_c, :], wo_ref[w_rows, :])
        part_ref[r_c, :] = (x_ref[0, r_c, :] + y) if first else (part_ref[r_c, :] + y)

    def project_heads(heads):
        cols = slice(heads[0] * HEAD_DIM, (heads[-1] + 1) * HEAD_DIM)
        w_rows = slice(GROUP_WIDTH + cols.start, GROUP_WIDTH + cols.stop)
        return _dot(yd_ref[:, cols], wo_ref[w_rows, :])

    step(consume=(i, group_a), produce=(i, group_b, True))
    finish(group_a[0])
    step(consume=(i, group_b))
    for c in range(tq // HEAD_DIM):
        project_rows(c, ym_ref, slice(0, GROUP_WIDTH), True)
    for c in range(tq // HEAD_DIM):
        project_rows(c, yc_ref, slice(2 * GROUP_WIDTH, 3 * GROUP_WIDTH), False)
    y_a = project_heads(group_a[0])
    finish(group_b[0])
    o_ref[0] = _rms(part_ref[...] + y_a + project_heads(group_b[0]), fg_ref[...])


def _diff_out_call(qd, kd, vdt, zd, dnorm_g, lq1, lk1, lq2, lk2, x, ym, yc, w_out, final_g):
    b, s, _ = qd.shape
    tq = DIFF_Q_ROWS
    nkv = vdt.shape[1]
    const = lambda shape: pl.BlockSpec(shape, lambda i, j: (0,) * len(shape))
    tok_spec = lambda w: pl.BlockSpec((1, tq, w), lambda i, j: (i, j, 0))
    return pl.pallas_call(
        _diff_kernel,
        grid=(b, s // tq),
        in_specs=[
            tok_spec(GROUP_WIDTH),
            pl.BlockSpec((1, s, GROUP_WIDTH), lambda i, j: (i, 0, 0)),
            pl.BlockSpec((1, nkv, N_HEADS * VT_ROWS, V_TILE), lambda i, j: (i, 0, 0, 0)),
            tok_spec(GROUP_WIDTH),
            const((1, GROUP_WIDTH)),
            const((1, DIFF_QK_DIM)), const((1, DIFF_QK_DIM)), const((1, DIFF_QK_DIM)), const((1, DIFF_QK_DIM)),
            tok_spec(D_MODEL), tok_spec(GROUP_WIDTH), tok_spec(GROUP_WIDTH),
            const((3 * GROUP_WIDTH, D_MODEL)), const((1, D_MODEL)),
        ],
        out_specs=tok_spec(D_MODEL),
        out_shape=jax.ShapeDtypeStruct((b, s, D_MODEL), F32),
        scratch_shapes=[pltpu.VMEM((N_HEADS, HEAD_DIM, 2 * tq), BF16),
                        pltpu.VMEM((N_HEADS, 1, 2 * tq), F32),
                        pltpu.VMEM((N_HEADS, 1, 2 * tq), F32),
                        pltpu.VMEM((N_HEADS, HEAD_DIM, 2 * tq), F32),
                        pltpu.VMEM((N_HEADS // 2, DIFF_KV_ROWS, 2 * tq), F32),
                        pltpu.VMEM((N_HEADS // 2, DIFF_KV_ROWS, 2 * tq), F32),
                        pltpu.VMEM((N_HEADS // 2, 1, 2 * tq), F32),
                        pltpu.VMEM((N_HEADS // 2, 1, 2 * tq), F32),
                        pltpu.VMEM((tq, GROUP_WIDTH), BF16),
                        pltpu.VMEM((tq, D_MODEL), F32)],
        compiler_params=pltpu.CompilerParams(
            dimension_semantics=("arbitrary", "arbitrary"),
            vmem_limit_bytes=58 * 1024 * 1024),
        name="diffattn_out",
    )(qd, kd, vdt, zd, dnorm_g, lq1, lk1, lq2, lk2, x, ym, yc, w_out, final_g)


def kernel(x, mem, norm_g, w_in, conv_w, conv_b, wq_m, wk_m, wv_m, w_if, b_if, mnorm_g, skip_m,
           lam_q1, lam_k1, lam_q2, lam_k2, dnorm_g, mem_norm_g, w_mem_kv, w_out, final_g):
    b, s, d = x.shape
    assert (d, s % DIFF_KV_ROWS, DIFF_KV_ROWS % V_TILE, FRONT_ROWS % V_TILE) == (D_MODEL, 0, 0, 0)
    assert (s % FRONT_ROWS, V_TILE % DIFF_KEY_CHUNK) == (0, 0)
    assert DIFF_Q_ROWS == DIFF_KV_ROWS and DIFF_KEY_CHUNK == HEAD_DIM
    assert norm_g.shape[0] == 1, "single-layer kernel"
    l = 0
    w_if_pad = jnp.pad(w_if[l], ((0, 0), (0, HEAD_DIM - w_if.shape[-1]))).astype(BF16)
    b_if_pad = jnp.pad(b_if[l], (0, HEAD_DIM - b_if.shape[-1]))[None, :]

    kv = _memkv_call(mem, mem_norm_g[l][None, :], w_mem_kv[l].astype(BF16))
    ym, qd, kd, vdt, zd, yc = _front_call(
        x, kv, norm_g[l][None, :], w_in[l].astype(BF16), conv_w[l], conv_b[l][None, :],
        jnp.concatenate([wq_m[l], wk_m[l]], axis=-1).astype(BF16), wv_m[l].astype(BF16), w_if_pad, b_if_pad,
        mnorm_g[l][None, :], skip_m[l][None, :])
    return _diff_out_call(qd, kd, vdt, zd, dnorm_g[l][None, :],
                          lam_q1[l][None, :], lam_k1[l][None, :], lam_q2[l][None, :], lam_k2[l][None, :],
                          x, ym, yc, w_out[l].astype(BF16), final_g[None, :])
```

```python
import math

import jax
import jax.numpy as jnp
from jax import lax
from jax.experimental import pallas as pl
from jax.experimental.pallas import tpu as pltpu

F32 = jnp.float32
BF16 = jnp.bfloat16

D_MODEL = 1024
N_HEADS = 4
HEAD_DIM = 128
SUBLANES = 8
GROUP_WIDTH = N_HEADS * HEAD_DIM
DIFF_QK_DIM = 64
CONV_WIDTH = 4
MLSTM_CHUNK = 128
MLSTM_STEP_CHUNKS = 4
MEM_LEN = 256
IN_WIDTH = 9 * GROUP_WIDTH
NORM_EPS = 1e-6
LAM_INIT = 0.8 - 0.6 * math.exp(-0.3 * 0)
LOG2E = math.log2(math.e)

OFF_XM, OFF_OM, OFF_ZM, OFF_QD, OFF_KD, OFF_VD, OFF_ZD, OFF_QC, OFF_ZC = (
    i * GROUP_WIDTH for i in range(9))

FRONT_ROWS = MLSTM_STEP_CHUNKS * MLSTM_CHUNK
V_TILE = 256
VT_PAD = 16
VT_ROWS = HEAD_DIM + VT_PAD
DIFF_Q_ROWS = 512
DIFF_KV_ROWS = 512
DIFF_KEY_CHUNK = 512
CONV_HALO = 8

NT_DIMS = (((1,), (1,)), ((), ()))
TN_DIMS = (((0,), (0,)), ((), ()))


def _dot(a, b):
    return jnp.dot(a, b, preferred_element_type=F32)


def _dot_nt(a, b):
    return lax.dot_general(a, b, NT_DIMS, preferred_element_type=F32)


def _silu(x):
    return x * jax.nn.sigmoid(x)


def _rms(x, g):
    return x * lax.rsqrt(jnp.mean(x * x, axis=-1, keepdims=True) + NORM_EPS) * g


def _head(h):
    return slice(h * HEAD_DIM, (h + 1) * HEAD_DIM)


def _fold_rows(x, op):
    r, n = x.shape
    return op(x.reshape(r // SUBLANES, SUBLANES, n), axis=0)


def _log_sigmoid(x):
    return jnp.minimum(x, 0.0) - jnp.log1p(jnp.exp(-jnp.abs(x)))


def _cumsum_rows(f, tril):
    hi = f.astype(BF16)
    r1 = f - hi.astype(F32)
    mid = r1.astype(BF16)
    lo = (r1 - mid.astype(F32)).astype(BF16)
    return _dot(tril, hi) + _dot(tril, mid) + _dot(tril, lo)


def _memkv_kernel(mem_ref, g_ref, w_ref, kv_ref):
    y = _rms(mem_ref[0], g_ref[...])
    kv_ref[0] = _dot(y.astype(BF16), w_ref[...]).astype(BF16)


def _memkv_call(mem, g, w):
    b = mem.shape[0]
    return pl.pallas_call(
        _memkv_kernel,
        grid=(b,),
        in_specs=[
            pl.BlockSpec((1, MEM_LEN, D_MODEL), lambda i: (i, 0, 0)),
            pl.BlockSpec((1, D_MODEL), lambda i: (0, 0)),
            pl.BlockSpec((D_MODEL, 2 * GROUP_WIDTH), lambda i: (0, 0)),
        ],
        out_specs=pl.BlockSpec((1, MEM_LEN, 2 * GROUP_WIDTH), lambda i: (i, 0, 0)),
        out_shape=jax.ShapeDtypeStruct((b, MEM_LEN, 2 * GROUP_WIDTH), BF16),
        compiler_params=pltpu.CompilerParams(dimension_semantics=("arbitrary",)),
        name="memkv",
    )(mem, g, w)


def _front_kernel(x_ref, kv_ref, ng_ref, win_ref, cw_ref, cb_ref, wqk_ref, wv_ref, wif_ref, bif_ref,
                  mg_ref, sk_ref,
                  ym_ref, qd_ref, kd_ref, vdt_ref, zd_ref, yc_ref,
                  conv_ref, c_ref, n_ref, m_ref, q_s, k_s, v_s, g_s, om_s, zm_s, xcv_s):
    tm = FRONT_ROWS
    L = MLSTM_CHUNK
    heads = range(N_HEADS)

    @pl.when(pl.program_id(1) == 0)
    def _():
        conv_ref[tm:tm + CONV_HALO, :] = jnp.zeros((CONV_HALO, GROUP_WIDTH), F32)
        c_ref[...] = jnp.zeros_like(c_ref)
        n_ref[...] = jnp.zeros_like(n_ref)
        m_ref[...] = jnp.zeros_like(m_ref)

    h = _rms(x_ref[0], ng_ref[...]).astype(BF16)

    def proj(off):
        return _dot(h, win_ref[:, off:off + GROUP_WIDTH])

    def put_vdt(val):
        v_t = val.T.astype(BF16)
        pad_row = lax.broadcasted_iota(jnp.int32, (VT_PAD, tm), 0)
        pad = jnp.where(pad_row == 0, 1.0, 0.0).astype(BF16)
        v_aug = jnp.concatenate([blk for hh in heads for blk in (v_t[_head(hh), :], pad)], axis=0)
        for j in range(tm // V_TILE):
            vdt_ref[0, j] = v_aug[:, j * V_TILE:(j + 1) * V_TILE]

    def put(ref, scale=None):
        def sink(val):
            ref[...] = (val if scale is None else val * scale).astype(BF16).reshape(ref.shape)
        return sink

    side_pieces = [(OFF_OM, put(om_s)), (OFF_ZM, put(zm_s)),
                   (OFF_QD, put(qd_ref, DIFF_QK_DIM ** -0.5 * LOG2E)), (OFF_KD, put(kd_ref)),
                   (OFF_VD, put_vdt), (OFF_ZD, put(zd_ref))]
    pending = []

    def side_start():
        if side_pieces:
            off, sink = side_pieces.pop(0)
            pending.append((proj(off), sink))

    def side_finish():
        if pending:
            val, sink = pending.pop(0)
            sink(val)

    x_m = proj(OFF_XM)
    q_cb = proj(OFF_QC).astype(BF16)
    z_c = proj(OFF_ZC)

    conv_ref[0:CONV_HALO, :] = conv_ref[tm:tm + CONV_HALO, :]
    conv_ref[CONV_HALO:CONV_HALO + tm, :] = x_m
    acc = jnp.broadcast_to(cb_ref[...], (tm, GROUP_WIDTH))
    for j in range(CONV_WIDTH):
        start = CONV_HALO - (CONV_WIDTH - 1) + j
        acc = acc + cw_ref[j:j + 1, :] * conv_ref[start:start + tm, :]
    sc = [_dot_nt(q_cb[:, _head(hh)], kv_ref[0, :, _head(hh)]) * HEAD_DIM ** -0.5 for hh in heads]
    x_cv = _silu(acc)
    xcv_b = x_cv.astype(BF16)
    xcv_s[...] = xcv_b
    xm_b = x_m.astype(BF16)

    qk_p = [_dot(xcv_b[:, _head(hh)], wqk_ref[hh]) for hh in heads]
    vv_p = [_dot(xm_b[:, _head(hh)], wv_ref[hh]) for hh in heads]

    q_all = jnp.concatenate([qk_p[hh][:, :HEAD_DIM] for hh in heads], axis=1).astype(BF16)
    k_all = jnp.concatenate([qk_p[hh][:, HEAD_DIM:] for hh in heads], axis=1)
    v_all = jnp.concatenate(vv_p, axis=1).astype(BF16)
    q_s[...] = q_all
    k_s[...] = (k_all * HEAD_DIM ** -0.5).astype(BF16)
    v_s[...] = v_all
    g_s[...] = (jnp.broadcast_to(bif_ref[...], (tm, HEAD_DIM))
                + _dot(q_all, wif_ref[0:GROUP_WIDTH, :])
                + _dot(k_all.astype(BF16), wif_ref[GROUP_WIDTH:2 * GROUP_WIDTH, :])
                + _dot(v_all, wif_ref[2 * GROUP_WIDTH:3 * GROUP_WIDTH, :]))

    mx = [jnp.max(sc[hh], axis=-1, keepdims=True) for hh in heads]
    pm = [jnp.exp(sc[hh] - mx[hh]) for hh in heads]
    inv = [1.0 / jnp.sum(pm[hh], axis=-1, keepdims=True) for hh in heads]
    pc = [(pm[hh] * inv[hh]).astype(BF16) for hh in heads]
    oc = [_dot(pc[hh], kv_ref[0, :, GROUP_WIDTH + hh * HEAD_DIM:GROUP_WIDTH + (hh + 1) * HEAD_DIM]) for hh in heads]
    for hh in heads:
        yc_ref[0, :, _head(hh)] = (oc[hh] * _silu(z_c[:, _head(hh)])).astype(BF16)

    row = lax.broadcasted_iota(jnp.int32, (L, L), 0)
    col = lax.broadcasted_iota(jnp.int32, (L, L), 1)
    causal = col <= row
    tril = jnp.where(causal, 1.0, 0.0).astype(BF16)
    chunks = range(MLSTM_STEP_CHUNKS)
    probs = [(c, hh) for c in chunks for hh in heads]
    rows = lambda c: slice(c * L, (c + 1) * L)
    q_of = lambda p: q_s[rows(p[0]), _head(p[1])]
    k_of = lambda p: k_s[rows(p[0]), _head(p[1])]
    v_of = lambda p: v_s[rows(p[0]), _head(p[1])]

    def each(f, side=False):
        if side:
            side_start()
        out = {p: f(p) for p in probs}
        if side:
            side_finish()
        return out

    a_cols, a_rows = [], []
    for c in chunks:
        g = g_s[rows(c), :]
        bb_all = _cumsum_rows(_log_sigmoid(g), tril)
        ac = jnp.where(col < N_HEADS, g, bb_all)
        a_cols.append(ac)
        a_rows.append(ac.T)
    ib_col = {(c, hh): a_cols[c][:, hh:hh + 1] for c, hh in probs}
    bb_col = {(c, hh): a_cols[c][:, N_HEADS + hh:N_HEADS + hh + 1] for c, hh in probs}
    ib_row = {(c, hh): a_rows[c][hh:hh + 1, :] for c, hh in probs}
    bb_row = {(c, hh): a_rows[c][N_HEADS + hh:N_HEADS + hh + 1, :] for c, hh in probs}
    b_end = {p: bb_col[p][L - 1:L, :] for p in probs}

    qk = {p: _dot_nt(q_of(p), k_of(p)) for p in probs}

    d = each(lambda p: jnp.where(causal, (bb_col[p] - bb_row[p]) + ib_row[p], -jnp.inf), side=True)
    r = each(lambda p: jnp.max(d[p], axis=-1, keepdims=True))
    s_loc = each(lambda p: qk[p] * jnp.exp(d[p] - r[p]), side=True)
    rs = each(lambda p: jnp.sum(s_loc[p], axis=-1, keepdims=True))
    s_bf = each(lambda p: s_loc[p].astype(BF16))
    a = each(lambda p: (b_end[p] - bb_col[p]) + ib_col[p], side=True)
    a_max = each(lambda p: jnp.max(a[p], axis=0, keepdims=True))
    w_loc = each(lambda p: jnp.exp(a[p] - a_max[p]))
    vw = each(lambda p: (v_of(p).astype(F32) * w_loc[p]).astype(BF16), side=True)
    nu = each(lambda p: jnp.sum(k_of(p).astype(F32) * w_loc[p], axis=0, keepdims=True))

    sv = {p: _dot(s_bf[p], v_of(p)) for p in probs}
    u = {p: lax.dot_general(vw[p], k_of(p), TN_DIMS, preferred_element_type=F32) for p in probs}

    m_in, c_in, n_in = {}, {}, {}
    for hh in heads:
        m_cur = m_ref[hh][:, 0:1]
        c_cur = c_ref[hh]
        n_cur = n_ref[hh]
        for c in chunks:
            p = (c, hh)
            m_in[p], c_in[p], n_in[p] = m_cur, c_cur.astype(BF16), n_cur
            m_new = jnp.maximum(b_end[p] + m_cur, a_max[p])
            decay = jnp.exp(b_end[p] + m_cur - m_new)
            e_upd = jnp.exp(a_max[p] - m_new)
            c_cur = decay * c_cur + e_upd * u[p]
            n_cur = decay * n_cur + e_upd * nu[p]
            m_cur = m_new
        c_ref[hh] = c_cur
        n_ref[hh] = n_cur
        m_ref[hh] = jnp.broadcast_to(m_cur, (1, HEAD_DIM))

    q_c = {p: _dot_nt(q_of(p), c_in[p]) for p in probs}

    tile_of = lambda ref, p: ref[rows(p[0]), _head(p[1])].astype(F32)
    g_col = each(lambda p: bb_col[p] + m_in[p])
    m_t = each(lambda p: jnp.maximum(g_col[p], r[p]))
    e_loc = each(lambda p: jnp.exp(r[p] - m_t[p]))
    inter = each(lambda p: jnp.exp(g_col[p] - m_t[p]))
    q_n = each(lambda p: jnp.sum(q_of(p).astype(F32) * n_in[p], axis=-1, keepdims=True))
    den = each(lambda p: e_loc[p] * rs[p] + inter[p] * q_n[p])
    scale = each(lambda p: 1.0 / jnp.maximum(jnp.abs(den[p]), jnp.exp(-m_t[p])))
    hg = each(lambda p: jax.nn.sigmoid(tile_of(om_s, p))
              * ((e_loc[p] * sv[p] + inter[p] * q_c[p]) * scale[p]), side=True)
    mu = each(lambda p: jnp.mean(hg[p], axis=-1, keepdims=True))
    cen = each(lambda p: hg[p] - mu[p], side=True)
    var = each(lambda p: jnp.mean(jnp.square(cen[p]), axis=-1, keepdims=True))
    assert not side_pieces and not pending
    for p in probs:
        sl = _head(p[1])
        y = cen[p] * lax.rsqrt(var[p] + NORM_EPS) * mg_ref[:, sl]
        y = (y + sk_ref[:, sl] * tile_of(xcv_s, p)) * _silu(tile_of(zm_s, p))
        ym_ref[0, rows(p[0]), sl] = y.astype(BF16)


def _front_call(x, kv, norm_g, w_in, conv_w, conv_b, wqk, wv, w_if, b_if, mnorm_g, skip_m):
    b, s, _ = x.shape
    tm = FRONT_ROWS
    ns = s // tm
    tok = lambda w: jax.ShapeDtypeStruct((b, s, w), BF16)
    tok_spec = lambda w: pl.BlockSpec((1, tm, w), lambda i, j: (i, j, 0))
    const = lambda shape: pl.BlockSpec(shape, lambda i, j: (0,) * len(shape))
    vdt_tiles = tm // V_TILE
    out_shape = (
        tok(GROUP_WIDTH),
        tok(GROUP_WIDTH), tok(GROUP_WIDTH),
        jax.ShapeDtypeStruct((b, ns * vdt_tiles, N_HEADS * VT_ROWS, V_TILE), BF16),
        tok(GROUP_WIDTH),
        tok(GROUP_WIDTH),
    )
    out_specs = (
        tok_spec(GROUP_WIDTH), tok_spec(GROUP_WIDTH), tok_spec(GROUP_WIDTH),
        pl.BlockSpec((1, vdt_tiles, N_HEADS * VT_ROWS, V_TILE), lambda i, j: (i, j, 0, 0)),
        tok_spec(GROUP_WIDTH), tok_spec(GROUP_WIDTH),
    )
    in_specs = [
        pl.BlockSpec((1, tm, D_MODEL), lambda i, j: (i, j, 0)),
        pl.BlockSpec((1, MEM_LEN, 2 * GROUP_WIDTH), lambda i, j: (i, 0, 0)),
        const((1, D_MODEL)),
        const((D_MODEL, IN_WIDTH)),
        const((CONV_WIDTH, GROUP_WIDTH)),
        const((1, GROUP_WIDTH)),
        const((N_HEADS, HEAD_DIM, 2 * HEAD_DIM)),
        const((N_HEADS, HEAD_DIM, HEAD_DIM)),
        const((3 * GROUP_WIDTH, HEAD_DIM)),
        const((1, HEAD_DIM)),
        const((1, GROUP_WIDTH)),
        const((1, GROUP_WIDTH)),
    ]
    tile_bf16 = pltpu.VMEM((tm, GROUP_WIDTH), BF16)
    return pl.pallas_call(
        _front_kernel,
        grid=(b, ns),
        in_specs=in_specs,
        out_specs=out_specs,
        out_shape=out_shape,
        scratch_shapes=[pltpu.VMEM((tm + CONV_HALO, GROUP_WIDTH), F32),
                        pltpu.VMEM((N_HEADS, HEAD_DIM, HEAD_DIM), F32),
                        pltpu.VMEM((N_HEADS, 1, HEAD_DIM), F32),
                        pltpu.VMEM((N_HEADS, 1, HEAD_DIM), F32),
                        tile_bf16, tile_bf16, tile_bf16,
                        pltpu.VMEM((tm, HEAD_DIM), F32),
                        tile_bf16, tile_bf16, tile_bf16],
        compiler_params=pltpu.CompilerParams(
            dimension_semantics=("arbitrary", "arbitrary"),
            vmem_limit_bytes=56 * 1024 * 1024),
        name="front",
    )(x, kv, norm_g, w_in, conv_w, conv_b, wqk, wv, w_if, b_if, mnorm_g, skip_m)


def _diff_kernel(q_ref, qn_ref, k_ref, vt_ref, zd_ref, g_ref, lq1_ref, lk1_ref, lq2_ref, lk2_ref,
                 x_ref, ym_ref, yc_ref, wo_ref, fg_ref, o_ref,
                 qs_ref, m_ref, l_ref, acc_ref, s0_ref, s1_ref, mx0_ref, mx1_ref, yd_ref, part_ref, qsn_ref):
    tq, tk, tv = DIFF_Q_ROWS, DIFF_KV_ROWS, V_TILE
    i = pl.program_id(1)
    feat = lax.broadcasted_iota(jnp.int32, (HEAD_DIM, tq), 0)
    for hh in range(N_HEADS):
        q_t = q_ref[0, :, _head(hh)].astype(F32).T
        qs_ref[hh, :, 0:tq] = jnp.where(feat < DIFF_QK_DIM, q_t, 0.0).astype(BF16)
        qs_ref[hh, :, tq:2 * tq] = jnp.where(feat >= DIFF_QK_DIM, q_t, 0.0).astype(BF16)
    m_ref[...] = jnp.full_like(m_ref, -jnp.inf)
    l_ref[...] = jnp.zeros_like(l_ref)
    acc_ref[...] = jnp.zeros_like(acc_ref)

    kc = DIFF_KEY_CHUNK
    sub = HEAD_DIM
    tri = (lax.broadcasted_iota(jnp.int32, (sub, sub), 0) <= lax.broadcasted_iota(jnp.int32, (sub, sub), 1))

    def mask_diagonal(s_c, c):
        hidden = jnp.full((sub, sub), -jnp.inf, F32)
        hidden_max = jnp.full((SUBLANES, sub), -jnp.inf, F32)
        row_blocks, maxes = [], None
        for r in range(kc // sub):
            kb = c * (kc // sub) + r
            blocks, row_max = [], []
            for b8 in range(2 * tq // sub):
                b = b8 % (tq // sub)
                if kb > b:
                    blocks.append(hidden)
                    row_max.append(hidden_max)
                    continue
                blk = s_c[r * sub:(r + 1) * sub, b8 * sub:(b8 + 1) * sub]
                if kb == b:
                    blk = jnp.where(tri, blk, -jnp.inf)
                blocks.append(blk)
                row_max.append(_fold_rows(blk, jnp.max))
            row_blocks.append(jnp.concatenate(blocks, axis=1))
            row_max = jnp.concatenate(row_max, axis=1)
            maxes = row_max if maxes is None else jnp.maximum(maxes, row_max)
        return jnp.concatenate(row_blocks, axis=0), maxes

    group_a = (tuple(range(0, N_HEADS // 2)), s0_ref, mx0_ref)
    group_b = (tuple(range(N_HEADS // 2, N_HEADS)), s1_ref, mx1_ref)

    def step(consume=None, produce=None):
        n_slots = N_HEADS // 2
        for slot in range(n_slots):
            if consume is not None:
                t_c, (heads_c, sc_ref, mxc_ref) = consume
                h_c = heads_c[slot]
                m_old = m_ref[h_c]
                m_new = jnp.maximum(m_old, mxc_ref[slot])
                alpha = jnp.exp2(m_old - m_new)
                pv, pb_prev = None, None
            if produce is not None:
                t_p, (heads_p, sp_ref, mxp_ref), diagonal = produce[:3]
                q_src = produce[3] if len(produce) > 3 else None
                h_p = heads_p[slot]
                mx = None
            for c in range(tk // kc):
                chunk = slice(c * kc, (c + 1) * kc)
                if produce is not None:
                    rows = pl.ds(pl.multiple_of(t_p * tk + c * kc, kc), kc)
                    q_mat = qs_ref[h_p] if q_src is None else q_src[slot]
                    s_c = _dot(k_ref[0, rows, _head(h_p)], q_mat)
                if consume is not None:
                    pb_c = jnp.exp2(sc_ref[slot, chunk, :] - m_new).astype(BF16)
                    pb_prev = pb_c if pb_prev is None else jnp.concatenate([pb_prev, pb_c], axis=0)
                    if c == tk // kc - 1:
                        v_rows = slice(h_c * VT_ROWS, (h_c + 1) * VT_ROWS)
                        vt = jnp.concatenate([vt_ref[0, t_c * (tk // tv) + j, v_rows, :]
                                              for j in range(tk // tv)], axis=1)
                        pv = _dot(vt, pb_prev)
                if produce is not None:
                    if diagonal:
                        s_c, c_max = mask_diagonal(s_c, c)
                    else:
                        c_max = _fold_rows(s_c, jnp.max)
                    sp_ref[slot, chunk, :] = s_c
                    mx = c_max if mx is None else jnp.maximum(mx, c_max)
            if consume is not None:
                l_ref[h_c] = alpha * l_ref[h_c] + pv[HEAD_DIM:HEAD_DIM + 1, :]
                acc_ref[h_c] = alpha * acc_ref[h_c] + pv[:HEAD_DIM, :]
                m_ref[h_c] = m_new
            if produce is not None:
                mxp_ref[slot] = jnp.max(mx, axis=0, keepdims=True)

    def tile(t, next_diagonal):
        step(consume=(t, group_a), produce=(t, group_b, False))
        step(consume=(t, group_b), produce=(t + 1, group_a, next_diagonal))

    def body(t, carry):
        tile(t, False)
        return carry

    @pl.when(i == 0)
    def _():
        step(produce=(0, group_a, True))

    @pl.when(i > 0)
    def _():
        lax.fori_loop(0, i - 1, body, 0)
        tile(i - 1, True)

    lam = (jnp.exp(jnp.sum(lq1_ref[...] * lk1_ref[...], axis=-1, keepdims=True))
           - jnp.exp(jnp.sum(lq2_ref[...] * lk2_ref[...], axis=-1, keepdims=True)) + LAM_INIT)

    def finish(heads):
        for hh in heads:
            sl = _head(hh)
            acc = acc_ref[hh]
            inv = 1.0 / l_ref[hh]
            o_t = acc[:, :tq] * inv[:, :tq] - lam * (acc[:, tq:] * inv[:, tq:])
            o = _rms(o_t.T, g_ref[:, sl]) * (1.0 - LAM_INIT)
            yd_ref[:, sl] = (o * _silu(zd_ref[0, :, sl].astype(F32))).astype(BF16)

    def project_rows(c, src_ref, w_rows, first):
        r_c = slice(c * HEAD_DIM, (c + 1) * HEAD_DIM)
        y = _dot(src_ref[0, r_c, :], wo_ref[w_rows, :])
        part_ref[r_c, :] = (x_ref[0, r_c, :] + y) if first else (part_ref[r_c, :] + y)

    def project_heads(heads):
        cols = slice(heads[0] * HEAD_DIM, (heads[-1] + 1) * HEAD_DIM)
        w_rows = slice(GROUP_WIDTH + cols.start, GROUP_WIDTH + cols.stop)
        return _dot(yd_ref[:, cols], wo_ref[w_rows, :])

    step(consume=(i, group_a), produce=(i, group_b, True))
    finish(group_a[0])

    last = pl.num_programs(1) - 1

    @pl.when(i < last)
    def _():
        for slot, hh in enumerate(group_a[0]):
            q_t = qn_ref[0, :, _head(hh)].astype(F32).T
            qsn_ref[slot, :, 0:tq] = jnp.where(feat < DIFF_QK_DIM, q_t, 0.0).astype(BF16)
            qsn_ref[slot, :, tq:2 * tq] = jnp.where(feat >= DIFF_QK_DIM, q_t, 0.0).astype(BF16)
        step(consume=(i, group_b), produce=(0, group_a, False, qsn_ref))

    @pl.when(i == last)
    def _():
        step(consume=(i, group_b))

    for c in range(tq // HEAD_DIM):
        project_rows(c, ym_ref, slice(0, GROUP_WIDTH), True)
    for c in range(tq // HEAD_DIM):
        project_rows(c, yc_ref, slice(2 * GROUP_WIDTH, 3 * GROUP_WIDTH), False)
    y_a = project_heads(group_a[0])
    finish(group_b[0])
    o_ref[0] = _rms(part_ref[...] + y_a + project_heads(group_b[0]), fg_ref[...])


def _diff_out_call(qd, kd, vdt, zd, dnorm_g, lq1, lk1, lq2, lk2, x, ym, yc, w_out, final_g):
    n_q = qd.shape[1] // DIFF_Q_ROWS
    b, s, _ = qd.shape
    tq = DIFF_Q_ROWS
    nkv = vdt.shape[1]
    const = lambda shape: pl.BlockSpec(shape, lambda i, j: (0,) * len(shape))
    tok_spec = lambda w: pl.BlockSpec((1, tq, w), lambda i, j: (i, j, 0))
    return pl.pallas_call(
        _diff_kernel,
        grid=(b, s // tq),
        in_specs=[
            tok_spec(GROUP_WIDTH),
            pl.BlockSpec((1, tq, GROUP_WIDTH), lambda i, j: (i, jnp.minimum(j + 1, n_q - 1), 0)),
            pl.BlockSpec((1, s, GROUP_WIDTH), lambda i, j: (i, 0, 0)),
            pl.BlockSpec((1, nkv, N_HEADS * VT_ROWS, V_TILE), lambda i, j: (i, 0, 0, 0)),
            tok_spec(GROUP_WIDTH),
            const((1, GROUP_WIDTH)),
            const((1, DIFF_QK_DIM)), const((1, DIFF_QK_DIM)), const((1, DIFF_QK_DIM)), const((1, DIFF_QK_DIM)),
            tok_spec(D_MODEL), tok_spec(GROUP_WIDTH), tok_spec(GROUP_WIDTH),
            const((3 * GROUP_WIDTH, D_MODEL)), const((1, D_MODEL)),
        ],
        out_specs=tok_spec(D_MODEL),
        out_shape=jax.ShapeDtypeStruct((b, s, D_MODEL), F32),
        scratch_shapes=[pltpu.VMEM((N_HEADS, HEAD_DIM, 2 * tq), BF16),
                        pltpu.VMEM((N_HEADS, 1, 2 * tq), F32),
                        pltpu.VMEM((N_HEADS, 1, 2 * tq), F32),
                        pltpu.VMEM((N_HEADS, HEAD_DIM, 2 * tq), F32),
                        pltpu.VMEM((N_HEADS // 2, DIFF_KV_ROWS, 2 * tq), F32),
                        pltpu.VMEM((N_HEADS // 2, DIFF_KV_ROWS, 2 * tq), F32),
                        pltpu.VMEM((N_HEADS // 2, 1, 2 * tq), F32),
                        pltpu.VMEM((N_HEADS // 2, 1, 2 * tq), F32),
                        pltpu.VMEM((tq, GROUP_WIDTH), BF16),
                        pltpu.VMEM((tq, D_MODEL), F32),
                        pltpu.VMEM((N_HEADS // 2, HEAD_DIM, 2 * tq), BF16)],
        compiler_params=pltpu.CompilerParams(
            dimension_semantics=("arbitrary", "arbitrary"),
            vmem_limit_bytes=58 * 1024 * 1024),
        name="diffattn_out",
    )(qd, qd, kd, vdt, zd, dnorm_g, lq1, lk1, lq2, lk2, x, ym, yc, w_out, final_g)


def kernel(x, mem, norm_g, w_in, conv_w, conv_b, wq_m, wk_m, wv_m, w_if, b_if, mnorm_g, skip_m,
           lam_q1, lam_k1, lam_q2, lam_k2, dnorm_g, mem_norm_g, w_mem_kv, w_out, final_g):
    b, s, d = x.shape
    assert (d, s % DIFF_KV_ROWS, DIFF_KV_ROWS % V_TILE, FRONT_ROWS % V_TILE) == (D_MODEL, 0, 0, 0)
    assert (s % FRONT_ROWS, DIFF_KV_ROWS % DIFF_KEY_CHUNK) == (0, 0)
    assert DIFF_Q_ROWS == DIFF_KV_ROWS and DIFF_KEY_CHUNK % HEAD_DIM == 0
    assert norm_g.shape[0] == 1, "single-layer kernel"
    l = 0
    w_if_pad = jnp.pad(w_if[l], ((0, 0), (0, HEAD_DIM - w_if.shape[-1]))).astype(BF16)
    b_if_pad = jnp.pad(b_if[l], (0, HEAD_DIM - b_if.shape[-1]))[None, :]

    kv = _memkv_call(mem, mem_norm_g[l][None, :], w_mem_kv[l].astype(BF16))
    ym, qd, kd, vdt, zd, yc = _front_call(
        x, kv, norm_g[l][None, :], w_in[l].astype(BF16), conv_w[l], conv_b[l][None, :],
        jnp.concatenate([wq_m[l], wk_m[l]], axis=-1).astype(BF16), wv_m[l].astype(BF16), w_if_pad, b_if_pad,
        mnorm_g[l][None, :], skip_m[l][None, :])
    return _diff_out_call(qd, kd, vdt, zd, dnorm_g[l][None, :],
                          lam_q1[l][None, :], lam_k1[l][None, :], lam_q2[l][None, :], lam_k2[l][None, :],
                          x, ym, yc, w_out[l].astype(BF16), final_g[None, :])
```

```python
import math

import jax
import jax.numpy as jnp
from jax import lax
from jax.experimental import pallas as pl
from jax.experimental.pallas import tpu as pltpu

F32 = jnp.float32
BF16 = jnp.bfloat16

D_MODEL = 1024
N_HEADS = 4
HEAD_DIM = 128
SUBLANES = 8
GROUP_WIDTH = N_HEADS * HEAD_DIM
DIFF_QK_DIM = 64
CONV_WIDTH = 4
MLSTM_CHUNK = 128
MLSTM_STEP_CHUNKS = 4
MEM_LEN = 256
IN_WIDTH = 9 * GROUP_WIDTH
NORM_EPS = 1e-6
LAM_INIT = 0.8 - 0.6 * math.exp(-0.3 * 0)
LOG2E = math.log2(math.e)

OFF_XM, OFF_OM, OFF_ZM, OFF_QD, OFF_KD, OFF_VD, OFF_ZD, OFF_QC, OFF_ZC = (
    i * GROUP_WIDTH for i in range(9))

FRONT_ROWS = MLSTM_STEP_CHUNKS * MLSTM_CHUNK
V_TILE = 256
VT_PAD = 16
VT_ROWS = HEAD_DIM + VT_PAD
DIFF_Q_ROWS = 512
DIFF_KV_ROWS = 512
DIFF_KEY_CHUNK = 512
CONV_HALO = 8

NT_DIMS = (((1,), (1,)), ((), ()))
TN_DIMS = (((0,), (0,)), ((), ()))


def _dot(a, b):
    return jnp.dot(a, b, preferred_element_type=F32)


def _dot_nt(a, b):
    return lax.dot_general(a, b, NT_DIMS, preferred_element_type=F32)


def _silu(x):
    return x * jax.nn.sigmoid(x)


def _rms(x, g):
    return x * lax.rsqrt(jnp.mean(x * x, axis=-1, keepdims=True) + NORM_EPS) * g


def _head(h):
    return slice(h * HEAD_DIM, (h + 1) * HEAD_DIM)


def _fold_rows(x, op):
    r, n = x.shape
    return op(x.reshape(r // SUBLANES, SUBLANES, n), axis=0)


def _log_sigmoid(x):
    return jnp.minimum(x, 0.0) - jnp.log1p(jnp.exp(-jnp.abs(x)))


def _cumsum_rows(f, tril):
    hi = f.astype(BF16)
    r1 = f - hi.astype(F32)
    mid = r1.astype(BF16)
    lo = (r1 - mid.astype(F32)).astype(BF16)
    return _dot(tril, hi) + _dot(tril, mid) + _dot(tril, lo)


def _memkv_kernel(mem_ref, g_ref, w_ref, kv_ref):
    y = _rms(mem_ref[0], g_ref[...])
    kv_ref[0] = _dot(y.astype(BF16), w_ref[...]).astype(BF16)


def _memkv_call(mem, g, w):
    b = mem.shape[0]
    return pl.pallas_call(
        _memkv_kernel,
        grid=(b,),
        in_specs=[
            pl.BlockSpec((1, MEM_LEN, D_MODEL), lambda i: (i, 0, 0)),
            pl.BlockSpec((1, D_MODEL), lambda i: (0, 0)),
            pl.BlockSpec((D_MODEL, 2 * GROUP_WIDTH), lambda i: (0, 0)),
        ],
        out_specs=pl.BlockSpec((1, MEM_LEN, 2 * GROUP_WIDTH), lambda i: (i, 0, 0)),
        out_shape=jax.ShapeDtypeStruct((b, MEM_LEN, 2 * GROUP_WIDTH), BF16),
        compiler_params=pltpu.CompilerParams(dimension_semantics=("arbitrary",)),
        name="memkv",
    )(mem, g, w)


def _front_kernel(x_ref, kv_ref, ng_ref, win_ref, cw_ref, cb_ref, wqk_ref, wv_ref, wif_ref, bif_ref,
                  mg_ref, sk_ref,
                  ym_ref, qd_ref, kd_ref, vdt_ref, zd_ref, yc_ref,
                  conv_ref, c_ref, n_ref, m_ref, q_s, k_s, v_s, g_s, om_s, zm_s, xcv_s):
    tm = FRONT_ROWS
    L = MLSTM_CHUNK
    heads = range(N_HEADS)

    @pl.when(pl.program_id(1) == 0)
    def _():
        conv_ref[tm:tm + CONV_HALO, :] = jnp.zeros((CONV_HALO, GROUP_WIDTH), F32)
        c_ref[...] = jnp.zeros_like(c_ref)
        n_ref[...] = jnp.zeros_like(n_ref)
        m_ref[...] = jnp.zeros_like(m_ref)

    h = _rms(x_ref[0], ng_ref[...]).astype(BF16)

    def proj(off):
        return _dot(h, win_ref[:, off:off + GROUP_WIDTH])

    def put_vdt(val):
        v_t = val.T.astype(BF16)
        pad_row = lax.broadcasted_iota(jnp.int32, (VT_PAD, tm), 0)
        pad = jnp.where(pad_row == 0, 1.0, 0.0).astype(BF16)
        v_aug = jnp.concatenate([blk for hh in heads for blk in (v_t[_head(hh), :], pad)], axis=0)
        for j in range(tm // V_TILE):
            vdt_ref[0, j] = v_aug[:, j * V_TILE:(j + 1) * V_TILE]

    def put(ref, scale=None):
        def sink(val):
            ref[...] = (val if scale is None else val * scale).astype(BF16).reshape(ref.shape)
        return sink

    side_pieces = [(OFF_OM, put(om_s)), (OFF_ZM, put(zm_s)),
                   (OFF_QD, put(qd_ref, DIFF_QK_DIM ** -0.5 * LOG2E)), (OFF_KD, put(kd_ref)),
                   (OFF_VD, put_vdt), (OFF_ZD, put(zd_ref))]
    pending = []

    def side_start():
        if side_pieces:
            off, sink = side_pieces.pop(0)
            pending.append((proj(off), sink))

    def side_finish():
        if pending:
            val, sink = pending.pop(0)
            sink(val)

    x_m = proj(OFF_XM)
    q_cb = proj(OFF_QC).astype(BF16)
    z_c = proj(OFF_ZC)

    conv_ref[0:CONV_HALO, :] = conv_ref[tm:tm + CONV_HALO, :]
    conv_ref[CONV_HALO:CONV_HALO + tm, :] = x_m
    acc = jnp.broadcast_to(cb_ref[...], (tm, GROUP_WIDTH))
    for j in range(CONV_WIDTH):
        start = CONV_HALO - (CONV_WIDTH - 1) + j
        acc = acc + cw_ref[j:j + 1, :] * conv_ref[start:start + tm, :]
    sc = [_dot_nt(q_cb[:, _head(hh)], kv_ref[0, :, _head(hh)]) * HEAD_DIM ** -0.5 for hh in heads]
    x_cv = _silu(acc)
    xcv_b = x_cv.astype(BF16)
    xcv_s[...] = xcv_b
    xm_b = x_m.astype(BF16)

    qk_p = [_dot(xcv_b[:, _head(hh)], wqk_ref[hh]) for hh in heads]
    vv_p = [_dot(xm_b[:, _head(hh)], wv_ref[hh]) for hh in heads]

    q_all = jnp.concatenate([qk_p[hh][:, :HEAD_DIM] for hh in heads], axis=1).astype(BF16)
    k_all = jnp.concatenate([qk_p[hh][:, HEAD_DIM:] for hh in heads], axis=1)
    v_all = jnp.concatenate(vv_p, axis=1).astype(BF16)
    q_s[...] = q_all
    k_s[...] = (k_all * HEAD_DIM ** -0.5).astype(BF16)
    v_s[...] = v_all
    g_s[...] = (jnp.broadcast_to(bif_ref[...], (tm, HEAD_DIM))
                + _dot(q_all, wif_ref[0:GROUP_WIDTH, :])
                + _dot(k_all.astype(BF16), wif_ref[GROUP_WIDTH:2 * GROUP_WIDTH, :])
                + _dot(v_all, wif_ref[2 * GROUP_WIDTH:3 * GROUP_WIDTH, :]))

    mx = [jnp.max(sc[hh], axis=-1, keepdims=True) for hh in heads]
    pm = [jnp.exp(sc[hh] - mx[hh]) for hh in heads]
    inv = [1.0 / jnp.sum(pm[hh], axis=-1, keepdims=True) for hh in heads]
    pc = [(pm[hh] * inv[hh]).astype(BF16) for hh in heads]
    oc = [_dot(pc[hh], kv_ref[0, :, GROUP_WIDTH + hh * HEAD_DIM:GROUP_WIDTH + (hh + 1) * HEAD_DIM]) for hh in heads]
    for hh in heads:
        yc_ref[0, :, _head(hh)] = (oc[hh] * _silu(z_c[:, _head(hh)])).astype(BF16)

    row = lax.broadcasted_iota(jnp.int32, (L, L), 0)
    col = lax.broadcasted_iota(jnp.int32, (L, L), 1)
    causal = col <= row
    tril = jnp.where(causal, 1.0, 0.0).astype(BF16)
    chunks = range(MLSTM_STEP_CHUNKS)
    probs = [(c, hh) for c in chunks for hh in heads]
    rows = lambda c: slice(c * L, (c + 1) * L)
    q_of = lambda p: q_s[rows(p[0]), _head(p[1])]
    k_of = lambda p: k_s[rows(p[0]), _head(p[1])]
    v_of = lambda p: v_s[rows(p[0]), _head(p[1])]

    def each(f, side=False):
        if side:
            side_start()
        out = {p: f(p) for p in probs}
        if side:
            side_finish()
        return out

    a_cols, a_rows = [], []
    for c in chunks:
        g = g_s[rows(c), :]
        bb_all = _cumsum_rows(_log_sigmoid(g), tril)
        ac = jnp.where(col < N_HEADS, g, bb_all)
        a_cols.append(ac)
        a_rows.append(ac.T)
    ib_col = {(c, hh): a_cols[c][:, hh:hh + 1] for c, hh in probs}
    bb_col = {(c, hh): a_cols[c][:, N_HEADS + hh:N_HEADS + hh + 1] for c, hh in probs}
    ib_row = {(c, hh): a_rows[c][hh:hh + 1, :] for c, hh in probs}
    bb_row = {(c, hh): a_rows[c][N_HEADS + hh:N_HEADS + hh + 1, :] for c, hh in probs}
    b_end = {p: bb_col[p][L - 1:L, :] for p in probs}

    qk = {p: _dot_nt(q_of(p), k_of(p)) for p in probs}

    bb_rep = each(lambda p: jnp.broadcast_to(bb_col[p], (L, L)), side=True)
    ib_rep = each(lambda p: jnp.broadcast_to(ib_col[p], (L, L)))
    d = each(lambda p: jnp.where(causal, (bb_rep[p] - bb_row[p]) + ib_row[p], -jnp.inf), side=True)
    r_rep = each(lambda p: jnp.broadcast_to(jnp.max(d[p], axis=-1, keepdims=True), (L, L)), side=True)
    s_bf = each(lambda p: (qk[p] * jnp.exp(d[p] - r_rep[p])).astype(BF16), side=True)
    a_rep = each(lambda p: (b_end[p] - bb_rep[p]) + ib_rep[p])
    a_max = each(lambda p: jnp.max(a_rep[p], axis=0, keepdims=True))
    w_rep = each(lambda p: jnp.exp(a_rep[p] - a_max[p]))
    vw = each(lambda p: (v_of(p).astype(F32) * w_rep[p]).astype(BF16), side=True)
    nu = each(lambda p: jnp.sum(k_of(p).astype(F32) * w_rep[p], axis=0, keepdims=True))

    ones_blk = jnp.ones((L, HEAD_DIM), BF16)
    sv_rs = {p: _dot(s_bf[p], jnp.concatenate([v_of(p), ones_blk], axis=1)) for p in probs}
    sv = {p: sv_rs[p][:, :HEAD_DIM] for p in probs}
    rs_rep = {p: sv_rs[p][:, HEAD_DIM:] for p in probs}
    u = {p: lax.dot_general(vw[p], k_of(p), TN_DIMS, preferred_element_type=F32) for p in probs}

    m_in, cn_in = {}, {}
    for hh in heads:
        m_cur = m_ref[hh][:, 0:1]
        c_cur = c_ref[hh]
        n_cur = n_ref[hh]
        for c in chunks:
            p = (c, hh)
            m_in[p] = m_cur
            cn_in[p] = jnp.concatenate([c_cur, jnp.broadcast_to(n_cur, (HEAD_DIM, HEAD_DIM))],
                                       axis=0).astype(BF16)
            a_top = a_max[p][:, 0:1]
            m_new = jnp.maximum(b_end[p] + m_cur, a_top)
            decay = jnp.exp(b_end[p] + m_cur - m_new)
            e_upd = jnp.exp(a_top - m_new)
            c_cur = decay * c_cur + e_upd * u[p]
            n_cur = decay * n_cur + e_upd * nu[p]
            m_cur = m_new
        c_ref[hh] = c_cur
        n_ref[hh] = n_cur
        m_ref[hh] = jnp.broadcast_to(m_cur, (1, HEAD_DIM))

    qc_qn = {p: _dot_nt(q_of(p), cn_in[p]) for p in probs}
    q_c = {p: qc_qn[p][:, :HEAD_DIM] for p in probs}
    qn_rep = {p: qc_qn[p][:, HEAD_DIM:] for p in probs}

    tile_of = lambda ref, p: ref[rows(p[0]), _head(p[1])].astype(F32)
    g_col = each(lambda p: bb_rep[p] + m_in[p])
    m_t = each(lambda p: jnp.maximum(g_col[p], r_rep[p]))
    e_loc = each(lambda p: jnp.exp(r_rep[p] - m_t[p]))
    inter = each(lambda p: jnp.exp(g_col[p] - m_t[p]))
    den = each(lambda p: e_loc[p] * rs_rep[p] + inter[p] * qn_rep[p])
    scale = each(lambda p: 1.0 / jnp.maximum(jnp.abs(den[p]), jnp.exp(-m_t[p])))
    hg = each(lambda p: jax.nn.sigmoid(tile_of(om_s, p))
              * ((e_loc[p] * sv[p] + inter[p] * q_c[p]) * scale[p]), side=True)
    mu = each(lambda p: jnp.mean(hg[p], axis=-1, keepdims=True))
    cen = each(lambda p: hg[p] - mu[p])
    var = each(lambda p: jnp.mean(jnp.square(cen[p]), axis=-1, keepdims=True))
    assert not side_pieces and not pending
    for p in probs:
        sl = _head(p[1])
        y = cen[p] * lax.rsqrt(var[p] + NORM_EPS) * mg_ref[:, sl]
        y = (y + sk_ref[:, sl] * tile_of(xcv_s, p)) * _silu(tile_of(zm_s, p))
        ym_ref[0, rows(p[0]), sl] = y.astype(BF16)


def _front_call(x, kv, norm_g, w_in, conv_w, conv_b, wqk, wv, w_if, b_if, mnorm_g, skip_m):
    b, s, _ = x.shape
    tm = FRONT_ROWS
    ns = s // tm
    tok = lambda w: jax.ShapeDtypeStruct((b, s, w), BF16)
    tok_spec = lambda w: pl.BlockSpec((1, tm, w), lambda i, j: (i, j, 0))
    const = lambda shape: pl.BlockSpec(shape, lambda i, j: (0,) * len(shape))
    vdt_tiles = tm // V_TILE
    out_shape = (
        tok(GROUP_WIDTH),
        tok(GROUP_WIDTH), tok(GROUP_WIDTH),
        jax.ShapeDtypeStruct((b, ns * vdt_tiles, N_HEADS * VT_ROWS, V_TILE), BF16),
        tok(GROUP_WIDTH),
        tok(GROUP_WIDTH),
    )
    out_specs = (
        tok_spec(GROUP_WIDTH), tok_spec(GROUP_WIDTH), tok_spec(GROUP_WIDTH),
        pl.BlockSpec((1, vdt_tiles, N_HEADS * VT_ROWS, V_TILE), lambda i, j: (i, j, 0, 0)),
        tok_spec(GROUP_WIDTH), tok_spec(GROUP_WIDTH),
    )
    in_specs = [
        pl.BlockSpec((1, tm, D_MODEL), lambda i, j: (i, j, 0)),
        pl.BlockSpec((1, MEM_LEN, 2 * GROUP_WIDTH), lambda i, j: (i, 0, 0)),
        const((1, D_MODEL)),
        const((D_MODEL, IN_WIDTH)),
        const((CONV_WIDTH, GROUP_WIDTH)),
        const((1, GROUP_WIDTH)),
        const((N_HEADS, HEAD_DIM, 2 * HEAD_DIM)),
        const((N_HEADS, HEAD_DIM, HEAD_DIM)),
        const((3 * GROUP_WIDTH, HEAD_DIM)),
        const((1, HEAD_DIM)),
        const((1, GROUP_WIDTH)),
        const((1, GROUP_WIDTH)),
    ]
    tile_bf16 = pltpu.VMEM((tm, GROUP_WIDTH), BF16)
    return pl.pallas_call(
        _front_kernel,
        grid=(b, ns),
        in_specs=in_specs,
        out_specs=out_specs,
        out_shape=out_shape,
        scratch_shapes=[pltpu.VMEM((tm + CONV_HALO, GROUP_WIDTH), F32),
                        pltpu.VMEM((N_HEADS, HEAD_DIM, HEAD_DIM), F32),
                        pltpu.VMEM((N_HEADS, 1, HEAD_DIM), F32),
                        pltpu.VMEM((N_HEADS, 1, HEAD_DIM), F32),
                        tile_bf16, tile_bf16, tile_bf16,
                        pltpu.VMEM((tm, HEAD_DIM), F32),
                        tile_bf16, tile_bf16, tile_bf16],
        compiler_params=pltpu.CompilerParams(
            dimension_semantics=("arbitrary", "arbitrary"),
            vmem_limit_bytes=56 * 1024 * 1024),
        name="front",
    )(x, kv, norm_g, w_in, conv_w, conv_b, wqk, wv, w_if, b_if, mnorm_g, skip_m)


def _diff_kernel(q_ref, qn_ref, k_ref, vt_ref, zd_ref, g_ref, lq1_ref, lk1_ref, lq2_ref, lk2_ref,
                 x_ref, ym_ref, yc_ref, wo_ref, fg_ref, o_ref,
                 qs_ref, m_ref, l_ref, acc_ref, s0_ref, s1_ref, mx0_ref, mx1_ref, yd_ref, part_ref, qsn_ref):
    tq, tk, tv = DIFF_Q_ROWS, DIFF_KV_ROWS, V_TILE
    i = pl.program_id(1)
    feat = lax.broadcasted_iota(jnp.int32, (HEAD_DIM, tq), 0)
    for hh in range(N_HEADS):
        q_t = q_ref[0, :, _head(hh)].astype(F32).T
        qs_ref[hh, :, 0:tq] = jnp.where(feat < DIFF_QK_DIM, q_t, 0.0).astype(BF16)
        qs_ref[hh, :, tq:2 * tq] = jnp.where(feat >= DIFF_QK_DIM, q_t, 0.0).astype(BF16)
    m_ref[...] = jnp.full_like(m_ref, -jnp.inf)
    l_ref[...] = jnp.zeros_like(l_ref)
    acc_ref[...] = jnp.zeros_like(acc_ref)

    kc = DIFF_KEY_CHUNK
    sub = HEAD_DIM
    tri = (lax.broadcasted_iota(jnp.int32, (sub, sub), 0) <= lax.broadcasted_iota(jnp.int32, (sub, sub), 1))

    def mask_diagonal(s_c, c):
        hidden = jnp.full((sub, sub), -jnp.inf, F32)
        hidden_max = jnp.full((SUBLANES, sub), -jnp.inf, F32)
        row_blocks, maxes = [], None
        for r in range(kc // sub):
            kb = c * (kc // sub) + r
            blocks, row_max = [], []
            for b8 in range(2 * tq // sub):
                b = b8 % (tq // sub)
                if kb > b:
                    blocks.append(hidden)
                    row_max.append(hidden_max)
                    continue
                blk = s_c[r * sub:(r + 1) * sub, b8 * sub:(b8 + 1) * sub]
                if kb == b:
                    blk = jnp.where(tri, blk, -jnp.inf)
                blocks.append(blk)
                row_max.append(_fold_rows(blk, jnp.max))
            row_blocks.append(jnp.concatenate(blocks, axis=1))
            row_max = jnp.concatenate(row_max, axis=1)
            maxes = row_max if maxes is None else jnp.maximum(maxes, row_max)
        return jnp.concatenate(row_blocks, axis=0), maxes

    group_a = (tuple(range(0, N_HEADS // 2)), s0_ref, mx0_ref)
    group_b = (tuple(range(N_HEADS // 2, N_HEADS)), s1_ref, mx1_ref)

    def step(consume=None, produce=None):
        n_slots = N_HEADS // 2
        for slot in range(n_slots):
            if consume is not None:
                t_c, (heads_c, sc_ref, mxc_ref) = consume
                h_c = heads_c[slot]
                m_old = m_ref[h_c]
                m_new = jnp.maximum(m_old, mxc_ref[slot])
                alpha = jnp.exp2(m_old - m_new)
                pv, pb_prev = None, None
            if produce is not None:
                t_p, (heads_p, sp_ref, mxp_ref), diagonal = produce[:3]
                q_src = produce[3] if len(produce) > 3 else None
                h_p = heads_p[slot]
                mx = None
            for c in range(tk // kc):
                chunk = slice(c * kc, (c + 1) * kc)
                if produce is not None:
                    rows = pl.ds(pl.multiple_of(t_p * tk + c * kc, kc), kc)
                    q_mat = qs_ref[h_p] if q_src is None else q_src[slot]
                    s_c = _dot(k_ref[0, rows, _head(h_p)], q_mat)
                if consume is not None:
                    pb_c = jnp.exp2(sc_ref[slot, chunk, :] - m_new).astype(BF16)
                    pb_prev = pb_c if pb_prev is None else jnp.concatenate([pb_prev, pb_c], axis=0)
                    if c == tk // kc - 1:
                        v_rows = slice(h_c * VT_ROWS, (h_c + 1) * VT_ROWS)
                        vt = jnp.concatenate([vt_ref[0, t_c * (tk // tv) + j, v_rows, :]
                                              for j in range(tk // tv)], axis=1)
                        pv = _dot(vt, pb_prev)
                if produce is not None:
                    if diagonal:
                        s_c, c_max = mask_diagonal(s_c, c)
                    else:
                        c_max = _fold_rows(s_c, jnp.max)
                    sp_ref[slot, chunk, :] = s_c
                    mx = c_max if mx is None else jnp.maximum(mx, c_max)
            if consume is not None:
                l_ref[h_c] = alpha * l_ref[h_c] + pv[HEAD_DIM:HEAD_DIM + 1, :]
                acc_ref[h_c] = alpha * acc_ref[h_c] + pv[:HEAD_DIM, :]
                m_ref[h_c] = m_new
            if produce is not None:
                mxp_ref[slot] = jnp.max(mx, axis=0, keepdims=True)

    def tile(t, next_diagonal):
        step(consume=(t, group_a), produce=(t, group_b, False))
        step(consume=(t, group_b), produce=(t + 1, group_a, next_diagonal))

    def body(t, carry):
        tile(t, False)
        return carry

    @pl.when(i == 0)
    def _():
        step(produce=(0, group_a, True))

    @pl.when(i > 0)
    def _():
        lax.fori_loop(0, i - 1, body, 0)
        tile(i - 1, True)

    lam = (jnp.exp(jnp.sum(lq1_ref[...] * lk1_ref[...], axis=-1, keepdims=True))
           - jnp.exp(jnp.sum(lq2_ref[...] * lk2_ref[...], axis=-1, keepdims=True)) + LAM_INIT)

    def finish(heads):
        for hh in heads:
            sl = _head(hh)
            acc = acc_ref[hh]
            inv = 1.0 / l_ref[hh]
            o_t = acc[:, :tq] * inv[:, :tq] - lam * (acc[:, tq:] * inv[:, tq:])
            o = _rms(o_t.T, g_ref[:, sl]) * (1.0 - LAM_INIT)
            yd_ref[:, sl] = (o * _silu(zd_ref[0, :, sl].astype(F32))).astype(BF16)

    def project_rows(c, src_ref, w_rows, first):
        r_c = slice(c * HEAD_DIM, (c + 1) * HEAD_DIM)
        y = _dot(src_ref[0, r_c, :], wo_ref[w_rows, :])
        part_ref[r_c, :] = (x_ref[0, r_c, :] + y) if first else (part_ref[r_c, :] + y)

    def project_heads(heads):
        cols = slice(heads[0] * HEAD_DIM, (heads[-1] + 1) * HEAD_DIM)
        w_rows = slice(GROUP_WIDTH + cols.start, GROUP_WIDTH + cols.stop)
        return _dot(yd_ref[:, cols], wo_ref[w_rows, :])

    step(consume=(i, group_a), produce=(i, group_b, True))
    finish(group_a[0])

    last = pl.num_programs(1) - 1

    @pl.when(i < last)
    def _():
        for slot, hh in enumerate(group_a[0]):
            q_t = qn_ref[0, :, _head(hh)].astype(F32).T
            qsn_ref[slot, :, 0:tq] = jnp.where(feat < DIFF_QK_DIM, q_t, 0.0).astype(BF16)
            qsn_ref[slot, :, tq:2 * tq] = jnp.where(feat >= DIFF_QK_DIM, q_t, 0.0).astype(BF16)
        step(consume=(i, group_b), produce=(0, group_a, False, qsn_ref))

    @pl.when(i == last)
    def _():
        step(consume=(i, group_b))

    for c in range(tq // HEAD_DIM):
        project_rows(c, ym_ref, slice(0, GROUP_WIDTH), True)
    for c in range(tq // HEAD_DIM):
        project_rows(c, yc_ref, slice(2 * GROUP_WIDTH, 3 * GROUP_WIDTH), False)
    y_a = project_heads(group_a[0])
    finish(group_b[0])
    o_ref[0] = _rms(part_ref[...] + y_a + project_heads(group_b[0]), fg_ref[...])


def _diff_out_call(qd, kd, vdt, zd, dnorm_g, lq1, lk1, lq2, lk2, x, ym, yc, w_out, final_g):
    n_q = qd.shape[1] // DIFF_Q_ROWS
    b, s, _ = qd.shape
    tq = DIFF_Q_ROWS
    nkv = vdt.shape[1]
    const = lambda shape: pl.BlockSpec(shape, lambda i, j: (0,) * len(shape))
    tok_spec = lambda w: pl.BlockSpec((1, tq, w), lambda i, j: (i, j, 0))
    return pl.pallas_call(
        _diff_kernel,
        grid=(b, s // tq),
        in_specs=[
            tok_spec(GROUP_WIDTH),
            pl.BlockSpec((1, tq, GROUP_WIDTH), lambda i, j: (i, jnp.minimum(j + 1, n_q - 1), 0)),
            pl.BlockSpec((1, s, GROUP_WIDTH), lambda i, j: (i, 0, 0)),
            pl.BlockSpec((1, nkv, N_HEADS * VT_ROWS, V_TILE), lambda i, j: (i, 0, 0, 0)),
            tok_spec(GROUP_WIDTH),
            const((1, GROUP_WIDTH)),
            const((1, DIFF_QK_DIM)), const((1, DIFF_QK_DIM)), const((1, DIFF_QK_DIM)), const((1, DIFF_QK_DIM)),
            tok_spec(D_MODEL), tok_spec(GROUP_WIDTH), tok_spec(GROUP_WIDTH),
            const((3 * GROUP_WIDTH, D_MODEL)), const((1, D_MODEL)),
        ],
        out_specs=tok_spec(D_MODEL),
        out_shape=jax.ShapeDtypeStruct((b, s, D_MODEL), F32),
        scratch_shapes=[pltpu.VMEM((N_HEADS, HEAD_DIM, 2 * tq), BF16),
                        pltpu.VMEM((N_HEADS, 1, 2 * tq), F32),
                        pltpu.VMEM((N_HEADS, 1, 2 * tq), F32),
                        pltpu.VMEM((N_HEADS, HEAD_DIM, 2 * tq), F32),
                        pltpu.VMEM((N_HEADS // 2, DIFF_KV_ROWS, 2 * tq), F32),
                        pltpu.VMEM((N_HEADS // 2, DIFF_KV_ROWS, 2 * tq), F32),
                        pltpu.VMEM((N_HEADS // 2, 1, 2 * tq), F32),
                        pltpu.VMEM((N_HEADS // 2, 1, 2 * tq), F32),
                        pltpu.VMEM((tq, GROUP_WIDTH), BF16),
                        pltpu.VMEM((tq, D_MODEL), F32),
                        pltpu.VMEM((N_HEADS // 2, HEAD_DIM, 2 * tq), BF16)],
        compiler_params=pltpu.CompilerParams(
            dimension_semantics=("arbitrary", "arbitrary"),
            vmem_limit_bytes=58 * 1024 * 1024),
        name="diffattn_out",
    )(qd, qd, kd, vdt, zd, dnorm_g, lq1, lk1, lq2, lk2, x, ym, yc, w_out, final_g)


def kernel(x, mem, norm_g, w_in, conv_w, conv_b, wq_m, wk_m, wv_m, w_if, b_if, mnorm_g, skip_m,
           lam_q1, lam_k1, lam_q2, lam_k2, dnorm_g, mem_norm_g, w_mem_kv, w_out, final_g):
    b, s, d = x.shape
    assert (d, s % DIFF_KV_ROWS, DIFF_KV_ROWS % V_TILE, FRONT_ROWS % V_TILE) == (D_MODEL, 0, 0, 0)
    assert (s % FRONT_ROWS, DIFF_KV_ROWS % DIFF_KEY_CHUNK) == (0, 0)
    assert DIFF_Q_ROWS == DIFF_KV_ROWS and DIFF_KEY_CHUNK % HEAD_DIM == 0
    assert norm_g.shape[0] == 1, "single-layer kernel"
    l = 0
    w_if_pad = jnp.pad(w_if[l], ((0, 0), (0, HEAD_DIM - w_if.shape[-1]))).astype(BF16)
    b_if_pad = jnp.pad(b_if[l], (0, HEAD_DIM - b_if.shape[-1]))[None, :]

    kv = _memkv_call(mem, mem_norm_g[l][None, :], w_mem_kv[l].astype(BF16))
    ym, qd, kd, vdt, zd, yc = _front_call(
        x, kv, norm_g[l][None, :], w_in[l].astype(BF16), conv_w[l], conv_b[l][None, :],
        jnp.concatenate([wq_m[l], wk_m[l]], axis=-1).astype(BF16), wv_m[l].astype(BF16), w_if_pad, b_if_pad,
        mnorm_g[l][None, :], skip_m[l][None, :])
    return _diff_out_call(qd, kd, vdt, zd, dnorm_g[l][None, :],
                          lam_q1[l][None, :], lam_k1[l][None, :], lam_q2[l][None, :], lam_k2[l][None, :],
                          x, ym, yc, w_out[l].astype(BF16), final_g[None, :])
```

```python
import math

import jax
import jax.numpy as jnp
from jax import lax
from jax.experimental import pallas as pl
from jax.experimental.pallas import tpu as pltpu

F32 = jnp.float32
BF16 = jnp.bfloat16

D_MODEL = 1024
N_HEADS = 4
HEAD_DIM = 128
SUBLANES = 8
V7X_VMEM_BYTES = 64 * 1024 * 1024
VMEM_LIMIT_BYTES = V7X_VMEM_BYTES - 6 * 1024 * 1024
GROUP_WIDTH = N_HEADS * HEAD_DIM
DIFF_QK_DIM = 64
CONV_WIDTH = 4
MLSTM_CHUNK = 128
MLSTM_STEP_CHUNKS = 4
MEM_LEN = 256
IN_WIDTH = 9 * GROUP_WIDTH
NORM_EPS = 1e-6
LAM_INIT = 0.8 - 0.6 * math.exp(-0.3 * 0)
LOG2E = math.log2(math.e)

OFF_XM, OFF_OM, OFF_ZM, OFF_QD, OFF_KD, OFF_VD, OFF_ZD, OFF_QC, OFF_ZC = (
    i * GROUP_WIDTH for i in range(9))

FRONT_ROWS = MLSTM_STEP_CHUNKS * MLSTM_CHUNK
V_TILE = 256
VT_PAD = 16
VT_ROWS = HEAD_DIM + VT_PAD
DIFF_Q_ROWS = 512
DIFF_KV_ROWS = 512
DIFF_KEY_CHUNK = 512
CONV_HALO = 8

NT_DIMS = (((1,), (1,)), ((), ()))
TN_DIMS = (((0,), (0,)), ((), ()))


def _dot(a, b):
    return jnp.dot(a, b, preferred_element_type=F32)


def _dot_nt(a, b):
    return lax.dot_general(a, b, NT_DIMS, preferred_element_type=F32)


def _silu(x):
    return x * jax.nn.sigmoid(x)


def _rms(x, g):
    return x * lax.rsqrt(jnp.mean(x * x, axis=-1, keepdims=True) + NORM_EPS) * g


def _head(h):
    return slice(h * HEAD_DIM, (h + 1) * HEAD_DIM)


def _fold_rows(x, op):
    r, n = x.shape
    return op(x.reshape(r // SUBLANES, SUBLANES, n), axis=0)


def _log_sigmoid(x):
    return jnp.minimum(x, 0.0) - jnp.log1p(jnp.exp(-jnp.abs(x)))


def _cumsum_rows(f, tril):
    hi = f.astype(BF16)
    r1 = f - hi.astype(F32)
    mid = r1.astype(BF16)
    lo = (r1 - mid.astype(F32)).astype(BF16)
    return _dot(tril, hi) + _dot(tril, mid) + _dot(tril, lo)


def _memkv_kernel(mem_ref, g_ref, w_ref, kv_ref):
    y = _rms(mem_ref[0], g_ref[...])
    kv_ref[0] = _dot(y.astype(BF16), w_ref[...]).astype(BF16)


def _memkv_call(mem, g, w):
    b = mem.shape[0]
    return pl.pallas_call(
        _memkv_kernel,
        grid=(b,),
        in_specs=[
            pl.BlockSpec((1, MEM_LEN, D_MODEL), lambda i: (i, 0, 0)),
            pl.BlockSpec((1, D_MODEL), lambda i: (0, 0)),
            pl.BlockSpec((D_MODEL, 2 * GROUP_WIDTH), lambda i: (0, 0)),
        ],
        out_specs=pl.BlockSpec((1, MEM_LEN, 2 * GROUP_WIDTH), lambda i: (i, 0, 0)),
        out_shape=jax.ShapeDtypeStruct((b, MEM_LEN, 2 * GROUP_WIDTH), BF16),
        compiler_params=pltpu.CompilerParams(dimension_semantics=("arbitrary",)),
        name="memkv",
    )(mem, g, w)


def _front_kernel(x_ref, kv_ref, ng_ref, win_ref, cw_ref, cb_ref, wqk_ref, wv_ref, wif_ref, bif_ref,
                  mg_ref, sk_ref,
                  ym_ref, qd_ref, kd_ref, vdt_ref, zd_ref, yc_ref,
                  conv_ref, c_ref, n_ref, m_ref, q_s, k_s, v_s, g_s, om_s, zm_s, xcv_s):
    tm = FRONT_ROWS
    L = MLSTM_CHUNK
    heads = range(N_HEADS)

    @pl.when(pl.program_id(1) == 0)
    def _():
        conv_ref[tm:tm + CONV_HALO, :] = jnp.zeros((CONV_HALO, GROUP_WIDTH), F32)
        c_ref[...] = jnp.zeros_like(c_ref)
        n_ref[...] = jnp.zeros_like(n_ref)
        m_ref[...] = jnp.zeros_like(m_ref)

    h = _rms(x_ref[0], ng_ref[...]).astype(BF16)

    def proj(off):
        return _dot(h, win_ref[:, off:off + GROUP_WIDTH])

    def put_vdt(val):
        v_t = val.T.astype(BF16)
        pad_row = lax.broadcasted_iota(jnp.int32, (VT_PAD, tm), 0)
        pad = jnp.where(pad_row == 0, 1.0, 0.0).astype(BF16)
        v_aug = jnp.concatenate([blk for hh in heads for blk in (v_t[_head(hh), :], pad)], axis=0)
        for j in range(tm // V_TILE):
            vdt_ref[0, j] = v_aug[:, j * V_TILE:(j + 1) * V_TILE]

    def put(ref, scale=None):
        def sink(val):
            ref[...] = (val if scale is None else val * scale).astype(BF16).reshape(ref.shape)
        return sink

    side_pieces = [(OFF_OM, put(om_s)), (OFF_ZM, put(zm_s)),
                   (OFF_QD, put(qd_ref, DIFF_QK_DIM ** -0.5 * LOG2E)), (OFF_KD, put(kd_ref)),
                   (OFF_VD, put_vdt), (OFF_ZD, put(zd_ref))]
    pending = []

    def side_start():
        if side_pieces:
            off, sink = side_pieces.pop(0)
            pending.append((proj(off), sink))

    def side_finish():
        if pending:
            val, sink = pending.pop(0)
            sink(val)

    x_m = proj(OFF_XM)
    q_cb = proj(OFF_QC).astype(BF16)
    z_c = proj(OFF_ZC)

    conv_ref[0:CONV_HALO, :] = conv_ref[tm:tm + CONV_HALO, :]
    conv_ref[CONV_HALO:CONV_HALO + tm, :] = x_m
    acc = jnp.broadcast_to(cb_ref[...], (tm, GROUP_WIDTH))
    for j in range(CONV_WIDTH):
        start = CONV_HALO - (CONV_WIDTH - 1) + j
        acc = acc + cw_ref[j:j + 1, :] * conv_ref[start:start + tm, :]
    sc = [_dot_nt(q_cb[:, _head(hh)], kv_ref[0, :, _head(hh)]) * HEAD_DIM ** -0.5 for hh in heads]
    x_cv = _silu(acc)
    xcv_b = x_cv.astype(BF16)
    xcv_s[...] = xcv_b
    xm_b = x_m.astype(BF16)

    qk_p = [_dot(xcv_b[:, _head(hh)], wqk_ref[hh]) for hh in heads]
    vv_p = [_dot(xm_b[:, _head(hh)], wv_ref[hh]) for hh in heads]

    q_all = jnp.concatenate([qk_p[hh][:, :HEAD_DIM] for hh in heads], axis=1).astype(BF16)
    k_all = jnp.concatenate([qk_p[hh][:, HEAD_DIM:] for hh in heads], axis=1)
    v_all = jnp.concatenate(vv_p, axis=1).astype(BF16)
    q_s[...] = q_all
    k_s[...] = (k_all * HEAD_DIM ** -0.5).astype(BF16)
    v_s[...] = v_all
    g_s[...] = (jnp.broadcast_to(bif_ref[...], (tm, HEAD_DIM))
                + _dot(q_all, wif_ref[0:GROUP_WIDTH, :])
                + _dot(k_all.astype(BF16), wif_ref[GROUP_WIDTH:2 * GROUP_WIDTH, :])
                + _dot(v_all, wif_ref[2 * GROUP_WIDTH:3 * GROUP_WIDTH, :]))

    mx = [jnp.max(sc[hh], axis=-1, keepdims=True) for hh in heads]
    pm = [jnp.exp(sc[hh] - mx[hh]) for hh in heads]
    inv = [1.0 / jnp.sum(pm[hh], axis=-1, keepdims=True) for hh in heads]
    pc = [(pm[hh] * inv[hh]).astype(BF16) for hh in heads]
    oc = [_dot(pc[hh], kv_ref[0, :, GROUP_WIDTH + hh * HEAD_DIM:GROUP_WIDTH + (hh + 1) * HEAD_DIM]) for hh in heads]
    for hh in heads:
        yc_ref[0, :, _head(hh)] = (oc[hh] * _silu(z_c[:, _head(hh)])).astype(BF16)

    row = lax.broadcasted_iota(jnp.int32, (L, L), 0)
    col = lax.broadcasted_iota(jnp.int32, (L, L), 1)
    causal = col <= row
    tril = jnp.where(causal, 1.0, 0.0).astype(BF16)
    chunks = range(MLSTM_STEP_CHUNKS)
    probs = [(c, hh) for c in chunks for hh in heads]
    rows = lambda c: slice(c * L, (c + 1) * L)
    q_of = lambda p: q_s[rows(p[0]), _head(p[1])]
    k_of = lambda p: k_s[rows(p[0]), _head(p[1])]
    v_of = lambda p: v_s[rows(p[0]), _head(p[1])]

    def each(f, side=False):
        if side:
            side_start()
        out = {p: f(p) for p in probs}
        if side:
            side_finish()
        return out

    a_cols, a_rows = [], []
    for c in chunks:
        g = g_s[rows(c), :]
        bb_all = _cumsum_rows(_log_sigmoid(g), tril)
        ac = jnp.where(col < N_HEADS, g, bb_all)
        a_cols.append(ac)
        a_rows.append(ac.T)
    ib_col = {(c, hh): a_cols[c][:, hh:hh + 1] for c, hh in probs}
    bb_col = {(c, hh): a_cols[c][:, N_HEADS + hh:N_HEADS + hh + 1] for c, hh in probs}
    ib_row = {(c, hh): a_rows[c][hh:hh + 1, :] for c, hh in probs}
    bb_row = {(c, hh): a_rows[c][N_HEADS + hh:N_HEADS + hh + 1, :] for c, hh in probs}
    b_end = {p: bb_col[p][L - 1:L, :] for p in probs}

    qk = {p: _dot_nt(q_of(p), k_of(p)) for p in probs}

    d = each(lambda p: jnp.where(causal, (bb_col[p] - bb_row[p]) + ib_row[p], -jnp.inf), side=True)
    r = each(lambda p: jnp.max(d[p], axis=-1, keepdims=True))
    s_loc = each(lambda p: qk[p] * jnp.exp(d[p] - r[p]), side=True)
    rs = each(lambda p: jnp.sum(s_loc[p], axis=-1, keepdims=True))
    s_bf = each(lambda p: s_loc[p].astype(BF16))
    a = each(lambda p: (b_end[p] - bb_col[p]) + ib_col[p], side=True)
    a_max = each(lambda p: jnp.max(a[p], axis=0, keepdims=True))
    w_loc = each(lambda p: jnp.broadcast_to(jnp.exp(a[p] - a_max[p]), (L, HEAD_DIM)))
    vw = each(lambda p: (v_of(p).astype(F32) * w_loc[p]).astype(BF16), side=True)
    nu = each(lambda p: jnp.sum(k_of(p).astype(F32) * w_loc[p], axis=0, keepdims=True))

    sv = {p: _dot(s_bf[p], v_of(p)) for p in probs}
    u = {p: lax.dot_general(vw[p], k_of(p), TN_DIMS, preferred_element_type=F32) for p in probs}

    m_in, c_in, n_in = {}, {}, {}
    for hh in heads:
        m_cur = m_ref[hh][:, 0:1]
        c_cur = c_ref[hh]
        n_cur = n_ref[hh]
        for c in chunks:
            p = (c, hh)
            m_in[p], c_in[p], n_in[p] = m_cur, c_cur.astype(BF16), n_cur
            m_new = jnp.maximum(b_end[p] + m_cur, a_max[p])
            decay = jnp.exp(b_end[p] + m_cur - m_new)
            e_upd = jnp.exp(a_max[p] - m_new)
            c_cur = decay * c_cur + e_upd * u[p]
            n_cur = decay * n_cur + e_upd * nu[p]
            m_cur = m_new
        c_ref[hh] = c_cur
        n_ref[hh] = n_cur
        m_ref[hh] = jnp.broadcast_to(m_cur, (1, HEAD_DIM))

    q_c = {p: _dot_nt(q_of(p), c_in[p]) for p in probs}

    tile_of = lambda ref, p: ref[rows(p[0]), _head(p[1])].astype(F32)
    g_col = each(lambda p: bb_col[p] + m_in[p])
    m_t = each(lambda p: jnp.maximum(g_col[p], r[p]))
    e_loc = each(lambda p: jnp.exp(r[p] - m_t[p]))
    inter = each(lambda p: jnp.exp(g_col[p] - m_t[p]))
    q_n = each(lambda p: jnp.sum(q_of(p).astype(F32) * n_in[p], axis=-1, keepdims=True))
    den = each(lambda p: e_loc[p] * rs[p] + inter[p] * q_n[p])
    scale = each(lambda p: 1.0 / jnp.maximum(jnp.abs(den[p]), jnp.exp(-m_t[p])))
    w_intra = each(lambda p: e_loc[p] * scale[p])
    w_inter = each(lambda p: inter[p] * scale[p])
    hg = each(lambda p: jax.nn.sigmoid(tile_of(om_s, p))
              * (w_intra[p] * sv[p] + w_inter[p] * q_c[p]), side=True)
    mu = each(lambda p: jnp.mean(hg[p], axis=-1, keepdims=True))
    cen = each(lambda p: hg[p] - mu[p], side=True)
    var = each(lambda p: jnp.mean(jnp.square(cen[p]), axis=-1, keepdims=True))
    assert not side_pieces and not pending
    for p in probs:
        sl = _head(p[1])
        y = cen[p] * lax.rsqrt(var[p] + NORM_EPS) * mg_ref[:, sl]
        y = (y + sk_ref[:, sl] * tile_of(xcv_s, p)) * _silu(tile_of(zm_s, p))
        ym_ref[0, rows(p[0]), sl] = y.astype(BF16)


def _front_call(x, kv, norm_g, w_in, conv_w, conv_b, wqk, wv, w_if, b_if, mnorm_g, skip_m):
    b, s, _ = x.shape
    tm = FRONT_ROWS
    ns = s // tm
    tok = lambda w: jax.ShapeDtypeStruct((b, s, w), BF16)
    tok_spec = lambda w: pl.BlockSpec((1, tm, w), lambda i, j: (i, j, 0))
    const = lambda shape: pl.BlockSpec(shape, lambda i, j: (0,) * len(shape))
    vdt_tiles = tm // V_TILE
    out_shape = (
        tok(GROUP_WIDTH),
        tok(GROUP_WIDTH), tok(GROUP_WIDTH),
        jax.ShapeDtypeStruct((b, ns * vdt_tiles, N_HEADS * VT_ROWS, V_TILE), BF16),
        tok(GROUP_WIDTH),
        tok(GROUP_WIDTH),
    )
    out_specs = (
        tok_spec(GROUP_WIDTH), tok_spec(GROUP_WIDTH), tok_spec(GROUP_WIDTH),
        pl.BlockSpec((1, vdt_tiles, N_HEADS * VT_ROWS, V_TILE), lambda i, j: (i, j, 0, 0)),
        tok_spec(GROUP_WIDTH), tok_spec(GROUP_WIDTH),
    )
    in_specs = [
        pl.BlockSpec((1, tm, D_MODEL), lambda i, j: (i, j, 0)),
        pl.BlockSpec((1, MEM_LEN, 2 * GROUP_WIDTH), lambda i, j: (i, 0, 0)),
        const((1, D_MODEL)),
        const((D_MODEL, IN_WIDTH)),
        const((CONV_WIDTH, GROUP_WIDTH)),
        const((1, GROUP_WIDTH)),
        const((N_HEADS, HEAD_DIM, 2 * HEAD_DIM)),
        const((N_HEADS, HEAD_DIM, HEAD_DIM)),
        const((3 * GROUP_WIDTH, HEAD_DIM)),
        const((1, HEAD_DIM)),
        const((1, GROUP_WIDTH)),
        const((1, GROUP_WIDTH)),
    ]
    tile_bf16 = pltpu.VMEM((tm, GROUP_WIDTH), BF16)
    return pl.pallas_call(
        _front_kernel,
        grid=(b, ns),
        in_specs=in_specs,
        out_specs=out_specs,
        out_shape=out_shape,
        scratch_shapes=[pltpu.VMEM((tm + CONV_HALO, GROUP_WIDTH), F32),
                        pltpu.VMEM((N_HEADS, HEAD_DIM, HEAD_DIM), F32),
                        pltpu.VMEM((N_HEADS, 1, HEAD_DIM), F32),
                        pltpu.VMEM((N_HEADS, 1, HEAD_DIM), F32),
                        tile_bf16, tile_bf16, tile_bf16,
                        pltpu.VMEM((tm, HEAD_DIM), F32),
                        tile_bf16, tile_bf16, tile_bf16],
        compiler_params=pltpu.CompilerParams(
            dimension_semantics=("arbitrary", "arbitrary"),
            vmem_limit_bytes=VMEM_LIMIT_BYTES),
        name="front",
    )(x, kv, norm_g, w_in, conv_w, conv_b, wqk, wv, w_if, b_if, mnorm_g, skip_m)


def _diff_kernel(q_ref, qn_ref, k_ref, vt_ref, zd_ref, g_ref, lq1_ref, lk1_ref, lq2_ref, lk2_ref,
                 x_ref, ym_ref, yc_ref, wo_ref, fg_ref, o_ref,
                 qs_ref, m_ref, l_ref, acc_ref, s0_ref, s1_ref, mx0_ref, mx1_ref, yd_ref, part_ref, qsn_ref):
    tq, tk, tv = DIFF_Q_ROWS, DIFF_KV_ROWS, V_TILE
    i = pl.program_id(1)
    feat = lax.broadcasted_iota(jnp.int32, (HEAD_DIM, tq), 0)
    for hh in range(N_HEADS):
        q_t = q_ref[0, :, _head(hh)].astype(F32).T
        qs_ref[hh, :, 0:tq] = jnp.where(feat < DIFF_QK_DIM, q_t, 0.0).astype(BF16)
        qs_ref[hh, :, tq:2 * tq] = jnp.where(feat >= DIFF_QK_DIM, q_t, 0.0).astype(BF16)
    m_ref[...] = jnp.full_like(m_ref, -jnp.inf)
    l_ref[...] = jnp.zeros_like(l_ref)
    acc_ref[...] = jnp.zeros_like(acc_ref)

    kc = DIFF_KEY_CHUNK
    sub = HEAD_DIM
    tri = (lax.broadcasted_iota(jnp.int32, (sub, sub), 0) <= lax.broadcasted_iota(jnp.int32, (sub, sub), 1))

    def mask_diagonal(s_c, c):
        hidden = jnp.full((sub, sub), -jnp.inf, F32)
        hidden_max = jnp.full((SUBLANES, sub), -jnp.inf, F32)
        row_blocks, maxes = [], None
        for r in range(kc // sub):
            kb = c * (kc // sub) + r
            blocks, row_max = [], []
            for b8 in range(2 * tq // sub):
                b = b8 % (tq // sub)
                if kb > b:
                    blocks.append(hidden)
                    row_max.append(hidden_max)
                    continue
                blk = s_c[r * sub:(r + 1) * sub, b8 * sub:(b8 + 1) * sub]
                if kb == b:
                    blk = jnp.where(tri, blk, -jnp.inf)
                blocks.append(blk)
                row_max.append(_fold_rows(blk, jnp.max))
            row_blocks.append(jnp.concatenate(blocks, axis=1))
            row_max = jnp.concatenate(row_max, axis=1)
            maxes = row_max if maxes is None else jnp.maximum(maxes, row_max)
        return jnp.concatenate(row_blocks, axis=0), maxes

    group_a = (tuple(range(0, N_HEADS // 2)), s0_ref, mx0_ref)
    group_b = (tuple(range(N_HEADS // 2, N_HEADS)), s1_ref, mx1_ref)

    def step(consume=None, produce=None):
        n_slots = N_HEADS // 2
        for slot in range(n_slots):
            if consume is not None:
                t_c, (heads_c, sc_ref, mxc_ref) = consume
                h_c = heads_c[slot]
                m_old = m_ref[h_c]
                m_new = jnp.maximum(m_old, mxc_ref[slot])
                alpha = jnp.exp2(m_old - m_new)
                pv, pb_prev = None, None
            if produce is not None:
                t_p, (heads_p, sp_ref, mxp_ref), diagonal = produce[:3]
                q_src = produce[3] if len(produce) > 3 else None
                h_p = heads_p[slot]
                mx = None
            for c in range(tk // kc):
                chunk = slice(c * kc, (c + 1) * kc)
                if produce is not None:
                    rows = pl.ds(pl.multiple_of(t_p * tk + c * kc, kc), kc)
                    q_mat = qs_ref[h_p] if q_src is None else q_src[slot]
                    s_c = _dot(k_ref[0, rows, _head(h_p)], q_mat)
                if consume is not None:
                    pb_c = jnp.exp2(sc_ref[slot, chunk, :] - m_new).astype(BF16)
                    pb_prev = pb_c if pb_prev is None else jnp.concatenate([pb_prev, pb_c], axis=0)
                    if c == tk // kc - 1:
                        v_rows = slice(h_c * VT_ROWS, (h_c + 1) * VT_ROWS)
                        vt = jnp.concatenate([vt_ref[0, t_c * (tk // tv) + j, v_rows, :]
                                              for j in range(tk // tv)], axis=1)
                        pv = _dot(vt, pb_prev)
                if produce is not None:
                    if diagonal:
                        s_c, c_max = mask_diagonal(s_c, c)
                    else:
                        c_max = _fold_rows(s_c, jnp.max)
                    sp_ref[slot, chunk, :] = s_c
                    mx = c_max if mx is None else jnp.maximum(mx, c_max)
            if consume is not None:
                l_ref[h_c] = alpha * l_ref[h_c] + pv[HEAD_DIM:HEAD_DIM + 1, :]
                acc_ref[h_c] = alpha * acc_ref[h_c] + pv[:HEAD_DIM, :]
                m_ref[h_c] = m_new
            if produce is not None:
                mxp_ref[slot] = jnp.max(mx, axis=0, keepdims=True)

    def tile(t, next_diagonal):
        step(consume=(t, group_a), produce=(t, group_b, False))
        step(consume=(t, group_b), produce=(t + 1, group_a, next_diagonal))

    def body(t, carry):
        tile(t, False)
        return carry

    @pl.when(i == 0)
    def _():
        step(produce=(0, group_a, True))

    @pl.when(i > 0)
    def _():
        lax.fori_loop(0, i - 1, body, 0)
        tile(i - 1, True)

    lam = (jnp.exp(jnp.sum(lq1_ref[...] * lk1_ref[...], axis=-1, keepdims=True))
           - jnp.exp(jnp.sum(lq2_ref[...] * lk2_ref[...], axis=-1, keepdims=True)) + LAM_INIT)

    def finish(heads):
        for hh in heads:
            sl = _head(hh)
            acc = acc_ref[hh]
            inv = 1.0 / l_ref[hh]
            o_t = acc[:, :tq] * inv[:, :tq] - lam * (acc[:, tq:] * inv[:, tq:])
            o = _rms(o_t.T, g_ref[:, sl]) * (1.0 - LAM_INIT)
            yd_ref[:, sl] = (o * _silu(zd_ref[0, :, sl].astype(F32))).astype(BF16)

    def project_rows(c, src_ref, w_rows, first):
        r_c = slice(c * HEAD_DIM, (c + 1) * HEAD_DIM)
        y = _dot(src_ref[0, r_c, :], wo_ref[w_rows, :])
        part_ref[r_c, :] = (x_ref[0, r_c, :] + y) if first else (part_ref[r_c, :] + y)

    def project_heads(heads):
        cols = slice(heads[0] * HEAD_DIM, (heads[-1] + 1) * HEAD_DIM)
        w_rows = slice(GROUP_WIDTH + cols.start, GROUP_WIDTH + cols.stop)
        return _dot(yd_ref[:, cols], wo_ref[w_rows, :])

    step(consume=(i, group_a), produce=(i, group_b, True))
    finish(group_a[0])

    last = pl.num_programs(1) - 1

    @pl.when(i < last)
    def _():
        for slot, hh in enumerate(group_a[0]):
            q_t = qn_ref[0, :, _head(hh)].astype(F32).T
            qsn_ref[slot, :, 0:tq] = jnp.where(feat < DIFF_QK_DIM, q_t, 0.0).astype(BF16)
            qsn_ref[slot, :, tq:2 * tq] = jnp.where(feat >= DIFF_QK_DIM, q_t, 0.0).astype(BF16)
        step(consume=(i, group_b), produce=(0, group_a, False, qsn_ref))

    @pl.when(i == last)
    def _():
        step(consume=(i, group_b))

    for c in range(tq // HEAD_DIM):
        project_rows(c, ym_ref, slice(0, GROUP_WIDTH), True)
    for c in range(tq // HEAD_DIM):
        project_rows(c, yc_ref, slice(2 * GROUP_WIDTH, 3 * GROUP_WIDTH), False)
    y_a = project_heads(group_a[0])
    finish(group_b[0])
    o_ref[0] = _rms(part_ref[...] + y_a + project_heads(group_b[0]), fg_ref[...])


def _diff_out_call(qd, kd, vdt, zd, dnorm_g, lq1, lk1, lq2, lk2, x, ym, yc, w_out, final_g):
    n_q = qd.shape[1] // DIFF_Q_ROWS
    b, s, _ = qd.shape
    tq = DIFF_Q_ROWS
    nkv = vdt.shape[1]
    const = lambda shape: pl.BlockSpec(shape, lambda i, j: (0,) * len(shape))
    tok_spec = lambda w: pl.BlockSpec((1, tq, w), lambda i, j: (i, j, 0))
    return pl.pallas_call(
        _diff_kernel,
        grid=(b, s // tq),
        in_specs=[
            tok_spec(GROUP_WIDTH),
            pl.BlockSpec((1, tq, GROUP_WIDTH), lambda i, j: (i, jnp.minimum(j + 1, n_q - 1), 0)),
            pl.BlockSpec((1, s, GROUP_WIDTH), lambda i, j: (i, 0, 0)),
            pl.BlockSpec((1, nkv, N_HEADS * VT_ROWS, V_TILE), lambda i, j: (i, 0, 0, 0)),
            tok_spec(GROUP_WIDTH),
            const((1, GROUP_WIDTH)),
            const((1, DIFF_QK_DIM)), const((1, DIFF_QK_DIM)), const((1, DIFF_QK_DIM)), const((1, DIFF_QK_DIM)),
            tok_spec(D_MODEL), tok_spec(GROUP_WIDTH), tok_spec(GROUP_WIDTH),
            const((3 * GROUP_WIDTH, D_MODEL)), const((1, D_MODEL)),
        ],
        out_specs=tok_spec(D_MODEL),
        out_shape=jax.ShapeDtypeStruct((b, s, D_MODEL), F32),
        scratch_shapes=[pltpu.VMEM((N_HEADS, HEAD_DIM, 2 * tq), BF16),
                        pltpu.VMEM((N_HEADS, 1, 2 * tq), F32),
                        pltpu.VMEM((N_HEADS, 1, 2 * tq), F32),
                        pltpu.VMEM((N_HEADS, HEAD_DIM, 2 * tq), F32),
                        pltpu.VMEM((N_HEADS // 2, DIFF_KV_ROWS, 2 * tq), F32),
                        pltpu.VMEM((N_HEADS // 2, DIFF_KV_ROWS, 2 * tq), F32),
                        pltpu.VMEM((N_HEADS // 2, 1, 2 * tq), F32),
                        pltpu.VMEM((N_HEADS // 2, 1, 2 * tq), F32),
                        pltpu.VMEM((tq, GROUP_WIDTH), BF16),
                        pltpu.VMEM((tq, D_MODEL), F32),
                        pltpu.VMEM((N_HEADS // 2, HEAD_DIM, 2 * tq), BF16)],
        compiler_params=pltpu.CompilerParams(
            dimension_semantics=("arbitrary", "arbitrary"),
            vmem_limit_bytes=VMEM_LIMIT_BYTES),
        name="diffattn_out",
    )(qd, qd, kd, vdt, zd, dnorm_g, lq1, lk1, lq2, lk2, x, ym, yc, w_out, final_g)


def kernel(x, mem, norm_g, w_in, conv_w, conv_b, wq_m, wk_m, wv_m, w_if, b_if, mnorm_g, skip_m,
           lam_q1, lam_k1, lam_q2, lam_k2, dnorm_g, mem_norm_g, w_mem_kv, w_out, final_g):
    b, s, d = x.shape
    assert (d, s % DIFF_KV_ROWS, DIFF_KV_ROWS % V_TILE, FRONT_ROWS % V_TILE) == (D_MODEL, 0, 0, 0)
    assert (s % FRONT_ROWS, DIFF_KV_ROWS % DIFF_KEY_CHUNK) == (0, 0)
    assert DIFF_Q_ROWS == DIFF_KV_ROWS and DIFF_KEY_CHUNK % HEAD_DIM == 0
    assert norm_g.shape[0] == 1, "single-layer kernel"
    l = 0
    w_if_pad = jnp.pad(w_if[l], ((0, 0), (0, HEAD_DIM - w_if.shape[-1]))).astype(BF16)
    b_if_pad = jnp.pad(b_if[l], (0, HEAD_DIM - b_if.shape[-1]))[None, :]

    kv = _memkv_call(mem, mem_norm_g[l][None, :], w_mem_kv[l].astype(BF16))
    ym, qd, kd, vdt, zd, yc = _front_call(
        x, kv, norm_g[l][None, :], w_in[l].astype(BF16), conv_w[l], conv_b[l][None, :],
        jnp.concatenate([wq_m[l], wk_m[l]], axis=-1).astype(BF16), wv_m[l].astype(BF16), w_if_pad, b_if_pad,
        mnorm_g[l][None, :], skip_m[l][None, :])
    return _diff_out_call(qd, kd, vdt, zd, dnorm_g[l][None, :],
                          lam_q1[l][None, :], lam_k1[l][None, :], lam_q2[l][None, :], lam_k2[l][None, :],
                          x, ym, yc, w_out[l].astype(BF16), final_g[None, :])
```

```python
import math

import jax
import jax.numpy as jnp
from jax import lax
from jax.experimental import pallas as pl
from jax.experimental.pallas import tpu as pltpu

F32 = jnp.float32
BF16 = jnp.bfloat16

D_MODEL = 1024
N_HEADS = 4
HEAD_DIM = 128
SUBLANES = 8
V7X_VMEM_BYTES = 64 * 1024 * 1024
VMEM_LIMIT_BYTES = V7X_VMEM_BYTES - 6 * 1024 * 1024
GROUP_WIDTH = N_HEADS * HEAD_DIM
DIFF_QK_DIM = 64
CONV_WIDTH = 4
MLSTM_CHUNK = 128
MLSTM_STEP_CHUNKS = 4
MEM_LEN = 256
IN_WIDTH = 9 * GROUP_WIDTH
NORM_EPS = 1e-6
LAM_INIT = 0.8 - 0.6 * math.exp(-0.3 * 0)
LOG2E = math.log2(math.e)

OFF_XM, OFF_OM, OFF_ZM, OFF_QD, OFF_KD, OFF_VD, OFF_ZD, OFF_QC, OFF_ZC = (
    i * GROUP_WIDTH for i in range(9))

FRONT_ROWS = MLSTM_STEP_CHUNKS * MLSTM_CHUNK
V_TILE = 256
VT_PAD = 16
VT_ROWS = HEAD_DIM + VT_PAD
DIFF_Q_ROWS = 512
DIFF_KV_ROWS = 512
DIFF_KEY_CHUNK = 512
CONV_HALO = 8

NT_DIMS = (((1,), (1,)), ((), ()))
TN_DIMS = (((0,), (0,)), ((), ()))


def _dot(a, b):
    return jnp.dot(a, b, preferred_element_type=F32)


def _dot_nt(a, b):
    return lax.dot_general(a, b, NT_DIMS, preferred_element_type=F32)


def _silu(x):
    return x * jax.nn.sigmoid(x)


def _rms(x, g):
    return x * lax.rsqrt(jnp.mean(x * x, axis=-1, keepdims=True) + NORM_EPS) * g


def _head(h):
    return slice(h * HEAD_DIM, (h + 1) * HEAD_DIM)


def _fold_rows(x, op):
    r, n = x.shape
    return op(x.reshape(r // SUBLANES, SUBLANES, n), axis=0)


def _log_sigmoid(x):
    return jnp.minimum(x, 0.0) - jnp.log1p(jnp.exp(-jnp.abs(x)))


def _cumsum_rows(f, tril):
    hi = f.astype(BF16)
    r1 = f - hi.astype(F32)
    mid = r1.astype(BF16)
    lo = (r1 - mid.astype(F32)).astype(BF16)
    return _dot(tril, hi) + _dot(tril, mid) + _dot(tril, lo)


def _memkv_kernel(mem_ref, g_ref, w_ref, kv_ref):
    y = _rms(mem_ref[0], g_ref[...])
    kv_ref[0] = _dot(y.astype(BF16), w_ref[...]).astype(BF16)


def _memkv_call(mem, g, w):
    b = mem.shape[0]
    return pl.pallas_call(
        _memkv_kernel,
        grid=(b,),
        in_specs=[
            pl.BlockSpec((1, MEM_LEN, D_MODEL), lambda i: (i, 0, 0)),
            pl.BlockSpec((1, D_MODEL), lambda i: (0, 0)),
            pl.BlockSpec((D_MODEL, 2 * GROUP_WIDTH), lambda i: (0, 0)),
        ],
        out_specs=pl.BlockSpec((1, MEM_LEN, 2 * GROUP_WIDTH), lambda i: (i, 0, 0)),
        out_shape=jax.ShapeDtypeStruct((b, MEM_LEN, 2 * GROUP_WIDTH), BF16),
        compiler_params=pltpu.CompilerParams(dimension_semantics=("arbitrary",)),
        name="memkv",
    )(mem, g, w)


def _front_kernel(x_ref, kv_ref, ng_ref, win_ref, cw_ref, cb_ref, wqk_ref, wv_ref, wif_ref, bif_ref,
                  mg_ref, sk_ref,
                  ym_ref, qd_ref, kd_ref, vdt_ref, zd_ref, yc_ref,
                  conv_ref, c_ref, n_ref, m_ref, q_s, k_s, v_s, g_s, om_s, zm_s, xcv_s):
    tm = FRONT_ROWS
    L = MLSTM_CHUNK
    heads = range(N_HEADS)

    @pl.when(pl.program_id(1) == 0)
    def _():
        conv_ref[tm:tm + CONV_HALO, :] = jnp.zeros((CONV_HALO, GROUP_WIDTH), F32)
        c_ref[...] = jnp.zeros_like(c_ref)
        n_ref[...] = jnp.zeros_like(n_ref)
        m_ref[...] = jnp.zeros_like(m_ref)

    h = _rms(x_ref[0], ng_ref[...]).astype(BF16)

    def proj(off):
        return _dot(h, win_ref[:, off:off + GROUP_WIDTH])

    def put_vdt(val):
        v_t = val.T.astype(BF16)
        pad_row = lax.broadcasted_iota(jnp.int32, (VT_PAD, tm), 0)
        pad = jnp.where(pad_row == 0, 1.0, 0.0).astype(BF16)
        v_aug = jnp.concatenate([blk for hh in heads for blk in (v_t[_head(hh), :], pad)], axis=0)
        for j in range(tm // V_TILE):
            vdt_ref[0, j] = v_aug[:, j * V_TILE:(j + 1) * V_TILE]

    def put(ref, scale=None):
        def sink(val):
            ref[...] = (val if scale is None else val * scale).astype(BF16).reshape(ref.shape)
        return sink

    side_pieces = [(OFF_OM, put(om_s)), (OFF_ZM, put(zm_s)),
                   (OFF_QD, put(qd_ref, DIFF_QK_DIM ** -0.5 * LOG2E)), (OFF_KD, put(kd_ref)),
                   (OFF_VD, put_vdt), (OFF_ZD, put(zd_ref))]
    pending = []

    def side_start():
        if side_pieces:
            off, sink = side_pieces.pop(0)
            pending.append((proj(off), sink))

    def side_finish():
        if pending:
            val, sink = pending.pop(0)
            sink(val)

    x_m = proj(OFF_XM)
    q_cb = proj(OFF_QC).astype(BF16)
    z_c = proj(OFF_ZC)

    conv_ref[0:CONV_HALO, :] = conv_ref[tm:tm + CONV_HALO, :]
    conv_ref[CONV_HALO:CONV_HALO + tm, :] = x_m
    acc = jnp.broadcast_to(cb_ref[...], (tm, GROUP_WIDTH))
    for j in range(CONV_WIDTH):
        start = CONV_HALO - (CONV_WIDTH - 1) + j
        acc = acc + cw_ref[j:j + 1, :] * conv_ref[start:start + tm, :]
    sc = [_dot_nt(q_cb[:, _head(hh)], kv_ref[0, :, _head(hh)]) * HEAD_DIM ** -0.5 for hh in heads]
    x_cv = _silu(acc)
    xcv_b = x_cv.astype(BF16)
    xcv_s[...] = xcv_b
    xm_b = x_m.astype(BF16)

    qk_p = [_dot(xcv_b[:, _head(hh)], wqk_ref[hh]) for hh in heads]
    vv_p = [_dot(xm_b[:, _head(hh)], wv_ref[hh]) for hh in heads]

    q_all = jnp.concatenate([qk_p[hh][:, :HEAD_DIM] for hh in heads], axis=1).astype(BF16)
    k_all = jnp.concatenate([qk_p[hh][:, HEAD_DIM:] for hh in heads], axis=1)
    v_all = jnp.concatenate(vv_p, axis=1).astype(BF16)
    q_s[...] = q_all
    k_s[...] = (k_all * HEAD_DIM ** -0.5).astype(BF16)
    v_s[...] = v_all
    g_s[...] = (jnp.broadcast_to(bif_ref[...], (tm, HEAD_DIM))
                + _dot(q_all, wif_ref[0:GROUP_WIDTH, :])
                + _dot(k_all.astype(BF16), wif_ref[GROUP_WIDTH:2 * GROUP_WIDTH, :])
                + _dot(v_all, wif_ref[2 * GROUP_WIDTH:3 * GROUP_WIDTH, :]))

    mx = [jnp.max(sc[hh], axis=-1, keepdims=True) for hh in heads]
    pm = [jnp.exp(sc[hh] - mx[hh]) for hh in heads]
    inv = [1.0 / jnp.sum(pm[hh], axis=-1, keepdims=True) for hh in heads]
    pc = [(pm[hh] * inv[hh]).astype(BF16) for hh in heads]
    oc = [_dot(pc[hh], kv_ref[0, :, GROUP_WIDTH + hh * HEAD_DIM:GROUP_WIDTH + (hh + 1) * HEAD_DIM]) for hh in heads]
    for hh in heads:
        yc_ref[0, :, _head(hh)] = (oc[hh] * _silu(z_c[:, _head(hh)])).astype(BF16)

    row = lax.broadcasted_iota(jnp.int32, (L, L), 0)
    col = lax.broadcasted_iota(jnp.int32, (L, L), 1)
    causal = col <= row
    tril = jnp.where(causal, 1.0, 0.0).astype(BF16)
    chunks = range(MLSTM_STEP_CHUNKS)
    probs = [(c, hh) for c in chunks for hh in heads]
    rows = lambda c: slice(c * L, (c + 1) * L)
    q_of = lambda p: q_s[rows(p[0]), _head(p[1])]
    k_of = lambda p: k_s[rows(p[0]), _head(p[1])]
    v_of = lambda p: v_s[rows(p[0]), _head(p[1])]

    def each(f, side=False):
        if side:
            side_start()
        out = {p: f(p) for p in probs}
        if side:
            side_finish()
        return out

    a_cols, a_rows = [], []
    for c in chunks:
        g = g_s[rows(c), :]
        bb_all = _cumsum_rows(_log_sigmoid(g), tril)
        ac = jnp.where(col < N_HEADS, g, bb_all)
        a_cols.append(ac)
        a_rows.append(ac.T)
    ib_col = {(c, hh): a_cols[c][:, hh:hh + 1] for c, hh in probs}
    bb_col = {(c, hh): a_cols[c][:, N_HEADS + hh:N_HEADS + hh + 1] for c, hh in probs}
    ib_row = {(c, hh): a_rows[c][hh:hh + 1, :] for c, hh in probs}
    bb_row = {(c, hh): a_rows[c][N_HEADS + hh:N_HEADS + hh + 1, :] for c, hh in probs}
    b_end = {p: bb_col[p][L - 1:L, :] for p in probs}

    qk = {p: _dot_nt(q_of(p), k_of(p)) for p in probs}

    u_row = {p: ib_row[p] - bb_row[p] for p in probs}
    d = each(lambda p: jnp.where(causal, jnp.broadcast_to(u_row[p], (L, L)), -jnp.inf), side=True)
    u_max = each(lambda p: jnp.max(d[p], axis=-1, keepdims=True))
    r = each(lambda p: bb_col[p] + u_max[p])
    s_loc = each(lambda p: qk[p] * jnp.exp(d[p] - u_max[p]), side=True)
    rs = each(lambda p: jnp.sum(s_loc[p], axis=-1, keepdims=True))
    s_bf = each(lambda p: s_loc[p].astype(BF16))
    a = each(lambda p: (b_end[p] - bb_col[p]) + ib_col[p], side=True)
    a_max = each(lambda p: jnp.max(a[p], axis=0, keepdims=True))
    w_loc = each(lambda p: jnp.broadcast_to(jnp.exp(a[p] - a_max[p]), (L, HEAD_DIM)))
    vw = each(lambda p: (v_of(p).astype(F32) * w_loc[p]).astype(BF16), side=True)
    nu = each(lambda p: jnp.sum(k_of(p).astype(F32) * w_loc[p], axis=0, keepdims=True))

    sv = {p: _dot(s_bf[p], v_of(p)) for p in probs}
    u = {p: lax.dot_general(vw[p], k_of(p), TN_DIMS, preferred_element_type=F32) for p in probs}

    m_in, c_in, n_in = {}, {}, {}
    for hh in heads:
        m_cur = m_ref[hh][:, 0:1]
        c_cur = c_ref[hh]
        n_cur = n_ref[hh]
        for c in chunks:
            p = (c, hh)
            m_in[p], c_in[p], n_in[p] = m_cur, c_cur.astype(BF16), n_cur
            m_new = jnp.maximum(b_end[p] + m_cur, a_max[p])
            decay = jnp.exp(b_end[p] + m_cur - m_new)
            e_upd = jnp.exp(a_max[p] - m_new)
            c_cur = decay * c_cur + e_upd * u[p]
            n_cur = decay * n_cur + e_upd * nu[p]
            m_cur = m_new
        c_ref[hh] = c_cur
        n_ref[hh] = n_cur
        m_ref[hh] = jnp.broadcast_to(m_cur, (1, HEAD_DIM))

    q_c = {p: _dot_nt(q_of(p), c_in[p]) for p in probs}

    tile_of = lambda ref, p: ref[rows(p[0]), _head(p[1])].astype(F32)
    g_col = each(lambda p: bb_col[p] + m_in[p])
    m_t = each(lambda p: jnp.maximum(g_col[p], r[p]))
    e_loc = each(lambda p: jnp.exp(r[p] - m_t[p]))
    inter = each(lambda p: jnp.exp(g_col[p] - m_t[p]))
    q_n = each(lambda p: jnp.sum(q_of(p).astype(F32) * n_in[p], axis=-1, keepdims=True))
    den = each(lambda p: e_loc[p] * rs[p] + inter[p] * q_n[p])
    scale = each(lambda p: 1.0 / jnp.maximum(jnp.abs(den[p]), jnp.exp(-m_t[p])))
    w_intra = each(lambda p: e_loc[p] * scale[p])
    w_inter = each(lambda p: inter[p] * scale[p])
    hg = each(lambda p: jax.nn.sigmoid(tile_of(om_s, p))
              * (w_intra[p] * sv[p] + w_inter[p] * q_c[p]), side=True)
    mu = each(lambda p: jnp.mean(hg[p], axis=-1, keepdims=True))
    cen = each(lambda p: hg[p] - mu[p], side=True)
    var = each(lambda p: jnp.mean(jnp.square(cen[p]), axis=-1, keepdims=True))
    assert not side_pieces and not pending
    for p in probs:
        sl = _head(p[1])
        y = cen[p] * lax.rsqrt(var[p] + NORM_EPS) * mg_ref[:, sl]
        y = (y + sk_ref[:, sl] * tile_of(xcv_s, p)) * _silu(tile_of(zm_s, p))
        ym_ref[0, rows(p[0]), sl] = y.astype(BF16)


def _front_call(x, kv, norm_g, w_in, conv_w, conv_b, wqk, wv, w_if, b_if, mnorm_g, skip_m):
    b, s, _ = x.shape
    tm = FRONT_ROWS
    ns = s // tm
    tok = lambda w: jax.ShapeDtypeStruct((b, s, w), BF16)
    tok_spec = lambda w: pl.BlockSpec((1, tm, w), lambda i, j: (i, j, 0))
    const = lambda shape: pl.BlockSpec(shape, lambda i, j: (0,) * len(shape))
    vdt_tiles = tm // V_TILE
    out_shape = (
        tok(GROUP_WIDTH),
        tok(GROUP_WIDTH), tok(GROUP_WIDTH),
        jax.ShapeDtypeStruct((b, ns * vdt_tiles, N_HEADS * VT_ROWS, V_TILE), BF16),
        tok(GROUP_WIDTH),
        tok(GROUP_WIDTH),
    )
    out_specs = (
        tok_spec(GROUP_WIDTH), tok_spec(GROUP_WIDTH), tok_spec(GROUP_WIDTH),
        pl.BlockSpec((1, vdt_tiles, N_HEADS * VT_ROWS, V_TILE), lambda i, j: (i, j, 0, 0)),
        tok_spec(GROUP_WIDTH), tok_spec(GROUP_WIDTH),
    )
    in_specs = [
        pl.BlockSpec((1, tm, D_MODEL), lambda i, j: (i, j, 0)),
        pl.BlockSpec((1, MEM_LEN, 2 * GROUP_WIDTH), lambda i, j: (i, 0, 0)),
        const((1, D_MODEL)),
        const((D_MODEL, IN_WIDTH)),
        const((CONV_WIDTH, GROUP_WIDTH)),
        const((1, GROUP_WIDTH)),
        const((N_HEADS, HEAD_DIM, 2 * HEAD_DIM)),
        const((N_HEADS, HEAD_DIM, HEAD_DIM)),
        const((3 * GROUP_WIDTH, HEAD_DIM)),
        const((1, HEAD_DIM)),
        const((1, GROUP_WIDTH)),
        const((1, GROUP_WIDTH)),
    ]
    tile_bf16 = pltpu.VMEM((tm, GROUP_WIDTH), BF16)
    return pl.pallas_call(
        _front_kernel,
        grid=(b, ns),
        in_specs=in_specs,
        out_specs=out_specs,
        out_shape=out_shape,
        scratch_shapes=[pltpu.VMEM((tm + CONV_HALO, GROUP_WIDTH), F32),
                        pltpu.VMEM((N_HEADS, HEAD_DIM, HEAD_DIM), F32),
                        pltpu.VMEM((N_HEADS, 1, HEAD_DIM), F32),
                        pltpu.VMEM((N_HEADS, 1, HEAD_DIM), F32),
                        tile_bf16, tile_bf16, tile_bf16,
                        pltpu.VMEM((tm, HEAD_DIM), F32),
                        tile_bf16, tile_bf16, tile_bf16],
        compiler_params=pltpu.CompilerParams(
            dimension_semantics=("arbitrary", "arbitrary"),
            vmem_limit_bytes=VMEM_LIMIT_BYTES),
        name="front",
    )(x, kv, norm_g, w_in, conv_w, conv_b, wqk, wv, w_if, b_if, mnorm_g, skip_m)


def _diff_kernel(q_ref, qn_ref, k_ref, vt_ref, zd_ref, g_ref, lq1_ref, lk1_ref, lq2_ref, lk2_ref,
                 x_ref, ym_ref, yc_ref, wo_ref, fg_ref, o_ref,
                 qs_ref, m_ref, l_ref, acc_ref, s0_ref, s1_ref, mx0_ref, mx1_ref, yd_ref, part_ref, qsn_ref):
    tq, tk, tv = DIFF_Q_ROWS, DIFF_KV_ROWS, V_TILE
    i = pl.program_id(1)
    feat = lax.broadcasted_iota(jnp.int32, (HEAD_DIM, tq), 0)
    for hh in range(N_HEADS):
        q_t = q_ref[0, :, _head(hh)].astype(F32).T
        qs_ref[hh, :, 0:tq] = jnp.where(feat < DIFF_QK_DIM, q_t, 0.0).astype(BF16)
        qs_ref[hh, :, tq:2 * tq] = jnp.where(feat >= DIFF_QK_DIM, q_t, 0.0).astype(BF16)
    m_ref[...] = jnp.full_like(m_ref, -jnp.inf)
    l_ref[...] = jnp.zeros_like(l_ref)
    acc_ref[...] = jnp.zeros_like(acc_ref)

    kc = DIFF_KEY_CHUNK
    sub = HEAD_DIM
    tri = (lax.broadcasted_iota(jnp.int32, (sub, sub), 0) <= lax.broadcasted_iota(jnp.int32, (sub, sub), 1))

    def mask_diagonal(s_c, c):
        hidden = jnp.full((sub, sub), -jnp.inf, F32)
        hidden_max = jnp.full((SUBLANES, sub), -jnp.inf, F32)
        row_blocks, maxes = [], None
        for r in range(kc // sub):
            kb = c * (kc // sub) + r
            blocks, row_max = [], []
            for b8 in range(2 * tq // sub):
                b = b8 % (tq // sub)
                if kb > b:
                    blocks.append(hidden)
                    row_max.append(hidden_max)
                    continue
                blk = s_c[r * sub:(r + 1) * sub, b8 * sub:(b8 + 1) * sub]
                if kb == b:
                    blk = jnp.where(tri, blk, -jnp.inf)
                blocks.append(blk)
                row_max.append(_fold_rows(blk, jnp.max))
            row_blocks.append(jnp.concatenate(blocks, axis=1))
            row_max = jnp.concatenate(row_max, axis=1)
            maxes = row_max if maxes is None else jnp.maximum(maxes, row_max)
        return jnp.concatenate(row_blocks, axis=0), maxes

    group_a = (tuple(range(0, N_HEADS // 2)), s0_ref, mx0_ref)
    group_b = (tuple(range(N_HEADS // 2, N_HEADS)), s1_ref, mx1_ref)

    def step(consume=None, produce=None):
        n_slots = N_HEADS // 2
        for slot in range(n_slots):
            if consume is not None:
                t_c, (heads_c, sc_ref, mxc_ref) = consume
                h_c = heads_c[slot]
                m_old = m_ref[h_c]
                m_new = jnp.maximum(m_old, mxc_ref[slot])
                alpha = jnp.exp2(m_old - m_new)
                pv, pb_prev = None, None
            if produce is not None:
                t_p, (heads_p, sp_ref, mxp_ref), diagonal = produce[:3]
                q_src = produce[3] if len(produce) > 3 else None
                h_p = heads_p[slot]
                mx = None
            for c in range(tk // kc):
                chunk = slice(c * kc, (c + 1) * kc)
                if produce is not None:
                    rows = pl.ds(pl.multiple_of(t_p * tk + c * kc, kc), kc)
                    q_mat = qs_ref[h_p] if q_src is None else q_src[slot]
                    s_c = _dot(k_ref[0, rows, _head(h_p)], q_mat)
                if consume is not None:
                    pb_c = jnp.exp2(sc_ref[slot, chunk, :] - m_new).astype(BF16)
                    pb_prev = pb_c if pb_prev is None else jnp.concatenate([pb_prev, pb_c], axis=0)
                    if c == tk // kc - 1:
                        v_rows = slice(h_c * VT_ROWS, (h_c + 1) * VT_ROWS)
                        vt = jnp.concatenate([vt_ref[0, t_c * (tk // tv) + j, v_rows, :]
                                              for j in range(tk // tv)], axis=1)
                        pv = _dot(vt, pb_prev)
                if produce is not None:
                    if diagonal:
                        s_c, c_max = mask_diagonal(s_c, c)
                    else:
                        c_max = _fold_rows(s_c, jnp.max)
                    sp_ref[slot, chunk, :] = s_c
                    mx = c_max if mx is None else jnp.maximum(mx, c_max)
            if consume is not None:
                l_ref[h_c] = alpha * l_ref[h_c] + pv[HEAD_DIM:HEAD_DIM + 1, :]
                acc_ref[h_c] = alpha * acc_ref[h_c] + pv[:HEAD_DIM, :]
                m_ref[h_c] = m_new
            if produce is not None:
                mxp_ref[slot] = jnp.max(mx, axis=0, keepdims=True)

    def tile(t, next_diagonal):
        step(consume=(t, group_a), produce=(t, group_b, False))
        step(consume=(t, group_b), produce=(t + 1, group_a, next_diagonal))

    def body(t, carry):
        tile(t, False)
        return carry

    @pl.when(i == 0)
    def _():
        step(produce=(0, group_a, True))

    @pl.when(i > 0)
    def _():
        lax.fori_loop(0, i - 1, body, 0)
        tile(i - 1, True)

    lam = (jnp.exp(jnp.sum(lq1_ref[...] * lk1_ref[...], axis=-1, keepdims=True))
           - jnp.exp(jnp.sum(lq2_ref[...] * lk2_ref[...], axis=-1, keepdims=True)) + LAM_INIT)

    def finish(heads):
        for hh in heads:
            sl = _head(hh)
            acc = acc_ref[hh]
            inv = 1.0 / l_ref[hh]
            o_t = acc[:, :tq] * inv[:, :tq] - lam * (acc[:, tq:] * inv[:, tq:])
            o = _rms(o_t.T, g_ref[:, sl]) * (1.0 - LAM_INIT)
            yd_ref[:, sl] = (o * _silu(zd_ref[0, :, sl].astype(F32))).astype(BF16)

    def project_rows(c, src_ref, w_rows, first):
        r_c = slice(c * HEAD_DIM, (c + 1) * HEAD_DIM)
        y = _dot(src_ref[0, r_c, :], wo_ref[w_rows, :])
        part_ref[r_c, :] = (x_ref[0, r_c, :] + y) if first else (part_ref[r_c, :] + y)

    def project_heads(heads):
        cols = slice(heads[0] * HEAD_DIM, (heads[-1] + 1) * HEAD_DIM)
        w_rows = slice(GROUP_WIDTH + cols.start, GROUP_WIDTH + cols.stop)
        return _dot(yd_ref[:, cols], wo_ref[w_rows, :])

    step(consume=(i, group_a), produce=(i, group_b, True))
    finish(group_a[0])

    last = pl.num_programs(1) - 1

    @pl.when(i < last)
    def _():
        for slot, hh in enumerate(group_a[0]):
            q_t = qn_ref[0, :, _head(hh)].astype(F32).T
            qsn_ref[slot, :, 0:tq] = jnp.where(feat < DIFF_QK_DIM, q_t, 0.0).astype(BF16)
            qsn_ref[slot, :, tq:2 * tq] = jnp.where(feat >= DIFF_QK_DIM, q_t, 0.0).astype(BF16)
        step(consume=(i, group_b), produce=(0, group_a, False, qsn_ref))

    @pl.when(i == last)
    def _():
        step(consume=(i, group_b))

    for c in range(tq // HEAD_DIM):
        project_rows(c, ym_ref, slice(0, GROUP_WIDTH), True)
    for c in range(tq // HEAD_DIM):
        project_rows(c, yc_ref, slice(2 * GROUP_WIDTH, 3 * GROUP_WIDTH), False)
    y_a = project_heads(group_a[0])
    finish(group_b[0])
    o_ref[0] = _rms(part_ref[...] + y_a + project_heads(group_b[0]), fg_ref[...])


def _diff_out_call(qd, kd, vdt, zd, dnorm_g, lq1, lk1, lq2, lk2, x, ym, yc, w_out, final_g):
    n_q = qd.shape[1] // DIFF_Q_ROWS
    b, s, _ = qd.shape
    tq = DIFF_Q_ROWS
    nkv = vdt.shape[1]
    const = lambda shape: pl.BlockSpec(shape, lambda i, j: (0,) * len(shape))
    tok_spec = lambda w: pl.BlockSpec((1, tq, w), lambda i, j: (i, j, 0))
    return pl.pallas_call(
        _diff_kernel,
        grid=(b, s // tq),
        in_specs=[
            tok_spec(GROUP_WIDTH),
            pl.BlockSpec((1, tq, GROUP_WIDTH), lambda i, j: (i, jnp.minimum(j + 1, n_q - 1), 0)),
            pl.BlockSpec((1, s, GROUP_WIDTH), lambda i, j: (i, 0, 0)),
            pl.BlockSpec((1, nkv, N_HEADS * VT_ROWS, V_TILE), lambda i, j: (i, 0, 0, 0)),
            tok_spec(GROUP_WIDTH),
            const((1, GROUP_WIDTH)),
            const((1, DIFF_QK_DIM)), const((1, DIFF_QK_DIM)), const((1, DIFF_QK_DIM)), const((1, DIFF_QK_DIM)),
            tok_spec(D_MODEL), tok_spec(GROUP_WIDTH), tok_spec(GROUP_WIDTH),
            const((3 * GROUP_WIDTH, D_MODEL)), const((1, D_MODEL)),
        ],
        out_specs=tok_spec(D_MODEL),
        out_shape=jax.ShapeDtypeStruct((b, s, D_MODEL), F32),
        scratch_shapes=[pltpu.VMEM((N_HEADS, HEAD_DIM, 2 * tq), BF16),
                        pltpu.VMEM((N_HEADS, 1, 2 * tq), F32),
                        pltpu.VMEM((N_HEADS, 1, 2 * tq), F32),
                        pltpu.VMEM((N_HEADS, HEAD_DIM, 2 * tq), F32),
                        pltpu.VMEM((N_HEADS // 2, DIFF_KV_ROWS, 2 * tq), F32),
                        pltpu.VMEM((N_HEADS // 2, DIFF_KV_ROWS, 2 * tq), F32),
                        pltpu.VMEM((N_HEADS // 2, 1, 2 * tq), F32),
                        pltpu.VMEM((N_HEADS // 2, 1, 2 * tq), F32),
                        pltpu.VMEM((tq, GROUP_WIDTH), BF16),
                        pltpu.VMEM((tq, D_MODEL), F32),
                        pltpu.VMEM((N_HEADS // 2, HEAD_DIM, 2 * tq), BF16)],
        compiler_params=pltpu.CompilerParams(
            dimension_semantics=("arbitrary", "arbitrary"),
            vmem_limit_bytes=VMEM_LIMIT_BYTES),
        name="diffattn_out",
    )(qd, qd, kd, vdt, zd, dnorm_g, lq1, lk1, lq2, lk2, x, ym, yc, w_out, final_g)


def kernel(x, mem, norm_g, w_in, conv_w, conv_b, wq_m, wk_m, wv_m, w_if, b_if, mnorm_g, skip_m,
           lam_q1, lam_k1, lam_q2, lam_k2, dnorm_g, mem_norm_g, w_mem_kv, w_out, final_g):
    b, s, d = x.shape
    assert (d, s % DIFF_KV_ROWS, DIFF_KV_ROWS % V_TILE, FRONT_ROWS % V_TILE) == (D_MODEL, 0, 0, 0)
    assert (s % FRONT_ROWS, DIFF_KV_ROWS % DIFF_KEY_CHUNK) == (0, 0)
    assert DIFF_Q_ROWS == DIFF_KV_ROWS and DIFF_KEY_CHUNK % HEAD_DIM == 0
    assert norm_g.shape[0] == 1, "single-layer kernel"
    l = 0
    w_if_pad = jnp.pad(w_if[l], ((0, 0), (0, HEAD_DIM - w_if.shape[-1]))).astype(BF16)
    b_if_pad = jnp.pad(b_if[l], (0, HEAD_DIM - b_if.shape[-1]))[None, :]

    kv = _memkv_call(mem, mem_norm_g[l][None, :], w_mem_kv[l].astype(BF16))
    ym, qd, kd, vdt, zd, yc = _front_call(
        x, kv, norm_g[l][None, :], w_in[l].astype(BF16), conv_w[l], conv_b[l][None, :],
        jnp.concatenate([wq_m[l], wk_m[l]], axis=-1).astype(BF16), wv_m[l].astype(BF16), w_if_pad, b_if_pad,
        mnorm_g[l][None, :], skip_m[l][None, :])
    return _diff_out_call(qd, kd, vdt, zd, dnorm_g[l][None, :],
                          lam_q1[l][None, :], lam_k1[l][None, :], lam_q2[l][None, :], lam_k2[l][None, :],
                          x, ym, yc, w_out[l].astype(BF16), final_g[None, :])
```

```python
import math

import jax
import jax.numpy as jnp
from jax import lax
from jax.experimental import pallas as pl
from jax.experimental.pallas import tpu as pltpu

F32 = jnp.float32
BF16 = jnp.bfloat16

D_MODEL = 1024
N_HEADS = 4
HEAD_DIM = 128
SUBLANES = 8
V7X_VMEM_BYTES = 64 * 1024 * 1024
VMEM_LIMIT_BYTES = V7X_VMEM_BYTES - 6 * 1024 * 1024
GROUP_WIDTH = N_HEADS * HEAD_DIM
DIFF_QK_DIM = 64
CONV_WIDTH = 4
MLSTM_CHUNK = 128
MLSTM_STEP_CHUNKS = 4
MEM_LEN = 256
IN_WIDTH = 9 * GROUP_WIDTH
NORM_EPS = 1e-6
LAM_INIT = 0.8 - 0.6 * math.exp(-0.3 * 0)
LOG2E = math.log2(math.e)

OFF_XM, OFF_OM, OFF_ZM, OFF_QD, OFF_KD, OFF_VD, OFF_ZD, OFF_QC, OFF_ZC = (
    i * GROUP_WIDTH for i in range(9))

FRONT_ROWS = MLSTM_STEP_CHUNKS * MLSTM_CHUNK
V_TILE = 256
VT_PAD = 16
VT_ROWS = HEAD_DIM + VT_PAD
DIFF_Q_ROWS = 512
DIFF_KV_ROWS = 512
DIFF_KEY_CHUNK = 512
CONV_HALO = 8

NT_DIMS = (((1,), (1,)), ((), ()))
TN_DIMS = (((0,), (0,)), ((), ()))


def _dot(a, b):
    return jnp.dot(a, b, preferred_element_type=F32)


def _dot_nt(a, b):
    return lax.dot_general(a, b, NT_DIMS, preferred_element_type=F32)


def _silu(x):
    return x * jax.nn.sigmoid(x)


def _rms(x, g):
    return x * lax.rsqrt(jnp.mean(x * x, axis=-1, keepdims=True) + NORM_EPS) * g


def _head(h):
    return slice(h * HEAD_DIM, (h + 1) * HEAD_DIM)


def _fold_rows(x, op):
    r, n = x.shape
    return op(x.reshape(r // SUBLANES, SUBLANES, n), axis=0)


def _log_sigmoid(x):
    return jnp.minimum(x, 0.0) - jnp.log1p(jnp.exp(-jnp.abs(x)))


def _cumsum_rows(f, tril):
    hi = f.astype(BF16)
    r1 = f - hi.astype(F32)
    mid = r1.astype(BF16)
    lo = (r1 - mid.astype(F32)).astype(BF16)
    return _dot(tril, hi) + _dot(tril, mid) + _dot(tril, lo)


def _memkv_kernel(mem_ref, g_ref, w_ref, kv_ref):
    y = _rms(mem_ref[0], g_ref[...])
    kv_ref[0] = _dot(y.astype(BF16), w_ref[...]).astype(BF16)


def _memkv_call(mem, g, w):
    b = mem.shape[0]
    return pl.pallas_call(
        _memkv_kernel,
        grid=(b,),
        in_specs=[
            pl.BlockSpec((1, MEM_LEN, D_MODEL), lambda i: (i, 0, 0)),
            pl.BlockSpec((1, D_MODEL), lambda i: (0, 0)),
            pl.BlockSpec((D_MODEL, 2 * GROUP_WIDTH), lambda i: (0, 0)),
        ],
        out_specs=pl.BlockSpec((1, MEM_LEN, 2 * GROUP_WIDTH), lambda i: (i, 0, 0)),
        out_shape=jax.ShapeDtypeStruct((b, MEM_LEN, 2 * GROUP_WIDTH), BF16),
        compiler_params=pltpu.CompilerParams(dimension_semantics=("arbitrary",)),
        name="memkv",
    )(mem, g, w)


def _front_kernel(x_ref, kv_ref, ng_ref, win_ref, cw_ref, cb_ref, wqk_ref, wv_ref, wif_ref, bif_ref,
                  mg_ref, sk_ref,
                  ym_ref, qd_ref, kd_ref, vdt_ref, zd_ref, yc_ref,
                  conv_ref, c_ref, n_ref, m_ref, q_s, k_s, v_s, g_s, om_s, zm_s, xcv_s):
    tm = FRONT_ROWS
    L = MLSTM_CHUNK
    heads = range(N_HEADS)

    @pl.when(pl.program_id(1) == 0)
    def _():
        conv_ref[tm:tm + CONV_HALO, :] = jnp.zeros((CONV_HALO, GROUP_WIDTH), F32)
        c_ref[...] = jnp.zeros_like(c_ref)
        n_ref[...] = jnp.zeros_like(n_ref)
        m_ref[...] = jnp.zeros_like(m_ref)

    h = _rms(x_ref[0], ng_ref[...]).astype(BF16)

    def proj(off):
        return _dot(h, win_ref[:, off:off + GROUP_WIDTH])

    def put_vdt(val):
        v_t = val.T.astype(BF16)
        pad_row = lax.broadcasted_iota(jnp.int32, (VT_PAD, tm), 0)
        pad = jnp.where(pad_row == 0, 1.0, 0.0).astype(BF16)
        v_aug = jnp.concatenate([blk for hh in heads for blk in (v_t[_head(hh), :], pad)], axis=0)
        for j in range(tm // V_TILE):
            vdt_ref[0, j] = v_aug[:, j * V_TILE:(j + 1) * V_TILE]

    def put(ref, scale=None):
        def sink(val):
            ref[...] = (val if scale is None else val * scale).astype(BF16).reshape(ref.shape)
        return sink

    side_pieces = [(OFF_OM, put(om_s)), (OFF_ZM, put(zm_s)),
                   (OFF_QD, put(qd_ref, DIFF_QK_DIM ** -0.5 * LOG2E)), (OFF_KD, put(kd_ref)),
                   (OFF_VD, put_vdt), (OFF_ZD, put(zd_ref))]
    pending = []

    def side_start():
        if side_pieces:
            off, sink = side_pieces.pop(0)
            pending.append((proj(off), sink))

    def side_finish():
        if pending:
            val, sink = pending.pop(0)
            sink(val)

    x_m = proj(OFF_XM)
    q_cb = proj(OFF_QC).astype(BF16)
    z_c = proj(OFF_ZC)

    conv_ref[0:CONV_HALO, :] = conv_ref[tm:tm + CONV_HALO, :]
    conv_ref[CONV_HALO:CONV_HALO + tm, :] = x_m
    acc = jnp.broadcast_to(cb_ref[...], (tm, GROUP_WIDTH))
    for j in range(CONV_WIDTH):
        start = CONV_HALO - (CONV_WIDTH - 1) + j
        acc = acc + cw_ref[j:j + 1, :] * conv_ref[start:start + tm, :]
    sc = [_dot_nt(q_cb[:, _head(hh)], kv_ref[0, :, _head(hh)]) * (HEAD_DIM ** -0.5 * LOG2E)
          for hh in heads]
    x_cv = _silu(acc)
    xcv_b = x_cv.astype(BF16)
    xcv_s[...] = xcv_b
    xm_b = x_m.astype(BF16)

    qk_p = [_dot(xcv_b[:, _head(hh)], wqk_ref[hh]) for hh in heads]
    vv_p = [_dot(xm_b[:, _head(hh)], wv_ref[hh]) for hh in heads]

    q_all = jnp.concatenate([qk_p[hh][:, :HEAD_DIM] for hh in heads], axis=1).astype(BF16)
    k_all = jnp.concatenate([qk_p[hh][:, HEAD_DIM:] for hh in heads], axis=1)
    v_all = jnp.concatenate(vv_p, axis=1).astype(BF16)
    q_s[...] = q_all
    k_s[...] = (k_all * HEAD_DIM ** -0.5).astype(BF16)
    v_s[...] = v_all
    g_s[...] = (jnp.broadcast_to(bif_ref[...], (tm, HEAD_DIM))
                + _dot(q_all, wif_ref[0:GROUP_WIDTH, :])
                + _dot(k_all.astype(BF16), wif_ref[GROUP_WIDTH:2 * GROUP_WIDTH, :])
                + _dot(v_all, wif_ref[2 * GROUP_WIDTH:3 * GROUP_WIDTH, :]))

    mx = [jnp.max(sc[hh], axis=-1, keepdims=True) for hh in heads]
    pm = [jnp.exp2(sc[hh] - mx[hh]) for hh in heads]
    inv = [1.0 / jnp.sum(pm[hh], axis=-1, keepdims=True) for hh in heads]
    pc = [(pm[hh] * inv[hh]).astype(BF16) for hh in heads]
    oc = [_dot(pc[hh], kv_ref[0, :, GROUP_WIDTH + hh * HEAD_DIM:GROUP_WIDTH + (hh + 1) * HEAD_DIM]) for hh in heads]
    for hh in heads:
        yc_ref[0, :, _head(hh)] = (oc[hh] * _silu(z_c[:, _head(hh)])).astype(BF16)

    row = lax.broadcasted_iota(jnp.int32, (L, L), 0)
    col = lax.broadcasted_iota(jnp.int32, (L, L), 1)
    causal = col <= row
    tril = jnp.where(causal, 1.0, 0.0).astype(BF16)
    chunks = range(MLSTM_STEP_CHUNKS)
    probs = [(c, hh) for c in chunks for hh in heads]
    rows = lambda c: slice(c * L, (c + 1) * L)
    q_of = lambda p: q_s[rows(p[0]), _head(p[1])]
    k_of = lambda p: k_s[rows(p[0]), _head(p[1])]
    v_of = lambda p: v_s[rows(p[0]), _head(p[1])]

    def each(f, side=False):
        if side:
            side_start()
        out = {p: f(p) for p in probs}
        if side:
            side_finish()
        return out

    a_cols, a_rows = [], []
    for c in chunks:
        g = g_s[rows(c), :]
        bb_all = _cumsum_rows(_log_sigmoid(g), tril)
        ac = jnp.where(col < N_HEADS, g, bb_all) * LOG2E
        a_cols.append(ac)
        a_rows.append(ac.T)
    ib_col = {(c, hh): a_cols[c][:, hh:hh + 1] for c, hh in probs}
    bb_col = {(c, hh): a_cols[c][:, N_HEADS + hh:N_HEADS + hh + 1] for c, hh in probs}
    ib_row = {(c, hh): a_rows[c][hh:hh + 1, :] for c, hh in probs}
    bb_row = {(c, hh): a_rows[c][N_HEADS + hh:N_HEADS + hh + 1, :] for c, hh in probs}
    b_end = {p: bb_col[p][L - 1:L, :] for p in probs}

    qk = {p: _dot_nt(q_of(p), k_of(p)) for p in probs}

    d = each(lambda p: jnp.where(causal, (bb_col[p] - bb_row[p]) + ib_row[p], -jnp.inf), side=True)
    r = each(lambda p: jnp.max(d[p], axis=-1, keepdims=True))
    s_loc = each(lambda p: qk[p] * jnp.exp2(d[p] - r[p]), side=True)
    rs = each(lambda p: jnp.sum(s_loc[p], axis=-1, keepdims=True))
    s_bf = each(lambda p: s_loc[p].astype(BF16))
    a = each(lambda p: (b_end[p] - bb_col[p]) + ib_col[p], side=True)
    a_max = each(lambda p: jnp.max(a[p], axis=0, keepdims=True))
    w_loc = each(lambda p: jnp.broadcast_to(jnp.exp2(a[p] - a_max[p]), (L, HEAD_DIM)))
    vw = each(lambda p: (v_of(p).astype(F32) * w_loc[p]).astype(BF16), side=True)
    nu = each(lambda p: jnp.sum(k_of(p).astype(F32) * w_loc[p], axis=0, keepdims=True))

    sv = {p: _dot(s_bf[p], v_of(p)) for p in probs}
    u = {p: lax.dot_general(vw[p], k_of(p), TN_DIMS, preferred_element_type=F32) for p in probs}

    m_in, c_in, n_in = {}, {}, {}
    for hh in heads:
        m_cur = m_ref[hh][:, 0:1]
        c_cur = c_ref[hh]
        n_cur = n_ref[hh]
        for c in chunks:
            p = (c, hh)
            m_in[p], c_in[p], n_in[p] = m_cur, c_cur.astype(BF16), n_cur
            m_new = jnp.maximum(b_end[p] + m_cur, a_max[p])
            decay = jnp.exp2(b_end[p] + m_cur - m_new)
            e_upd = jnp.exp2(a_max[p] - m_new)
            c_cur = decay * c_cur + e_upd * u[p]
            n_cur = decay * n_cur + e_upd * nu[p]
            m_cur = m_new
        c_ref[hh] = c_cur
        n_ref[hh] = n_cur
        m_ref[hh] = jnp.broadcast_to(m_cur, (1, HEAD_DIM))

    q_c = {p: _dot_nt(q_of(p), c_in[p]) for p in probs}

    tile_of = lambda ref, p: ref[rows(p[0]), _head(p[1])].astype(F32)
    g_col = each(lambda p: bb_col[p] + m_in[p])
    m_t = each(lambda p: jnp.maximum(g_col[p], r[p]))
    e_loc = each(lambda p: jnp.exp2(r[p] - m_t[p]))
    inter = each(lambda p: jnp.exp2(g_col[p] - m_t[p]))
    q_n = each(lambda p: jnp.sum(q_of(p).astype(F32) * n_in[p], axis=-1, keepdims=True))
    den = each(lambda p: e_loc[p] * rs[p] + inter[p] * q_n[p])
    scale = each(lambda p: 1.0 / jnp.maximum(jnp.abs(den[p]), jnp.exp2(-m_t[p])))
    w_intra = each(lambda p: e_loc[p] * scale[p])
    w_inter = each(lambda p: inter[p] * scale[p])
    hg = each(lambda p: jax.nn.sigmoid(tile_of(om_s, p))
              * (w_intra[p] * sv[p] + w_inter[p] * q_c[p]), side=True)
    mu = each(lambda p: jnp.mean(hg[p], axis=-1, keepdims=True))
    cen = each(lambda p: hg[p] - mu[p], side=True)
    var = each(lambda p: jnp.mean(jnp.square(cen[p]), axis=-1, keepdims=True))
    assert not side_pieces and not pending
    for p in probs:
        sl = _head(p[1])
        y = cen[p] * lax.rsqrt(var[p] + NORM_EPS) * mg_ref[:, sl]
        y = (y + sk_ref[:, sl] * tile_of(xcv_s, p)) * _silu(tile_of(zm_s, p))
        ym_ref[0, rows(p[0]), sl] = y.astype(BF16)


def _front_call(x, kv, norm_g, w_in, conv_w, conv_b, wqk, wv, w_if, b_if, mnorm_g, skip_m):
    b, s, _ = x.shape
    tm = FRONT_ROWS
    ns = s // tm
    tok = lambda w: jax.ShapeDtypeStruct((b, s, w), BF16)
    tok_spec = lambda w: pl.BlockSpec((1, tm, w), lambda i, j: (i, j, 0))
    const = lambda shape: pl.BlockSpec(shape, lambda i, j: (0,) * len(shape))
    vdt_tiles = tm // V_TILE
    out_shape = (
        tok(GROUP_WIDTH),
        tok(GROUP_WIDTH), tok(GROUP_WIDTH),
        jax.ShapeDtypeStruct((b, ns * vdt_tiles, N_HEADS * VT_ROWS, V_TILE), BF16),
        tok(GROUP_WIDTH),
        tok(GROUP_WIDTH),
    )
    out_specs = (
        tok_spec(GROUP_WIDTH), tok_spec(GROUP_WIDTH), tok_spec(GROUP_WIDTH),
        pl.BlockSpec((1, vdt_tiles, N_HEADS * VT_ROWS, V_TILE), lambda i, j: (i, j, 0, 0)),
        tok_spec(GROUP_WIDTH), tok_spec(GROUP_WIDTH),
    )
    in_specs = [
        pl.BlockSpec((1, tm, D_MODEL), lambda i, j: (i, j, 0)),
        pl.BlockSpec((1, MEM_LEN, 2 * GROUP_WIDTH), lambda i, j: (i, 0, 0)),
        const((1, D_MODEL)),
        const((D_MODEL, IN_WIDTH)),
        const((CONV_WIDTH, GROUP_WIDTH)),
        const((1, GROUP_WIDTH)),
        const((N_HEADS, HEAD_DIM, 2 * HEAD_DIM)),
        const((N_HEADS, HEAD_DIM, HEAD_DIM)),
        const((3 * GROUP_WIDTH, HEAD_DIM)),
        const((1, HEAD_DIM)),
        const((1, GROUP_WIDTH)),
        const((1, GROUP_WIDTH)),
    ]
    tile_bf16 = pltpu.VMEM((tm, GROUP_WIDTH), BF16)
    return pl.pallas_call(
        _front_kernel,
        grid=(b, ns),
        in_specs=in_specs,
        out_specs=out_specs,
        out_shape=out_shape,
        scratch_shapes=[pltpu.VMEM((tm + CONV_HALO, GROUP_WIDTH), F32),
                        pltpu.VMEM((N_HEADS, HEAD_DIM, HEAD_DIM), F32),
                        pltpu.VMEM((N_HEADS, 1, HEAD_DIM), F32),
                        pltpu.VMEM((N_HEADS, 1, HEAD_DIM), F32),
                        tile_bf16, tile_bf16, tile_bf16,
                        pltpu.VMEM((tm, HEAD_DIM), F32),
                        tile_bf16, tile_bf16, tile_bf16],
        compiler_params=pltpu.CompilerParams(
            dimension_semantics=("arbitrary", "arbitrary"),
            vmem_limit_bytes=VMEM_LIMIT_BYTES),
        name="front",
    )(x, kv, norm_g, w_in, conv_w, conv_b, wqk, wv, w_if, b_if, mnorm_g, skip_m)


def _diff_kernel(q_ref, qn_ref, k_ref, vt_ref, zd_ref, g_ref, lq1_ref, lk1_ref, lq2_ref, lk2_ref,
                 x_ref, ym_ref, yc_ref, wo_ref, fg_ref, o_ref,
                 qs_ref, m_ref, l_ref, acc_ref, s0_ref, s1_ref, mx0_ref, mx1_ref, yd_ref, part_ref, qsn_ref):
    tq, tk, tv = DIFF_Q_ROWS, DIFF_KV_ROWS, V_TILE
    i = pl.program_id(1)
    feat = lax.broadcasted_iota(jnp.int32, (HEAD_DIM, tq), 0)
    for hh in range(N_HEADS):
        q_t = q_ref[0, :, _head(hh)].astype(F32).T
        qs_ref[hh, :, 0:tq] = jnp.where(feat < DIFF_QK_DIM, q_t, 0.0).astype(BF16)
        qs_ref[hh, :, tq:2 * tq] = jnp.where(feat >= DIFF_QK_DIM, q_t, 0.0).astype(BF16)
    m_ref[...] = jnp.full_like(m_ref, -jnp.inf)
    l_ref[...] = jnp.zeros_like(l_ref)
    acc_ref[...] = jnp.zeros_like(acc_ref)

    kc = DIFF_KEY_CHUNK
    sub = HEAD_DIM
    tri = (lax.broadcasted_iota(jnp.int32, (sub, sub), 0) <= lax.broadcasted_iota(jnp.int32, (sub, sub), 1))

    def mask_diagonal(s_c, c):
        hidden = jnp.full((sub, sub), -jnp.inf, F32)
        hidden_max = jnp.full((SUBLANES, sub), -jnp.inf, F32)
        row_blocks, maxes = [], None
        for r in range(kc // sub):
            kb = c * (kc // sub) + r
            blocks, row_max = [], []
            for b8 in range(2 * tq // sub):
                b = b8 % (tq // sub)
                if kb > b:
                    blocks.append(hidden)
                    row_max.append(hidden_max)
                    continue
                blk = s_c[r * sub:(r + 1) * sub, b8 * sub:(b8 + 1) * sub]
                if kb == b:
                    blk = jnp.where(tri, blk, -jnp.inf)
                blocks.append(blk)
                row_max.append(_fold_rows(blk, jnp.max))
            row_blocks.append(jnp.concatenate(blocks, axis=1))
            row_max = jnp.concatenate(row_max, axis=1)
            maxes = row_max if maxes is None else jnp.maximum(maxes, row_max)
        return jnp.concatenate(row_blocks, axis=0), maxes

    group_a = (tuple(range(0, N_HEADS // 2)), s0_ref, mx0_ref)
    group_b = (tuple(range(N_HEADS // 2, N_HEADS)), s1_ref, mx1_ref)

    def step(consume=None, produce=None):
        n_slots = N_HEADS // 2
        for slot in range(n_slots):
            if consume is not None:
                t_c, (heads_c, sc_ref, mxc_ref) = consume
                h_c = heads_c[slot]
                m_old = m_ref[h_c]
                m_new = jnp.maximum(m_old, mxc_ref[slot])
                alpha = jnp.exp2(m_old - m_new)
                pv, pb_prev = None, None
            if produce is not None:
                t_p, (heads_p, sp_ref, mxp_ref), diagonal = produce[:3]
                q_src = produce[3] if len(produce) > 3 else None
                h_p = heads_p[slot]
                mx = None
            for c in range(tk // kc):
                chunk = slice(c * kc, (c + 1) * kc)
                if produce is not None:
                    rows = pl.ds(pl.multiple_of(t_p * tk + c * kc, kc), kc)
                    q_mat = qs_ref[h_p] if q_src is None else q_src[slot]
                    s_c = _dot(k_ref[0, rows, _head(h_p)], q_mat)
                if consume is not None:
                    pb_c = jnp.exp2(sc_ref[slot, chunk, :] - m_new).astype(BF16)
                    pb_prev = pb_c if pb_prev is None else jnp.concatenate([pb_prev, pb_c], axis=0)
                    if c == tk // kc - 1:
                        v_rows = slice(h_c * VT_ROWS, (h_c + 1) * VT_ROWS)
                        vt = jnp.concatenate([vt_ref[0, t_c * (tk // tv) + j, v_rows, :]
                                              for j in range(tk // tv)], axis=1)
                        pv = _dot(vt, pb_prev)
                if produce is not None:
                    if diagonal:
                        s_c, c_max = mask_diagonal(s_c, c)
                    else:
                        c_max = _fold_rows(s_c, jnp.max)
                    sp_ref[slot, chunk, :] = s_c
                    mx = c_max if mx is None else jnp.maximum(mx, c_max)
            if consume is not None:
                l_ref[h_c] = alpha * l_ref[h_c] + pv[HEAD_DIM:HEAD_DIM + 1, :]
                acc_ref[h_c] = alpha * acc_ref[h_c] + pv[:HEAD_DIM, :]
                m_ref[h_c] = m_new
            if produce is not None:
                mxp_ref[slot] = jnp.max(mx, axis=0, keepdims=True)

    def tile(t, next_diagonal):
        step(consume=(t, group_a), produce=(t, group_b, False))
        step(consume=(t, group_b), produce=(t + 1, group_a, next_diagonal))

    def body(t, carry):
        tile(t, False)
        return carry

    @pl.when(i == 0)
    def _():
        step(produce=(0, group_a, True))

    @pl.when(i > 0)
    def _():
        lax.fori_loop(0, i - 1, body, 0)
        tile(i - 1, True)

    lam = (jnp.exp(jnp.sum(lq1_ref[...] * lk1_ref[...], axis=-1, keepdims=True))
           - jnp.exp(jnp.sum(lq2_ref[...] * lk2_ref[...], axis=-1, keepdims=True)) + LAM_INIT)

    def finish(heads):
        for hh in heads:
            sl = _head(hh)
            acc = acc_ref[hh]
            inv = 1.0 / l_ref[hh]
            o_t = acc[:, :tq] * inv[:, :tq] - lam * (acc[:, tq:] * inv[:, tq:])
            o = _rms(o_t.T, g_ref[:, sl]) * (1.0 - LAM_INIT)
            yd_ref[:, sl] = (o * _silu(zd_ref[0, :, sl].astype(F32))).astype(BF16)

    def project_rows(c, src_ref, w_rows, first):
        r_c = slice(c * HEAD_DIM, (c + 1) * HEAD_DIM)
        y = _dot(src_ref[0, r_c, :], wo_ref[w_rows, :])
        part_ref[r_c, :] = (x_ref[0, r_c, :] + y) if first else (part_ref[r_c, :] + y)

    def project_heads(heads):
        cols = slice(heads[0] * HEAD_DIM, (heads[-1] + 1) * HEAD_DIM)
        w_rows = slice(GROUP_WIDTH + cols.start, GROUP_WIDTH + cols.stop)
        return _dot(yd_ref[:, cols], wo_ref[w_rows, :])

    step(consume=(i, group_a), produce=(i, group_b, True))
    finish(group_a[0])

    last = pl.num_programs(1) - 1

    @pl.when(i < last)
    def _():
        for slot, hh in enumerate(group_a[0]):
            q_t = qn_ref[0, :, _head(hh)].astype(F32).T
            qsn_ref[slot, :, 0:tq] = jnp.where(feat < DIFF_QK_DIM, q_t, 0.0).astype(BF16)
            qsn_ref[slot, :, tq:2 * tq] = jnp.where(feat >= DIFF_QK_DIM, q_t, 0.0).astype(BF16)
        step(consume=(i, group_b), produce=(0, group_a, False, qsn_ref))

    @pl.when(i == last)
    def _():
        step(consume=(i, group_b))

    for c in range(tq // HEAD_DIM):
        project_rows(c, ym_ref, slice(0, GROUP_WIDTH), True)
    for c in range(tq // HEAD_DIM):
        project_rows(c, yc_ref, slice(2 * GROUP_WIDTH, 3 * GROUP_WIDTH), False)
    y_a = project_heads(group_a[0])
    finish(group_b[0])
    o_ref[0] = _rms(part_ref[...] + y_a + project_heads(group_b[0]), fg_ref[...])


def _diff_out_call(qd, kd, vdt, zd, dnorm_g, lq1, lk1, lq2, lk2, x, ym, yc, w_out, final_g):
    n_q = qd.shape[1] // DIFF_Q_ROWS
    b, s, _ = qd.shape
    tq = DIFF_Q_ROWS
    nkv = vdt.shape[1]
    const = lambda shape: pl.BlockSpec(shape, lambda i, j: (0,) * len(shape))
    tok_spec = lambda w: pl.BlockSpec((1, tq, w), lambda i, j: (i, j, 0))
    return pl.pallas_call(
        _diff_kernel,
        grid=(b, s // tq),
        in_specs=[
            tok_spec(GROUP_WIDTH),
            pl.BlockSpec((1, tq, GROUP_WIDTH), lambda i, j: (i, jnp.minimum(j + 1, n_q - 1), 0)),
            pl.BlockSpec((1, s, GROUP_WIDTH), lambda i, j: (i, 0, 0)),
            pl.BlockSpec((1, nkv, N_HEADS * VT_ROWS, V_TILE), lambda i, j: (i, 0, 0, 0)),
            tok_spec(GROUP_WIDTH),
            const((1, GROUP_WIDTH)),
            const((1, DIFF_QK_DIM)), const((1, DIFF_QK_DIM)), const((1, DIFF_QK_DIM)), const((1, DIFF_QK_DIM)),
            tok_spec(D_MODEL), tok_spec(GROUP_WIDTH), tok_spec(GROUP_WIDTH),
            const((3 * GROUP_WIDTH, D_MODEL)), const((1, D_MODEL)),
        ],
        out_specs=tok_spec(D_MODEL),
        out_shape=jax.ShapeDtypeStruct((b, s, D_MODEL), F32),
        scratch_shapes=[pltpu.VMEM((N_HEADS, HEAD_DIM, 2 * tq), BF16),
                        pltpu.VMEM((N_HEADS, 1, 2 * tq), F32),
                        pltpu.VMEM((N_HEADS, 1, 2 * tq), F32),
                        pltpu.VMEM((N_HEADS, HEAD_DIM, 2 * tq), F32),
                        pltpu.VMEM((N_HEADS // 2, DIFF_KV_ROWS, 2 * tq), F32),
                        pltpu.VMEM((N_HEADS // 2, DIFF_KV_ROWS, 2 * tq), F32),
                        pltpu.VMEM((N_HEADS // 2, 1, 2 * tq), F32),
                        pltpu.VMEM((N_HEADS // 2, 1, 2 * tq), F32),
                        pltpu.VMEM((tq, GROUP_WIDTH), BF16),
                        pltpu.VMEM((tq, D_MODEL), F32),
                        pltpu.VMEM((N_HEADS // 2, HEAD_DIM, 2 * tq), BF16)],
        compiler_params=pltpu.CompilerParams(
            dimension_semantics=("arbitrary", "arbitrary"),
            vmem_limit_bytes=VMEM_LIMIT_BYTES),
        name="diffattn_out",
    )(qd, qd, kd, vdt, zd, dnorm_g, lq1, lk1, lq2, lk2, x, ym, yc, w_out, final_g)


def kernel(x, mem, norm_g, w_in, conv_w, conv_b, wq_m, wk_m, wv_m, w_if, b_if, mnorm_g, skip_m,
           lam_q1, lam_k1, lam_q2, lam_k2, dnorm_g, mem_norm_g, w_mem_kv, w_out, final_g):
    b, s, d = x.shape
    assert (d, s % DIFF_KV_ROWS, DIFF_KV_ROWS % V_TILE, FRONT_ROWS % V_TILE) == (D_MODEL, 0, 0, 0)
    assert (s % FRONT_ROWS, DIFF_KV_ROWS % DIFF_KEY_CHUNK) == (0, 0)
    assert DIFF_Q_ROWS == DIFF_KV_ROWS and DIFF_KEY_CHUNK % HEAD_DIM == 0
    assert norm_g.shape[0] == 1, "single-layer kernel"
    l = 0
    w_if_pad = jnp.pad(w_if[l], ((0, 0), (0, HEAD_DIM - w_if.shape[-1]))).astype(BF16)
    b_if_pad = jnp.pad(b_if[l], (0, HEAD_DIM - b_if.shape[-1]))[None, :]

    kv = _memkv_call(mem, mem_norm_g[l][None, :], w_mem_kv[l].astype(BF16))
    ym, qd, kd, vdt, zd, yc = _front_call(
        x, kv, norm_g[l][None, :], w_in[l].astype(BF16), conv_w[l], conv_b[l][None, :],
        jnp.concatenate([wq_m[l], wk_m[l]], axis=-1).astype(BF16), wv_m[l].astype(BF16), w_if_pad, b_if_pad,
        mnorm_g[l][None, :], skip_m[l][None, :])
    return _diff_out_call(qd, kd, vdt, zd, dnorm_g[l][None, :],
                          lam_q1[l][None, :], lam_k1[l][None, :], lam_q2[l][None, :], lam_k2[l][None, :],
                          x, ym, yc, w_out[l].astype(BF16), final_g[None, :])
```

```python
import math

import jax
import jax.numpy as jnp
from jax import lax
from jax.experimental import pallas as pl
from jax.experimental.pallas import tpu as pltpu

F32 = jnp.float32
BF16 = jnp.bfloat16

D_MODEL = 1024
N_HEADS = 4
HEAD_DIM = 128
SUBLANES = 8
V7X_VMEM_BYTES = 64 * 1024 * 1024
VMEM_LIMIT_BYTES = V7X_VMEM_BYTES - 6 * 1024 * 1024
GROUP_WIDTH = N_HEADS * HEAD_DIM
DIFF_QK_DIM = 64
CONV_WIDTH = 4
MLSTM_CHUNK = 128
MLSTM_STEP_CHUNKS = 4
MEM_LEN = 256
IN_WIDTH = 9 * GROUP_WIDTH
NORM_EPS = 1e-6
LAM_INIT = 0.8 - 0.6 * math.exp(-0.3 * 0)
LOG2E = math.log2(math.e)

OFF_XM, OFF_OM, OFF_ZM, OFF_QD, OFF_KD, OFF_VD, OFF_ZD, OFF_QC, OFF_ZC = (
    i * GROUP_WIDTH for i in range(9))

FRONT_ROWS = MLSTM_STEP_CHUNKS * MLSTM_CHUNK
V_TILE = 256
VT_PAD = 16
VT_ROWS = HEAD_DIM + VT_PAD
DIFF_Q_ROWS = 512
DIFF_KV_ROWS = 512
DIFF_KEY_CHUNK = 512
CONV_HALO = 8

NT_DIMS = (((1,), (1,)), ((), ()))
TN_DIMS = (((0,), (0,)), ((), ()))


def _dot(a, b):
    return jnp.dot(a, b, preferred_element_type=F32)


def _dot_nt(a, b):
    return lax.dot_general(a, b, NT_DIMS, preferred_element_type=F32)


def _sigmoid(x):
    return 0.5 * jnp.tanh(0.5 * x) + 0.5


def _silu(x):
    return x * _sigmoid(x)


def _rms(x, g):
    return x * lax.rsqrt(jnp.mean(x * x, axis=-1, keepdims=True) + NORM_EPS) * g


def _head(h):
    return slice(h * HEAD_DIM, (h + 1) * HEAD_DIM)


def _fold_rows(x, op):
    r, n = x.shape
    return op(x.reshape(r // SUBLANES, SUBLANES, n), axis=0)


def _log_sigmoid(x):
    return jnp.minimum(x, 0.0) - jnp.log1p(jnp.exp(-jnp.abs(x)))


def _cumsum_rows(f, tril):
    hi = f.astype(BF16)
    r1 = f - hi.astype(F32)
    mid = r1.astype(BF16)
    lo = (r1 - mid.astype(F32)).astype(BF16)
    return _dot(tril, hi) + _dot(tril, mid) + _dot(tril, lo)


def _memkv_kernel(mem_ref, g_ref, w_ref, kv_ref):
    y = _rms(mem_ref[0], g_ref[...])
    kv_ref[0] = _dot(y.astype(BF16), w_ref[...]).astype(BF16)


def _memkv_call(mem, g, w):
    b = mem.shape[0]
    return pl.pallas_call(
        _memkv_kernel,
        grid=(b,),
        in_specs=[
            pl.BlockSpec((1, MEM_LEN, D_MODEL), lambda i: (i, 0, 0)),
            pl.BlockSpec((1, D_MODEL), lambda i: (0, 0)),
            pl.BlockSpec((D_MODEL, 2 * GROUP_WIDTH), lambda i: (0, 0)),
        ],
        out_specs=pl.BlockSpec((1, MEM_LEN, 2 * GROUP_WIDTH), lambda i: (i, 0, 0)),
        out_shape=jax.ShapeDtypeStruct((b, MEM_LEN, 2 * GROUP_WIDTH), BF16),
        compiler_params=pltpu.CompilerParams(dimension_semantics=("arbitrary",)),
        name="memkv",
    )(mem, g, w)


def _front_kernel(x_ref, kv_ref, ng_ref, win_ref, cw_ref, cb_ref, wqk_ref, wv_ref, wif_ref, bif_ref,
                  mg_ref, sk_ref,
                  ym_ref, qd_ref, kd_ref, vdt_ref, zd_ref, yc_ref,
                  conv_ref, c_ref, n_ref, m_ref, q_s, k_s, v_s, g_s, om_s, zm_s, xcv_s):
    tm = FRONT_ROWS
    L = MLSTM_CHUNK
    heads = range(N_HEADS)

    @pl.when(pl.program_id(1) == 0)
    def _():
        conv_ref[tm:tm + CONV_HALO, :] = jnp.zeros((CONV_HALO, GROUP_WIDTH), F32)
        c_ref[...] = jnp.zeros_like(c_ref)
        n_ref[...] = jnp.zeros_like(n_ref)
        m_ref[...] = jnp.zeros_like(m_ref)

    h = _rms(x_ref[0], ng_ref[...]).astype(BF16)

    def proj(off):
        return _dot(h, win_ref[:, off:off + GROUP_WIDTH])

    def put_vdt(val):
        v_t = val.T.astype(BF16)
        pad_row = lax.broadcasted_iota(jnp.int32, (VT_PAD, tm), 0)
        pad = jnp.where(pad_row == 0, 1.0, 0.0).astype(BF16)
        v_aug = jnp.concatenate([blk for hh in heads for blk in (v_t[_head(hh), :], pad)], axis=0)
        for j in range(tm // V_TILE):
            vdt_ref[0, j] = v_aug[:, j * V_TILE:(j + 1) * V_TILE]

    def put(ref, scale=None):
        def sink(val):
            ref[...] = (val if scale is None else val * scale).astype(BF16).reshape(ref.shape)
        return sink

    side_pieces = [(OFF_OM, put(om_s)), (OFF_ZM, put(zm_s)),
                   (OFF_QD, put(qd_ref, DIFF_QK_DIM ** -0.5 * LOG2E)), (OFF_KD, put(kd_ref)),
                   (OFF_VD, put_vdt), (OFF_ZD, put(zd_ref))]
    pending = []

    def side_start():
        if side_pieces:
            off, sink = side_pieces.pop(0)
            pending.append((proj(off), sink))

    def side_finish():
        if pending:
            val, sink = pending.pop(0)
            sink(val)

    x_m = proj(OFF_XM)
    q_cb = proj(OFF_QC).astype(BF16)
    z_c = proj(OFF_ZC)

    conv_ref[0:CONV_HALO, :] = conv_ref[tm:tm + CONV_HALO, :]
    conv_ref[CONV_HALO:CONV_HALO + tm, :] = x_m
    acc = jnp.broadcast_to(cb_ref[...], (tm, GROUP_WIDTH))
    for j in range(CONV_WIDTH):
        start = CONV_HALO - (CONV_WIDTH - 1) + j
        acc = acc + cw_ref[j:j + 1, :] * conv_ref[start:start + tm, :]
    sc = [_dot_nt(q_cb[:, _head(hh)], kv_ref[0, :, _head(hh)]) * (HEAD_DIM ** -0.5 * LOG2E)
          for hh in heads]
    x_cv = _silu(acc)
    xcv_b = x_cv.astype(BF16)
    xcv_s[...] = xcv_b
    xm_b = x_m.astype(BF16)

    qk_p = [_dot(xcv_b[:, _head(hh)], wqk_ref[hh]) for hh in heads]
    vv_p = [_dot(xm_b[:, _head(hh)], wv_ref[hh]) for hh in heads]

    q_all = jnp.concatenate([qk_p[hh][:, :HEAD_DIM] for hh in heads], axis=1).astype(BF16)
    k_all = jnp.concatenate([qk_p[hh][:, HEAD_DIM:] for hh in heads], axis=1)
    v_all = jnp.concatenate(vv_p, axis=1).astype(BF16)
    q_s[...] = q_all
    k_s[...] = (k_all * HEAD_DIM ** -0.5).astype(BF16)
    v_s[...] = v_all
    g_s[...] = (jnp.broadcast_to(bif_ref[...], (tm, HEAD_DIM))
                + _dot(q_all, wif_ref[0:GROUP_WIDTH, :])
                + _dot(k_all.astype(BF16), wif_ref[GROUP_WIDTH:2 * GROUP_WIDTH, :])
                + _dot(v_all, wif_ref[2 * GROUP_WIDTH:3 * GROUP_WIDTH, :]))

    mx = [jnp.max(sc[hh], axis=-1, keepdims=True) for hh in heads]
    pm = [jnp.exp2(sc[hh] - mx[hh]) for hh in heads]
    inv = [1.0 / jnp.sum(pm[hh], axis=-1, keepdims=True) for hh in heads]
    pc = [(pm[hh] * inv[hh]).astype(BF16) for hh in heads]
    oc = [_dot(pc[hh], kv_ref[0, :, GROUP_WIDTH + hh * HEAD_DIM:GROUP_WIDTH + (hh + 1) * HEAD_DIM]) for hh in heads]
    for hh in heads:
        yc_ref[0, :, _head(hh)] = (oc[hh] * _silu(z_c[:, _head(hh)])).astype(BF16)

    row = lax.broadcasted_iota(jnp.int32, (L, L), 0)
    col = lax.broadcasted_iota(jnp.int32, (L, L), 1)
    causal = col <= row
    tril = jnp.where(causal, 1.0, 0.0).astype(BF16)
    chunks = range(MLSTM_STEP_CHUNKS)
    probs = [(c, hh) for c in chunks for hh in heads]
    rows = lambda c: slice(c * L, (c + 1) * L)
    q_of = lambda p: q_s[rows(p[0]), _head(p[1])]
    k_of = lambda p: k_s[rows(p[0]), _head(p[1])]
    v_of = lambda p: v_s[rows(p[0]), _head(p[1])]

    def each(f, side=False):
        if side:
            side_start()
        out = {p: f(p) for p in probs}
        if side:
            side_finish()
        return out

    a_cols, a_rows = [], []
    for c in chunks:
        g = g_s[rows(c), :]
        bb_all = _cumsum_rows(_log_sigmoid(g), tril)
        ac = jnp.where(col < N_HEADS, g, bb_all) * LOG2E
        a_cols.append(ac)
        a_rows.append(ac.T)
    ib_col = {(c, hh): a_cols[c][:, hh:hh + 1] for c, hh in probs}
    bb_col = {(c, hh): a_cols[c][:, N_HEADS + hh:N_HEADS + hh + 1] for c, hh in probs}
    ib_row = {(c, hh): a_rows[c][hh:hh + 1, :] for c, hh in probs}
    bb_row = {(c, hh): a_rows[c][N_HEADS + hh:N_HEADS + hh + 1, :] for c, hh in probs}
    b_end = {p: bb_col[p][L - 1:L, :] for p in probs}

    qk = {p: _dot_nt(q_of(p), k_of(p)) for p in probs}

    d = each(lambda p: jnp.where(causal, (bb_col[p] - bb_row[p]) + ib_row[p], -jnp.inf), side=True)
    r = each(lambda p: jnp.max(d[p], axis=-1, keepdims=True))
    s_loc = each(lambda p: qk[p] * jnp.exp2(d[p] - r[p]), side=True)
    rs = each(lambda p: jnp.sum(s_loc[p], axis=-1, keepdims=True))
    s_bf = each(lambda p: s_loc[p].astype(BF16))
    a = each(lambda p: (b_end[p] - bb_col[p]) + ib_col[p], side=True)
    a_max = each(lambda p: jnp.max(a[p], axis=0, keepdims=True))
    w_loc = each(lambda p: jnp.broadcast_to(jnp.exp2(a[p] - a_max[p]), (L, HEAD_DIM)))
    vw = each(lambda p: (v_of(p).astype(F32) * w_loc[p]).astype(BF16), side=True)
    nu = each(lambda p: jnp.sum(k_of(p).astype(F32) * w_loc[p], axis=0, keepdims=True))

    sv = {p: _dot(s_bf[p], v_of(p)) for p in probs}
    u = {p: lax.dot_general(vw[p], k_of(p), TN_DIMS, preferred_element_type=F32) for p in probs}

    m_in, c_in, n_in = {}, {}, {}
    for hh in heads:
        m_cur = m_ref[hh][:, 0:1]
        c_cur = c_ref[hh]
        n_cur = n_ref[hh]
        for c in chunks:
            p = (c, hh)
            m_in[p], c_in[p], n_in[p] = m_cur, c_cur.astype(BF16), n_cur
            m_new = jnp.maximum(b_end[p] + m_cur, a_max[p])
            decay = jnp.exp2(b_end[p] + m_cur - m_new)
            e_upd = jnp.exp2(a_max[p] - m_new)
            c_cur = decay * c_cur + e_upd * u[p]
            n_cur = decay * n_cur + e_upd * nu[p]
            m_cur = m_new
        c_ref[hh] = c_cur
        n_ref[hh] = n_cur
        m_ref[hh] = jnp.broadcast_to(m_cur, (1, HEAD_DIM))

    q_c = {p: _dot_nt(q_of(p), c_in[p]) for p in probs}

    tile_of = lambda ref, p: ref[rows(p[0]), _head(p[1])].astype(F32)
    g_col = each(lambda p: bb_col[p] + m_in[p])
    m_t = each(lambda p: jnp.maximum(g_col[p], r[p]))
    e_loc = each(lambda p: jnp.exp2(r[p] - m_t[p]))
    inter = each(lambda p: jnp.exp2(g_col[p] - m_t[p]))
    q_n = each(lambda p: jnp.sum(q_of(p).astype(F32) * n_in[p], axis=-1, keepdims=True))
    den = each(lambda p: e_loc[p] * rs[p] + inter[p] * q_n[p])
    scale = each(lambda p: 1.0 / jnp.maximum(jnp.abs(den[p]), jnp.exp2(-m_t[p])))
    w_intra = each(lambda p: e_loc[p] * scale[p])
    w_inter = each(lambda p: inter[p] * scale[p])
    hg = each(lambda p: _sigmoid(tile_of(om_s, p))
              * (w_intra[p] * sv[p] + w_inter[p] * q_c[p]), side=True)
    mu = each(lambda p: jnp.mean(hg[p], axis=-1, keepdims=True))
    cen = each(lambda p: hg[p] - mu[p], side=True)
    var = each(lambda p: jnp.mean(jnp.square(cen[p]), axis=-1, keepdims=True))
    assert not side_pieces and not pending
    for p in probs:
        sl = _head(p[1])
        y = cen[p] * lax.rsqrt(var[p] + NORM_EPS) * mg_ref[:, sl]
        y = (y + sk_ref[:, sl] * tile_of(xcv_s, p)) * _silu(tile_of(zm_s, p))
        ym_ref[0, rows(p[0]), sl] = y.astype(BF16)


def _front_call(x, kv, norm_g, w_in, conv_w, conv_b, wqk, wv, w_if, b_if, mnorm_g, skip_m):
    b, s, _ = x.shape
    tm = FRONT_ROWS
    ns = s // tm
    tok = lambda w: jax.ShapeDtypeStruct((b, s, w), BF16)
    tok_spec = lambda w: pl.BlockSpec((1, tm, w), lambda i, j: (i, j, 0))
    const = lambda shape: pl.BlockSpec(shape, lambda i, j: (0,) * len(shape))
    vdt_tiles = tm // V_TILE
    out_shape = (
        tok(GROUP_WIDTH),
        tok(GROUP_WIDTH), tok(GROUP_WIDTH),
        jax.ShapeDtypeStruct((b, ns * vdt_tiles, N_HEADS * VT_ROWS, V_TILE), BF16),
        tok(GROUP_WIDTH),
        tok(GROUP_WIDTH),
    )
    out_specs = (
        tok_spec(GROUP_WIDTH), tok_spec(GROUP_WIDTH), tok_spec(GROUP_WIDTH),
        pl.BlockSpec((1, vdt_tiles, N_HEADS * VT_ROWS, V_TILE), lambda i, j: (i, j, 0, 0)),
        tok_spec(GROUP_WIDTH), tok_spec(GROUP_WIDTH),
    )
    in_specs = [
        pl.BlockSpec((1, tm, D_MODEL), lambda i, j: (i, j, 0)),
        pl.BlockSpec((1, MEM_LEN, 2 * GROUP_WIDTH), lambda i, j: (i, 0, 0)),
        const((1, D_MODEL)),
        const((D_MODEL, IN_WIDTH)),
        const((CONV_WIDTH, GROUP_WIDTH)),
        const((1, GROUP_WIDTH)),
        const((N_HEADS, HEAD_DIM, 2 * HEAD_DIM)),
        const((N_HEADS, HEAD_DIM, HEAD_DIM)),
        const((3 * GROUP_WIDTH, HEAD_DIM)),
        const((1, HEAD_DIM)),
        const((1, GROUP_WIDTH)),
        const((1, GROUP_WIDTH)),
    ]
    tile_bf16 = pltpu.VMEM((tm, GROUP_WIDTH), BF16)
    return pl.pallas_call(
        _front_kernel,
        grid=(b, ns),
        in_specs=in_specs,
        out_specs=out_specs,
        out_shape=out_shape,
        scratch_shapes=[pltpu.VMEM((tm + CONV_HALO, GROUP_WIDTH), F32),
                        pltpu.VMEM((N_HEADS, HEAD_DIM, HEAD_DIM), F32),
                        pltpu.VMEM((N_HEADS, 1, HEAD_DIM), F32),
                        pltpu.VMEM((N_HEADS, 1, HEAD_DIM), F32),
                        tile_bf16, tile_bf16, tile_bf16,
                        pltpu.VMEM((tm, HEAD_DIM), F32),
                        tile_bf16, tile_bf16, tile_bf16],
        compiler_params=pltpu.CompilerParams(
            dimension_semantics=("arbitrary", "arbitrary"),
            vmem_limit_bytes=VMEM_LIMIT_BYTES),
        name="front",
    )(x, kv, norm_g, w_in, conv_w, conv_b, wqk, wv, w_if, b_if, mnorm_g, skip_m)


def _diff_kernel(q_ref, qn_ref, k_ref, vt_ref, zd_ref, g_ref, lq1_ref, lk1_ref, lq2_ref, lk2_ref,
                 x_ref, ym_ref, yc_ref, wo_ref, fg_ref, o_ref,
                 qs_ref, m_ref, l_ref, acc_ref, s0_ref, s1_ref, mx0_ref, mx1_ref, yd_ref, part_ref, qsn_ref):
    tq, tk, tv = DIFF_Q_ROWS, DIFF_KV_ROWS, V_TILE
    i = pl.program_id(1)
    feat = lax.broadcasted_iota(jnp.int32, (HEAD_DIM, tq), 0)
    for hh in range(N_HEADS):
        q_t = q_ref[0, :, _head(hh)].astype(F32).T
        qs_ref[hh, :, 0:tq] = jnp.where(feat < DIFF_QK_DIM, q_t, 0.0).astype(BF16)
        qs_ref[hh, :, tq:2 * tq] = jnp.where(feat >= DIFF_QK_DIM, q_t, 0.0).astype(BF16)
    m_ref[...] = jnp.full_like(m_ref, -jnp.inf)
    l_ref[...] = jnp.zeros_like(l_ref)
    acc_ref[...] = jnp.zeros_like(acc_ref)

    kc = DIFF_KEY_CHUNK
    sub = HEAD_DIM
    tri = (lax.broadcasted_iota(jnp.int32, (sub, sub), 0) <= lax.broadcasted_iota(jnp.int32, (sub, sub), 1))

    def mask_diagonal(s_c, c):
        hidden = jnp.full((sub, sub), -jnp.inf, F32)
        hidden_max = jnp.full((SUBLANES, sub), -jnp.inf, F32)
        row_blocks, maxes = [], None
        for r in range(kc // sub):
            kb = c * (kc // sub) + r
            blocks, row_max = [], []
            for b8 in range(2 * tq // sub):
                b = b8 % (tq // sub)
                if kb > b:
                    blocks.append(hidden)
                    row_max.append(hidden_max)
                    continue
                blk = s_c[r * sub:(r + 1) * sub, b8 * sub:(b8 + 1) * sub]
                if kb == b:
                    blk = jnp.where(tri, blk, -jnp.inf)
                blocks.append(blk)
                row_max.append(_fold_rows(blk, jnp.max))
            row_blocks.append(jnp.concatenate(blocks, axis=1))
            row_max = jnp.concatenate(row_max, axis=1)
            maxes = row_max if maxes is None else jnp.maximum(maxes, row_max)
        return jnp.concatenate(row_blocks, axis=0), maxes

    group_a = (tuple(range(0, N_HEADS // 2)), s0_ref, mx0_ref)
    group_b = (tuple(range(N_HEADS // 2, N_HEADS)), s1_ref, mx1_ref)

    def step(consume=None, produce=None):
        n_slots = N_HEADS // 2
        for slot in range(n_slots):
            if consume is not None:
                t_c, (heads_c, sc_ref, mxc_ref) = consume
                h_c = heads_c[slot]
                m_old = m_ref[h_c]
                m_new = jnp.maximum(m_old, mxc_ref[slot])
                alpha = jnp.exp2(m_old - m_new)
                pv, pb_prev = None, None
            if produce is not None:
                t_p, (heads_p, sp_ref, mxp_ref), diagonal = produce[:3]
                q_src = produce[3] if len(produce) > 3 else None
                h_p = heads_p[slot]
                mx = None
            for c in range(tk // kc):
                chunk = slice(c * kc, (c + 1) * kc)
                if produce is not None:
                    rows = pl.ds(pl.multiple_of(t_p * tk + c * kc, kc), kc)
                    q_mat = qs_ref[h_p] if q_src is None else q_src[slot]
                    s_c = _dot(k_ref[0, rows, _head(h_p)], q_mat)
                if consume is not None:
                    pb_c = jnp.exp2(sc_ref[slot, chunk, :] - m_new).astype(BF16)
                    pb_prev = pb_c if pb_prev is None else jnp.concatenate([pb_prev, pb_c], axis=0)
                    if c == tk // kc - 1:
                        v_rows = slice(h_c * VT_ROWS, (h_c + 1) * VT_ROWS)
                        vt = jnp.concatenate([vt_ref[0, t_c * (tk // tv) + j, v_rows, :]
                                              for j in range(tk // tv)], axis=1)
                        pv = _dot(vt, pb_prev)
                if produce is not None:
                    if diagonal:
                        s_c, c_max = mask_diagonal(s_c, c)
                    else:
                        c_max = _fold_rows(s_c, jnp.max)
                    sp_ref[slot, chunk, :] = s_c
                    mx = c_max if mx is None else jnp.maximum(mx, c_max)
            if consume is not None:
                l_ref[h_c] = alpha * l_ref[h_c] + pv[HEAD_DIM:HEAD_DIM + 1, :]
                acc_ref[h_c] = alpha * acc_ref[h_c] + pv[:HEAD_DIM, :]
                m_ref[h_c] = m_new
            if produce is not None:
                mxp_ref[slot] = jnp.max(mx, axis=0, keepdims=True)

    def tile(t, next_diagonal):
        step(consume=(t, group_a), produce=(t, group_b, False))
        step(consume=(t, group_b), produce=(t + 1, group_a, next_diagonal))

    def body(t, carry):
        tile(t, False)
        return carry

    @pl.when(i == 0)
    def _():
        step(produce=(0, group_a, True))

    @pl.when(i > 0)
    def _():
        lax.fori_loop(0, i - 1, body, 0)
        tile(i - 1, True)

    lam = (jnp.exp(jnp.sum(lq1_ref[...] * lk1_ref[...], axis=-1, keepdims=True))
           - jnp.exp(jnp.sum(lq2_ref[...] * lk2_ref[...], axis=-1, keepdims=True)) + LAM_INIT)

    def finish(heads):
        for hh in heads:
            sl = _head(hh)
            acc = acc_ref[hh]
            inv = 1.0 / l_ref[hh]
            o_t = acc[:, :tq] * inv[:, :tq] - lam * (acc[:, tq:] * inv[:, tq:])
            o = _rms(o_t.T, g_ref[:, sl]) * (1.0 - LAM_INIT)
            yd_ref[:, sl] = (o * _silu(zd_ref[0, :, sl].astype(F32))).astype(BF16)

    def project_rows(c, src_ref, w_rows, first):
        r_c = slice(c * HEAD_DIM, (c + 1) * HEAD_DIM)
        y = _dot(src_ref[0, r_c, :], wo_ref[w_rows, :])
        part_ref[r_c, :] = (x_ref[0, r_c, :] + y) if first else (part_ref[r_c, :] + y)

    def project_heads(heads):
        cols = slice(heads[0] * HEAD_DIM, (heads[-1] + 1) * HEAD_DIM)
        w_rows = slice(GROUP_WIDTH + cols.start, GROUP_WIDTH + cols.stop)
        return _dot(yd_ref[:, cols], wo_ref[w_rows, :])

    step(consume=(i, group_a), produce=(i, group_b, True))
    finish(group_a[0])

    last = pl.num_programs(1) - 1

    @pl.when(i < last)
    def _():
        for slot, hh in enumerate(group_a[0]):
            q_t = qn_ref[0, :, _head(hh)].astype(F32).T
            qsn_ref[slot, :, 0:tq] = jnp.where(feat < DIFF_QK_DIM, q_t, 0.0).astype(BF16)
            qsn_ref[slot, :, tq:2 * tq] = jnp.where(feat >= DIFF_QK_DIM, q_t, 0.0).astype(BF16)
        step(consume=(i, group_b), produce=(0, group_a, False, qsn_ref))

    @pl.when(i == last)
    def _():
        step(consume=(i, group_b))

    for c in range(tq // HEAD_DIM):
        project_rows(c, ym_ref, slice(0, GROUP_WIDTH), True)
    for c in range(tq // HEAD_DIM):
        project_rows(c, yc_ref, slice(2 * GROUP_WIDTH, 3 * GROUP_WIDTH), False)
    y_a = project_heads(group_a[0])
    finish(group_b[0])
    o_ref[0] = _rms(part_ref[...] + y_a + project_heads(group_b[0]), fg_ref[...])


def _diff_out_call(qd, kd, vdt, zd, dnorm_g, lq1, lk1, lq2, lk2, x, ym, yc, w_out, final_g):
    n_q = qd.shape[1] // DIFF_Q_ROWS
    b, s, _ = qd.shape
    tq = DIFF_Q_ROWS
    nkv = vdt.shape[1]
    const = lambda shape: pl.BlockSpec(shape, lambda i, j: (0,) * len(shape))
    tok_spec = lambda w: pl.BlockSpec((1, tq, w), lambda i, j: (i, j, 0))
    return pl.pallas_call(
        _diff_kernel,
        grid=(b, s // tq),
        in_specs=[
            tok_spec(GROUP_WIDTH),
            pl.BlockSpec((1, tq, GROUP_WIDTH), lambda i, j: (i, jnp.minimum(j + 1, n_q - 1), 0)),
            pl.BlockSpec((1, s, GROUP_WIDTH), lambda i, j: (i, 0, 0)),
            pl.BlockSpec((1, nkv, N_HEADS * VT_ROWS, V_TILE), lambda i, j: (i, 0, 0, 0)),
            tok_spec(GROUP_WIDTH),
            const((1, GROUP_WIDTH)),
            const((1, DIFF_QK_DIM)), const((1, DIFF_QK_DIM)), const((1, DIFF_QK_DIM)), const((1, DIFF_QK_DIM)),
            tok_spec(D_MODEL), tok_spec(GROUP_WIDTH), tok_spec(GROUP_WIDTH),
            const((3 * GROUP_WIDTH, D_MODEL)), const((1, D_MODEL)),
        ],
        out_specs=tok_spec(D_MODEL),
        out_shape=jax.ShapeDtypeStruct((b, s, D_MODEL), F32),
        scratch_shapes=[pltpu.VMEM((N_HEADS, HEAD_DIM, 2 * tq), BF16),
                        pltpu.VMEM((N_HEADS, 1, 2 * tq), F32),
                        pltpu.VMEM((N_HEADS, 1, 2 * tq), F32),
                        pltpu.VMEM((N_HEADS, HEAD_DIM, 2 * tq), F32),
                        pltpu.VMEM((N_HEADS // 2, DIFF_KV_ROWS, 2 * tq), F32),
                        pltpu.VMEM((N_HEADS // 2, DIFF_KV_ROWS, 2 * tq), F32),
                        pltpu.VMEM((N_HEADS // 2, 1, 2 * tq), F32),
                        pltpu.VMEM((N_HEADS // 2, 1, 2 * tq), F32),
                        pltpu.VMEM((tq, GROUP_WIDTH), BF16),
                        pltpu.VMEM((tq, D_MODEL), F32),
                        pltpu.VMEM((N_HEADS // 2, HEAD_DIM, 2 * tq), BF16)],
        compiler_params=pltpu.CompilerParams(
            dimension_semantics=("arbitrary", "arbitrary"),
            vmem_limit_bytes=VMEM_LIMIT_BYTES),
        name="diffattn_out",
    )(qd, qd, kd, vdt, zd, dnorm_g, lq1, lk1, lq2, lk2, x, ym, yc, w_out, final_g)


def kernel(x, mem, norm_g, w_in, conv_w, conv_b, wq_m, wk_m, wv_m, w_if, b_if, mnorm_g, skip_m,
           lam_q1, lam_k1, lam_q2, lam_k2, dnorm_g, mem_norm_g, w_mem_kv, w_out, final_g):
    b, s, d = x.shape
    assert (d, s % DIFF_KV_ROWS, DIFF_KV_ROWS % V_TILE, FRONT_ROWS % V_TILE) == (D_MODEL, 0, 0, 0)
    assert (s % FRONT_ROWS, DIFF_KV_ROWS % DIFF_KEY_CHUNK) == (0, 0)
    assert DIFF_Q_ROWS == DIFF_KV_ROWS and DIFF_KEY_CHUNK % HEAD_DIM == 0
    assert norm_g.shape[0] == 1, "single-layer kernel"
    l = 0
    w_if_pad = jnp.pad(w_if[l], ((0, 0), (0, HEAD_DIM - w_if.shape[-1]))).astype(BF16)
    b_if_pad = jnp.pad(b_if[l], (0, HEAD_DIM - b_if.shape[-1]))[None, :]

    kv = _memkv_call(mem, mem_norm_g[l][None, :], w_mem_kv[l].astype(BF16))
    ym, qd, kd, vdt, zd, yc = _front_call(
        x, kv, norm_g[l][None, :], w_in[l].astype(BF16), conv_w[l], conv_b[l][None, :],
        jnp.concatenate([wq_m[l], wk_m[l]], axis=-1).astype(BF16), wv_m[l].astype(BF16), w_if_pad, b_if_pad,
        mnorm_g[l][None, :], skip_m[l][None, :])
    return _diff_out_call(qd, kd, vdt, zd, dnorm_g[l][None, :],
                          lam_q1[l][None, :], lam_k1[l][None, :], lam_q2[l][None, :], lam_k2[l][None, :],
                          x, ym, yc, w_out[l].astype(BF16), final_g[None, :])
```

```python
import math

import jax
import jax.numpy as jnp
from jax import lax
from jax.experimental import pallas as pl
from jax.experimental.pallas import tpu as pltpu

F32 = jnp.float32
BF16 = jnp.bfloat16

D_MODEL = 1024
N_HEADS = 4
HEAD_DIM = 128
SUBLANES = 8
V7X_VMEM_BYTES = 64 * 1024 * 1024
VMEM_LIMIT_BYTES = V7X_VMEM_BYTES - 6 * 1024 * 1024
GROUP_WIDTH = N_HEADS * HEAD_DIM
DIFF_QK_DIM = 64
CONV_WIDTH = 4
MLSTM_CHUNK = 128
MLSTM_STEP_CHUNKS = 4
MEM_LEN = 256
IN_WIDTH = 9 * GROUP_WIDTH
NORM_EPS = 1e-6
LAM_INIT = 0.8 - 0.6 * math.exp(-0.3 * 0)
LOG2E = math.log2(math.e)

OFF_XM, OFF_OM, OFF_ZM, OFF_QD, OFF_KD, OFF_VD, OFF_ZD, OFF_QC, OFF_ZC = (
    i * GROUP_WIDTH for i in range(9))

FRONT_ROWS = MLSTM_STEP_CHUNKS * MLSTM_CHUNK
V_TILE = 256
VT_PAD = 16
VT_ROWS = HEAD_DIM + VT_PAD
DIFF_Q_ROWS = 512
DIFF_KV_ROWS = 512
DIFF_KEY_CHUNK = 512
CONV_HALO = 8

NT_DIMS = (((1,), (1,)), ((), ()))
TN_DIMS = (((0,), (0,)), ((), ()))


def _dot(a, b):
    return jnp.dot(a, b, preferred_element_type=F32)


def _dot_nt(a, b):
    return lax.dot_general(a, b, NT_DIMS, preferred_element_type=F32)


def _sigmoid(x):
    return 0.5 * jnp.tanh(0.5 * x) + 0.5


def _silu(x):
    half = 0.5 * x
    return half * (jnp.tanh(half) + 1.0)


def _rms(x, g):
    return x * lax.rsqrt(jnp.mean(x * x, axis=-1, keepdims=True) + NORM_EPS) * g


def _head(h):
    return slice(h * HEAD_DIM, (h + 1) * HEAD_DIM)


def _fold_rows(x, op):
    r, n = x.shape
    return op(x.reshape(r // SUBLANES, SUBLANES, n), axis=0)


def _log_sigmoid(x):
    return jnp.minimum(x, 0.0) - jnp.log1p(jnp.exp(-jnp.abs(x)))


def _cumsum_rows(f, tril):
    hi = f.astype(BF16)
    r1 = f - hi.astype(F32)
    mid = r1.astype(BF16)
    lo = (r1 - mid.astype(F32)).astype(BF16)
    return _dot(tril, hi) + _dot(tril, mid) + _dot(tril, lo)


def _memkv_kernel(mem_ref, g_ref, w_ref, kv_ref):
    y = _rms(mem_ref[0], g_ref[...])
    kv_ref[0] = _dot(y.astype(BF16), w_ref[...]).astype(BF16)


def _memkv_call(mem, g, w):
    b = mem.shape[0]
    return pl.pallas_call(
        _memkv_kernel,
        grid=(b,),
        in_specs=[
            pl.BlockSpec((1, MEM_LEN, D_MODEL), lambda i: (i, 0, 0)),
            pl.BlockSpec((1, D_MODEL), lambda i: (0, 0)),
            pl.BlockSpec((D_MODEL, 2 * GROUP_WIDTH), lambda i: (0, 0)),
        ],
        out_specs=pl.BlockSpec((1, MEM_LEN, 2 * GROUP_WIDTH), lambda i: (i, 0, 0)),
        out_shape=jax.ShapeDtypeStruct((b, MEM_LEN, 2 * GROUP_WIDTH), BF16),
        compiler_params=pltpu.CompilerParams(dimension_semantics=("arbitrary",)),
        name="memkv",
    )(mem, g, w)


def _front_kernel(x_ref, kv_ref, ng_ref, win_ref, cw_ref, cb_ref, wqk_ref, wv_ref, wif_ref, bif_ref,
                  mg_ref, sk_ref,
                  ym_ref, qd_ref, kd_ref, vdt_ref, zd_ref, yc_ref,
                  conv_ref, c_ref, n_ref, m_ref, q_s, k_s, v_s, g_s, om_s, zm_s, xcv_s, q32_s, k32_s, v32_s):
    tm = FRONT_ROWS
    L = MLSTM_CHUNK
    heads = range(N_HEADS)

    @pl.when(pl.program_id(1) == 0)
    def _():
        conv_ref[tm:tm + CONV_HALO, :] = jnp.zeros((CONV_HALO, GROUP_WIDTH), F32)
        c_ref[...] = jnp.zeros_like(c_ref)
        n_ref[...] = jnp.zeros_like(n_ref)
        m_ref[...] = jnp.zeros_like(m_ref)

    h = _rms(x_ref[0], ng_ref[...]).astype(BF16)

    def proj(off):
        return _dot(h, win_ref[:, off:off + GROUP_WIDTH])

    def put_vdt(val):
        v_t = val.T.astype(BF16)
        pad_row = lax.broadcasted_iota(jnp.int32, (VT_PAD, tm), 0)
        pad = jnp.where(pad_row == 0, 1.0, 0.0).astype(BF16)
        v_aug = jnp.concatenate([blk for hh in heads for blk in (v_t[_head(hh), :], pad)], axis=0)
        for j in range(tm // V_TILE):
            vdt_ref[0, j] = v_aug[:, j * V_TILE:(j + 1) * V_TILE]

    def put(ref, scale=None):
        def sink(val):
            ref[...] = (val if scale is None else val * scale).astype(ref.dtype).reshape(ref.shape)
        return sink

    side_pieces = [(OFF_OM, put(om_s)), (OFF_ZM, put(zm_s)),
                   (OFF_QD, put(qd_ref, DIFF_QK_DIM ** -0.5 * LOG2E)), (OFF_KD, put(kd_ref)),
                   (OFF_VD, put_vdt), (OFF_ZD, put(zd_ref))]
    pending = []

    def side_start():
        if side_pieces:
            off, sink = side_pieces.pop(0)
            pending.append((proj(off), sink))

    def side_finish():
        if pending:
            val, sink = pending.pop(0)
            sink(val)

    x_m = proj(OFF_XM)
    q_cb = proj(OFF_QC).astype(BF16)
    z_c = proj(OFF_ZC)

    conv_ref[0:CONV_HALO, :] = conv_ref[tm:tm + CONV_HALO, :]
    conv_ref[CONV_HALO:CONV_HALO + tm, :] = x_m
    acc = jnp.broadcast_to(cb_ref[...], (tm, GROUP_WIDTH))
    for j in range(CONV_WIDTH):
        start = CONV_HALO - (CONV_WIDTH - 1) + j
        acc = acc + cw_ref[j:j + 1, :] * conv_ref[start:start + tm, :]
    sc = [_dot_nt(q_cb[:, _head(hh)], kv_ref[0, :, _head(hh)]) * (HEAD_DIM ** -0.5 * LOG2E)
          for hh in heads]
    x_cv = _silu(acc)
    xcv_b = x_cv.astype(BF16)
    xcv_s[...] = x_cv
    xm_b = x_m.astype(BF16)

    qk_p = [_dot(xcv_b[:, _head(hh)], wqk_ref[hh]) for hh in heads]
    vv_p = [_dot(xm_b[:, _head(hh)], wv_ref[hh]) for hh in heads]

    q_f32 = jnp.concatenate([qk_p[hh][:, :HEAD_DIM] for hh in heads], axis=1)
    k_all = jnp.concatenate([qk_p[hh][:, HEAD_DIM:] for hh in heads], axis=1)
    v_f32 = jnp.concatenate(vv_p, axis=1)
    k_scaled = k_all * HEAD_DIM ** -0.5
    q_all = q_f32.astype(BF16)
    v_all = v_f32.astype(BF16)
    q_s[...], q32_s[...] = q_all, q_f32
    k_s[...], k32_s[...] = k_scaled.astype(BF16), k_scaled
    v_s[...], v32_s[...] = v_all, v_f32
    g_s[...] = (jnp.broadcast_to(bif_ref[...], (tm, HEAD_DIM))
                + _dot(q_all, wif_ref[0:GROUP_WIDTH, :])
                + _dot(k_all.astype(BF16), wif_ref[GROUP_WIDTH:2 * GROUP_WIDTH, :])
                + _dot(v_all, wif_ref[2 * GROUP_WIDTH:3 * GROUP_WIDTH, :]))

    mx = [jnp.max(sc[hh], axis=-1, keepdims=True) for hh in heads]
    pm = [jnp.exp2(sc[hh] - mx[hh]) for hh in heads]
    inv = [1.0 / jnp.sum(pm[hh], axis=-1, keepdims=True) for hh in heads]
    pc = [(pm[hh] * inv[hh]).astype(BF16) for hh in heads]
    oc = [_dot(pc[hh], kv_ref[0, :, GROUP_WIDTH + hh * HEAD_DIM:GROUP_WIDTH + (hh + 1) * HEAD_DIM]) for hh in heads]
    for hh in heads:
        yc_ref[0, :, _head(hh)] = (oc[hh] * _silu(z_c[:, _head(hh)])).astype(BF16)

    row = lax.broadcasted_iota(jnp.int32, (L, L), 0)
    col = lax.broadcasted_iota(jnp.int32, (L, L), 1)
    causal = col <= row
    tril = jnp.where(causal, 1.0, 0.0).astype(BF16)
    chunks = range(MLSTM_STEP_CHUNKS)
    probs = [(c, hh) for c in chunks for hh in heads]
    rows = lambda c: slice(c * L, (c + 1) * L)
    q_of = lambda p: q_s[rows(p[0]), _head(p[1])]
    k_of = lambda p: k_s[rows(p[0]), _head(p[1])]
    v_of = lambda p: v_s[rows(p[0]), _head(p[1])]
    f32_of = lambda ref, p: ref[rows(p[0]), _head(p[1])]

    def each(f, side=False):
        if side:
            side_start()
        out = {p: f(p) for p in probs}
        if side:
            side_finish()
        return out

    a_cols, a_rows = [], []
    for c in chunks:
        g = g_s[rows(c), :]
        bb_all = _cumsum_rows(_log_sigmoid(g), tril)
        ac = jnp.where(col < N_HEADS, g, bb_all) * LOG2E
        a_cols.append(ac)
        a_rows.append(ac.T)
    ib_col = {(c, hh): a_cols[c][:, hh:hh + 1] for c, hh in probs}
    bb_col = {(c, hh): a_cols[c][:, N_HEADS + hh:N_HEADS + hh + 1] for c, hh in probs}
    ib_row = {(c, hh): a_rows[c][hh:hh + 1, :] for c, hh in probs}
    bb_row = {(c, hh): a_rows[c][N_HEADS + hh:N_HEADS + hh + 1, :] for c, hh in probs}
    b_end = {p: bb_col[p][L - 1:L, :] for p in probs}

    qk = {p: _dot_nt(q_of(p), k_of(p)) for p in probs}

    d = each(lambda p: jnp.where(causal, (bb_col[p] - bb_row[p]) + ib_row[p], -jnp.inf), side=True)
    r = each(lambda p: jnp.max(d[p], axis=-1, keepdims=True))
    s_loc = each(lambda p: qk[p] * jnp.exp2(d[p] - r[p]), side=True)
    rs = each(lambda p: jnp.sum(s_loc[p], axis=-1, keepdims=True))
    s_bf = each(lambda p: s_loc[p].astype(BF16))
    a = each(lambda p: (b_end[p] - bb_col[p]) + ib_col[p], side=True)
    a_max = each(lambda p: jnp.max(a[p], axis=0, keepdims=True))
    w_loc = each(lambda p: jnp.broadcast_to(jnp.exp2(a[p] - a_max[p]), (L, HEAD_DIM)))
    vw = each(lambda p: (f32_of(v32_s, p) * w_loc[p]).astype(BF16), side=True)
    nu = each(lambda p: jnp.sum(f32_of(k32_s, p) * w_loc[p], axis=0, keepdims=True))

    sv = {p: _dot(s_bf[p], v_of(p)) for p in probs}
    u = {p: lax.dot_general(vw[p], k_of(p), TN_DIMS, preferred_element_type=F32) for p in probs}

    m_in, c_in, n_in = {}, {}, {}
    for hh in heads:
        m_cur = m_ref[hh][:, 0:1]
        c_cur = c_ref[hh]
        n_cur = n_ref[hh]
        for c in chunks:
            p = (c, hh)
            m_in[p], c_in[p], n_in[p] = m_cur, c_cur.astype(BF16), n_cur
            m_new = jnp.maximum(b_end[p] + m_cur, a_max[p])
            decay = jnp.exp2(b_end[p] + m_cur - m_new)
            e_upd = jnp.exp2(a_max[p] - m_new)
            c_cur = decay * c_cur + e_upd * u[p]
            n_cur = decay * n_cur + e_upd * nu[p]
            m_cur = m_new
        c_ref[hh] = c_cur
        n_ref[hh] = n_cur
        m_ref[hh] = jnp.broadcast_to(m_cur, (1, HEAD_DIM))

    q_c = {p: _dot_nt(q_of(p), c_in[p]) for p in probs}

    tile_of = f32_of
    g_col = each(lambda p: bb_col[p] + m_in[p])
    m_t = each(lambda p: jnp.maximum(g_col[p], r[p]))
    e_loc = each(lambda p: jnp.exp2(r[p] - m_t[p]))
    inter = each(lambda p: jnp.exp2(g_col[p] - m_t[p]))
    q_n = each(lambda p: jnp.sum(f32_of(q32_s, p) * n_in[p], axis=-1, keepdims=True))
    den = each(lambda p: e_loc[p] * rs[p] + inter[p] * q_n[p])
    scale = each(lambda p: 1.0 / jnp.maximum(jnp.abs(den[p]), jnp.exp2(-m_t[p])))
    w_intra = each(lambda p: e_loc[p] * scale[p])
    w_inter = each(lambda p: inter[p] * scale[p])
    hg = each(lambda p: _sigmoid(tile_of(om_s, p))
              * (w_intra[p] * sv[p] + w_inter[p] * q_c[p]), side=True)
    mu = each(lambda p: jnp.mean(hg[p], axis=-1, keepdims=True))
    cen = each(lambda p: hg[p] - mu[p], side=True)
    var = each(lambda p: jnp.mean(jnp.square(cen[p]), axis=-1, keepdims=True))
    assert not side_pieces and not pending
    for p in probs:
        sl = _head(p[1])
        y = cen[p] * lax.rsqrt(var[p] + NORM_EPS) * mg_ref[:, sl]
        y = (y + sk_ref[:, sl] * tile_of(xcv_s, p)) * _silu(tile_of(zm_s, p))
        ym_ref[0, rows(p[0]), sl] = y.astype(BF16)


def _front_call(x, kv, norm_g, w_in, conv_w, conv_b, wqk, wv, w_if, b_if, mnorm_g, skip_m):
    b, s, _ = x.shape
    tm = FRONT_ROWS
    ns = s // tm
    tok = lambda w: jax.ShapeDtypeStruct((b, s, w), BF16)
    tok_spec = lambda w: pl.BlockSpec((1, tm, w), lambda i, j: (i, j, 0))
    const = lambda shape: pl.BlockSpec(shape, lambda i, j: (0,) * len(shape))
    vdt_tiles = tm // V_TILE
    out_shape = (
        tok(GROUP_WIDTH),
        tok(GROUP_WIDTH), tok(GROUP_WIDTH),
        jax.ShapeDtypeStruct((b, ns * vdt_tiles, N_HEADS * VT_ROWS, V_TILE), BF16),
        tok(GROUP_WIDTH),
        tok(GROUP_WIDTH),
    )
    out_specs = (
        tok_spec(GROUP_WIDTH), tok_spec(GROUP_WIDTH), tok_spec(GROUP_WIDTH),
        pl.BlockSpec((1, vdt_tiles, N_HEADS * VT_ROWS, V_TILE), lambda i, j: (i, j, 0, 0)),
        tok_spec(GROUP_WIDTH), tok_spec(GROUP_WIDTH),
    )
    in_specs = [
        pl.BlockSpec((1, tm, D_MODEL), lambda i, j: (i, j, 0)),
        pl.BlockSpec((1, MEM_LEN, 2 * GROUP_WIDTH), lambda i, j: (i, 0, 0)),
        const((1, D_MODEL)),
        const((D_MODEL, IN_WIDTH)),
        const((CONV_WIDTH, GROUP_WIDTH)),
        const((1, GROUP_WIDTH)),
        const((N_HEADS, HEAD_DIM, 2 * HEAD_DIM)),
        const((N_HEADS, HEAD_DIM, HEAD_DIM)),
        const((3 * GROUP_WIDTH, HEAD_DIM)),
        const((1, HEAD_DIM)),
        const((1, GROUP_WIDTH)),
        const((1, GROUP_WIDTH)),
    ]
    tile_bf16 = pltpu.VMEM((tm, GROUP_WIDTH), BF16)
    tile_f32 = pltpu.VMEM((tm, GROUP_WIDTH), F32)
    return pl.pallas_call(
        _front_kernel,
        grid=(b, ns),
        in_specs=in_specs,
        out_specs=out_specs,
        out_shape=out_shape,
        scratch_shapes=[pltpu.VMEM((tm + CONV_HALO, GROUP_WIDTH), F32),
                        pltpu.VMEM((N_HEADS, HEAD_DIM, HEAD_DIM), F32),
                        pltpu.VMEM((N_HEADS, 1, HEAD_DIM), F32),
                        pltpu.VMEM((N_HEADS, 1, HEAD_DIM), F32),
                        tile_bf16, tile_bf16, tile_bf16,
                        pltpu.VMEM((tm, HEAD_DIM), F32),
                        tile_f32, tile_f32, tile_f32,
                        tile_f32, tile_f32, tile_f32],
        compiler_params=pltpu.CompilerParams(
            dimension_semantics=("arbitrary", "arbitrary"),
            vmem_limit_bytes=VMEM_LIMIT_BYTES),
        name="front",
    )(x, kv, norm_g, w_in, conv_w, conv_b, wqk, wv, w_if, b_if, mnorm_g, skip_m)


def _diff_kernel(q_ref, qn_ref, k_ref, vt_ref, zd_ref, g_ref, lq1_ref, lk1_ref, lq2_ref, lk2_ref,
                 x_ref, ym_ref, yc_ref, wo_ref, fg_ref, o_ref,
                 qs_ref, m_ref, l_ref, acc_ref, s0_ref, s1_ref, mx0_ref, mx1_ref, yd_ref, part_ref, qsn_ref):
    tq, tk, tv = DIFF_Q_ROWS, DIFF_KV_ROWS, V_TILE
    i = pl.program_id(1)
    feat = lax.broadcasted_iota(jnp.int32, (HEAD_DIM, tq), 0)
    for hh in range(N_HEADS):
        q_t = q_ref[0, :, _head(hh)].astype(F32).T
        qs_ref[hh, :, 0:tq] = jnp.where(feat < DIFF_QK_DIM, q_t, 0.0).astype(BF16)
        qs_ref[hh, :, tq:2 * tq] = jnp.where(feat >= DIFF_QK_DIM, q_t, 0.0).astype(BF16)
    m_ref[...] = jnp.full_like(m_ref, -jnp.inf)
    l_ref[...] = jnp.zeros_like(l_ref)
    acc_ref[...] = jnp.zeros_like(acc_ref)

    kc = DIFF_KEY_CHUNK
    sub = HEAD_DIM
    tri = (lax.broadcasted_iota(jnp.int32, (sub, sub), 0) <= lax.broadcasted_iota(jnp.int32, (sub, sub), 1))

    def mask_diagonal(s_c, c):
        hidden = jnp.full((sub, sub), -jnp.inf, F32)
        hidden_max = jnp.full((SUBLANES, sub), -jnp.inf, F32)
        row_blocks, maxes = [], None
        for r in range(kc // sub):
            kb = c * (kc // sub) + r
            blocks, row_max = [], []
            for b8 in range(2 * tq // sub):
                b = b8 % (tq // sub)
                if kb > b:
                    blocks.append(hidden)
                    row_max.append(hidden_max)
                    continue
                blk = s_c[r * sub:(r + 1) * sub, b8 * sub:(b8 + 1) * sub]
                if kb == b:
                    blk = jnp.where(tri, blk, -jnp.inf)
                blocks.append(blk)
                row_max.append(_fold_rows(blk, jnp.max))
            row_blocks.append(jnp.concatenate(blocks, axis=1))
            row_max = jnp.concatenate(row_max, axis=1)
            maxes = row_max if maxes is None else jnp.maximum(maxes, row_max)
        return jnp.concatenate(row_blocks, axis=0), maxes

    group_a = (tuple(range(0, N_HEADS // 2)), s0_ref, mx0_ref)
    group_b = (tuple(range(N_HEADS // 2, N_HEADS)), s1_ref, mx1_ref)

    def step(consume=None, produce=None):
        n_slots = N_HEADS // 2
        for slot in range(n_slots):
            if consume is not None:
                t_c, (heads_c, sc_ref, mxc_ref) = consume
                h_c = heads_c[slot]
                m_old = m_ref[h_c]
                m_new = jnp.maximum(m_old, mxc_ref[slot])
                alpha = jnp.exp2(m_old - m_new)
                pv, pb_prev = None, None
            if produce is not None:
                t_p, (heads_p, sp_ref, mxp_ref), diagonal = produce[:3]
                q_src = produce[3] if len(produce) > 3 else None
                h_p = heads_p[slot]
                mx = None
            for c in range(tk // kc):
                chunk = slice(c * kc, (c + 1) * kc)
                if produce is not None:
                    rows = pl.ds(pl.multiple_of(t_p * tk + c * kc, kc), kc)
                    q_mat = qs_ref[h_p] if q_src is None else q_src[slot]
                    s_c = _dot(k_ref[0, rows, _head(h_p)], q_mat)
                if consume is not None:
                    pb_c = jnp.exp2(sc_ref[slot, chunk, :] - m_new).astype(BF16)
                    pb_prev = pb_c if pb_prev is None else jnp.concatenate([pb_prev, pb_c], axis=0)
                    if c == tk // kc - 1:
                        v_rows = slice(h_c * VT_ROWS, (h_c + 1) * VT_ROWS)
                        vt = jnp.concatenate([vt_ref[0, t_c * (tk // tv) + j, v_rows, :]
                                              for j in range(tk // tv)], axis=1)
                        pv = _dot(vt, pb_prev)
                if produce is not None:
                    if diagonal:
                        s_c, c_max = mask_diagonal(s_c, c)
                    else:
                        c_max = _fold_rows(s_c, jnp.max)
                    sp_ref[slot, chunk, :] = s_c
                    mx = c_max if mx is None else jnp.maximum(mx, c_max)
            if consume is not None:
                l_ref[h_c] = alpha * l_ref[h_c] + pv[HEAD_DIM:HEAD_DIM + 1, :]
                acc_ref[h_c] = alpha * acc_ref[h_c] + pv[:HEAD_DIM, :]
                m_ref[h_c] = m_new
            if produce is not None:
                mxp_ref[slot] = jnp.max(mx, axis=0, keepdims=True)

    def tile(t, next_diagonal):
        step(consume=(t, group_a), produce=(t, group_b, False))
        step(consume=(t, group_b), produce=(t + 1, group_a, next_diagonal))

    def body(t, carry):
        tile(t, False)
        return carry

    @pl.when(i == 0)
    def _():
        step(produce=(0, group_a, True))

    @pl.when(i > 0)
    def _():
        lax.fori_loop(0, i - 1, body, 0)
        tile(i - 1, True)

    lam = (jnp.exp(jnp.sum(lq1_ref[...] * lk1_ref[...], axis=-1, keepdims=True))
           - jnp.exp(jnp.sum(lq2_ref[...] * lk2_ref[...], axis=-1, keepdims=True)) + LAM_INIT)

    def finish(heads):
        for hh in heads:
            sl = _head(hh)
            acc = acc_ref[hh]
            inv = 1.0 / l_ref[hh]
            o_t = acc[:, :tq] * inv[:, :tq] - lam * (acc[:, tq:] * inv[:, tq:])
            o = _rms(o_t.T, g_ref[:, sl]) * (1.0 - LAM_INIT)
            yd_ref[:, sl] = (o * _silu(zd_ref[0, :, sl].astype(F32))).astype(BF16)

    def project_rows(c, src_ref, w_rows, first):
        r_c = slice(c * HEAD_DIM, (c + 1) * HEAD_DIM)
        y = _dot(src_ref[0, r_c, :], wo_ref[w_rows, :])
        part_ref[r_c, :] = (x_ref[0, r_c, :] + y) if first else (part_ref[r_c, :] + y)

    def project_heads(heads):
        cols = slice(heads[0] * HEAD_DIM, (heads[-1] + 1) * HEAD_DIM)
        w_rows = slice(GROUP_WIDTH + cols.start, GROUP_WIDTH + cols.stop)
        return _dot(yd_ref[:, cols], wo_ref[w_rows, :])

    step(consume=(i, group_a), produce=(i, group_b, True))
    finish(group_a[0])

    last = pl.num_programs(1) - 1

    @pl.when(i < last)
    def _():
        for slot, hh in enumerate(group_a[0]):
            q_t = qn_ref[0, :, _head(hh)].astype(F32).T
            qsn_ref[slot, :, 0:tq] = jnp.where(feat < DIFF_QK_DIM, q_t, 0.0).astype(BF16)
            qsn_ref[slot, :, tq:2 * tq] = jnp.where(feat >= DIFF_QK_DIM, q_t, 0.0).astype(BF16)
        step(consume=(i, group_b), produce=(0, group_a, False, qsn_ref))

    @pl.when(i == last)
    def _():
        step(consume=(i, group_b))

    for c in range(tq // HEAD_DIM):
        project_rows(c, ym_ref, slice(0, GROUP_WIDTH), True)
    for c in range(tq // HEAD_DIM):
        project_rows(c, yc_ref, slice(2 * GROUP_WIDTH, 3 * GROUP_WIDTH), False)
    y_a = project_heads(group_a[0])
    finish(group_b[0])
    o_ref[0] = _rms(part_ref[...] + y_a + project_heads(group_b[0]), fg_ref[...])


def _diff_out_call(qd, kd, vdt, zd, dnorm_g, lq1, lk1, lq2, lk2, x, ym, yc, w_out, final_g):
    n_q = qd.shape[1] // DIFF_Q_ROWS
    b, s, _ = qd.shape
    tq = DIFF_Q_ROWS
    nkv = vdt.shape[1]
    const = lambda shape: pl.BlockSpec(shape, lambda i, j: (0,) * len(shape))
    tok_spec = lambda w: pl.BlockSpec((1, tq, w), lambda i, j: (i, j, 0))
    return pl.pallas_call(
        _diff_kernel,
        grid=(b, s // tq),
        in_specs=[
            tok_spec(GROUP_WIDTH),
            pl.BlockSpec((1, tq, GROUP_WIDTH), lambda i, j: (i, jnp.minimum(j + 1, n_q - 1), 0)),
            pl.BlockSpec((1, s, GROUP_WIDTH), lambda i, j: (i, 0, 0)),
            pl.BlockSpec((1, nkv, N_HEADS * VT_ROWS, V_TILE), lambda i, j: (i, 0, 0, 0)),
            tok_spec(GROUP_WIDTH),
            const((1, GROUP_WIDTH)),
            const((1, DIFF_QK_DIM)), const((1, DIFF_QK_DIM)), const((1, DIFF_QK_DIM)), const((1, DIFF_QK_DIM)),
            tok_spec(D_MODEL), tok_spec(GROUP_WIDTH), tok_spec(GROUP_WIDTH),
            const((3 * GROUP_WIDTH, D_MODEL)), const((1, D_MODEL)),
        ],
        out_specs=tok_spec(D_MODEL),
        out_shape=jax.ShapeDtypeStruct((b, s, D_MODEL), F32),
        scratch_shapes=[pltpu.VMEM((N_HEADS, HEAD_DIM, 2 * tq), BF16),
                        pltpu.VMEM((N_HEADS, 1, 2 * tq), F32),
                        pltpu.VMEM((N_HEADS, 1, 2 * tq), F32),
                        pltpu.VMEM((N_HEADS, HEAD_DIM, 2 * tq), F32),
                        pltpu.VMEM((N_HEADS // 2, DIFF_KV_ROWS, 2 * tq), F32),
                        pltpu.VMEM((N_HEADS // 2, DIFF_KV_ROWS, 2 * tq), F32),
                        pltpu.VMEM((N_HEADS // 2, 1, 2 * tq), F32),
                        pltpu.VMEM((N_HEADS // 2, 1, 2 * tq), F32),
                        pltpu.VMEM((tq, GROUP_WIDTH), BF16),
                        pltpu.VMEM((tq, D_MODEL), F32),
                        pltpu.VMEM((N_HEADS // 2, HEAD_DIM, 2 * tq), BF16)],
        compiler_params=pltpu.CompilerParams(
            dimension_semantics=("arbitrary", "arbitrary"),
            vmem_limit_bytes=VMEM_LIMIT_BYTES),
        name="diffattn_out",
    )(qd, qd, kd, vdt, zd, dnorm_g, lq1, lk1, lq2, lk2, x, ym, yc, w_out, final_g)


def kernel(x, mem, norm_g, w_in, conv_w, conv_b, wq_m, wk_m, wv_m, w_if, b_if, mnorm_g, skip_m,
           lam_q1, lam_k1, lam_q2, lam_k2, dnorm_g, mem_norm_g, w_mem_kv, w_out, final_g):
    b, s, d = x.shape
    assert (d, s % DIFF_KV_ROWS, DIFF_KV_ROWS % V_TILE, FRONT_ROWS % V_TILE) == (D_MODEL, 0, 0, 0)
    assert (s % FRONT_ROWS, DIFF_KV_ROWS % DIFF_KEY_CHUNK) == (0, 0)
    assert DIFF_Q_ROWS == DIFF_KV_ROWS and DIFF_KEY_CHUNK % HEAD_DIM == 0
    assert norm_g.shape[0] == 1, "single-layer kernel"
    l = 0
    w_if_pad = jnp.pad(w_if[l], ((0, 0), (0, HEAD_DIM - w_if.shape[-1]))).astype(BF16)
    b_if_pad = jnp.pad(b_if[l], (0, HEAD_DIM - b_if.shape[-1]))[None, :]

    kv = _memkv_call(mem, mem_norm_g[l][None, :], w_mem_kv[l].astype(BF16))
    ym, qd, kd, vdt, zd, yc = _front_call(
        x, kv, norm_g[l][None, :], w_in[l].astype(BF16), conv_w[l], conv_b[l][None, :],
        jnp.concatenate([wq_m[l], wk_m[l]], axis=-1).astype(BF16), wv_m[l].astype(BF16), w_if_pad, b_if_pad,
        mnorm_g[l][None, :], skip_m[l][None, :])
    return _diff_out_call(qd, kd, vdt, zd, dnorm_g[l][None, :],
                          lam_q1[l][None, :], lam_k1[l][None, :], lam_q2[l][None, :], lam_k2[l][None, :],
                          x, ym, yc, w_out[l].astype(BF16), final_g[None, :])
```

```python
import math

import jax
import jax.numpy as jnp
from jax import lax
from jax.experimental import pallas as pl
from jax.experimental.pallas import tpu as pltpu

F32 = jnp.float32
BF16 = jnp.bfloat16

D_MODEL = 1024
N_HEADS = 4
HEAD_DIM = 128
SUBLANES = 8
V7X_VMEM_BYTES = 64 * 1024 * 1024
VMEM_LIMIT_BYTES = V7X_VMEM_BYTES - 6 * 1024 * 1024
GROUP_WIDTH = N_HEADS * HEAD_DIM
DIFF_QK_DIM = 64
CONV_WIDTH = 4
MLSTM_CHUNK = 128
MLSTM_STEP_CHUNKS = 4
MEM_LEN = 256
IN_WIDTH = 9 * GROUP_WIDTH
NORM_EPS = 1e-6
LAM_INIT = 0.8 - 0.6 * math.exp(-0.3 * 0)
LOG2E = math.log2(math.e)

OFF_XM, OFF_OM, OFF_ZM, OFF_QD, OFF_KD, OFF_VD, OFF_ZD, OFF_QC, OFF_ZC = (
    i * GROUP_WIDTH for i in range(9))

FRONT_ROWS = MLSTM_STEP_CHUNKS * MLSTM_CHUNK
V_TILE = 256
VT_PAD = 16
VT_ROWS = HEAD_DIM + VT_PAD
DIFF_Q_ROWS = 512
DIFF_KV_ROWS = 512
DIFF_KEY_CHUNK = 512
CONV_HALO = 8

NT_DIMS = (((1,), (1,)), ((), ()))
TN_DIMS = (((0,), (0,)), ((), ()))


def _dot(a, b):
    return jnp.dot(a, b, preferred_element_type=F32)


def _dot_nt(a, b):
    return lax.dot_general(a, b, NT_DIMS, preferred_element_type=F32)


def _sigmoid(x):
    return 0.5 * jnp.tanh(0.5 * x) + 0.5


def _silu(x):
    half = 0.5 * x
    return half * (jnp.tanh(half) + 1.0)


def _rms(x, g):
    return x * lax.rsqrt(jnp.mean(x * x, axis=-1, keepdims=True) + NORM_EPS) * g


def _head(h):
    return slice(h * HEAD_DIM, (h + 1) * HEAD_DIM)


def _fold_rows(x, op):
    r, n = x.shape
    return op(x.reshape(r // SUBLANES, SUBLANES, n), axis=0)


def _log_sigmoid(x):
    return jnp.minimum(x, 0.0) - jnp.log1p(jnp.exp(-jnp.abs(x)))


def _cumsum_rows(f, tril):
    hi = f.astype(BF16)
    r1 = f - hi.astype(F32)
    mid = r1.astype(BF16)
    lo = (r1 - mid.astype(F32)).astype(BF16)
    return _dot(tril, hi) + _dot(tril, mid) + _dot(tril, lo)


def _memkv_kernel(mem_ref, g_ref, w_ref, kv_ref):
    y = _rms(mem_ref[0], g_ref[...])
    kv_ref[0] = _dot(y.astype(BF16), w_ref[...]).astype(BF16)


def _memkv_call(mem, g, w):
    b = mem.shape[0]
    return pl.pallas_call(
        _memkv_kernel,
        grid=(b,),
        in_specs=[
            pl.BlockSpec((1, MEM_LEN, D_MODEL), lambda i: (i, 0, 0)),
            pl.BlockSpec((1, D_MODEL), lambda i: (0, 0)),
            pl.BlockSpec((D_MODEL, 2 * GROUP_WIDTH), lambda i: (0, 0)),
        ],
        out_specs=pl.BlockSpec((1, MEM_LEN, 2 * GROUP_WIDTH), lambda i: (i, 0, 0)),
        out_shape=jax.ShapeDtypeStruct((b, MEM_LEN, 2 * GROUP_WIDTH), BF16),
        compiler_params=pltpu.CompilerParams(dimension_semantics=("arbitrary",)),
        name="memkv",
    )(mem, g, w)


def _front_kernel(x_ref, kv_ref, ng_ref, win_ref, cw_ref, cb_ref, wqk_ref, wv_ref, wif_ref, bif_ref,
                  mg_ref, sk_ref,
                  ym_ref, qd_ref, kd_ref, vdt_ref, zd_ref, yc_ref,
                  conv_ref, c_ref, n_ref, m_ref, q_s, k_s, v_s, g_s, om_s, zm_s, xcv_s):
    tm = FRONT_ROWS
    L = MLSTM_CHUNK
    heads = range(N_HEADS)

    @pl.when(pl.program_id(1) == 0)
    def _():
        conv_ref[tm:tm + CONV_HALO, :] = jnp.zeros((CONV_HALO, GROUP_WIDTH), F32)
        c_ref[...] = jnp.zeros_like(c_ref)
        n_ref[...] = jnp.zeros_like(n_ref)
        m_ref[...] = jnp.zeros_like(m_ref)

    h = _rms(x_ref[0], ng_ref[...]).astype(BF16)

    def proj(off):
        return _dot(h, win_ref[:, off:off + GROUP_WIDTH])

    def put_vdt(val):
        v_t = val.T.astype(BF16)
        pad_row = lax.broadcasted_iota(jnp.int32, (VT_PAD, tm), 0)
        pad = jnp.where(pad_row == 0, 1.0, 0.0).astype(BF16)
        v_aug = jnp.concatenate([blk for hh in heads for blk in (v_t[_head(hh), :], pad)], axis=0)
        for j in range(tm // V_TILE):
            vdt_ref[0, j] = v_aug[:, j * V_TILE:(j + 1) * V_TILE]

    def put(ref, scale=None):
        def sink(val):
            ref[...] = (val if scale is None else val * scale).astype(BF16).reshape(ref.shape)
        return sink

    side_pieces = [(OFF_OM, put(om_s)), (OFF_ZM, put(zm_s)),
                   (OFF_QD, put(qd_ref, DIFF_QK_DIM ** -0.5 * LOG2E)), (OFF_KD, put(kd_ref)),
                   (OFF_VD, put_vdt), (OFF_ZD, put(zd_ref))]
    pending = []

    def side_start():
        if side_pieces:
            off, sink = side_pieces.pop(0)
            pending.append((proj(off), sink))

    def side_finish():
        if pending:
            val, sink = pending.pop(0)
            sink(val)

    x_m = proj(OFF_XM)
    q_cb = proj(OFF_QC).astype(BF16)
    z_c = proj(OFF_ZC)

    conv_ref[0:CONV_HALO, :] = conv_ref[tm:tm + CONV_HALO, :]
    conv_ref[CONV_HALO:CONV_HALO + tm, :] = x_m
    acc = jnp.broadcast_to(cb_ref[...], (tm, GROUP_WIDTH))
    for j in range(CONV_WIDTH):
        start = CONV_HALO - (CONV_WIDTH - 1) + j
        acc = acc + cw_ref[j:j + 1, :] * conv_ref[start:start + tm, :]
    sc = [_dot_nt(q_cb[:, _head(hh)], kv_ref[0, :, _head(hh)]) * (HEAD_DIM ** -0.5 * LOG2E)
          for hh in heads]
    x_cv = _silu(acc)
    xcv_b = x_cv.astype(BF16)
    xcv_s[...] = xcv_b
    xm_b = x_m.astype(BF16)

    qk_p = [_dot(xcv_b[:, _head(hh)], wqk_ref[hh]) for hh in heads]
    vv_p = [_dot(xm_b[:, _head(hh)], wv_ref[hh]) for hh in heads]

    q_all = jnp.concatenate([qk_p[hh][:, :HEAD_DIM] for hh in heads], axis=1).astype(BF16)
    k_all = jnp.concatenate([qk_p[hh][:, HEAD_DIM:] for hh in heads], axis=1)
    v_all = jnp.concatenate(vv_p, axis=1).astype(BF16)
    q_s[...] = q_all
    k_s[...] = (k_all * HEAD_DIM ** -0.5).astype(BF16)
    v_s[...] = v_all
    g_s[...] = (jnp.broadcast_to(bif_ref[...], (tm, HEAD_DIM))
                + _dot(q_all, wif_ref[0:GROUP_WIDTH, :])
                + _dot(k_all.astype(BF16), wif_ref[GROUP_WIDTH:2 * GROUP_WIDTH, :])
                + _dot(v_all, wif_ref[2 * GROUP_WIDTH:3 * GROUP_WIDTH, :]))

    mx = [jnp.max(sc[hh], axis=-1, keepdims=True) for hh in heads]
    pm = [jnp.exp2(sc[hh] - mx[hh]) for hh in heads]
    inv = [1.0 / jnp.sum(pm[hh], axis=-1, keepdims=True) for hh in heads]
    oc = [_dot(pm[hh].astype(BF16), kv_ref[0, :, GROUP_WIDTH + hh * HEAD_DIM:GROUP_WIDTH + (hh + 1) * HEAD_DIM])
          for hh in heads]
    for hh in heads:
        yc_ref[0, :, _head(hh)] = (oc[hh] * inv[hh] * _silu(z_c[:, _head(hh)])).astype(BF16)

    row = lax.broadcasted_iota(jnp.int32, (L, L), 0)
    col = lax.broadcasted_iota(jnp.int32, (L, L), 1)
    causal = col <= row
    tril = jnp.where(causal, 1.0, 0.0).astype(BF16)
    chunks = range(MLSTM_STEP_CHUNKS)
    probs = [(c, hh) for c in chunks for hh in heads]
    rows = lambda c: slice(c * L, (c + 1) * L)
    q_of = lambda p: q_s[rows(p[0]), _head(p[1])]
    k_of = lambda p: k_s[rows(p[0]), _head(p[1])]
    v_of = lambda p: v_s[rows(p[0]), _head(p[1])]

    def each(f, side=False):
        if side:
            side_start()
        out = {p: f(p) for p in probs}
        if side:
            side_finish()
        return out

    a_cols, a_rows = [], []
    for c in chunks:
        g = g_s[rows(c), :]
        bb_all = _cumsum_rows(_log_sigmoid(g), tril)
        ac = jnp.where(col < N_HEADS, g, bb_all) * LOG2E
        a_cols.append(ac)
        a_rows.append(ac.T)
    ib_col = {(c, hh): a_cols[c][:, hh:hh + 1] for c, hh in probs}
    bb_col = {(c, hh): a_cols[c][:, N_HEADS + hh:N_HEADS + hh + 1] for c, hh in probs}
    ib_row = {(c, hh): a_rows[c][hh:hh + 1, :] for c, hh in probs}
    bb_row = {(c, hh): a_rows[c][N_HEADS + hh:N_HEADS + hh + 1, :] for c, hh in probs}
    b_end = {p: bb_col[p][L - 1:L, :] for p in probs}

    qk = {p: _dot_nt(q_of(p), k_of(p)) for p in probs}

    d = each(lambda p: jnp.where(causal, (bb_col[p] - bb_row[p]) + ib_row[p], -jnp.inf), side=True)
    r = each(lambda p: jnp.max(d[p], axis=-1, keepdims=True))
    s_loc = each(lambda p: qk[p] * jnp.exp2(d[p] - r[p]), side=True)
    rs = each(lambda p: jnp.sum(s_loc[p], axis=-1, keepdims=True))
    s_bf = each(lambda p: s_loc[p].astype(BF16))
    a = each(lambda p: (b_end[p] - bb_col[p]) + ib_col[p], side=True)
    a_max = each(lambda p: jnp.max(a[p], axis=0, keepdims=True))
    w_loc = each(lambda p: jnp.broadcast_to(jnp.exp2(a[p] - a_max[p]), (L, HEAD_DIM)))
    vw = each(lambda p: (v_of(p).astype(F32) * w_loc[p]).astype(BF16), side=True)
    nu = each(lambda p: jnp.sum(k_of(p).astype(F32) * w_loc[p], axis=0, keepdims=True))

    sv = {p: _dot(s_bf[p], v_of(p)) for p in probs}
    u = {p: lax.dot_general(vw[p], k_of(p), TN_DIMS, preferred_element_type=F32) for p in probs}

    m_in, c_in, n_in = {}, {}, {}
    for hh in heads:
        m_cur = m_ref[hh][:, 0:1]
        c_cur = c_ref[hh]
        n_cur = n_ref[hh]
        for c in chunks:
            p = (c, hh)
            m_in[p], c_in[p], n_in[p] = m_cur, c_cur.astype(BF16), n_cur
            m_new = jnp.maximum(b_end[p] + m_cur, a_max[p])
            decay = jnp.exp2(b_end[p] + m_cur - m_new)
            e_upd = jnp.exp2(a_max[p] - m_new)
            c_cur = decay * c_cur + e_upd * u[p]
            n_cur = decay * n_cur + e_upd * nu[p]
            m_cur = m_new
        c_ref[hh] = c_cur
        n_ref[hh] = n_cur
        m_ref[hh] = jnp.broadcast_to(m_cur, (1, HEAD_DIM))

    q_c = {p: _dot_nt(q_of(p), c_in[p]) for p in probs}

    tile_of = lambda ref, p: ref[rows(p[0]), _head(p[1])].astype(F32)
    g_col = each(lambda p: bb_col[p] + m_in[p])
    m_t = each(lambda p: jnp.maximum(g_col[p], r[p]))
    e_loc = each(lambda p: jnp.exp2(r[p] - m_t[p]))
    inter = each(lambda p: jnp.exp2(g_col[p] - m_t[p]))
    q_n = each(lambda p: jnp.sum(q_of(p).astype(F32) * n_in[p], axis=-1, keepdims=True))
    den = each(lambda p: e_loc[p] * rs[p] + inter[p] * q_n[p])
    scale = each(lambda p: 1.0 / jnp.maximum(jnp.abs(den[p]), jnp.exp2(-m_t[p])))
    w_intra = each(lambda p: e_loc[p] * scale[p])
    w_inter = each(lambda p: inter[p] * scale[p])
    hg = each(lambda p: _sigmoid(tile_of(om_s, p))
              * (w_intra[p] * sv[p] + w_inter[p] * q_c[p]), side=True)
    mu = each(lambda p: jnp.mean(hg[p], axis=-1, keepdims=True))
    cen = each(lambda p: hg[p] - mu[p], side=True)
    var = each(lambda p: jnp.mean(jnp.square(cen[p]), axis=-1, keepdims=True))
    assert not side_pieces and not pending
    for p in probs:
        sl = _head(p[1])
        y = cen[p] * lax.rsqrt(var[p] + NORM_EPS) * mg_ref[:, sl]
        y = (y + sk_ref[:, sl] * tile_of(xcv_s, p)) * _silu(tile_of(zm_s, p))
        ym_ref[0, rows(p[0]), sl] = y.astype(BF16)


def _front_call(x, kv, norm_g, w_in, conv_w, conv_b, wqk, wv, w_if, b_if, mnorm_g, skip_m):
    b, s, _ = x.shape
    tm = FRONT_ROWS
    ns = s // tm
    tok = lambda w: jax.ShapeDtypeStruct((b, s, w), BF16)
    tok_spec = lambda w: pl.BlockSpec((1, tm, w), lambda i, j: (i, j, 0))
    const = lambda shape: pl.BlockSpec(shape, lambda i, j: (0,) * len(shape))
    vdt_tiles = tm // V_TILE
    out_shape = (
        tok(GROUP_WIDTH),
        tok(GROUP_WIDTH), tok(GROUP_WIDTH),
        jax.ShapeDtypeStruct((b, ns * vdt_tiles, N_HEADS * VT_ROWS, V_TILE), BF16),
        tok(GROUP_WIDTH),
        tok(GROUP_WIDTH),
    )
    out_specs = (
        tok_spec(GROUP_WIDTH), tok_spec(GROUP_WIDTH), tok_spec(GROUP_WIDTH),
        pl.BlockSpec((1, vdt_tiles, N_HEADS * VT_ROWS, V_TILE), lambda i, j: (i, j, 0, 0)),
        tok_spec(GROUP_WIDTH), tok_spec(GROUP_WIDTH),
    )
    in_specs = [
        pl.BlockSpec((1, tm, D_MODEL), lambda i, j: (i, j, 0)),
        pl.BlockSpec((1, MEM_LEN, 2 * GROUP_WIDTH), lambda i, j: (i, 0, 0)),
        const((1, D_MODEL)),
        const((D_MODEL, IN_WIDTH)),
        const((CONV_WIDTH, GROUP_WIDTH)),
        const((1, GROUP_WIDTH)),
        const((N_HEADS, HEAD_DIM, 2 * HEAD_DIM)),
        const((N_HEADS, HEAD_DIM, HEAD_DIM)),
        const((3 * GROUP_WIDTH, HEAD_DIM)),
        const((1, HEAD_DIM)),
        const((1, GROUP_WIDTH)),
        const((1, GROUP_WIDTH)),
    ]
    tile_bf16 = pltpu.VMEM((tm, GROUP_WIDTH), BF16)
    return pl.pallas_call(
        _front_kernel,
        grid=(b, ns),
        in_specs=in_specs,
        out_specs=out_specs,
        out_shape=out_shape,
        scratch_shapes=[pltpu.VMEM((tm + CONV_HALO, GROUP_WIDTH), F32),
                        pltpu.VMEM((N_HEADS, HEAD_DIM, HEAD_DIM), F32),
                        pltpu.VMEM((N_HEADS, 1, HEAD_DIM), F32),
                        pltpu.VMEM((N_HEADS, 1, HEAD_DIM), F32),
                        tile_bf16, tile_bf16, tile_bf16,
                        pltpu.VMEM((tm, HEAD_DIM), F32),
                        tile_bf16, tile_bf16, tile_bf16],
        compiler_params=pltpu.CompilerParams(
            dimension_semantics=("arbitrary", "arbitrary"),
            vmem_limit_bytes=VMEM_LIMIT_BYTES),
        name="front",
    )(x, kv, norm_g, w_in, conv_w, conv_b, wqk, wv, w_if, b_if, mnorm_g, skip_m)


def _diff_kernel(q_ref, qn_ref, k_ref, vt_ref, zd_ref, g_ref, lq1_ref, lk1_ref, lq2_ref, lk2_ref,
                 x_ref, ym_ref, yc_ref, wo_ref, fg_ref, o_ref,
                 qs_ref, m_ref, l_ref, acc_ref, s0_ref, s1_ref, mx0_ref, mx1_ref, yd_ref, part_ref, qsn_ref):
    tq, tk, tv = DIFF_Q_ROWS, DIFF_KV_ROWS, V_TILE
    i = pl.program_id(1)
    feat = lax.broadcasted_iota(jnp.int32, (HEAD_DIM, tq), 0)
    for hh in range(N_HEADS):
        q_t = q_ref[0, :, _head(hh)].astype(F32).T
        qs_ref[hh, :, 0:tq] = jnp.where(feat < DIFF_QK_DIM, q_t, 0.0).astype(BF16)
        qs_ref[hh, :, tq:2 * tq] = jnp.where(feat >= DIFF_QK_DIM, q_t, 0.0).astype(BF16)
    m_ref[...] = jnp.full_like(m_ref, -jnp.inf)
    l_ref[...] = jnp.zeros_like(l_ref)
    acc_ref[...] = jnp.zeros_like(acc_ref)

    kc = DIFF_KEY_CHUNK
    sub = HEAD_DIM
    tri = (lax.broadcasted_iota(jnp.int32, (sub, sub), 0) <= lax.broadcasted_iota(jnp.int32, (sub, sub), 1))

    def mask_diagonal(s_c, c):
        hidden = jnp.full((sub, sub), -jnp.inf, F32)
        hidden_max = jnp.full((SUBLANES, sub), -jnp.inf, F32)
        row_blocks, maxes = [], None
        for r in range(kc // sub):
            kb = c * (kc // sub) + r
            blocks, row_max = [], []
            for b8 in range(2 * tq // sub):
                b = b8 % (tq // sub)
                if kb > b:
                    blocks.append(hidden)
                    row_max.append(hidden_max)
                    continue
                blk = s_c[r * sub:(r + 1) * sub, b8 * sub:(b8 + 1) * sub]
                if kb == b:
                    blk = jnp.where(tri, blk, -jnp.inf)
                blocks.append(blk)
                row_max.append(_fold_rows(blk, jnp.max))
            row_blocks.append(jnp.concatenate(blocks, axis=1))
            row_max = jnp.concatenate(row_max, axis=1)
            maxes = row_max if maxes is None else jnp.maximum(maxes, row_max)
        return jnp.concatenate(row_blocks, axis=0), maxes

    group_a = (tuple(range(0, N_HEADS // 2)), s0_ref, mx0_ref)
    group_b = (tuple(range(N_HEADS // 2, N_HEADS)), s1_ref, mx1_ref)

    def step(consume=None, produce=None):
        n_slots = N_HEADS // 2
        for slot in range(n_slots):
            if consume is not None:
                t_c, (heads_c, sc_ref, mxc_ref) = consume
                h_c = heads_c[slot]
                m_old = m_ref[h_c]
                m_new = jnp.maximum(m_old, mxc_ref[slot])
                alpha = jnp.exp2(m_old - m_new)
                pv, pb_prev = None, None
            if produce is not None:
                t_p, (heads_p, sp_ref, mxp_ref), diagonal = produce[:3]
                q_src = produce[3] if len(produce) > 3 else None
                h_p = heads_p[slot]
                mx = None
            for c in range(tk // kc):
                chunk = slice(c * kc, (c + 1) * kc)
                if produce is not None:
                    rows = pl.ds(pl.multiple_of(t_p * tk + c * kc, kc), kc)
                    q_mat = qs_ref[h_p] if q_src is None else q_src[slot]
                    s_c = _dot(k_ref[0, rows, _head(h_p)], q_mat)
                if consume is not None:
                    pb_c = jnp.exp2(sc_ref[slot, chunk, :] - m_new).astype(BF16)
                    pb_prev = pb_c if pb_prev is None else jnp.concatenate([pb_prev, pb_c], axis=0)
                    if c == tk // kc - 1:
                        v_rows = slice(h_c * VT_ROWS, (h_c + 1) * VT_ROWS)
                        vt = jnp.concatenate([vt_ref[0, t_c * (tk // tv) + j, v_rows, :]
                                              for j in range(tk // tv)], axis=1)
                        pv = _dot(vt, pb_prev)
                if produce is not None:
                    if diagonal:
                        s_c, c_max = mask_diagonal(s_c, c)
                    else:
                        c_max = _fold_rows(s_c, jnp.max)
                    sp_ref[slot, chunk, :] = s_c
                    mx = c_max if mx is None else jnp.maximum(mx, c_max)
            if consume is not None:
                l_ref[h_c] = alpha * l_ref[h_c] + pv[HEAD_DIM:HEAD_DIM + 1, :]
                acc_ref[h_c] = alpha * acc_ref[h_c] + pv[:HEAD_DIM, :]
                m_ref[h_c] = m_new
            if produce is not None:
                mxp_ref[slot] = jnp.max(mx, axis=0, keepdims=True)

    def tile(t, next_diagonal):
        step(consume=(t, group_a), produce=(t, group_b, False))
        step(consume=(t, group_b), produce=(t + 1, group_a, next_diagonal))

    def body(t, carry):
        tile(t, False)
        return carry

    @pl.when(i == 0)
    def _():
        step(produce=(0, group_a, True))

    @pl.when(i > 0)
    def _():
        lax.fori_loop(0, i - 1, body, 0)
        tile(i - 1, True)

    lam = (jnp.exp(jnp.sum(lq1_ref[...] * lk1_ref[...], axis=-1, keepdims=True))
           - jnp.exp(jnp.sum(lq2_ref[...] * lk2_ref[...], axis=-1, keepdims=True)) + LAM_INIT)

    def finish(heads):
        for hh in heads:
            sl = _head(hh)
            acc = acc_ref[hh]
            inv = 1.0 / l_ref[hh]
            o_t = acc[:, :tq] * inv[:, :tq] - lam * (acc[:, tq:] * inv[:, tq:])
            o = _rms(o_t.T, g_ref[:, sl]) * (1.0 - LAM_INIT)
            yd_ref[:, sl] = (o * _silu(zd_ref[0, :, sl].astype(F32))).astype(BF16)

    def project_rows(c, src_ref, w_rows, first):
        r_c = slice(c * HEAD_DIM, (c + 1) * HEAD_DIM)
        y = _dot(src_ref[0, r_c, :], wo_ref[w_rows, :])
        part_ref[r_c, :] = (x_ref[0, r_c, :] + y) if first else (part_ref[r_c, :] + y)

    def project_heads(heads):
        cols = slice(heads[0] * HEAD_DIM, (heads[-1] + 1) * HEAD_DIM)
        w_rows = slice(GROUP_WIDTH + cols.start, GROUP_WIDTH + cols.stop)
        return _dot(yd_ref[:, cols], wo_ref[w_rows, :])

    step(consume=(i, group_a), produce=(i, group_b, True))
    finish(group_a[0])

    last = pl.num_programs(1) - 1

    @pl.when(i < last)
    def _():
        for slot, hh in enumerate(group_a[0]):
            q_t = qn_ref[0, :, _head(hh)].astype(F32).T
            qsn_ref[slot, :, 0:tq] = jnp.where(feat < DIFF_QK_DIM, q_t, 0.0).astype(BF16)
            qsn_ref[slot, :, tq:2 * tq] = jnp.where(feat >= DIFF_QK_DIM, q_t, 0.0).astype(BF16)
        step(consume=(i, group_b), produce=(0, group_a, False, qsn_ref))

    @pl.when(i == last)
    def _():
        step(consume=(i, group_b))

    for c in range(tq // HEAD_DIM):
        project_rows(c, ym_ref, slice(0, GROUP_WIDTH), True)
    for c in range(tq // HEAD_DIM):
        project_rows(c, yc_ref, slice(2 * GROUP_WIDTH, 3 * GROUP_WIDTH), False)
    y_a = project_heads(group_a[0])
    finish(group_b[0])
    o_ref[0] = _rms(part_ref[...] + y_a + project_heads(group_b[0]), fg_ref[...])


def _diff_out_call(qd, kd, vdt, zd, dnorm_g, lq1, lk1, lq2, lk2, x, ym, yc, w_out, final_g):
    n_q = qd.shape[1] // DIFF_Q_ROWS
    b, s, _ = qd.shape
    tq = DIFF_Q_ROWS
    nkv = vdt.shape[1]
    const = lambda shape: pl.BlockSpec(shape, lambda i, j: (0,) * len(shape))
    tok_spec = lambda w: pl.BlockSpec((1, tq, w), lambda i, j: (i, j, 0))
    return pl.pallas_call(
        _diff_kernel,
        grid=(b, s // tq),
        in_specs=[
            tok_spec(GROUP_WIDTH),
            pl.BlockSpec((1, tq, GROUP_WIDTH), lambda i, j: (i, jnp.minimum(j + 1, n_q - 1), 0)),
            pl.BlockSpec((1, s, GROUP_WIDTH), lambda i, j: (i, 0, 0)),
            pl.BlockSpec((1, nkv, N_HEADS * VT_ROWS, V_TILE), lambda i, j: (i, 0, 0, 0)),
            tok_spec(GROUP_WIDTH),
            const((1, GROUP_WIDTH)),
            const((1, DIFF_QK_DIM)), const((1, DIFF_QK_DIM)), const((1, DIFF_QK_DIM)), const((1, DIFF_QK_DIM)),
            tok_spec(D_MODEL), tok_spec(GROUP_WIDTH), tok_spec(GROUP_WIDTH),
            const((3 * GROUP_WIDTH, D_MODEL)), const((1, D_MODEL)),
        ],
        out_specs=tok_spec(D_MODEL),
        out_shape=jax.ShapeDtypeStruct((b, s, D_MODEL), F32),
        scratch_shapes=[pltpu.VMEM((N_HEADS, HEAD_DIM, 2 * tq), BF16),
                        pltpu.VMEM((N_HEADS, 1, 2 * tq), F32),
                        pltpu.VMEM((N_HEADS, 1, 2 * tq), F32),
                        pltpu.VMEM((N_HEADS, HEAD_DIM, 2 * tq), F32),
                        pltpu.VMEM((N_HEADS // 2, DIFF_KV_ROWS, 2 * tq), F32),
                        pltpu.VMEM((N_HEADS // 2, DIFF_KV_ROWS, 2 * tq), F32),
                        pltpu.VMEM((N_HEADS // 2, 1, 2 * tq), F32),
                        pltpu.VMEM((N_HEADS // 2, 1, 2 * tq), F32),
                        pltpu.VMEM((tq, GROUP_WIDTH), BF16),
                        pltpu.VMEM((tq, D_MODEL), F32),
                        pltpu.VMEM((N_HEADS // 2, HEAD_DIM, 2 * tq), BF16)],
        compiler_params=pltpu.CompilerParams(
            dimension_semantics=("arbitrary", "arbitrary"),
            vmem_limit_bytes=VMEM_LIMIT_BYTES),
        name="diffattn_out",
    )(qd, qd, kd, vdt, zd, dnorm_g, lq1, lk1, lq2, lk2, x, ym, yc, w_out, final_g)


def kernel(x, mem, norm_g, w_in, conv_w, conv_b, wq_m, wk_m, wv_m, w_if, b_if, mnorm_g, skip_m,
           lam_q1, lam_k1, lam_q2, lam_k2, dnorm_g, mem_norm_g, w_mem_kv, w_out, final_g):
    b, s, d = x.shape
    assert (d, s % DIFF_KV_ROWS, DIFF_KV_ROWS % V_TILE, FRONT_ROWS % V_TILE) == (D_MODEL, 0, 0, 0)
    assert (s % FRONT_ROWS, DIFF_KV_ROWS % DIFF_KEY_CHUNK) == (0, 0)
    assert DIFF_Q_ROWS == DIFF_KV_ROWS and DIFF_KEY_CHUNK % HEAD_DIM == 0
    assert norm_g.shape[0] == 1, "single-layer kernel"
    l = 0
    w_if_pad = jnp.pad(w_if[l], ((0, 0), (0, HEAD_DIM - w_if.shape[-1]))).astype(BF16)
    b_if_pad = jnp.pad(b_if[l], (0, HEAD_DIM - b_if.shape[-1]))[None, :]

    kv = _memkv_call(mem, mem_norm_g[l][None, :], w_mem_kv[l].astype(BF16))
    ym, qd, kd, vdt, zd, yc = _front_call(
        x, kv, norm_g[l][None, :], w_in[l].astype(BF16), conv_w[l], conv_b[l][None, :],
        jnp.concatenate([wq_m[l], wk_m[l]], axis=-1).astype(BF16), wv_m[l].astype(BF16), w_if_pad, b_if_pad,
        mnorm_g[l][None, :], skip_m[l][None, :])
    return _diff_out_call(qd, kd, vdt, zd, dnorm_g[l][None, :],
                          lam_q1[l][None, :], lam_k1[l][None, :], lam_q2[l][None, :], lam_k2[l][None, :],
                          x, ym, yc, w_out[l].astype(BF16), final_g[None, :])
```

```python
import math

import jax
import jax.numpy as jnp
from jax import lax
from jax.experimental import pallas as pl
from jax.experimental.pallas import tpu as pltpu

F32 = jnp.float32
BF16 = jnp.bfloat16

D_MODEL = 1024
N_HEADS = 4
HEAD_DIM = 128
SUBLANES = 8
V7X_VMEM_BYTES = 64 * 1024 * 1024
VMEM_LIMIT_BYTES = V7X_VMEM_BYTES - 6 * 1024 * 1024
GROUP_WIDTH = N_HEADS * HEAD_DIM
DIFF_QK_DIM = 64
CONV_WIDTH = 4
MLSTM_CHUNK = 128
MLSTM_STEP_CHUNKS = 4
MEM_LEN = 256
IN_WIDTH = 9 * GROUP_WIDTH
NORM_EPS = 1e-6
LAM_INIT = 0.8 - 0.6 * math.exp(-0.3 * 0)
LOG2E = math.log2(math.e)

OFF_XM, OFF_OM, OFF_ZM, OFF_QD, OFF_KD, OFF_VD, OFF_ZD, OFF_QC, OFF_ZC = (
    i * GROUP_WIDTH for i in range(9))

FRONT_ROWS = MLSTM_STEP_CHUNKS * MLSTM_CHUNK
V_TILE = 256
VT_PAD = 16
VT_ROWS = HEAD_DIM + VT_PAD
DIFF_Q_ROWS = 512
DIFF_KV_ROWS = 512
DIFF_KEY_CHUNK = 512
CONV_HALO = 8

NT_DIMS = (((1,), (1,)), ((), ()))
TN_DIMS = (((0,), (0,)), ((), ()))


def _dot(a, b):
    return jnp.dot(a, b, preferred_element_type=F32)


def _dot_nt(a, b):
    return lax.dot_general(a, b, NT_DIMS, preferred_element_type=F32)


def _sigmoid(x):
    return 0.5 * jnp.tanh(0.5 * x) + 0.5


def _silu(x):
    return x * _sigmoid(x)


def _rms(x, g):
    return x * lax.rsqrt(jnp.mean(x * x, axis=-1, keepdims=True) + NORM_EPS) * g


def _head(h):
    return slice(h * HEAD_DIM, (h + 1) * HEAD_DIM)


def _fold_rows(x, op):
    r, n = x.shape
    return op(x.reshape(r // SUBLANES, SUBLANES, n), axis=0)


def _log_sigmoid(x):
    return jnp.minimum(x, 0.0) - jnp.log1p(jnp.exp(-jnp.abs(x)))


def _cumsum_rows(f, tril):
    hi = f.astype(BF16)
    r1 = f - hi.astype(F32)
    mid = r1.astype(BF16)
    lo = (r1 - mid.astype(F32)).astype(BF16)
    return _dot(tril, hi) + _dot(tril, mid) + _dot(tril, lo)


def _memkv_kernel(mem_ref, g_ref, w_ref, kv_ref):
    y = _rms(mem_ref[0], g_ref[...])
    kv_ref[0] = _dot(y.astype(BF16), w_ref[...]).astype(BF16)


def _memkv_call(mem, g, w):
    b = mem.shape[0]
    return pl.pallas_call(
        _memkv_kernel,
        grid=(b,),
        in_specs=[
            pl.BlockSpec((1, MEM_LEN, D_MODEL), lambda i: (i, 0, 0)),
            pl.BlockSpec((1, D_MODEL), lambda i: (0, 0)),
            pl.BlockSpec((D_MODEL, 2 * GROUP_WIDTH), lambda i: (0, 0)),
        ],
        out_specs=pl.BlockSpec((1, MEM_LEN, 2 * GROUP_WIDTH), lambda i: (i, 0, 0)),
        out_shape=jax.ShapeDtypeStruct((b, MEM_LEN, 2 * GROUP_WIDTH), BF16),
        compiler_params=pltpu.CompilerParams(dimension_semantics=("arbitrary",)),
        name="memkv",
    )(mem, g, w)


def _front_kernel(x_ref, kv_ref, ng_ref, win_ref, cw_ref, cb_ref, wqk_ref, wv_ref, wif_ref, bif_ref,
                  mg_ref, sk_ref,
                  ym_ref, qd_ref, kd_ref, vdt_ref, zd_ref, yc_ref,
                  conv_ref, c_ref, n_ref, m_ref, q_s, k_s, v_s, g_s, om_s, zm_s, xcv_s):
    tm = FRONT_ROWS
    L = MLSTM_CHUNK
    heads = range(N_HEADS)

    @pl.when(pl.program_id(1) == 0)
    def _():
        conv_ref[tm:tm + CONV_HALO, :] = jnp.zeros((CONV_HALO, GROUP_WIDTH), F32)
        c_ref[...] = jnp.zeros_like(c_ref)
        n_ref[...] = jnp.zeros_like(n_ref)
        m_ref[...] = jnp.zeros_like(m_ref)

    h = _rms(x_ref[0], ng_ref[...]).astype(BF16)

    def proj(off):
        return _dot(h, win_ref[:, off:off + GROUP_WIDTH])

    def put_vdt(val):
        v_t = val.T.astype(BF16)
        pad_row = lax.broadcasted_iota(jnp.int32, (VT_PAD, tm), 0)
        pad = jnp.where(pad_row == 0, 1.0, 0.0).astype(BF16)
        v_aug = jnp.concatenate([blk for hh in heads for blk in (v_t[_head(hh), :], pad)], axis=0)
        for j in range(tm // V_TILE):
            vdt_ref[0, j] = v_aug[:, j * V_TILE:(j + 1) * V_TILE]

    def put(ref, scale=None):
        def sink(val):
            ref[...] = (val if scale is None else val * scale).astype(BF16).reshape(ref.shape)
        return sink

    side_pieces = [(OFF_OM, put(om_s)), (OFF_ZM, put(zm_s)),
                   (OFF_QD, put(qd_ref, DIFF_QK_DIM ** -0.5 * LOG2E)), (OFF_KD, put(kd_ref)),
                   (OFF_VD, put_vdt), (OFF_ZD, put(zd_ref))]
    pending = []

    def side_start():
        if side_pieces:
            off, sink = side_pieces.pop(0)
            pending.append((proj(off), sink))

    def side_finish():
        if pending:
            val, sink = pending.pop(0)
            sink(val)

    x_m = proj(OFF_XM)
    q_cb = proj(OFF_QC).astype(BF16)
    z_c = proj(OFF_ZC)

    conv_ref[0:CONV_HALO, :] = conv_ref[tm:tm + CONV_HALO, :]
    conv_ref[CONV_HALO:CONV_HALO + tm, :] = x_m
    acc = jnp.broadcast_to(cb_ref[...], (tm, GROUP_WIDTH))
    for j in range(CONV_WIDTH):
        start = CONV_HALO - (CONV_WIDTH - 1) + j
        acc = acc + cw_ref[j:j + 1, :] * conv_ref[start:start + tm, :]
    sc = [_dot_nt(q_cb[:, _head(hh)], kv_ref[0, :, _head(hh)]) * (HEAD_DIM ** -0.5 * LOG2E)
          for hh in heads]
    x_cv = _silu(acc)
    xcv_b = x_cv.astype(BF16)
    xcv_s[...] = xcv_b
    xm_b = x_m.astype(BF16)

    qk_p = [_dot(xcv_b[:, _head(hh)], wqk_ref[hh]) for hh in heads]
    vv_p = [_dot(xm_b[:, _head(hh)], wv_ref[hh]) for hh in heads]

    q_all = jnp.concatenate([qk_p[hh][:, :HEAD_DIM] for hh in heads], axis=1).astype(BF16)
    k_all = jnp.concatenate([qk_p[hh][:, HEAD_DIM:] for hh in heads], axis=1)
    v_all = jnp.concatenate(vv_p, axis=1).astype(BF16)
    q_s[...] = q_all
    k_s[...] = (k_all * HEAD_DIM ** -0.5).astype(BF16)
    v_s[...] = v_all
    g_s[...] = (jnp.broadcast_to(bif_ref[...], (tm, HEAD_DIM))
                + _dot(q_all, wif_ref[0:GROUP_WIDTH, :])
                + _dot(k_all.astype(BF16), wif_ref[GROUP_WIDTH:2 * GROUP_WIDTH, :])
                + _dot(v_all, wif_ref[2 * GROUP_WIDTH:3 * GROUP_WIDTH, :]))

    mx = [jnp.max(sc[hh], axis=-1, keepdims=True) for hh in heads]
    pm = [jnp.exp2(sc[hh] - mx[hh]) for hh in heads]
    inv = [1.0 / jnp.sum(pm[hh], axis=-1, keepdims=True) for hh in heads]
    pc = [(pm[hh] * inv[hh]).astype(BF16) for hh in heads]
    oc = [_dot(pc[hh], kv_ref[0, :, GROUP_WIDTH + hh * HEAD_DIM:GROUP_WIDTH + (hh + 1) * HEAD_DIM]) for hh in heads]
    for hh in heads:
        yc_ref[0, :, _head(hh)] = (oc[hh] * _silu(z_c[:, _head(hh)])).astype(BF16)

    row = lax.broadcasted_iota(jnp.int32, (L, L), 0)
    col = lax.broadcasted_iota(jnp.int32, (L, L), 1)
    causal = col <= row
    tril = jnp.where(causal, 1.0, 0.0).astype(BF16)
    chunks = range(MLSTM_STEP_CHUNKS)
    probs = [(c, hh) for c in chunks for hh in heads]
    rows = lambda c: slice(c * L, (c + 1) * L)
    q_of = lambda p: q_s[rows(p[0]), _head(p[1])]
    k_of = lambda p: k_s[rows(p[0]), _head(p[1])]
    v_of = lambda p: v_s[rows(p[0]), _head(p[1])]

    def each(f, side=False):
        if side:
            side_start()
        out = {p: f(p) for p in probs}
        if side:
            side_finish()
        return out

    a_cols, a_rows = [], []
    for c in chunks:
        g = g_s[rows(c), :]
        bb_all = _cumsum_rows(_log_sigmoid(g), tril)
        ac = jnp.where(col < N_HEADS, g, bb_all) * LOG2E
        a_cols.append(ac)
        a_rows.append(ac.T)
    ib_col = {(c, hh): a_cols[c][:, hh:hh + 1] for c, hh in probs}
    bb_col = {(c, hh): a_cols[c][:, N_HEADS + hh:N_HEADS + hh + 1] for c, hh in probs}
    ib_row = {(c, hh): a_rows[c][hh:hh + 1, :] for c, hh in probs}
    bb_row = {(c, hh): a_rows[c][N_HEADS + hh:N_HEADS + hh + 1, :] for c, hh in probs}
    b_end = {p: bb_col[p][L - 1:L, :] for p in probs}

    qk = {p: _dot_nt(q_of(p), k_of(p)) for p in probs}

    d = each(lambda p: jnp.where(causal, (bb_col[p] - bb_row[p]) + ib_row[p], -jnp.inf), side=True)
    r = each(lambda p: jnp.max(d[p], axis=-1, keepdims=True))
    s_loc = each(lambda p: qk[p] * jnp.exp2(d[p] - r[p]), side=True)
    rs = each(lambda p: jnp.sum(s_loc[p], axis=-1, keepdims=True))
    s_bf = each(lambda p: s_loc[p].astype(BF16))
    a = each(lambda p: (b_end[p] - bb_col[p]) + ib_col[p], side=True)
    a_max = each(lambda p: jnp.max(a[p], axis=0, keepdims=True))
    w_loc = each(lambda p: jnp.broadcast_to(jnp.exp2(a[p] - a_max[p]), (L, HEAD_DIM)))
    vw = each(lambda p: (v_of(p).astype(F32) * w_loc[p]).astype(BF16), side=True)
    nu = each(lambda p: jnp.sum(k_of(p).astype(F32) * w_loc[p], axis=0, keepdims=True))

    sv = {p: _dot(s_bf[p], v_of(p)) for p in probs}
    u = {p: lax.dot_general(vw[p], k_of(p), TN_DIMS, preferred_element_type=F32) for p in probs}

    m_in, c_in, n_in = {}, {}, {}
    for hh in heads:
        m_cur = m_ref[hh][:, 0:1]
        c_cur = c_ref[hh]
        n_cur = n_ref[hh]
        for c in chunks:
            p = (c, hh)
            m_in[p], c_in[p], n_in[p] = m_cur, c_cur.astype(BF16), n_cur
            m_new = jnp.maximum(b_end[p] + m_cur, a_max[p])
            decay = jnp.exp2(b_end[p] + m_cur - m_new)
            e_upd = jnp.exp2(a_max[p] - m_new)
            c_cur = decay * c_cur + e_upd * u[p]
            n_cur = decay * n_cur + e_upd * nu[p]
            m_cur = m_new
        c_ref[hh] = c_cur
        n_ref[hh] = n_cur
        m_ref[hh] = jnp.broadcast_to(m_cur, (1, HEAD_DIM))

    q_c = {p: _dot_nt(q_of(p), c_in[p]) for p in probs}

    tile_of = lambda ref, p: ref[rows(p[0]), _head(p[1])].astype(F32)
    g_col = each(lambda p: bb_col[p] + m_in[p])
    m_t = each(lambda p: jnp.maximum(g_col[p], r[p]))
    e_loc = each(lambda p: jnp.exp2(r[p] - m_t[p]))
    inter = each(lambda p: jnp.exp2(g_col[p] - m_t[p]))
    q_n = each(lambda p: jnp.sum(q_of(p).astype(F32) * n_in[p], axis=-1, keepdims=True))
    den = each(lambda p: e_loc[p] * rs[p] + inter[p] * q_n[p])
    scale = each(lambda p: 1.0 / jnp.maximum(jnp.abs(den[p]), jnp.exp2(-m_t[p])))
    w_intra = each(lambda p: e_loc[p] * scale[p])
    w_inter = each(lambda p: inter[p] * scale[p])
    hg = each(lambda p: _sigmoid(tile_of(om_s, p))
              * (w_intra[p] * sv[p] + w_inter[p] * q_c[p]), side=True)
    mu = each(lambda p: jnp.mean(hg[p], axis=-1, keepdims=True))
    cen = each(lambda p: hg[p] - mu[p], side=True)
    var = each(lambda p: jnp.mean(jnp.square(cen[p]), axis=-1, keepdims=True))
    assert not side_pieces and not pending
    for p in probs:
        sl = _head(p[1])
        y = cen[p] * lax.rsqrt(var[p] + NORM_EPS) * mg_ref[:, sl]
        y = (y + sk_ref[:, sl] * tile_of(xcv_s, p)) * _silu(tile_of(zm_s, p))
        ym_ref[0, rows(p[0]), sl] = y.astype(BF16)


def _front_call(x, kv, norm_g, w_in, conv_w, conv_b, wqk, wv, w_if, b_if, mnorm_g, skip_m):
    b, s, _ = x.shape
    tm = FRONT_ROWS
    ns = s // tm
    tok = lambda w: jax.ShapeDtypeStruct((b, s, w), BF16)
    tok_spec = lambda w: pl.BlockSpec((1, tm, w), lambda i, j: (i, j, 0))
    const = lambda shape: pl.BlockSpec(shape, lambda i, j: (0,) * len(shape))
    vdt_tiles = tm // V_TILE
    out_shape = (
        tok(GROUP_WIDTH),
        tok(GROUP_WIDTH), tok(GROUP_WIDTH),
        jax.ShapeDtypeStruct((b, ns * vdt_tiles, N_HEADS * VT_ROWS, V_TILE), BF16),
        tok(GROUP_WIDTH),
        tok(GROUP_WIDTH),
    )
    out_specs = (
        tok_spec(GROUP_WIDTH), tok_spec(GROUP_WIDTH), tok_spec(GROUP_WIDTH),
        pl.BlockSpec((1, vdt_tiles, N_HEADS * VT_ROWS, V_TILE), lambda i, j: (i, j, 0, 0)),
        tok_spec(GROUP_WIDTH), tok_spec(GROUP_WIDTH),
    )
    in_specs = [
        pl.BlockSpec((1, tm, D_MODEL), lambda i, j: (i, j, 0)),
        pl.BlockSpec((1, MEM_LEN, 2 * GROUP_WIDTH), lambda i, j: (i, 0, 0)),
        const((1, D_MODEL)),
        const((D_MODEL, IN_WIDTH)),
        const((CONV_WIDTH, GROUP_WIDTH)),
        const((1, GROUP_WIDTH)),
        const((N_HEADS, HEAD_DIM, 2 * HEAD_DIM)),
        const((N_HEADS, HEAD_DIM, HEAD_DIM)),
        const((3 * GROUP_WIDTH, HEAD_DIM)),
        const((1, HEAD_DIM)),
        const((1, GROUP_WIDTH)),
        const((1, GROUP_WIDTH)),
    ]
    tile_bf16 = pltpu.VMEM((tm, GROUP_WIDTH), BF16)
    return pl.pallas_call(
        _front_kernel,
        grid=(b, ns),
        in_specs=in_specs,
        out_specs=out_specs,
        out_shape=out_shape,
        scratch_shapes=[pltpu.VMEM((tm + CONV_HALO, GROUP_WIDTH), F32),
                        pltpu.VMEM((N_HEADS, HEAD_DIM, HEAD_DIM), F32),
                        pltpu.VMEM((N_HEADS, 1, HEAD_DIM), F32),
                        pltpu.VMEM((N_HEADS, 1, HEAD_DIM), F32),
                        tile_bf16, tile_bf16, tile_bf16,
                        pltpu.VMEM((tm, HEAD_DIM), F32),
                        tile_bf16, tile_bf16, tile_bf16],
        compiler_params=pltpu.CompilerParams(
            dimension_semantics=("arbitrary", "arbitrary"),
            vmem_limit_bytes=VMEM_LIMIT_BYTES),
        name="front",
    )(x, kv, norm_g, w_in, conv_w, conv_b, wqk, wv, w_if, b_if, mnorm_g, skip_m)


def _diff_kernel(q_ref, qn_ref, k_ref, vt_ref, zd_ref, g_ref, lq1_ref, lk1_ref, lq2_ref, lk2_ref,
                 x_ref, ym_ref, yc_ref, wo_ref, fg_ref, o_ref,
                 qs_ref, m_ref, l_ref, acc_ref, s0_ref, s1_ref, mx0_ref, mx1_ref, yd_ref, part_ref, qsn_ref):
    tq, tk, tv = DIFF_Q_ROWS, DIFF_KV_ROWS, V_TILE
    i = pl.program_id(1)
    feat = lax.broadcasted_iota(jnp.int32, (HEAD_DIM, tq), 0)
    for hh in range(N_HEADS):
        q_t = q_ref[0, :, _head(hh)].astype(F32).T
        qs_ref[hh, :, 0:tq] = jnp.where(feat < DIFF_QK_DIM, q_t, 0.0).astype(BF16)
        qs_ref[hh, :, tq:2 * tq] = jnp.where(feat >= DIFF_QK_DIM, q_t, 0.0).astype(BF16)
    m_ref[...] = jnp.full_like(m_ref, -jnp.inf)
    l_ref[...] = jnp.zeros_like(l_ref)
    acc_ref[...] = jnp.zeros_like(acc_ref)

    kc = DIFF_KEY_CHUNK
    sub = HEAD_DIM
    tri = (lax.broadcasted_iota(jnp.int32, (sub, sub), 0) <= lax.broadcasted_iota(jnp.int32, (sub, sub), 1))

    def mask_diagonal(s_c, c):
        hidden = jnp.full((sub, sub), -jnp.inf, F32)
        hidden_max = jnp.full((SUBLANES, sub), -jnp.inf, F32)
        row_blocks, maxes = [], None
        for r in range(kc // sub):
            kb = c * (kc // sub) + r
            blocks, row_max = [], []
            for b8 in range(2 * tq // sub):
                b = b8 % (tq // sub)
                if kb > b:
                    blocks.append(hidden)
                    row_max.append(hidden_max)
                    continue
                blk = s_c[r * sub:(r + 1) * sub, b8 * sub:(b8 + 1) * sub]
                if kb == b:
                    blk = jnp.where(tri, blk, -jnp.inf)
                blocks.append(blk)
                row_max.append(_fold_rows(blk, jnp.max))
            row_blocks.append(jnp.concatenate(blocks, axis=1))
            row_max = jnp.concatenate(row_max, axis=1)
            maxes = row_max if maxes is None else jnp.maximum(maxes, row_max)
        return jnp.concatenate(row_blocks, axis=0), maxes

    group_a = (tuple(range(0, N_HEADS // 2)), s0_ref, mx0_ref)
    group_b = (tuple(range(N_HEADS // 2, N_HEADS)), s1_ref, mx1_ref)

    def step(consume=None, produce=None):
        n_slots = N_HEADS // 2
        for slot in range(n_slots):
            if consume is not None:
                t_c, (heads_c, sc_ref, mxc_ref) = consume
                h_c = heads_c[slot]
                m_old = m_ref[h_c]
                m_new = jnp.maximum(m_old, mxc_ref[slot])
                alpha = jnp.exp2(m_old - m_new)
                pv, pb_prev = None, None
            if produce is not None:
                t_p, (heads_p, sp_ref, mxp_ref), diagonal = produce[:3]
                q_src = produce[3] if len(produce) > 3 else None
                h_p = heads_p[slot]
                mx = None
            for c in range(tk // kc):
                chunk = slice(c * kc, (c + 1) * kc)
                if produce is not None:
                    rows = pl.ds(pl.multiple_of(t_p * tk + c * kc, kc), kc)
                    q_mat = qs_ref[h_p] if q_src is None else q_src[slot]
                    s_c = _dot(k_ref[0, rows, _head(h_p)], q_mat)
                if consume is not None:
                    pb_c = jnp.exp2(sc_ref[slot, chunk, :] - m_new).astype(BF16)
                    pb_prev = pb_c if pb_prev is None else jnp.concatenate([pb_prev, pb_c], axis=0)
                    if c == tk // kc - 1:
                        v_rows = slice(h_c * VT_ROWS, (h_c + 1) * VT_ROWS)
                        vt = jnp.concatenate([vt_ref[0, t_c * (tk // tv) + j, v_rows, :]
                                              for j in range(tk // tv)], axis=1)
                        pv = _dot(vt, pb_prev)
                if produce is not None:
                    if diagonal:
                        s_c, c_max = mask_diagonal(s_c, c)
                    else:
                        c_max = _fold_rows(s_c, jnp.max)
                    sp_ref[slot, chunk, :] = s_c
                    mx = c_max if mx is None else jnp.maximum(mx, c_max)
            if consume is not None:
                l_ref[h_c] = alpha * l_ref[h_c] + pv[HEAD_DIM:HEAD_DIM + 1, :]
                acc_ref[h_c] = alpha * acc_ref[h_c] + pv[:HEAD_DIM, :]
                m_ref[h_c] = m_new
            if produce is not None:
                mxp_ref[slot] = jnp.max(mx, axis=0, keepdims=True)

    def tile(t, next_diagonal):
        step(consume=(t, group_a), produce=(t, group_b, False))
        step(consume=(t, group_b), produce=(t + 1, group_a, next_diagonal))

    def body(t, carry):
        tile(t, False)
        return carry

    @pl.when(i == 0)
    def _():
        step(produce=(0, group_a, True))

    @pl.when(i > 0)
    def _():
        lax.fori_loop(0, i - 1, body, 0)
        tile(i - 1, True)

    lam = (jnp.exp(jnp.sum(lq1_ref[...] * lk1_ref[...], axis=-1, keepdims=True))
           - jnp.exp(jnp.sum(lq2_ref[...] * lk2_ref[...], axis=-1, keepdims=True)) + LAM_INIT)

    def finish(heads):
        for hh in heads:
            sl = _head(hh)
            acc = acc_ref[hh]
            inv = 1.0 / l_ref[hh]
            o_t = acc[:, :tq] * inv[:, :tq] - lam * (acc[:, tq:] * inv[:, tq:])
            rstd = lax.rsqrt(jnp.mean(o_t * o_t, axis=0, keepdims=True) + NORM_EPS)
            o = (o_t * rstd * (g_ref[sl, :] * (1.0 - LAM_INIT))).T
            yd_ref[:, sl] = (o * _silu(zd_ref[0, :, sl].astype(F32))).astype(BF16)

    def project_rows(c, src_ref, w_rows, first):
        r_c = slice(c * HEAD_DIM, (c + 1) * HEAD_DIM)
        y = _dot(src_ref[0, r_c, :], wo_ref[w_rows, :])
        part_ref[r_c, :] = (x_ref[0, r_c, :] + y) if first else (part_ref[r_c, :] + y)

    def project_heads(heads):
        cols = slice(heads[0] * HEAD_DIM, (heads[-1] + 1) * HEAD_DIM)
        w_rows = slice(GROUP_WIDTH + cols.start, GROUP_WIDTH + cols.stop)
        return _dot(yd_ref[:, cols], wo_ref[w_rows, :])

    step(consume=(i, group_a), produce=(i, group_b, True))
    finish(group_a[0])

    last = pl.num_programs(1) - 1

    @pl.when(i < last)
    def _():
        for slot, hh in enumerate(group_a[0]):
            q_t = qn_ref[0, :, _head(hh)].astype(F32).T
            qsn_ref[slot, :, 0:tq] = jnp.where(feat < DIFF_QK_DIM, q_t, 0.0).astype(BF16)
            qsn_ref[slot, :, tq:2 * tq] = jnp.where(feat >= DIFF_QK_DIM, q_t, 0.0).astype(BF16)
        step(consume=(i, group_b), produce=(0, group_a, False, qsn_ref))

    @pl.when(i == last)
    def _():
        step(consume=(i, group_b))

    for c in range(tq // HEAD_DIM):
        project_rows(c, ym_ref, slice(0, GROUP_WIDTH), True)
    for c in range(tq // HEAD_DIM):
        project_rows(c, yc_ref, slice(2 * GROUP_WIDTH, 3 * GROUP_WIDTH), False)
    y_a = project_heads(group_a[0])
    finish(group_b[0])
    o_ref[0] = _rms(part_ref[...] + y_a + project_heads(group_b[0]), fg_ref[...])


def _diff_out_call(qd, kd, vdt, zd, dnorm_g, lq1, lk1, lq2, lk2, x, ym, yc, w_out, final_g):
    n_q = qd.shape[1] // DIFF_Q_ROWS
    b, s, _ = qd.shape
    tq = DIFF_Q_ROWS
    nkv = vdt.shape[1]
    const = lambda shape: pl.BlockSpec(shape, lambda i, j: (0,) * len(shape))
    tok_spec = lambda w: pl.BlockSpec((1, tq, w), lambda i, j: (i, j, 0))
    return pl.pallas_call(
        _diff_kernel,
        grid=(b, s // tq),
        in_specs=[
            tok_spec(GROUP_WIDTH),
            pl.BlockSpec((1, tq, GROUP_WIDTH), lambda i, j: (i, jnp.minimum(j + 1, n_q - 1), 0)),
            pl.BlockSpec((1, s, GROUP_WIDTH), lambda i, j: (i, 0, 0)),
            pl.BlockSpec((1, nkv, N_HEADS * VT_ROWS, V_TILE), lambda i, j: (i, 0, 0, 0)),
            tok_spec(GROUP_WIDTH),
            const((GROUP_WIDTH, 1)),
            const((1, DIFF_QK_DIM)), const((1, DIFF_QK_DIM)), const((1, DIFF_QK_DIM)), const((1, DIFF_QK_DIM)),
            tok_spec(D_MODEL), tok_spec(GROUP_WIDTH), tok_spec(GROUP_WIDTH),
            const((3 * GROUP_WIDTH, D_MODEL)), const((1, D_MODEL)),
        ],
        out_specs=tok_spec(D_MODEL),
        out_shape=jax.ShapeDtypeStruct((b, s, D_MODEL), F32),
        scratch_shapes=[pltpu.VMEM((N_HEADS, HEAD_DIM, 2 * tq), BF16),
                        pltpu.VMEM((N_HEADS, 1, 2 * tq), F32),
                        pltpu.VMEM((N_HEADS, 1, 2 * tq), F32),
                        pltpu.VMEM((N_HEADS, HEAD_DIM, 2 * tq), F32),
                        pltpu.VMEM((N_HEADS // 2, DIFF_KV_ROWS, 2 * tq), F32),
                        pltpu.VMEM((N_HEADS // 2, DIFF_KV_ROWS, 2 * tq), F32),
                        pltpu.VMEM((N_HEADS // 2, 1, 2 * tq), F32),
                        pltpu.VMEM((N_HEADS // 2, 1, 2 * tq), F32),
                        pltpu.VMEM((tq, GROUP_WIDTH), BF16),
                        pltpu.VMEM((tq, D_MODEL), F32),
                        pltpu.VMEM((N_HEADS // 2, HEAD_DIM, 2 * tq), BF16)],
        compiler_params=pltpu.CompilerParams(
            dimension_semantics=("arbitrary", "arbitrary"),
            vmem_limit_bytes=VMEM_LIMIT_BYTES),
        name="diffattn_out",
    )(qd, qd, kd, vdt, zd, dnorm_g, lq1, lk1, lq2, lk2, x, ym, yc, w_out, final_g)


def kernel(x, mem, norm_g, w_in, conv_w, conv_b, wq_m, wk_m, wv_m, w_if, b_if, mnorm_g, skip_m,
           lam_q1, lam_k1, lam_q2, lam_k2, dnorm_g, mem_norm_g, w_mem_kv, w_out, final_g):
    b, s, d = x.shape
    assert (d, s % DIFF_KV_ROWS, DIFF_KV_ROWS % V_TILE, FRONT_ROWS % V_TILE) == (D_MODEL, 0, 0, 0)
    assert (s % FRONT_ROWS, DIFF_KV_ROWS % DIFF_KEY_CHUNK) == (0, 0)
    assert DIFF_Q_ROWS == DIFF_KV_ROWS and DIFF_KEY_CHUNK % HEAD_DIM == 0
    assert norm_g.shape[0] == 1, "single-layer kernel"
    l = 0
    w_if_pad = jnp.pad(w_if[l], ((0, 0), (0, HEAD_DIM - w_if.shape[-1]))).astype(BF16)
    b_if_pad = jnp.pad(b_if[l], (0, HEAD_DIM - b_if.shape[-1]))[None, :]

    kv = _memkv_call(mem, mem_norm_g[l][None, :], w_mem_kv[l].astype(BF16))
    ym, qd, kd, vdt, zd, yc = _front_call(
        x, kv, norm_g[l][None, :], w_in[l].astype(BF16), conv_w[l], conv_b[l][None, :],
        jnp.concatenate([wq_m[l], wk_m[l]], axis=-1).astype(BF16), wv_m[l].astype(BF16), w_if_pad, b_if_pad,
        mnorm_g[l][None, :], skip_m[l][None, :])
    return _diff_out_call(qd, kd, vdt, zd, dnorm_g[l][:, None],
                          lam_q1[l][None, :], lam_k1[l][None, :], lam_q2[l][None, :], lam_k2[l][None, :],
                          x, ym, yc, w_out[l].astype(BF16), final_g[None, :])
```

```python
import math

import jax
import jax.numpy as jnp
from jax import lax
from jax.experimental import pallas as pl
from jax.experimental.pallas import tpu as pltpu

F32 = jnp.float32
BF16 = jnp.bfloat16

D_MODEL = 1024
N_HEADS = 4
HEAD_DIM = 128
SUBLANES = 8
V7X_VMEM_BYTES = 64 * 1024 * 1024
VMEM_LIMIT_BYTES = V7X_VMEM_BYTES - 6 * 1024 * 1024
GROUP_WIDTH = N_HEADS * HEAD_DIM
DIFF_QK_DIM = 64
CONV_WIDTH = 4
MLSTM_CHUNK = 128
MLSTM_STEP_CHUNKS = 4
MEM_LEN = 256
IN_WIDTH = 9 * GROUP_WIDTH
NORM_EPS = 1e-6
LAM_INIT = 0.8 - 0.6 * math.exp(-0.3 * 0)
LOG2E = math.log2(math.e)

OFF_XM, OFF_OM, OFF_ZM, OFF_QD, OFF_KD, OFF_VD, OFF_ZD, OFF_QC, OFF_ZC = (
    i * GROUP_WIDTH for i in range(9))

FRONT_ROWS = MLSTM_STEP_CHUNKS * MLSTM_CHUNK
V_TILE = 256
VT_PAD = 16
VT_ROWS = HEAD_DIM + VT_PAD
DIFF_Q_ROWS = 512
DIFF_KV_ROWS = 512
DIFF_KEY_CHUNK = 512
CONV_HALO = 8

NT_DIMS = (((1,), (1,)), ((), ()))
TN_DIMS = (((0,), (0,)), ((), ()))


def _dot(a, b):
    return jnp.dot(a, b, preferred_element_type=F32)


def _dot_nt(a, b):
    return lax.dot_general(a, b, NT_DIMS, preferred_element_type=F32)


def _sigmoid(x):
    return 0.5 * jnp.tanh(0.5 * x) + 0.5


def _silu(x):
    return x * _sigmoid(x)


def _rms(x, g):
    return x * lax.rsqrt(jnp.mean(x * x, axis=-1, keepdims=True) + NORM_EPS) * g


def _head(h):
    return slice(h * HEAD_DIM, (h + 1) * HEAD_DIM)


def _fold_rows(x, op):
    r, n = x.shape
    return op(x.reshape(r // SUBLANES, SUBLANES, n), axis=0)


def _log_sigmoid(x):
    return jnp.minimum(x, 0.0) - jnp.log1p(jnp.exp(-jnp.abs(x)))


def _cumsum_rows(f, tril):
    hi = f.astype(BF16)
    r1 = f - hi.astype(F32)
    mid = r1.astype(BF16)
    lo = (r1 - mid.astype(F32)).astype(BF16)
    return _dot(tril, hi) + _dot(tril, mid) + _dot(tril, lo)


def _memkv_kernel(mem_ref, g_ref, w_ref, kv_ref):
    y = _rms(mem_ref[0], g_ref[...])
    kv_ref[0] = _dot(y.astype(BF16), w_ref[...]).astype(BF16)


def _memkv_call(mem, g, w):
    b = mem.shape[0]
    return pl.pallas_call(
        _memkv_kernel,
        grid=(b,),
        in_specs=[
            pl.BlockSpec((1, MEM_LEN, D_MODEL), lambda i: (i, 0, 0)),
            pl.BlockSpec((1, D_MODEL), lambda i: (0, 0)),
            pl.BlockSpec((D_MODEL, 2 * GROUP_WIDTH), lambda i: (0, 0)),
        ],
        out_specs=pl.BlockSpec((1, MEM_LEN, 2 * GROUP_WIDTH), lambda i: (i, 0, 0)),
        out_shape=jax.ShapeDtypeStruct((b, MEM_LEN, 2 * GROUP_WIDTH), BF16),
        compiler_params=pltpu.CompilerParams(dimension_semantics=("arbitrary",)),
        name="memkv",
    )(mem, g, w)


def _front_kernel(x_ref, kv_ref, ng_ref, win_ref, cw_ref, cb_ref, wqk_ref, wv_ref, wif_ref, bif_ref,
                  mg_ref, sk_ref,
                  ym_ref, qd_ref, kd_ref, vdt_ref, zd_ref, yc_ref,
                  conv_ref, c_ref, n_ref, m_ref, q_s, k_s, v_s, g_s, om_s, zm_s, xcv_s):
    tm = FRONT_ROWS
    L = MLSTM_CHUNK
    heads = range(N_HEADS)

    @pl.when(pl.program_id(1) == 0)
    def _():
        conv_ref[tm:tm + CONV_HALO, :] = jnp.zeros((CONV_HALO, GROUP_WIDTH), F32)
        c_ref[...] = jnp.zeros_like(c_ref)
        n_ref[...] = jnp.zeros_like(n_ref)
        m_ref[...] = jnp.zeros_like(m_ref)

    h = _rms(x_ref[0], ng_ref[...]).astype(BF16)

    def proj(off):
        return _dot(h, win_ref[:, off:off + GROUP_WIDTH])

    def put_vdt(val):
        v_t = val.T.astype(BF16)
        pad_row = lax.broadcasted_iota(jnp.int32, (VT_PAD, tm), 0)
        pad = jnp.where(pad_row == 0, 1.0, 0.0).astype(BF16)
        v_aug = jnp.concatenate([blk for hh in heads for blk in (v_t[_head(hh), :], pad)], axis=0)
        for j in range(tm // V_TILE):
            vdt_ref[0, j] = v_aug[:, j * V_TILE:(j + 1) * V_TILE]

    def put(ref, scale=None):
        def sink(val):
            ref[...] = (val if scale is None else val * scale).astype(BF16).reshape(ref.shape)
        return sink

    side_pieces = [(OFF_OM, put(om_s)), (OFF_ZM, put(zm_s)),
                   (OFF_QD, put(qd_ref, DIFF_QK_DIM ** -0.5 * LOG2E)), (OFF_KD, put(kd_ref)),
                   (OFF_VD, put_vdt), (OFF_ZD, put(zd_ref))]
    pending = []

    def side_start():
        if side_pieces:
            off, sink = side_pieces.pop(0)
            pending.append((proj(off), sink))

    def side_finish():
        if pending:
            val, sink = pending.pop(0)
            sink(val)

    x_m = proj(OFF_XM)
    q_cb = proj(OFF_QC).astype(BF16)
    z_c = proj(OFF_ZC)

    conv_ref[0:CONV_HALO, :] = conv_ref[tm:tm + CONV_HALO, :]
    conv_ref[CONV_HALO:CONV_HALO + tm, :] = x_m
    acc = jnp.broadcast_to(cb_ref[...], (tm, GROUP_WIDTH))
    for j in range(CONV_WIDTH):
        start = CONV_HALO - (CONV_WIDTH - 1) + j
        acc = acc + cw_ref[j:j + 1, :] * conv_ref[start:start + tm, :]
    sc = [_dot_nt(q_cb[:, _head(hh)], kv_ref[0, :, _head(hh)]) * (HEAD_DIM ** -0.5 * LOG2E)
          for hh in heads]
    x_cv = _silu(acc)
    xcv_b = x_cv.astype(BF16)
    xcv_s[...] = xcv_b
    xm_b = x_m.astype(BF16)

    qk_p = [_dot(xcv_b[:, _head(hh)], wqk_ref[hh]) for hh in heads]
    vv_p = [_dot(xm_b[:, _head(hh)], wv_ref[hh]) for hh in heads]

    q_all = jnp.concatenate([qk_p[hh][:, :HEAD_DIM] for hh in heads], axis=1).astype(BF16)
    k_all = jnp.concatenate([qk_p[hh][:, HEAD_DIM:] for hh in heads], axis=1)
    v_all = jnp.concatenate(vv_p, axis=1).astype(BF16)
    q_s[...] = q_all
    k_s[...] = (k_all * HEAD_DIM ** -0.5).astype(BF16)
    v_s[...] = v_all
    g_s[...] = (jnp.broadcast_to(bif_ref[...], (tm, HEAD_DIM))
                + _dot(q_all, wif_ref[0:GROUP_WIDTH, :])
                + _dot(k_all.astype(BF16), wif_ref[GROUP_WIDTH:2 * GROUP_WIDTH, :])
                + _dot(v_all, wif_ref[2 * GROUP_WIDTH:3 * GROUP_WIDTH, :]))

    mx = [jnp.max(sc[hh], axis=-1, keepdims=True) for hh in heads]
    pm = [jnp.exp2(sc[hh] - mx[hh]) for hh in heads]
    ones_mem = jnp.ones((MEM_LEN, HEAD_DIM), BF16)
    oc = [_dot(pm[hh].astype(BF16),
               jnp.concatenate([kv_ref[0, :, GROUP_WIDTH + hh * HEAD_DIM:GROUP_WIDTH + (hh + 1) * HEAD_DIM],
                                ones_mem], axis=1)) for hh in heads]
    for hh in heads:
        yc_ref[0, :, _head(hh)] = (oc[hh][:, :HEAD_DIM] * (1.0 / oc[hh][:, HEAD_DIM:])
                                   * _silu(z_c[:, _head(hh)])).astype(BF16)

    row = lax.broadcasted_iota(jnp.int32, (L, L), 0)
    col = lax.broadcasted_iota(jnp.int32, (L, L), 1)
    causal = col <= row
    tril = jnp.where(causal, 1.0, 0.0).astype(BF16)
    chunks = range(MLSTM_STEP_CHUNKS)
    probs = [(c, hh) for c in chunks for hh in heads]
    rows = lambda c: slice(c * L, (c + 1) * L)
    q_of = lambda p: q_s[rows(p[0]), _head(p[1])]
    k_of = lambda p: k_s[rows(p[0]), _head(p[1])]
    v_of = lambda p: v_s[rows(p[0]), _head(p[1])]

    def each(f, side=False):
        if side:
            side_start()
        out = {p: f(p) for p in probs}
        if side:
            side_finish()
        return out

    a_cols, a_rows = [], []
    for c in chunks:
        g = g_s[rows(c), :]
        bb_all = _cumsum_rows(_log_sigmoid(g), tril)
        ac = jnp.where(col < N_HEADS, g, bb_all) * LOG2E
        a_cols.append(ac)
        a_rows.append(ac.T)
    ib_col = {(c, hh): a_cols[c][:, hh:hh + 1] for c, hh in probs}
    bb_col = {(c, hh): a_cols[c][:, N_HEADS + hh:N_HEADS + hh + 1] for c, hh in probs}
    ib_row = {(c, hh): a_rows[c][hh:hh + 1, :] for c, hh in probs}
    bb_row = {(c, hh): a_rows[c][N_HEADS + hh:N_HEADS + hh + 1, :] for c, hh in probs}
    b_end = {p: bb_col[p][L - 1:L, :] for p in probs}

    qk = {p: _dot_nt(q_of(p), k_of(p)) for p in probs}

    d = each(lambda p: jnp.where(causal, (bb_col[p] - bb_row[p]) + ib_row[p], -jnp.inf), side=True)
    r = each(lambda p: jnp.max(d[p], axis=-1, keepdims=True))
    s_loc = each(lambda p: qk[p] * jnp.exp2(d[p] - r[p]), side=True)
    rs = each(lambda p: jnp.sum(s_loc[p], axis=-1, keepdims=True))
    s_bf = each(lambda p: s_loc[p].astype(BF16))
    a = each(lambda p: (b_end[p] - bb_col[p]) + ib_col[p], side=True)
    a_max = each(lambda p: jnp.max(a[p], axis=0, keepdims=True))
    w_loc = each(lambda p: jnp.broadcast_to(jnp.exp2(a[p] - a_max[p]), (L, HEAD_DIM)))
    vw = each(lambda p: (v_of(p).astype(F32) * w_loc[p]).astype(BF16), side=True)
    nu = each(lambda p: jnp.sum(k_of(p).astype(F32) * w_loc[p], axis=0, keepdims=True))

    sv = {p: _dot(s_bf[p], v_of(p)) for p in probs}
    u = {p: lax.dot_general(vw[p], k_of(p), TN_DIMS, preferred_element_type=F32) for p in probs}

    m_in, c_in, n_in = {}, {}, {}
    for hh in heads:
        m_cur = m_ref[hh][:, 0:1]
        c_cur = c_ref[hh]
        n_cur = n_ref[hh]
        for c in chunks:
            p = (c, hh)
            m_in[p], c_in[p], n_in[p] = m_cur, c_cur.astype(BF16), n_cur
            m_new = jnp.maximum(b_end[p] + m_cur, a_max[p])
            decay = jnp.exp2(b_end[p] + m_cur - m_new)
            e_upd = jnp.exp2(a_max[p] - m_new)
            c_cur = decay * c_cur + e_upd * u[p]
            n_cur = decay * n_cur + e_upd * nu[p]
            m_cur = m_new
        c_ref[hh] = c_cur
        n_ref[hh] = n_cur
        m_ref[hh] = jnp.broadcast_to(m_cur, (1, HEAD_DIM))

    q_c = {p: _dot_nt(q_of(p), c_in[p]) for p in probs}

    tile_of = lambda ref, p: ref[rows(p[0]), _head(p[1])].astype(F32)
    g_col = each(lambda p: bb_col[p] + m_in[p])
    m_t = each(lambda p: jnp.maximum(g_col[p], r[p]))
    e_loc = each(lambda p: jnp.exp2(r[p] - m_t[p]))
    inter = each(lambda p: jnp.exp2(g_col[p] - m_t[p]))
    q_n = each(lambda p: jnp.sum(q_of(p).astype(F32) * n_in[p], axis=-1, keepdims=True))
    den = each(lambda p: e_loc[p] * rs[p] + inter[p] * q_n[p])
    scale = each(lambda p: 1.0 / jnp.maximum(jnp.abs(den[p]), jnp.exp2(-m_t[p])))
    w_intra = each(lambda p: e_loc[p] * scale[p])
    w_inter = each(lambda p: inter[p] * scale[p])
    hg = each(lambda p: _sigmoid(tile_of(om_s, p))
              * (w_intra[p] * sv[p] + w_inter[p] * q_c[p]), side=True)
    mu = each(lambda p: jnp.mean(hg[p], axis=-1, keepdims=True))
    cen = each(lambda p: hg[p] - mu[p], side=True)
    var = each(lambda p: jnp.mean(jnp.square(cen[p]), axis=-1, keepdims=True))
    assert not side_pieces and not pending
    for p in probs:
        sl = _head(p[1])
        y = cen[p] * lax.rsqrt(var[p] + NORM_EPS) * mg_ref[:, sl]
        y = (y + sk_ref[:, sl] * tile_of(xcv_s, p)) * _silu(tile_of(zm_s, p))
        ym_ref[0, rows(p[0]), sl] = y.astype(BF16)


def _front_call(x, kv, norm_g, w_in, conv_w, conv_b, wqk, wv, w_if, b_if, mnorm_g, skip_m):
    b, s, _ = x.shape
    tm = FRONT_ROWS
    ns = s // tm
    tok = lambda w: jax.ShapeDtypeStruct((b, s, w), BF16)
    tok_spec = lambda w: pl.BlockSpec((1, tm, w), lambda i, j: (i, j, 0))
    const = lambda shape: pl.BlockSpec(shape, lambda i, j: (0,) * len(shape))
    vdt_tiles = tm // V_TILE
    out_shape = (
        tok(GROUP_WIDTH),
        tok(GROUP_WIDTH), tok(GROUP_WIDTH),
        jax.ShapeDtypeStruct((b, ns * vdt_tiles, N_HEADS * VT_ROWS, V_TILE), BF16),
        tok(GROUP_WIDTH),
        tok(GROUP_WIDTH),
    )
    out_specs = (
        tok_spec(GROUP_WIDTH), tok_spec(GROUP_WIDTH), tok_spec(GROUP_WIDTH),
        pl.BlockSpec((1, vdt_tiles, N_HEADS * VT_ROWS, V_TILE), lambda i, j: (i, j, 0, 0)),
        tok_spec(GROUP_WIDTH), tok_spec(GROUP_WIDTH),
    )
    in_specs = [
        pl.BlockSpec((1, tm, D_MODEL), lambda i, j: (i, j, 0)),
        pl.BlockSpec((1, MEM_LEN, 2 * GROUP_WIDTH), lambda i, j: (i, 0, 0)),
        const((1, D_MODEL)),
        const((D_MODEL, IN_WIDTH)),
        const((CONV_WIDTH, GROUP_WIDTH)),
        const((1, GROUP_WIDTH)),
        const((N_HEADS, HEAD_DIM, 2 * HEAD_DIM)),
        const((N_HEADS, HEAD_DIM, HEAD_DIM)),
        const((3 * GROUP_WIDTH, HEAD_DIM)),
        const((1, HEAD_DIM)),
        const((1, GROUP_WIDTH)),
        const((1, GROUP_WIDTH)),
    ]
    tile_bf16 = pltpu.VMEM((tm, GROUP_WIDTH), BF16)
    return pl.pallas_call(
        _front_kernel,
        grid=(b, ns),
        in_specs=in_specs,
        out_specs=out_specs,
        out_shape=out_shape,
        scratch_shapes=[pltpu.VMEM((tm + CONV_HALO, GROUP_WIDTH), F32),
                        pltpu.VMEM((N_HEADS, HEAD_DIM, HEAD_DIM), F32),
                        pltpu.VMEM((N_HEADS, 1, HEAD_DIM), F32),
                        pltpu.VMEM((N_HEADS, 1, HEAD_DIM), F32),
                        tile_bf16, tile_bf16, tile_bf16,
                        pltpu.VMEM((tm, HEAD_DIM), F32),
                        tile_bf16, tile_bf16, tile_bf16],
        compiler_params=pltpu.CompilerParams(
            dimension_semantics=("arbitrary", "arbitrary"),
            vmem_limit_bytes=VMEM_LIMIT_BYTES),
        name="front",
    )(x, kv, norm_g, w_in, conv_w, conv_b, wqk, wv, w_if, b_if, mnorm_g, skip_m)


def _diff_kernel(q_ref, qn_ref, k_ref, vt_ref, zd_ref, g_ref, lq1_ref, lk1_ref, lq2_ref, lk2_ref,
                 x_ref, ym_ref, yc_ref, wo_ref, fg_ref, o_ref,
                 qs_ref, m_ref, l_ref, acc_ref, s0_ref, s1_ref, mx0_ref, mx1_ref, yd_ref, part_ref, qsn_ref):
    tq, tk, tv = DIFF_Q_ROWS, DIFF_KV_ROWS, V_TILE
    i = pl.program_id(1)
    feat = lax.broadcasted_iota(jnp.int32, (HEAD_DIM, tq), 0)
    for hh in range(N_HEADS):
        q_t = q_ref[0, :, _head(hh)].astype(F32).T
        qs_ref[hh, :, 0:tq] = jnp.where(feat < DIFF_QK_DIM, q_t, 0.0).astype(BF16)
        qs_ref[hh, :, tq:2 * tq] = jnp.where(feat >= DIFF_QK_DIM, q_t, 0.0).astype(BF16)
    m_ref[...] = jnp.full_like(m_ref, -jnp.inf)
    l_ref[...] = jnp.zeros_like(l_ref)
    acc_ref[...] = jnp.zeros_like(acc_ref)

    kc = DIFF_KEY_CHUNK
    sub = HEAD_DIM
    tri = (lax.broadcasted_iota(jnp.int32, (sub, sub), 0) <= lax.broadcasted_iota(jnp.int32, (sub, sub), 1))

    def mask_diagonal(s_c, c):
        hidden = jnp.full((sub, sub), -jnp.inf, F32)
        hidden_max = jnp.full((SUBLANES, sub), -jnp.inf, F32)
        row_blocks, maxes = [], None
        for r in range(kc // sub):
            kb = c * (kc // sub) + r
            blocks, row_max = [], []
            for b8 in range(2 * tq // sub):
                b = b8 % (tq // sub)
                if kb > b:
                    blocks.append(hidden)
                    row_max.append(hidden_max)
                    continue
                blk = s_c[r * sub:(r + 1) * sub, b8 * sub:(b8 + 1) * sub]
                if kb == b:
                    blk = jnp.where(tri, blk, -jnp.inf)
                blocks.append(blk)
                row_max.append(_fold_rows(blk, jnp.max))
            row_blocks.append(jnp.concatenate(blocks, axis=1))
            row_max = jnp.concatenate(row_max, axis=1)
            maxes = row_max if maxes is None else jnp.maximum(maxes, row_max)
        return jnp.concatenate(row_blocks, axis=0), maxes

    group_a = (tuple(range(0, N_HEADS // 2)), s0_ref, mx0_ref)
    group_b = (tuple(range(N_HEADS // 2, N_HEADS)), s1_ref, mx1_ref)

    def step(consume=None, produce=None):
        n_slots = N_HEADS // 2
        for slot in range(n_slots):
            if consume is not None:
                t_c, (heads_c, sc_ref, mxc_ref) = consume
                h_c = heads_c[slot]
                m_old = m_ref[h_c]
                m_new = jnp.maximum(m_old, mxc_ref[slot])
                alpha = jnp.exp2(m_old - m_new)
                pv, pb_prev = None, None
            if produce is not None:
                t_p, (heads_p, sp_ref, mxp_ref), diagonal = produce[:3]
                q_src = produce[3] if len(produce) > 3 else None
                h_p = heads_p[slot]
                mx = None
            for c in range(tk // kc):
                chunk = slice(c * kc, (c + 1) * kc)
                if produce is not None:
                    rows = pl.ds(pl.multiple_of(t_p * tk + c * kc, kc), kc)
                    q_mat = qs_ref[h_p] if q_src is None else q_src[slot]
                    s_c = _dot(k_ref[0, rows, _head(h_p)], q_mat)
                if consume is not None:
                    pb_c = jnp.exp2(sc_ref[slot, chunk, :] - m_new).astype(BF16)
                    pb_prev = pb_c if pb_prev is None else jnp.concatenate([pb_prev, pb_c], axis=0)
                    if c == tk // kc - 1:
                        v_rows = slice(h_c * VT_ROWS, (h_c + 1) * VT_ROWS)
                        vt = jnp.concatenate([vt_ref[0, t_c * (tk // tv) + j, v_rows, :]
                                              for j in range(tk // tv)], axis=1)
                        pv = _dot(vt, pb_prev)
                if produce is not None:
                    if diagonal:
                        s_c, c_max = mask_diagonal(s_c, c)
                    else:
                        c_max = _fold_rows(s_c, jnp.max)
                    sp_ref[slot, chunk, :] = s_c
                    mx = c_max if mx is None else jnp.maximum(mx, c_max)
            if consume is not None:
                l_ref[h_c] = alpha * l_ref[h_c] + pv[HEAD_DIM:HEAD_DIM + 1, :]
                acc_ref[h_c] = alpha * acc_ref[h_c] + pv[:HEAD_DIM, :]
                m_ref[h_c] = m_new
            if produce is not None:
                mxp_ref[slot] = jnp.max(mx, axis=0, keepdims=True)

    def tile(t, next_diagonal):
        step(consume=(t, group_a), produce=(t, group_b, False))
        step(consume=(t, group_b), produce=(t + 1, group_a, next_diagonal))

    def body(t, carry):
        tile(t, False)
        return carry

    @pl.when(i == 0)
    def _():
        step(produce=(0, group_a, True))

    @pl.when(i > 0)
    def _():
        lax.fori_loop(0, i - 1, body, 0)
        tile(i - 1, True)

    lam = (jnp.exp(jnp.sum(lq1_ref[...] * lk1_ref[...], axis=-1, keepdims=True))
           - jnp.exp(jnp.sum(lq2_ref[...] * lk2_ref[...], axis=-1, keepdims=True)) + LAM_INIT)

    def finish(heads):
        for hh in heads:
            sl = _head(hh)
            acc = acc_ref[hh]
            inv = 1.0 / l_ref[hh]
            o_t = acc[:, :tq] * inv[:, :tq] - lam * (acc[:, tq:] * inv[:, tq:])
            rstd = lax.rsqrt(jnp.mean(o_t * o_t, axis=0, keepdims=True) + NORM_EPS)
            o = (o_t * rstd * (g_ref[sl, :] * (1.0 - LAM_INIT))).T
            yd_ref[:, sl] = (o * _silu(zd_ref[0, :, sl].astype(F32))).astype(BF16)

    def project_rows(c, src_ref, w_rows, first):
        r_c = slice(c * HEAD_DIM, (c + 1) * HEAD_DIM)
        y = _dot(src_ref[0, r_c, :], wo_ref[w_rows, :])
        part_ref[r_c, :] = (x_ref[0, r_c, :] + y) if first else (part_ref[r_c, :] + y)

    def project_heads(heads):
        cols = slice(heads[0] * HEAD_DIM, (heads[-1] + 1) * HEAD_DIM)
        w_rows = slice(GROUP_WIDTH + cols.start, GROUP_WIDTH + cols.stop)
        return _dot(yd_ref[:, cols], wo_ref[w_rows, :])

    step(consume=(i, group_a), produce=(i, group_b, True))
    finish(group_a[0])

    last = pl.num_programs(1) - 1

    @pl.when(i < last)
    def _():
        for slot, hh in enumerate(group_a[0]):
            q_t = qn_ref[0, :, _head(hh)].astype(F32).T
            qsn_ref[slot, :, 0:tq] = jnp.where(feat < DIFF_QK_DIM, q_t, 0.0).astype(BF16)
            qsn_ref[slot, :, tq:2 * tq] = jnp.where(feat >= DIFF_QK_DIM, q_t, 0.0).astype(BF16)
        step(consume=(i, group_b), produce=(0, group_a, False, qsn_ref))

    @pl.when(i == last)
    def _():
        step(consume=(i, group_b))

    for c in range(tq // HEAD_DIM):
        project_rows(c, ym_ref, slice(0, GROUP_WIDTH), True)
    for c in range(tq // HEAD_DIM):
        project_rows(c, yc_ref, slice(2 * GROUP_WIDTH, 3 * GROUP_WIDTH), False)
    y_a = project_heads(group_a[0])
    finish(group_b[0])
    o_ref[0] = _rms(part_ref[...] + y_a + project_heads(group_b[0]), fg_ref[...])


def _diff_out_call(qd, kd, vdt, zd, dnorm_g, lq1, lk1, lq2, lk2, x, ym, yc, w_out, final_g):
    n_q = qd.shape[1] // DIFF_Q_ROWS
    b, s, _ = qd.shape
    tq = DIFF_Q_ROWS
    nkv = vdt.shape[1]
    const = lambda shape: pl.BlockSpec(shape, lambda i, j: (0,) * len(shape))
    tok_spec = lambda w: pl.BlockSpec((1, tq, w), lambda i, j: (i, j, 0))
    return pl.pallas_call(
        _diff_kernel,
        grid=(b, s // tq),
        in_specs=[
            tok_spec(GROUP_WIDTH),
            pl.BlockSpec((1, tq, GROUP_WIDTH), lambda i, j: (i, jnp.minimum(j + 1, n_q - 1), 0)),
            pl.BlockSpec((1, s, GROUP_WIDTH), lambda i, j: (i, 0, 0)),
            pl.BlockSpec((1, nkv, N_HEADS * VT_ROWS, V_TILE), lambda i, j: (i, 0, 0, 0)),
            tok_spec(GROUP_WIDTH),
            const((GROUP_WIDTH, 1)),
            const((1, DIFF_QK_DIM)), const((1, DIFF_QK_DIM)), const((1, DIFF_QK_DIM)), const((1, DIFF_QK_DIM)),
            tok_spec(D_MODEL), tok_spec(GROUP_WIDTH), tok_spec(GROUP_WIDTH),
            const((3 * GROUP_WIDTH, D_MODEL)), const((1, D_MODEL)),
        ],
        out_specs=tok_spec(D_MODEL),
        out_shape=jax.ShapeDtypeStruct((b, s, D_MODEL), F32),
        scratch_shapes=[pltpu.VMEM((N_HEADS, HEAD_DIM, 2 * tq), BF16),
                        pltpu.VMEM((N_HEADS, 1, 2 * tq), F32),
                        pltpu.VMEM((N_HEADS, 1, 2 * tq), F32),
                        pltpu.VMEM((N_HEADS, HEAD_DIM, 2 * tq), F32),
                        pltpu.VMEM((N_HEADS // 2, DIFF_KV_ROWS, 2 * tq), F32),
                        pltpu.VMEM((N_HEADS // 2, DIFF_KV_ROWS, 2 * tq), F32),
                        pltpu.VMEM((N_HEADS // 2, 1, 2 * tq), F32),
                        pltpu.VMEM((N_HEADS // 2, 1, 2 * tq), F32),
                        pltpu.VMEM((tq, GROUP_WIDTH), BF16),
                        pltpu.VMEM((tq, D_MODEL), F32),
                        pltpu.VMEM((N_HEADS // 2, HEAD_DIM, 2 * tq), BF16)],
        compiler_params=pltpu.CompilerParams(
            dimension_semantics=("arbitrary", "arbitrary"),
            vmem_limit_bytes=VMEM_LIMIT_BYTES),
        name="diffattn_out",
    )(qd, qd, kd, vdt, zd, dnorm_g, lq1, lk1, lq2, lk2, x, ym, yc, w_out, final_g)


def kernel(x, mem, norm_g, w_in, conv_w, conv_b, wq_m, wk_m, wv_m, w_if, b_if, mnorm_g, skip_m,
           lam_q1, lam_k1, lam_q2, lam_k2, dnorm_g, mem_norm_g, w_mem_kv, w_out, final_g):
    b, s, d = x.shape
    assert (d, s % DIFF_KV_ROWS, DIFF_KV_ROWS % V_TILE, FRONT_ROWS % V_TILE) == (D_MODEL, 0, 0, 0)
    assert (s % FRONT_ROWS, DIFF_KV_ROWS % DIFF_KEY_CHUNK) == (0, 0)
    assert DIFF_Q_ROWS == DIFF_KV_ROWS and DIFF_KEY_CHUNK % HEAD_DIM == 0
    assert norm_g.shape[0] == 1, "single-layer kernel"
    l = 0
    w_if_pad = jnp.pad(w_if[l], ((0, 0), (0, HEAD_DIM - w_if.shape[-1]))).astype(BF16)
    b_if_pad = jnp.pad(b_if[l], (0, HEAD_DIM - b_if.shape[-1]))[None, :]

    kv = _memkv_call(mem, mem_norm_g[l][None, :], w_mem_kv[l].astype(BF16))
    ym, qd, kd, vdt, zd, yc = _front_call(
        x, kv, norm_g[l][None, :], w_in[l].astype(BF16), conv_w[l], conv_b[l][None, :],
        jnp.concatenate([wq_m[l], wk_m[l]], axis=-1).astype(BF16), wv_m[l].astype(BF16), w_if_pad, b_if_pad,
        mnorm_g[l][None, :], skip_m[l][None, :])
    return _diff_out_call(qd, kd, vdt, zd, dnorm_g[l][:, None],
                          lam_q1[l][None, :], lam_k1[l][None, :], lam_q2[l][None, :], lam_k2[l][None, :],
                          x, ym, yc, w_out[l].astype(BF16), final_g[None, :])
```

```python
import math

import jax
import jax.numpy as jnp
from jax import lax
from jax.experimental import pallas as pl
from jax.experimental.pallas import tpu as pltpu

F32 = jnp.float32
BF16 = jnp.bfloat16

D_MODEL = 1024
N_HEADS = 4
HEAD_DIM = 128
SUBLANES = 8
V7X_VMEM_BYTES = 64 * 1024 * 1024
VMEM_LIMIT_BYTES = V7X_VMEM_BYTES - 6 * 1024 * 1024
GROUP_WIDTH = N_HEADS * HEAD_DIM
DIFF_QK_DIM = 64
CONV_WIDTH = 4
MLSTM_CHUNK = 128
MLSTM_STEP_CHUNKS = 4
MEM_LEN = 256
IN_WIDTH = 9 * GROUP_WIDTH
NORM_EPS = 1e-6
LAM_INIT = 0.8 - 0.6 * math.exp(-0.3 * 0)
LOG2E = math.log2(math.e)

OFF_XM, OFF_OM, OFF_ZM, OFF_QD, OFF_KD, OFF_VD, OFF_ZD, OFF_QC, OFF_ZC = (
    i * GROUP_WIDTH for i in range(9))

FRONT_ROWS = MLSTM_STEP_CHUNKS * MLSTM_CHUNK
V_TILE = 256
VT_PAD = 16
VT_ROWS = HEAD_DIM + VT_PAD
DIFF_Q_ROWS = 512
DIFF_KV_ROWS = 512
DIFF_KEY_CHUNK = 256
CONV_HALO = 8

NT_DIMS = (((1,), (1,)), ((), ()))
TN_DIMS = (((0,), (0,)), ((), ()))


def _dot(a, b):
    return jnp.dot(a, b, preferred_element_type=F32)


def _dot_nt(a, b):
    return lax.dot_general(a, b, NT_DIMS, preferred_element_type=F32)


def _sigmoid(x):
    return 0.5 * jnp.tanh(0.5 * x) + 0.5


def _silu(x):
    return x * _sigmoid(x)


def _rms(x, g):
    return x * lax.rsqrt(jnp.mean(x * x, axis=-1, keepdims=True) + NORM_EPS) * g


def _head(h):
    return slice(h * HEAD_DIM, (h + 1) * HEAD_DIM)


def _fold_rows(x, op):
    r, n = x.shape
    return op(x.reshape(r // SUBLANES, SUBLANES, n), axis=0)


def _log_sigmoid(x):
    return jnp.minimum(x, 0.0) - jnp.log1p(jnp.exp(-jnp.abs(x)))


def _cumsum_rows(f, tril):
    hi = f.astype(BF16)
    r1 = f - hi.astype(F32)
    mid = r1.astype(BF16)
    lo = (r1 - mid.astype(F32)).astype(BF16)
    return _dot(tril, hi) + _dot(tril, mid) + _dot(tril, lo)


def _memkv_kernel(mem_ref, g_ref, w_ref, kv_ref):
    y = _rms(mem_ref[0], g_ref[...])
    kv_ref[0] = _dot(y.astype(BF16), w_ref[...]).astype(BF16)


def _memkv_call(mem, g, w):
    b = mem.shape[0]
    return pl.pallas_call(
        _memkv_kernel,
        grid=(b,),
        in_specs=[
            pl.BlockSpec((1, MEM_LEN, D_MODEL), lambda i: (i, 0, 0)),
            pl.BlockSpec((1, D_MODEL), lambda i: (0, 0)),
            pl.BlockSpec((D_MODEL, 2 * GROUP_WIDTH), lambda i: (0, 0)),
        ],
        out_specs=pl.BlockSpec((1, MEM_LEN, 2 * GROUP_WIDTH), lambda i: (i, 0, 0)),
        out_shape=jax.ShapeDtypeStruct((b, MEM_LEN, 2 * GROUP_WIDTH), BF16),
        compiler_params=pltpu.CompilerParams(dimension_semantics=("arbitrary",)),
        name="memkv",
    )(mem, g, w)


def _front_kernel(x_ref, kv_ref, ng_ref, win_ref, cw_ref, cb_ref, wqk_ref, wv_ref, wif_ref, bif_ref,
                  mg_ref, sk_ref,
                  ym_ref, qd_ref, kd_ref, vdt_ref, zd_ref, yc_ref,
                  conv_ref, c_ref, n_ref, m_ref, q_s, k_s, v_s, g_s, om_s, zm_s, xcv_s):
    tm = FRONT_ROWS
    L = MLSTM_CHUNK
    heads = range(N_HEADS)

    @pl.when(pl.program_id(1) == 0)
    def _():
        conv_ref[tm:tm + CONV_HALO, :] = jnp.zeros((CONV_HALO, GROUP_WIDTH), F32)
        c_ref[...] = jnp.zeros_like(c_ref)
        n_ref[...] = jnp.zeros_like(n_ref)
        m_ref[...] = jnp.zeros_like(m_ref)

    h = _rms(x_ref[0], ng_ref[...]).astype(BF16)

    def proj(off):
        return _dot(h, win_ref[:, off:off + GROUP_WIDTH])

    def put_vdt(val):
        v_t = val.T.astype(BF16)
        pad_row = lax.broadcasted_iota(jnp.int32, (VT_PAD, tm), 0)
        pad = jnp.where(pad_row == 0, 1.0, 0.0).astype(BF16)
        v_aug = jnp.concatenate([blk for hh in heads for blk in (v_t[_head(hh), :], pad)], axis=0)
        for j in range(tm // V_TILE):
            vdt_ref[0, j] = v_aug[:, j * V_TILE:(j + 1) * V_TILE]

    def put(ref, scale=None):
        def sink(val):
            ref[...] = (val if scale is None else val * scale).astype(BF16).reshape(ref.shape)
        return sink

    side_pieces = [(OFF_OM, put(om_s)), (OFF_ZM, put(zm_s)),
                   (OFF_QD, put(qd_ref, DIFF_QK_DIM ** -0.5 * LOG2E)), (OFF_KD, put(kd_ref)),
                   (OFF_VD, put_vdt), (OFF_ZD, put(zd_ref))]
    pending = []

    def side_start():
        if side_pieces:
            off, sink = side_pieces.pop(0)
            pending.append((proj(off), sink))

    def side_finish():
        if pending:
            val, sink = pending.pop(0)
            sink(val)

    x_m = proj(OFF_XM)
    q_cb = proj(OFF_QC).astype(BF16)
    z_c = proj(OFF_ZC)

    conv_ref[0:CONV_HALO, :] = conv_ref[tm:tm + CONV_HALO, :]
    conv_ref[CONV_HALO:CONV_HALO + tm, :] = x_m
    acc = jnp.broadcast_to(cb_ref[...], (tm, GROUP_WIDTH))
    for j in range(CONV_WIDTH):
        start = CONV_HALO - (CONV_WIDTH - 1) + j
        acc = acc + cw_ref[j:j + 1, :] * conv_ref[start:start + tm, :]
    sc = [_dot_nt(q_cb[:, _head(hh)], kv_ref[0, :, _head(hh)]) * (HEAD_DIM ** -0.5 * LOG2E)
          for hh in heads]
    x_cv = _silu(acc)
    xcv_b = x_cv.astype(BF16)
    xcv_s[...] = xcv_b
    xm_b = x_m.astype(BF16)

    qk_p = [_dot(xcv_b[:, _head(hh)], wqk_ref[hh]) for hh in heads]
    vv_p = [_dot(xm_b[:, _head(hh)], wv_ref[hh]) for hh in heads]

    q_all = jnp.concatenate([qk_p[hh][:, :HEAD_DIM] for hh in heads], axis=1).astype(BF16)
    k_all = jnp.concatenate([qk_p[hh][:, HEAD_DIM:] for hh in heads], axis=1)
    v_all = jnp.concatenate(vv_p, axis=1).astype(BF16)
    q_s[...] = q_all
    k_s[...] = (k_all * HEAD_DIM ** -0.5).astype(BF16)
    v_s[...] = v_all
    g_s[...] = (jnp.broadcast_to(bif_ref[...], (tm, HEAD_DIM))
                + _dot(q_all, wif_ref[0:GROUP_WIDTH, :])
                + _dot(k_all.astype(BF16), wif_ref[GROUP_WIDTH:2 * GROUP_WIDTH, :])
                + _dot(v_all, wif_ref[2 * GROUP_WIDTH:3 * GROUP_WIDTH, :]))

    mx = [jnp.max(sc[hh], axis=-1, keepdims=True) for hh in heads]
    pm = [jnp.exp2(sc[hh] - mx[hh]) for hh in heads]
    inv = [1.0 / jnp.sum(pm[hh], axis=-1, keepdims=True) for hh in heads]
    pc = [(pm[hh] * inv[hh]).astype(BF16) for hh in heads]
    oc = [_dot(pc[hh], kv_ref[0, :, GROUP_WIDTH + hh * HEAD_DIM:GROUP_WIDTH + (hh + 1) * HEAD_DIM]) for hh in heads]
    for hh in heads:
        yc_ref[0, :, _head(hh)] = (oc[hh] * _silu(z_c[:, _head(hh)])).astype(BF16)

    row = lax.broadcasted_iota(jnp.int32, (L, L), 0)
    col = lax.broadcasted_iota(jnp.int32, (L, L), 1)
    causal = col <= row
    tril = jnp.where(causal, 1.0, 0.0).astype(BF16)
    chunks = range(MLSTM_STEP_CHUNKS)
    probs = [(c, hh) for c in chunks for hh in heads]
    rows = lambda c: slice(c * L, (c + 1) * L)
    q_of = lambda p: q_s[rows(p[0]), _head(p[1])]
    k_of = lambda p: k_s[rows(p[0]), _head(p[1])]
    v_of = lambda p: v_s[rows(p[0]), _head(p[1])]

    def each(f, side=False):
        if side:
            side_start()
        out = {p: f(p) for p in probs}
        if side:
            side_finish()
        return out

    a_cols, a_rows = [], []
    for c in chunks:
        g = g_s[rows(c), :]
        bb_all = _cumsum_rows(_log_sigmoid(g), tril)
        ac = jnp.where(col < N_HEADS, g, bb_all) * LOG2E
        a_cols.append(ac)
        a_rows.append(ac.T)
    ib_col = {(c, hh): a_cols[c][:, hh:hh + 1] for c, hh in probs}
    bb_col = {(c, hh): a_cols[c][:, N_HEADS + hh:N_HEADS + hh + 1] for c, hh in probs}
    ib_row = {(c, hh): a_rows[c][hh:hh + 1, :] for c, hh in probs}
    bb_row = {(c, hh): a_rows[c][N_HEADS + hh:N_HEADS + hh + 1, :] for c, hh in probs}
    b_end = {p: bb_col[p][L - 1:L, :] for p in probs}

    qk = {p: _dot_nt(q_of(p), k_of(p)) for p in probs}

    d = each(lambda p: jnp.where(causal, (bb_col[p] - bb_row[p]) + ib_row[p], -jnp.inf), side=True)
    r = each(lambda p: jnp.max(d[p], axis=-1, keepdims=True))
    s_loc = each(lambda p: qk[p] * jnp.exp2(d[p] - r[p]), side=True)
    rs = each(lambda p: jnp.sum(s_loc[p], axis=-1, keepdims=True))
    s_bf = each(lambda p: s_loc[p].astype(BF16))
    a = each(lambda p: (b_end[p] - bb_col[p]) + ib_col[p], side=True)
    a_max = each(lambda p: jnp.max(a[p], axis=0, keepdims=True))
    w_loc = each(lambda p: jnp.broadcast_to(jnp.exp2(a[p] - a_max[p]), (L, HEAD_DIM)))
    vw = each(lambda p: (v_of(p).astype(F32) * w_loc[p]).astype(BF16), side=True)
    nu = each(lambda p: jnp.sum(k_of(p).astype(F32) * w_loc[p], axis=0, keepdims=True))

    sv = {p: _dot(s_bf[p], v_of(p)) for p in probs}
    u = {p: lax.dot_general(vw[p], k_of(p), TN_DIMS, preferred_element_type=F32) for p in probs}

    m_in, c_in, n_in = {}, {}, {}
    for hh in heads:
        m_cur = m_ref[hh][:, 0:1]
        c_cur = c_ref[hh]
        n_cur = n_ref[hh]
        for c in chunks:
            p = (c, hh)
            m_in[p], c_in[p], n_in[p] = m_cur, c_cur.astype(BF16), n_cur
            m_new = jnp.maximum(b_end[p] + m_cur, a_max[p])
            decay = jnp.exp2(b_end[p] + m_cur - m_new)
            e_upd = jnp.exp2(a_max[p] - m_new)
            c_cur = decay * c_cur + e_upd * u[p]
            n_cur = decay * n_cur + e_upd * nu[p]
            m_cur = m_new
        c_ref[hh] = c_cur
        n_ref[hh] = n_cur
        m_ref[hh] = jnp.broadcast_to(m_cur, (1, HEAD_DIM))

    q_c = {p: _dot_nt(q_of(p), c_in[p]) for p in probs}

    tile_of = lambda ref, p: ref[rows(p[0]), _head(p[1])].astype(F32)
    g_col = each(lambda p: bb_col[p] + m_in[p])
    m_t = each(lambda p: jnp.maximum(g_col[p], r[p]))
    e_loc = each(lambda p: jnp.exp2(r[p] - m_t[p]))
    inter = each(lambda p: jnp.exp2(g_col[p] - m_t[p]))
    q_n = each(lambda p: jnp.sum(q_of(p).astype(F32) * n_in[p], axis=-1, keepdims=True))
    den = each(lambda p: e_loc[p] * rs[p] + inter[p] * q_n[p])
    scale = each(lambda p: 1.0 / jnp.maximum(jnp.abs(den[p]), jnp.exp2(-m_t[p])))
    w_intra = each(lambda p: e_loc[p] * scale[p])
    w_inter = each(lambda p: inter[p] * scale[p])
    hg = each(lambda p: _sigmoid(tile_of(om_s, p))
              * (w_intra[p] * sv[p] + w_inter[p] * q_c[p]), side=True)
    mu = each(lambda p: jnp.mean(hg[p], axis=-1, keepdims=True))
    cen = each(lambda p: hg[p] - mu[p], side=True)
    var = each(lambda p: jnp.mean(jnp.square(cen[p]), axis=-1, keepdims=True))
    assert not side_pieces and not pending
    for p in probs:
        sl = _head(p[1])
        y = cen[p] * lax.rsqrt(var[p] + NORM_EPS) * mg_ref[:, sl]
        y = (y + sk_ref[:, sl] * tile_of(xcv_s, p)) * _silu(tile_of(zm_s, p))
        ym_ref[0, rows(p[0]), sl] = y.astype(BF16)


def _front_call(x, kv, norm_g, w_in, conv_w, conv_b, wqk, wv, w_if, b_if, mnorm_g, skip_m):
    b, s, _ = x.shape
    tm = FRONT_ROWS
    ns = s // tm
    tok = lambda w: jax.ShapeDtypeStruct((b, s, w), BF16)
    tok_spec = lambda w: pl.BlockSpec((1, tm, w), lambda i, j: (i, j, 0))
    const = lambda shape: pl.BlockSpec(shape, lambda i, j: (0,) * len(shape))
    vdt_tiles = tm // V_TILE
    out_shape = (
        tok(GROUP_WIDTH),
        tok(GROUP_WIDTH), tok(GROUP_WIDTH),
        jax.ShapeDtypeStruct((b, ns * vdt_tiles, N_HEADS * VT_ROWS, V_TILE), BF16),
        tok(GROUP_WIDTH),
        tok(GROUP_WIDTH),
    )
    out_specs = (
        tok_spec(GROUP_WIDTH), tok_spec(GROUP_WIDTH), tok_spec(GROUP_WIDTH),
        pl.BlockSpec((1, vdt_tiles, N_HEADS * VT_ROWS, V_TILE), lambda i, j: (i, j, 0, 0)),
        tok_spec(GROUP_WIDTH), tok_spec(GROUP_WIDTH),
    )
    in_specs = [
        pl.BlockSpec((1, tm, D_MODEL), lambda i, j: (i, j, 0)),
        pl.BlockSpec((1, MEM_LEN, 2 * GROUP_WIDTH), lambda i, j: (i, 0, 0)),
        const((1, D_MODEL)),
        const((D_MODEL, IN_WIDTH)),
        const((CONV_WIDTH, GROUP_WIDTH)),
        const((1, GROUP_WIDTH)),
        const((N_HEADS, HEAD_DIM, 2 * HEAD_DIM)),
        const((N_HEADS, HEAD_DIM, HEAD_DIM)),
        const((3 * GROUP_WIDTH, HEAD_DIM)),
        const((1, HEAD_DIM)),
        const((1, GROUP_WIDTH)),
        const((1, GROUP_WIDTH)),
    ]
    tile_bf16 = pltpu.VMEM((tm, GROUP_WIDTH), BF16)
    return pl.pallas_call(
        _front_kernel,
        grid=(b, ns),
        in_specs=in_specs,
        out_specs=out_specs,
        out_shape=out_shape,
        scratch_shapes=[pltpu.VMEM((tm + CONV_HALO, GROUP_WIDTH), F32),
                        pltpu.VMEM((N_HEADS, HEAD_DIM, HEAD_DIM), F32),
                        pltpu.VMEM((N_HEADS, 1, HEAD_DIM), F32),
                        pltpu.VMEM((N_HEADS, 1, HEAD_DIM), F32),
                        tile_bf16, tile_bf16, tile_bf16,
                        pltpu.VMEM((tm, HEAD_DIM), F32),
                        tile_bf16, tile_bf16, tile_bf16],
        compiler_params=pltpu.CompilerParams(
            dimension_semantics=("arbitrary", "arbitrary"),
            vmem_limit_bytes=VMEM_LIMIT_BYTES),
        name="front",
    )(x, kv, norm_g, w_in, conv_w, conv_b, wqk, wv, w_if, b_if, mnorm_g, skip_m)


def _diff_kernel(q_ref, qn_ref, k_ref, vt_ref, zd_ref, g_ref, lq1_ref, lk1_ref, lq2_ref, lk2_ref,
                 x_ref, ym_ref, yc_ref, wo_ref, fg_ref, o_ref,
                 qs_ref, m_ref, l_ref, acc_ref, s0_ref, s1_ref, mx0_ref, mx1_ref, yd_ref, part_ref, qsn_ref):
    tq, tk, tv = DIFF_Q_ROWS, DIFF_KV_ROWS, V_TILE
    i = pl.program_id(1)
    feat = lax.broadcasted_iota(jnp.int32, (HEAD_DIM, tq), 0)
    for hh in range(N_HEADS):
        q_t = q_ref[0, :, _head(hh)].astype(F32).T
        qs_ref[hh, :, 0:tq] = jnp.where(feat < DIFF_QK_DIM, q_t, 0.0).astype(BF16)
        qs_ref[hh, :, tq:2 * tq] = jnp.where(feat >= DIFF_QK_DIM, q_t, 0.0).astype(BF16)
    m_ref[...] = jnp.full_like(m_ref, -jnp.inf)
    l_ref[...] = jnp.zeros_like(l_ref)
    acc_ref[...] = jnp.zeros_like(acc_ref)

    kc = DIFF_KEY_CHUNK
    sub = HEAD_DIM
    tri = (lax.broadcasted_iota(jnp.int32, (sub, sub), 0) <= lax.broadcasted_iota(jnp.int32, (sub, sub), 1))

    def mask_diagonal(s_c, c):
        hidden = jnp.full((sub, sub), -jnp.inf, F32)
        hidden_max = jnp.full((SUBLANES, sub), -jnp.inf, F32)
        row_blocks, maxes = [], None
        for r in range(kc // sub):
            kb = c * (kc // sub) + r
            blocks, row_max = [], []
            for b8 in range(2 * tq // sub):
                b = b8 % (tq // sub)
                if kb > b:
                    blocks.append(hidden)
                    row_max.append(hidden_max)
                    continue
                blk = s_c[r * sub:(r + 1) * sub, b8 * sub:(b8 + 1) * sub]
                if kb == b:
                    blk = jnp.where(tri, blk, -jnp.inf)
                blocks.append(blk)
                row_max.append(_fold_rows(blk, jnp.max))
            row_blocks.append(jnp.concatenate(blocks, axis=1))
            row_max = jnp.concatenate(row_max, axis=1)
            maxes = row_max if maxes is None else jnp.maximum(maxes, row_max)
        return jnp.concatenate(row_blocks, axis=0), maxes

    group_a = (tuple(range(0, N_HEADS // 2)), s0_ref, mx0_ref)
    group_b = (tuple(range(N_HEADS // 2, N_HEADS)), s1_ref, mx1_ref)

    def step(consume=None, produce=None):
        n_slots = N_HEADS // 2
        for slot in range(n_slots):
            if consume is not None:
                t_c, (heads_c, sc_ref, mxc_ref) = consume
                h_c = heads_c[slot]
                m_old = m_ref[h_c]
                m_new = jnp.maximum(m_old, mxc_ref[slot])
                alpha = jnp.exp2(m_old - m_new)
                pv, pb_prev = None, None
            if produce is not None:
                t_p, (heads_p, sp_ref, mxp_ref), diagonal = produce[:3]
                q_src = produce[3] if len(produce) > 3 else None
                h_p = heads_p[slot]
                mx = None
            for c in range(tk // kc):
                chunk = slice(c * kc, (c + 1) * kc)
                if produce is not None:
                    rows = pl.ds(pl.multiple_of(t_p * tk + c * kc, kc), kc)
                    q_mat = qs_ref[h_p] if q_src is None else q_src[slot]
                    s_c = _dot(k_ref[0, rows, _head(h_p)], q_mat)
                if consume is not None:
                    pb_c = jnp.exp2(sc_ref[slot, chunk, :] - m_new).astype(BF16)
                    pb_prev = pb_c if pb_prev is None else jnp.concatenate([pb_prev, pb_c], axis=0)
                    if c == tk // kc - 1:
                        v_rows = slice(h_c * VT_ROWS, (h_c + 1) * VT_ROWS)
                        vt = jnp.concatenate([vt_ref[0, t_c * (tk // tv) + j, v_rows, :]
                                              for j in range(tk // tv)], axis=1)
                        pv = _dot(vt, pb_prev)
                if produce is not None:
                    if diagonal:
                        s_c, c_max = mask_diagonal(s_c, c)
                    else:
                        c_max = _fold_rows(s_c, jnp.max)
                    sp_ref[slot, chunk, :] = s_c
                    mx = c_max if mx is None else jnp.maximum(mx, c_max)
            if consume is not None:
                l_ref[h_c] = alpha * l_ref[h_c] + pv[HEAD_DIM:HEAD_DIM + 1, :]
                acc_ref[h_c] = alpha * acc_ref[h_c] + pv[:HEAD_DIM, :]
                m_ref[h_c] = m_new
            if produce is not None:
                mxp_ref[slot] = jnp.max(mx, axis=0, keepdims=True)

    def tile(t, next_diagonal):
        step(consume=(t, group_a), produce=(t, group_b, False))
        step(consume=(t, group_b), produce=(t + 1, group_a, next_diagonal))

    def body(t, carry):
        tile(t, False)
        return carry

    @pl.when(i == 0)
    def _():
        step(produce=(0, group_a, True))

    @pl.when(i > 0)
    def _():
        lax.fori_loop(0, i - 1, body, 0)
        tile(i - 1, True)

    lam = (jnp.exp(jnp.sum(lq1_ref[...] * lk1_ref[...], axis=-1, keepdims=True))
           - jnp.exp(jnp.sum(lq2_ref[...] * lk2_ref[...], axis=-1, keepdims=True)) + LAM_INIT)

    def finish(heads):
        for hh in heads:
            sl = _head(hh)
            acc = acc_ref[hh]
            inv = 1.0 / l_ref[hh]
            o_t = acc[:, :tq] * inv[:, :tq] - lam * (acc[:, tq:] * inv[:, tq:])
            rstd = lax.rsqrt(jnp.mean(o_t * o_t, axis=0, keepdims=True) + NORM_EPS)
            o = (o_t * rstd * (g_ref[sl, :] * (1.0 - LAM_INIT))).T
            yd_ref[:, sl] = (o * _silu(zd_ref[0, :, sl].astype(F32))).astype(BF16)

    def project_rows(c, src_ref, w_rows, first):
        r_c = slice(c * HEAD_DIM, (c + 1) * HEAD_DIM)
        y = _dot(src_ref[0, r_c, :], wo_ref[w_rows, :])
        part_ref[r_c, :] = (x_ref[0, r_c, :] + y) if first else (part_ref[r_c, :] + y)

    def project_heads(heads):
        cols = slice(heads[0] * HEAD_DIM, (heads[-1] + 1) * HEAD_DIM)
        w_rows = slice(GROUP_WIDTH + cols.start, GROUP_WIDTH + cols.stop)
        return _dot(yd_ref[:, cols], wo_ref[w_rows, :])

    step(consume=(i, group_a), produce=(i, group_b, True))
    finish(group_a[0])

    last = pl.num_programs(1) - 1

    @pl.when(i < last)
    def _():
        for slot, hh in enumerate(group_a[0]):
            q_t = qn_ref[0, :, _head(hh)].astype(F32).T
            qsn_ref[slot, :, 0:tq] = jnp.where(feat < DIFF_QK_DIM, q_t, 0.0).astype(BF16)
            qsn_ref[slot, :, tq:2 * tq] = jnp.where(feat >= DIFF_QK_DIM, q_t, 0.0).astype(BF16)
        step(consume=(i, group_b), produce=(0, group_a, False, qsn_ref))

    @pl.when(i == last)
    def _():
        step(consume=(i, group_b))

    for c in range(tq // HEAD_DIM):
        project_rows(c, ym_ref, slice(0, GROUP_WIDTH), True)
    for c in range(tq // HEAD_DIM):
        project_rows(c, yc_ref, slice(2 * GROUP_WIDTH, 3 * GROUP_WIDTH), False)
    y_a = project_heads(group_a[0])
    finish(group_b[0])
    o_ref[0] = _rms(part_ref[...] + y_a + project_heads(group_b[0]), fg_ref[...])


def _diff_out_call(qd, kd, vdt, zd, dnorm_g, lq1, lk1, lq2, lk2, x, ym, yc, w_out, final_g):
    n_q = qd.shape[1] // DIFF_Q_ROWS
    b, s, _ = qd.shape
    tq = DIFF_Q_ROWS
    nkv = vdt.shape[1]
    const = lambda shape: pl.BlockSpec(shape, lambda i, j: (0,) * len(shape))
    tok_spec = lambda w: pl.BlockSpec((1, tq, w), lambda i, j: (i, j, 0))
    return pl.pallas_call(
        _diff_kernel,
        grid=(b, s // tq),
        in_specs=[
            tok_spec(GROUP_WIDTH),
            pl.BlockSpec((1, tq, GROUP_WIDTH), lambda i, j: (i, jnp.minimum(j + 1, n_q - 1), 0)),
            pl.BlockSpec((1, s, GROUP_WIDTH), lambda i, j: (i, 0, 0)),
            pl.BlockSpec((1, nkv, N_HEADS * VT_ROWS, V_TILE), lambda i, j: (i, 0, 0, 0)),
            tok_spec(GROUP_WIDTH),
            const((GROUP_WIDTH, 1)),
            const((1, DIFF_QK_DIM)), const((1, DIFF_QK_DIM)), const((1, DIFF_QK_DIM)), const((1, DIFF_QK_DIM)),
            tok_spec(D_MODEL), tok_spec(GROUP_WIDTH), tok_spec(GROUP_WIDTH),
            const((3 * GROUP_WIDTH, D_MODEL)), const((1, D_MODEL)),
        ],
        out_specs=tok_spec(D_MODEL),
        out_shape=jax.ShapeDtypeStruct((b, s, D_MODEL), F32),
        scratch_shapes=[pltpu.VMEM((N_HEADS, HEAD_DIM, 2 * tq), BF16),
                        pltpu.VMEM((N_HEADS, 1, 2 * tq), F32),
                        pltpu.VMEM((N_HEADS, 1, 2 * tq), F32),
                        pltpu.VMEM((N_HEADS, HEAD_DIM, 2 * tq), F32),
                        pltpu.VMEM((N_HEADS // 2, DIFF_KV_ROWS, 2 * tq), F32),
                        pltpu.VMEM((N_HEADS // 2, DIFF_KV_ROWS, 2 * tq), F32),
                        pltpu.VMEM((N_HEADS // 2, 1, 2 * tq), F32),
                        pltpu.VMEM((N_HEADS // 2, 1, 2 * tq), F32),
                        pltpu.VMEM((tq, GROUP_WIDTH), BF16),
                        pltpu.VMEM((tq, D_MODEL), F32),
                        pltpu.VMEM((N_HEADS // 2, HEAD_DIM, 2 * tq), BF16)],
        compiler_params=pltpu.CompilerParams(
            dimension_semantics=("arbitrary", "arbitrary"),
            vmem_limit_bytes=VMEM_LIMIT_BYTES),
        name="diffattn_out",
    )(qd, qd, kd, vdt, zd, dnorm_g, lq1, lk1, lq2, lk2, x, ym, yc, w_out, final_g)


def kernel(x, mem, norm_g, w_in, conv_w, conv_b, wq_m, wk_m, wv_m, w_if, b_if, mnorm_g, skip_m,
           lam_q1, lam_k1, lam_q2, lam_k2, dnorm_g, mem_norm_g, w_mem_kv, w_out, final_g):
    b, s, d = x.shape
    assert (d, s % DIFF_KV_ROWS, DIFF_KV_ROWS % V_TILE, FRONT_ROWS % V_TILE) == (D_MODEL, 0, 0, 0)
    assert (s % FRONT_ROWS, DIFF_KV_ROWS % DIFF_KEY_CHUNK) == (0, 0)
    assert DIFF_Q_ROWS == DIFF_KV_ROWS and DIFF_KEY_CHUNK % HEAD_DIM == 0
    assert norm_g.shape[0] == 1, "single-layer kernel"
    l = 0
    w_if_pad = jnp.pad(w_if[l], ((0, 0), (0, HEAD_DIM - w_if.shape[-1]))).astype(BF16)
    b_if_pad = jnp.pad(b_if[l], (0, HEAD_DIM - b_if.shape[-1]))[None, :]

    kv = _memkv_call(mem, mem_norm_g[l][None, :], w_mem_kv[l].astype(BF16))
    ym, qd, kd, vdt, zd, yc = _front_call(
        x, kv, norm_g[l][None, :], w_in[l].astype(BF16), conv_w[l], conv_b[l][None, :],
        jnp.concatenate([wq_m[l], wk_m[l]], axis=-1).astype(BF16), wv_m[l].astype(BF16), w_if_pad, b_if_pad,
        mnorm_g[l][None, :], skip_m[l][None, :])
    return _diff_out_call(qd, kd, vdt, zd, dnorm_g[l][:, None],
                          lam_q1[l][None, :], lam_k1[l][None, :], lam_q2[l][None, :], lam_k2[l][None, :],
                          x, ym, yc, w_out[l].astype(BF16), final_g[None, :])
```

```python
import math

import jax
import jax.numpy as jnp
from jax import lax
from jax.experimental import pallas as pl
from jax.experimental.pallas import tpu as pltpu

F32 = jnp.float32
BF16 = jnp.bfloat16

D_MODEL = 1024
N_HEADS = 4
HEAD_DIM = 128
SUBLANES = 8
V7X_VMEM_BYTES = 64 * 1024 * 1024
VMEM_LIMIT_BYTES = V7X_VMEM_BYTES - 6 * 1024 * 1024
GROUP_WIDTH = N_HEADS * HEAD_DIM
DIFF_QK_DIM = 64
CONV_WIDTH = 4
MLSTM_CHUNK = 128
MLSTM_STEP_CHUNKS = 4
MEM_LEN = 256
IN_WIDTH = 9 * GROUP_WIDTH
NORM_EPS = 1e-6
LAM_INIT = 0.8 - 0.6 * math.exp(-0.3 * 0)
LOG2E = math.log2(math.e)

OFF_XM, OFF_OM, OFF_ZM, OFF_QD, OFF_KD, OFF_VD, OFF_ZD, OFF_QC, OFF_ZC = (
    i * GROUP_WIDTH for i in range(9))

FRONT_ROWS = MLSTM_STEP_CHUNKS * MLSTM_CHUNK
V_TILE = 256
VT_PAD = 16
VT_ROWS = HEAD_DIM + VT_PAD
DIFF_Q_ROWS = 512
DIFF_KV_ROWS = 512
DIFF_KEY_CHUNK = 512
CONV_HALO = 8

NT_DIMS = (((1,), (1,)), ((), ()))
TN_DIMS = (((0,), (0,)), ((), ()))


def _dot(a, b):
    return jnp.dot(a, b, preferred_element_type=F32)


def _dot_nt(a, b):
    return lax.dot_general(a, b, NT_DIMS, preferred_element_type=F32)


def _sigmoid(x):
    return 0.5 * jnp.tanh(0.5 * x) + 0.5


def _silu(x):
    return x * _sigmoid(x)


def _rms(x, g):
    return x * lax.rsqrt(jnp.mean(x * x, axis=-1, keepdims=True) + NORM_EPS) * g


def _head(h):
    return slice(h * HEAD_DIM, (h + 1) * HEAD_DIM)


def _fold_rows(x, op):
    r, n = x.shape
    return op(x.reshape(r // SUBLANES, SUBLANES, n), axis=0)


def _log_sigmoid(x):
    return jnp.minimum(x, 0.0) - jnp.log1p(jnp.exp(-jnp.abs(x)))


def _cumsum_rows(f, tril):
    hi = f.astype(BF16)
    r1 = f - hi.astype(F32)
    mid = r1.astype(BF16)
    lo = (r1 - mid.astype(F32)).astype(BF16)
    return _dot(tril, hi) + _dot(tril, mid) + _dot(tril, lo)


def _memkv_kernel(mem_ref, g_ref, w_ref, kv_ref):
    y = _rms(mem_ref[0], g_ref[...])
    kv_ref[0] = _dot(y.astype(BF16), w_ref[...]).astype(BF16)


def _memkv_call(mem, g, w):
    b = mem.shape[0]
    return pl.pallas_call(
        _memkv_kernel,
        grid=(b,),
        in_specs=[
            pl.BlockSpec((1, MEM_LEN, D_MODEL), lambda i: (i, 0, 0)),
            pl.BlockSpec((1, D_MODEL), lambda i: (0, 0)),
            pl.BlockSpec((D_MODEL, 2 * GROUP_WIDTH), lambda i: (0, 0)),
        ],
        out_specs=pl.BlockSpec((1, MEM_LEN, 2 * GROUP_WIDTH), lambda i: (i, 0, 0)),
        out_shape=jax.ShapeDtypeStruct((b, MEM_LEN, 2 * GROUP_WIDTH), BF16),
        compiler_params=pltpu.CompilerParams(
            dimension_semantics=("arbitrary",),
            allow_input_fusion=[False, False, True]),
        name="memkv",
    )(mem, g, w)


def _front_kernel(x_ref, kv_ref, ng_ref, win_ref, cw_ref, cb_ref, wqk_ref, wv_ref, wif_ref, bif_ref,
                  mg_ref, sk_ref,
                  ym_ref, qd_ref, kd_ref, vdt_ref, zd_ref, yc_ref,
                  conv_ref, c_ref, n_ref, m_ref, q_s, k_s, v_s, g_s, om_s, zm_s, xcv_s):
    tm = FRONT_ROWS
    L = MLSTM_CHUNK
    heads = range(N_HEADS)

    @pl.when(pl.program_id(1) == 0)
    def _():
        conv_ref[tm:tm + CONV_HALO, :] = jnp.zeros((CONV_HALO, GROUP_WIDTH), F32)
        c_ref[...] = jnp.zeros_like(c_ref)
        n_ref[...] = jnp.zeros_like(n_ref)
        m_ref[...] = jnp.zeros_like(m_ref)

    h = _rms(x_ref[0], ng_ref[...]).astype(BF16)

    def proj(off):
        return _dot(h, win_ref[:, off:off + GROUP_WIDTH])

    def put_vdt(val):
        v_t = val.T.astype(BF16)
        pad_row = lax.broadcasted_iota(jnp.int32, (VT_PAD, tm), 0)
        pad = jnp.where(pad_row == 0, 1.0, 0.0).astype(BF16)
        v_aug = jnp.concatenate([blk for hh in heads for blk in (v_t[_head(hh), :], pad)], axis=0)
        for j in range(tm // V_TILE):
            vdt_ref[0, j] = v_aug[:, j * V_TILE:(j + 1) * V_TILE]

    def put(ref, scale=None):
        def sink(val):
            ref[...] = (val if scale is None else val * scale).astype(BF16).reshape(ref.shape)
        return sink

    side_pieces = [(OFF_OM, put(om_s)), (OFF_ZM, put(zm_s)),
                   (OFF_QD, put(qd_ref, DIFF_QK_DIM ** -0.5 * LOG2E)), (OFF_KD, put(kd_ref)),
                   (OFF_VD, put_vdt), (OFF_ZD, put(zd_ref))]
    pending = []

    def side_start():
        if side_pieces:
            off, sink = side_pieces.pop(0)
            pending.append((proj(off), sink))

    def side_finish():
        if pending:
            val, sink = pending.pop(0)
            sink(val)

    x_m = proj(OFF_XM)
    q_cb = proj(OFF_QC).astype(BF16)
    z_c = proj(OFF_ZC)

    conv_ref[0:CONV_HALO, :] = conv_ref[tm:tm + CONV_HALO, :]
    conv_ref[CONV_HALO:CONV_HALO + tm, :] = x_m
    acc = jnp.broadcast_to(cb_ref[...], (tm, GROUP_WIDTH))
    for j in range(CONV_WIDTH):
        start = CONV_HALO - (CONV_WIDTH - 1) + j
        acc = acc + cw_ref[j:j + 1, :] * conv_ref[start:start + tm, :]
    sc = [_dot_nt(q_cb[:, _head(hh)], kv_ref[0, :, _head(hh)]) * (HEAD_DIM ** -0.5 * LOG2E)
          for hh in heads]
    x_cv = _silu(acc)
    xcv_b = x_cv.astype(BF16)
    xcv_s[...] = xcv_b
    xm_b = x_m.astype(BF16)

    qk_p = [_dot(xcv_b[:, _head(hh)], wqk_ref[hh]) for hh in heads]
    vv_p = [_dot(xm_b[:, _head(hh)], wv_ref[hh]) for hh in heads]

    q_all = jnp.concatenate([qk_p[hh][:, :HEAD_DIM] for hh in heads], axis=1).astype(BF16)
    k_all = jnp.concatenate([qk_p[hh][:, HEAD_DIM:] for hh in heads], axis=1)
    v_all = jnp.concatenate(vv_p, axis=1).astype(BF16)
    q_s[...] = q_all
    k_s[...] = (k_all * HEAD_DIM ** -0.5).astype(BF16)
    v_s[...] = v_all
    g_s[...] = (jnp.broadcast_to(bif_ref[...], (tm, HEAD_DIM))
                + _dot(q_all, wif_ref[0:GROUP_WIDTH, :])
                + _dot(k_all.astype(BF16), wif_ref[GROUP_WIDTH:2 * GROUP_WIDTH, :])
                + _dot(v_all, wif_ref[2 * GROUP_WIDTH:3 * GROUP_WIDTH, :]))

    mx = [jnp.max(sc[hh], axis=-1, keepdims=True) for hh in heads]
    pm = [jnp.exp2(sc[hh] - mx[hh]) for hh in heads]
    inv = [1.0 / jnp.sum(pm[hh], axis=-1, keepdims=True) for hh in heads]
    pc = [(pm[hh] * inv[hh]).astype(BF16) for hh in heads]
    oc = [_dot(pc[hh], kv_ref[0, :, GROUP_WIDTH + hh * HEAD_DIM:GROUP_WIDTH + (hh + 1) * HEAD_DIM]) for hh in heads]
    for hh in heads:
        yc_ref[0, :, _head(hh)] = (oc[hh] * _silu(z_c[:, _head(hh)])).astype(BF16)

    row = lax.broadcasted_iota(jnp.int32, (L, L), 0)
    col = lax.broadcasted_iota(jnp.int32, (L, L), 1)
    causal = col <= row
    tril = jnp.where(causal, 1.0, 0.0).astype(BF16)
    chunks = range(MLSTM_STEP_CHUNKS)
    probs = [(c, hh) for c in chunks for hh in heads]
    rows = lambda c: slice(c * L, (c + 1) * L)
    q_of = lambda p: q_s[rows(p[0]), _head(p[1])]
    k_of = lambda p: k_s[rows(p[0]), _head(p[1])]
    v_of = lambda p: v_s[rows(p[0]), _head(p[1])]

    def each(f, side=False):
        if side:
            side_start()
        out = {p: f(p) for p in probs}
        if side:
            side_finish()
        return out

    a_cols, a_rows = [], []
    for c in chunks:
        g = g_s[rows(c), :]
        bb_all = _cumsum_rows(_log_sigmoid(g), tril)
        ac = jnp.where(col < N_HEADS, g, bb_all) * LOG2E
        a_cols.append(ac)
        a_rows.append(ac.T)
    ib_col = {(c, hh): a_cols[c][:, hh:hh + 1] for c, hh in probs}
    bb_col = {(c, hh): a_cols[c][:, N_HEADS + hh:N_HEADS + hh + 1] for c, hh in probs}
    ib_row = {(c, hh): a_rows[c][hh:hh + 1, :] for c, hh in probs}
    bb_row = {(c, hh): a_rows[c][N_HEADS + hh:N_HEADS + hh + 1, :] for c, hh in probs}
    b_end = {p: bb_col[p][L - 1:L, :] for p in probs}

    qk = {p: _dot_nt(q_of(p), k_of(p)) for p in probs}

    d = each(lambda p: jnp.where(causal, (bb_col[p] - bb_row[p]) + ib_row[p], -jnp.inf), side=True)
    r = each(lambda p: jnp.max(d[p], axis=-1, keepdims=True))
    s_loc = each(lambda p: qk[p] * jnp.exp2(d[p] - r[p]), side=True)
    rs = each(lambda p: jnp.sum(s_loc[p], axis=-1, keepdims=True))
    s_bf = each(lambda p: s_loc[p].astype(BF16))
    a = each(lambda p: (b_end[p] - bb_col[p]) + ib_col[p], side=True)
    a_max = each(lambda p: jnp.max(a[p], axis=0, keepdims=True))
    w_loc = each(lambda p: jnp.broadcast_to(jnp.exp2(a[p] - a_max[p]), (L, HEAD_DIM)))
    vw = each(lambda p: (v_of(p).astype(F32) * w_loc[p]).astype(BF16), side=True)
    nu = each(lambda p: jnp.sum(k_of(p).astype(F32) * w_loc[p], axis=0, keepdims=True))

    sv = {p: _dot(s_bf[p], v_of(p)) for p in probs}
    u = {p: lax.dot_general(vw[p], k_of(p), TN_DIMS, preferred_element_type=F32) for p in probs}

    m_in, c_in, n_in = {}, {}, {}
    for hh in heads:
        m_cur = m_ref[hh][:, 0:1]
        c_cur = c_ref[hh]
        n_cur = n_ref[hh]
        for c in chunks:
            p = (c, hh)
            m_in[p], c_in[p], n_in[p] = m_cur, c_cur.astype(BF16), n_cur
            m_new = jnp.maximum(b_end[p] + m_cur, a_max[p])
            decay = jnp.exp2(b_end[p] + m_cur - m_new)
            e_upd = jnp.exp2(a_max[p] - m_new)
            c_cur = decay * c_cur + e_upd * u[p]
            n_cur = decay * n_cur + e_upd * nu[p]
            m_cur = m_new
        c_ref[hh] = c_cur
        n_ref[hh] = n_cur
        m_ref[hh] = jnp.broadcast_to(m_cur, (1, HEAD_DIM))

    q_c = {p: _dot_nt(q_of(p), c_in[p]) for p in probs}

    tile_of = lambda ref, p: ref[rows(p[0]), _head(p[1])].astype(F32)
    g_col = each(lambda p: bb_col[p] + m_in[p])
    m_t = each(lambda p: jnp.maximum(g_col[p], r[p]))
    e_loc = each(lambda p: jnp.exp2(r[p] - m_t[p]))
    inter = each(lambda p: jnp.exp2(g_col[p] - m_t[p]))
    q_n = each(lambda p: jnp.sum(q_of(p).astype(F32) * n_in[p], axis=-1, keepdims=True))
    den = each(lambda p: e_loc[p] * rs[p] + inter[p] * q_n[p])
    scale = each(lambda p: 1.0 / jnp.maximum(jnp.abs(den[p]), jnp.exp2(-m_t[p])))
    w_intra = each(lambda p: e_loc[p] * scale[p])
    w_inter = each(lambda p: inter[p] * scale[p])
    hg = each(lambda p: _sigmoid(tile_of(om_s, p))
              * (w_intra[p] * sv[p] + w_inter[p] * q_c[p]), side=True)
    mu = each(lambda p: jnp.mean(hg[p], axis=-1, keepdims=True))
    cen = each(lambda p: hg[p] - mu[p], side=True)
    var = each(lambda p: jnp.mean(jnp.square(cen[p]), axis=-1, keepdims=True))
    assert not side_pieces and not pending
    for p in probs:
        sl = _head(p[1])
        y = cen[p] * lax.rsqrt(var[p] + NORM_EPS) * mg_ref[:, sl]
        y = (y + sk_ref[:, sl] * tile_of(xcv_s, p)) * _silu(tile_of(zm_s, p))
        ym_ref[0, rows(p[0]), sl] = y.astype(BF16)


def _front_call(x, kv, norm_g, w_in, conv_w, conv_b, wqk, wv, w_if, b_if, mnorm_g, skip_m):
    b, s, _ = x.shape
    tm = FRONT_ROWS
    ns = s // tm
    tok = lambda w: jax.ShapeDtypeStruct((b, s, w), BF16)
    tok_spec = lambda w: pl.BlockSpec((1, tm, w), lambda i, j: (i, j, 0))
    const = lambda shape: pl.BlockSpec(shape, lambda i, j: (0,) * len(shape))
    vdt_tiles = tm // V_TILE
    out_shape = (
        tok(GROUP_WIDTH),
        tok(GROUP_WIDTH), tok(GROUP_WIDTH),
        jax.ShapeDtypeStruct((b, ns * vdt_tiles, N_HEADS * VT_ROWS, V_TILE), BF16),
        tok(GROUP_WIDTH),
        tok(GROUP_WIDTH),
    )
    out_specs = (
        tok_spec(GROUP_WIDTH), tok_spec(GROUP_WIDTH), tok_spec(GROUP_WIDTH),
        pl.BlockSpec((1, vdt_tiles, N_HEADS * VT_ROWS, V_TILE), lambda i, j: (i, j, 0, 0)),
        tok_spec(GROUP_WIDTH), tok_spec(GROUP_WIDTH),
    )
    in_specs = [
        pl.BlockSpec((1, tm, D_MODEL), lambda i, j: (i, j, 0)),
        pl.BlockSpec((1, MEM_LEN, 2 * GROUP_WIDTH), lambda i, j: (i, 0, 0)),
        const((1, D_MODEL)),
        const((D_MODEL, IN_WIDTH)),
        const((CONV_WIDTH, GROUP_WIDTH)),
        const((1, GROUP_WIDTH)),
        const((N_HEADS, HEAD_DIM, 2 * HEAD_DIM)),
        const((N_HEADS, HEAD_DIM, HEAD_DIM)),
        const((3 * GROUP_WIDTH, HEAD_DIM)),
        const((1, HEAD_DIM)),
        const((1, GROUP_WIDTH)),
        const((1, GROUP_WIDTH)),
    ]
    tile_bf16 = pltpu.VMEM((tm, GROUP_WIDTH), BF16)
    return pl.pallas_call(
        _front_kernel,
        grid=(b, ns),
        in_specs=in_specs,
        out_specs=out_specs,
        out_shape=out_shape,
        scratch_shapes=[pltpu.VMEM((tm + CONV_HALO, GROUP_WIDTH), F32),
                        pltpu.VMEM((N_HEADS, HEAD_DIM, HEAD_DIM), F32),
                        pltpu.VMEM((N_HEADS, 1, HEAD_DIM), F32),
                        pltpu.VMEM((N_HEADS, 1, HEAD_DIM), F32),
                        tile_bf16, tile_bf16, tile_bf16,
                        pltpu.VMEM((tm, HEAD_DIM), F32),
                        tile_bf16, tile_bf16, tile_bf16],
        compiler_params=pltpu.CompilerParams(
            dimension_semantics=("arbitrary", "arbitrary"),
            vmem_limit_bytes=VMEM_LIMIT_BYTES,
            allow_input_fusion=[i in (3, 6, 7, 8, 9) for i in range(12)]),
        name="front",
    )(x, kv, norm_g, w_in, conv_w, conv_b, wqk, wv, w_if, b_if, mnorm_g, skip_m)


def _diff_kernel(q_ref, qn_ref, k_ref, vt_ref, zd_ref, g_ref, lq1_ref, lk1_ref, lq2_ref, lk2_ref,
                 x_ref, ym_ref, yc_ref, wo_ref, fg_ref, o_ref,
                 qs_ref, m_ref, l_ref, acc_ref, s0_ref, s1_ref, mx0_ref, mx1_ref, yd_ref, part_ref, qsn_ref):
    tq, tk, tv = DIFF_Q_ROWS, DIFF_KV_ROWS, V_TILE
    i = pl.program_id(1)
    feat = lax.broadcasted_iota(jnp.int32, (HEAD_DIM, tq), 0)
    for hh in range(N_HEADS):
        q_t = q_ref[0, :, _head(hh)].astype(F32).T
        qs_ref[hh, :, 0:tq] = jnp.where(feat < DIFF_QK_DIM, q_t, 0.0).astype(BF16)
        qs_ref[hh, :, tq:2 * tq] = jnp.where(feat >= DIFF_QK_DIM, q_t, 0.0).astype(BF16)
    m_ref[...] = jnp.full_like(m_ref, -jnp.inf)
    l_ref[...] = jnp.zeros_like(l_ref)
    acc_ref[...] = jnp.zeros_like(acc_ref)

    kc = DIFF_KEY_CHUNK
    sub = HEAD_DIM
    tri = (lax.broadcasted_iota(jnp.int32, (sub, sub), 0) <= lax.broadcasted_iota(jnp.int32, (sub, sub), 1))

    def mask_diagonal(s_c, c):
        hidden = jnp.full((sub, sub), -jnp.inf, F32)
        hidden_max = jnp.full((SUBLANES, sub), -jnp.inf, F32)
        row_blocks, maxes = [], None
        for r in range(kc // sub):
            kb = c * (kc // sub) + r
            blocks, row_max = [], []
            for b8 in range(2 * tq // sub):
                b = b8 % (tq // sub)
                if kb > b:
                    blocks.append(hidden)
                    row_max.append(hidden_max)
                    continue
                blk = s_c[r * sub:(r + 1) * sub, b8 * sub:(b8 + 1) * sub]
                if kb == b:
                    blk = jnp.where(tri, blk, -jnp.inf)
                blocks.append(blk)
                row_max.append(_fold_rows(blk, jnp.max))
            row_blocks.append(jnp.concatenate(blocks, axis=1))
            row_max = jnp.concatenate(row_max, axis=1)
            maxes = row_max if maxes is None else jnp.maximum(maxes, row_max)
        return jnp.concatenate(row_blocks, axis=0), maxes

    group_a = (tuple(range(0, N_HEADS // 2)), s0_ref, mx0_ref)
    group_b = (tuple(range(N_HEADS // 2, N_HEADS)), s1_ref, mx1_ref)

    def step(consume=None, produce=None):
        n_slots = N_HEADS // 2
        for slot in range(n_slots):
            if consume is not None:
                t_c, (heads_c, sc_ref, mxc_ref) = consume
                h_c = heads_c[slot]
                m_old = m_ref[h_c]
                m_new = jnp.maximum(m_old, mxc_ref[slot])
                alpha = jnp.exp2(m_old - m_new)
                pv, pb_prev = None, None
            if produce is not None:
                t_p, (heads_p, sp_ref, mxp_ref), diagonal = produce[:3]
                q_src = produce[3] if len(produce) > 3 else None
                h_p = heads_p[slot]
                mx = None
            for c in range(tk // kc):
                chunk = slice(c * kc, (c + 1) * kc)
                if produce is not None:
                    rows = pl.ds(pl.multiple_of(t_p * tk + c * kc, kc), kc)
                    q_mat = qs_ref[h_p] if q_src is None else q_src[slot]
                    s_c = _dot(k_ref[0, rows, _head(h_p)], q_mat)
                if consume is not None:
                    pb_c = jnp.exp2(sc_ref[slot, chunk, :] - m_new).astype(BF16)
                    pb_prev = pb_c if pb_prev is None else jnp.concatenate([pb_prev, pb_c], axis=0)
                    if c == tk // kc - 1:
                        v_rows = slice(h_c * VT_ROWS, (h_c + 1) * VT_ROWS)
                        vt = jnp.concatenate([vt_ref[0, t_c * (tk // tv) + j, v_rows, :]
                                              for j in range(tk // tv)], axis=1)
                        pv = _dot(vt, pb_prev)
                if produce is not None:
                    if diagonal:
                        s_c, c_max = mask_diagonal(s_c, c)
                    else:
                        c_max = _fold_rows(s_c, jnp.max)
                    sp_ref[slot, chunk, :] = s_c
                    mx = c_max if mx is None else jnp.maximum(mx, c_max)
            if consume is not None:
                l_ref[h_c] = alpha * l_ref[h_c] + pv[HEAD_DIM:HEAD_DIM + 1, :]
                acc_ref[h_c] = alpha * acc_ref[h_c] + pv[:HEAD_DIM, :]
                m_ref[h_c] = m_new
            if produce is not None:
                mxp_ref[slot] = jnp.max(mx, axis=0, keepdims=True)

    def tile(t, next_diagonal):
        step(consume=(t, group_a), produce=(t, group_b, False))
        step(consume=(t, group_b), produce=(t + 1, group_a, next_diagonal))

    def body(t, carry):
        tile(t, False)
        return carry

    @pl.when(i == 0)
    def _():
        step(produce=(0, group_a, True))

    @pl.when(i > 0)
    def _():
        lax.fori_loop(0, i - 1, body, 0)
        tile(i - 1, True)

    lam = (jnp.exp(jnp.sum(lq1_ref[...] * lk1_ref[...], axis=-1, keepdims=True))
           - jnp.exp(jnp.sum(lq2_ref[...] * lk2_ref[...], axis=-1, keepdims=True)) + LAM_INIT)

    def finish(heads):
        for hh in heads:
            sl = _head(hh)
            acc = acc_ref[hh]
            inv = 1.0 / l_ref[hh]
            o_t = acc[:, :tq] * inv[:, :tq] - lam * (acc[:, tq:] * inv[:, tq:])
            rstd = lax.rsqrt(jnp.mean(o_t * o_t, axis=0, keepdims=True) + NORM_EPS)
            o = (o_t * rstd * (g_ref[sl, :] * (1.0 - LAM_INIT))).T
            yd_ref[:, sl] = (o * _silu(zd_ref[0, :, sl].astype(F32))).astype(BF16)

    def project_rows(c, src_ref, w_rows, first):
        r_c = slice(c * HEAD_DIM, (c + 1) * HEAD_DIM)
        y = _dot(src_ref[0, r_c, :], wo_ref[w_rows, :])
        part_ref[r_c, :] = (x_ref[0, r_c, :] + y) if first else (part_ref[r_c, :] + y)

    def project_heads(heads):
        cols = slice(heads[0] * HEAD_DIM, (heads[-1] + 1) * HEAD_DIM)
        w_rows = slice(GROUP_WIDTH + cols.start, GROUP_WIDTH + cols.stop)
        return _dot(yd_ref[:, cols], wo_ref[w_rows, :])

    step(consume=(i, group_a), produce=(i, group_b, True))
    finish(group_a[0])

    last = pl.num_programs(1) - 1

    @pl.when(i < last)
    def _():
        for slot, hh in enumerate(group_a[0]):
            q_t = qn_ref[0, :, _head(hh)].astype(F32).T
            qsn_ref[slot, :, 0:tq] = jnp.where(feat < DIFF_QK_DIM, q_t, 0.0).astype(BF16)
            qsn_ref[slot, :, tq:2 * tq] = jnp.where(feat >= DIFF_QK_DIM, q_t, 0.0).astype(BF16)
        step(consume=(i, group_b), produce=(0, group_a, False, qsn_ref))

    @pl.when(i == last)
    def _():
        step(consume=(i, group_b))

    for c in range(tq // HEAD_DIM):
        project_rows(c, ym_ref, slice(0, GROUP_WIDTH), True)
    for c in range(tq // HEAD_DIM):
        project_rows(c, yc_ref, slice(2 * GROUP_WIDTH, 3 * GROUP_WIDTH), False)
    y_a = project_heads(group_a[0])
    finish(group_b[0])
    o_ref[0] = _rms(part_ref[...] + y_a + project_heads(group_b[0]), fg_ref[...])


def _diff_out_call(qd, kd, vdt, zd, dnorm_g, lq1, lk1, lq2, lk2, x, ym, yc, w_out, final_g):
    n_q = qd.shape[1] // DIFF_Q_ROWS
    b, s, _ = qd.shape
    tq = DIFF_Q_ROWS
    nkv = vdt.shape[1]
    const = lambda shape: pl.BlockSpec(shape, lambda i, j: (0,) * len(shape))
    tok_spec = lambda w: pl.BlockSpec((1, tq, w), lambda i, j: (i, j, 0))
    return pl.pallas_call(
        _diff_kernel,
        grid=(b, s // tq),
        in_specs=[
            tok_spec(GROUP_WIDTH),
            pl.BlockSpec((1, tq, GROUP_WIDTH), lambda i, j: (i, jnp.minimum(j + 1, n_q - 1), 0)),
            pl.BlockSpec((1, s, GROUP_WIDTH), lambda i, j: (i, 0, 0)),
            pl.BlockSpec((1, nkv, N_HEADS * VT_ROWS, V_TILE), lambda i, j: (i, 0, 0, 0)),
            tok_spec(GROUP_WIDTH),
            const((GROUP_WIDTH, 1)),
            const((1, DIFF_QK_DIM)), const((1, DIFF_QK_DIM)), const((1, DIFF_QK_DIM)), const((1, DIFF_QK_DIM)),
            tok_spec(D_MODEL), tok_spec(GROUP_WIDTH), tok_spec(GROUP_WIDTH),
            const((3 * GROUP_WIDTH, D_MODEL)), const((1, D_MODEL)),
        ],
        out_specs=tok_spec(D_MODEL),
        out_shape=jax.ShapeDtypeStruct((b, s, D_MODEL), F32),
        scratch_shapes=[pltpu.VMEM((N_HEADS, HEAD_DIM, 2 * tq), BF16),
                        pltpu.VMEM((N_HEADS, 1, 2 * tq), F32),
                        pltpu.VMEM((N_HEADS, 1, 2 * tq), F32),
                        pltpu.VMEM((N_HEADS, HEAD_DIM, 2 * tq), F32),
                        pltpu.VMEM((N_HEADS // 2, DIFF_KV_ROWS, 2 * tq), F32),
                        pltpu.VMEM((N_HEADS // 2, DIFF_KV_ROWS, 2 * tq), F32),
                        pltpu.VMEM((N_HEADS // 2, 1, 2 * tq), F32),
                        pltpu.VMEM((N_HEADS // 2, 1, 2 * tq), F32),
                        pltpu.VMEM((tq, GROUP_WIDTH), BF16),
                        pltpu.VMEM((tq, D_MODEL), F32),
                        pltpu.VMEM((N_HEADS // 2, HEAD_DIM, 2 * tq), BF16)],
        compiler_params=pltpu.CompilerParams(
            dimension_semantics=("arbitrary", "arbitrary"),
            vmem_limit_bytes=VMEM_LIMIT_BYTES,
            allow_input_fusion=[i == 13 for i in range(15)]),
        name="diffattn_out",
    )(qd, qd, kd, vdt, zd, dnorm_g, lq1, lk1, lq2, lk2, x, ym, yc, w_out, final_g)


def kernel(x, mem, norm_g, w_in, conv_w, conv_b, wq_m, wk_m, wv_m, w_if, b_if, mnorm_g, skip_m,
           lam_q1, lam_k1, lam_q2, lam_k2, dnorm_g, mem_norm_g, w_mem_kv, w_out, final_g):
    b, s, d = x.shape
    assert (d, s % DIFF_KV_ROWS, DIFF_KV_ROWS % V_TILE, FRONT_ROWS % V_TILE) == (D_MODEL, 0, 0, 0)
    assert (s % FRONT_ROWS, DIFF_KV_ROWS % DIFF_KEY_CHUNK) == (0, 0)
    assert DIFF_Q_ROWS == DIFF_KV_ROWS and DIFF_KEY_CHUNK % HEAD_DIM == 0
    assert norm_g.shape[0] == 1, "single-layer kernel"
    l = 0
    w_if_pad = jnp.pad(w_if[l], ((0, 0), (0, HEAD_DIM - w_if.shape[-1]))).astype(BF16)
    b_if_pad = jnp.pad(b_if[l], (0, HEAD_DIM - b_if.shape[-1]))[None, :]

    kv = _memkv_call(mem, mem_norm_g[l][None, :], w_mem_kv[l].astype(BF16))
    ym, qd, kd, vdt, zd, yc = _front_call(
        x, kv, norm_g[l][None, :], w_in[l].astype(BF16), conv_w[l], conv_b[l][None, :],
        jnp.concatenate([wq_m[l], wk_m[l]], axis=-1).astype(BF16), wv_m[l].astype(BF16), w_if_pad, b_if_pad,
        mnorm_g[l][None, :], skip_m[l][None, :])
    return _diff_out_call(qd, kd, vdt, zd, dnorm_g[l][:, None],
                          lam_q1[l][None, :], lam_k1[l][None, :], lam_q2[l][None, :], lam_k2[l][None, :],
                          x, ym, yc, w_out[l].astype(BF16), final_g[None, :])
```

```python
import math

import jax
import jax.numpy as jnp
from jax import lax
from jax.experimental import pallas as pl
from jax.experimental.pallas import tpu as pltpu

F32 = jnp.float32
BF16 = jnp.bfloat16

D_MODEL = 1024
N_HEADS = 4
HEAD_DIM = 128
SUBLANES = 8
V7X_VMEM_BYTES = 64 * 1024 * 1024
VMEM_LIMIT_BYTES = V7X_VMEM_BYTES - 6 * 1024 * 1024
GROUP_WIDTH = N_HEADS * HEAD_DIM
DIFF_QK_DIM = 64
CONV_WIDTH = 4
MLSTM_CHUNK = 128
MLSTM_STEP_CHUNKS = 4
MEM_LEN = 256
IN_WIDTH = 9 * GROUP_WIDTH
NORM_EPS = 1e-6
LAM_INIT = 0.8 - 0.6 * math.exp(-0.3 * 0)
LOG2E = math.log2(math.e)

OFF_XM, OFF_OM, OFF_ZM, OFF_QD, OFF_KD, OFF_VD, OFF_ZD, OFF_QC, OFF_ZC = (
    i * GROUP_WIDTH for i in range(9))

FRONT_ROWS = MLSTM_STEP_CHUNKS * MLSTM_CHUNK
V_TILE = 256
VT_PAD = 16
VT_ROWS = HEAD_DIM + VT_PAD
DIFF_Q_ROWS = 512
DIFF_KV_ROWS = 512
DIFF_KEY_CHUNK = 512
CONV_HALO = 8

NT_DIMS = (((1,), (1,)), ((), ()))
TN_DIMS = (((0,), (0,)), ((), ()))


def _dot(a, b):
    return jnp.dot(a, b, preferred_element_type=F32)


def _dot_nt(a, b):
    return lax.dot_general(a, b, NT_DIMS, preferred_element_type=F32)


def _sigmoid(x):
    return 0.5 * jnp.tanh(0.5 * x) + 0.5


def _silu(x):
    return x * _sigmoid(x)


def _rms(x, g):
    return x * lax.rsqrt(jnp.mean(x * x, axis=-1, keepdims=True) + NORM_EPS) * g


def _head(h):
    return slice(h * HEAD_DIM, (h + 1) * HEAD_DIM)


def _fold_rows(x, op):
    r, n = x.shape
    return op(x.reshape(r // SUBLANES, SUBLANES, n), axis=0)


def _log_sigmoid(x):
    return jnp.minimum(x, 0.0) - jnp.log1p(jnp.exp(-jnp.abs(x)))


def _cumsum_rows(f, tril):
    hi = f.astype(BF16)
    r1 = f - hi.astype(F32)
    mid = r1.astype(BF16)
    lo = (r1 - mid.astype(F32)).astype(BF16)
    return _dot(tril, hi) + _dot(tril, mid) + _dot(tril, lo)


def _memkv_kernel(mem_ref, g_ref, w_ref, kv_ref):
    y = _rms(mem_ref[0], g_ref[...])
    kv_ref[0] = _dot(y.astype(BF16), w_ref[...]).astype(BF16)


def _memkv_call(mem, g, w):
    b = mem.shape[0]
    return pl.pallas_call(
        _memkv_kernel,
        grid=(b,),
        in_specs=[
            pl.BlockSpec((1, MEM_LEN, D_MODEL), lambda i: (i, 0, 0)),
            pl.BlockSpec((1, D_MODEL), lambda i: (0, 0)),
            pl.BlockSpec((D_MODEL, 2 * GROUP_WIDTH), lambda i: (0, 0)),
        ],
        out_specs=pl.BlockSpec((1, MEM_LEN, 2 * GROUP_WIDTH), lambda i: (i, 0, 0)),
        out_shape=jax.ShapeDtypeStruct((b, MEM_LEN, 2 * GROUP_WIDTH), BF16),
        compiler_params=pltpu.CompilerParams(
            dimension_semantics=("arbitrary",),
            allow_input_fusion=[False, False, True]),
        name="memkv",
    )(mem, g, w)


def _front_kernel(x_ref, kv_ref, ng_ref, win_ref, cw_ref, cb_ref, wqk_ref, wv_ref, wif_ref, bif_ref,
                  mg_ref, sk_ref,
                  ym_ref, qd_ref, kd_ref, vdt_ref, zd_ref, yc_ref,
                  conv_ref, c_ref, n_ref, m_ref, q_s, k_s, v_s, g_s, om_s, zm_s, xcv_s):
    tm = FRONT_ROWS
    L = MLSTM_CHUNK
    heads = range(N_HEADS)

    @pl.when(pl.program_id(1) == 0)
    def _():
        conv_ref[tm:tm + CONV_HALO, :] = jnp.zeros((CONV_HALO, GROUP_WIDTH), F32)
        c_ref[...] = jnp.zeros_like(c_ref)
        n_ref[...] = jnp.zeros_like(n_ref)
        m_ref[...] = jnp.zeros_like(m_ref)

    h = _rms(x_ref[0], ng_ref[...]).astype(BF16)

    def proj(off):
        return _dot(h, win_ref[:, off:off + GROUP_WIDTH])

    def put_vdt(val):
        v_t = val.T.astype(BF16)
        pad_row = lax.broadcasted_iota(jnp.int32, (VT_PAD, tm), 0)
        pad = jnp.where(pad_row == 0, 1.0, 0.0).astype(BF16)
        v_aug = jnp.concatenate([blk for hh in heads for blk in (v_t[_head(hh), :], pad)], axis=0)
        for j in range(tm // V_TILE):
            vdt_ref[0, j] = v_aug[:, j * V_TILE:(j + 1) * V_TILE]

    def put(ref, scale=None):
        def sink(val):
            ref[...] = (val if scale is None else val * scale).astype(BF16).reshape(ref.shape)
        return sink

    side_pieces = [(OFF_OM, put(om_s)), (OFF_ZM, put(zm_s)),
                   (OFF_QD, put(qd_ref, DIFF_QK_DIM ** -0.5 * LOG2E)), (OFF_KD, put(kd_ref)),
                   (OFF_VD, put_vdt), (OFF_ZD, put(zd_ref))]
    pending = []

    def side_start():
        if side_pieces:
            off, sink = side_pieces.pop(0)
            pending.append((proj(off), sink))

    def side_finish():
        if pending:
            val, sink = pending.pop(0)
            sink(val)

    x_m = proj(OFF_XM)
    q_cb = proj(OFF_QC).astype(BF16)
    z_c = proj(OFF_ZC)

    conv_ref[0:CONV_HALO, :] = conv_ref[tm:tm + CONV_HALO, :]
    conv_ref[CONV_HALO:CONV_HALO + tm, :] = x_m
    acc = jnp.broadcast_to(cb_ref[...], (tm, GROUP_WIDTH))
    for j in range(CONV_WIDTH):
        start = CONV_HALO - (CONV_WIDTH - 1) + j
        acc = acc + cw_ref[j:j + 1, :] * conv_ref[start:start + tm, :]
    sc = [_dot_nt(q_cb[:, _head(hh)], kv_ref[0, :, _head(hh)]) * (HEAD_DIM ** -0.5 * LOG2E)
          for hh in heads]
    x_cv = _silu(acc)
    xcv_b = x_cv.astype(BF16)
    xcv_s[...] = xcv_b
    xm_b = x_m.astype(BF16)

    qk_p = [_dot(xcv_b[:, _head(hh)], wqk_ref[hh]) for hh in heads]
    vv_p = [_dot(xm_b[:, _head(hh)], wv_ref[hh]) for hh in heads]

    q_all = jnp.concatenate([qk_p[hh][:, :HEAD_DIM] for hh in heads], axis=1).astype(BF16)
    k_all = jnp.concatenate([qk_p[hh][:, HEAD_DIM:] for hh in heads], axis=1)
    v_all = jnp.concatenate(vv_p, axis=1).astype(BF16)
    q_s[...] = q_all
    k_s[...] = (k_all * HEAD_DIM ** -0.5).astype(BF16)
    v_s[...] = v_all
    g_s[...] = (jnp.broadcast_to(bif_ref[...], (tm, HEAD_DIM))
                + _dot(q_all, wif_ref[0:GROUP_WIDTH, :])
                + _dot(k_all.astype(BF16), wif_ref[GROUP_WIDTH:2 * GROUP_WIDTH, :])
                + _dot(v_all, wif_ref[2 * GROUP_WIDTH:3 * GROUP_WIDTH, :]))

    mx = [jnp.max(sc[hh], axis=-1, keepdims=True) for hh in heads]
    pm = [jnp.exp2(sc[hh] - mx[hh]) for hh in heads]
    inv = [1.0 / jnp.sum(pm[hh], axis=-1, keepdims=True) for hh in heads]
    pc = [(pm[hh] * inv[hh]).astype(BF16) for hh in heads]
    oc = [_dot(pc[hh], kv_ref[0, :, GROUP_WIDTH + hh * HEAD_DIM:GROUP_WIDTH + (hh + 1) * HEAD_DIM]) for hh in heads]
    for hh in heads:
        yc_ref[0, :, _head(hh)] = (oc[hh] * _silu(z_c[:, _head(hh)])).astype(BF16)

    row = lax.broadcasted_iota(jnp.int32, (L, L), 0)
    col = lax.broadcasted_iota(jnp.int32, (L, L), 1)
    causal = col <= row
    tril = jnp.where(causal, 1.0, 0.0).astype(BF16)
    chunks = range(MLSTM_STEP_CHUNKS)
    probs = [(c, hh) for c in chunks for hh in heads]
    rows = lambda c: slice(c * L, (c + 1) * L)
    q_of = lambda p: q_s[rows(p[0]), _head(p[1])]
    k_of = lambda p: k_s[rows(p[0]), _head(p[1])]
    v_of = lambda p: v_s[rows(p[0]), _head(p[1])]

    def each(f, side=False):
        if side:
            side_start()
        out = {p: f(p) for p in probs}
        if side:
            side_finish()
        return out

    a_cols, a_rows = [], []
    for c in chunks:
        g = g_s[rows(c), :]
        bb_all = _cumsum_rows(_log_sigmoid(g), tril)
        ac = jnp.where(col < N_HEADS, g, bb_all) * LOG2E
        a_cols.append(ac)
        a_rows.append(ac.T)
    ib_col = {(c, hh): a_cols[c][:, hh:hh + 1] for c, hh in probs}
    bb_col = {(c, hh): a_cols[c][:, N_HEADS + hh:N_HEADS + hh + 1] for c, hh in probs}
    ib_row = {(c, hh): a_rows[c][hh:hh + 1, :] for c, hh in probs}
    bb_row = {(c, hh): a_rows[c][N_HEADS + hh:N_HEADS + hh + 1, :] for c, hh in probs}
    b_end = {p: bb_col[p][L - 1:L, :] for p in probs}

    qk = {p: _dot_nt(q_of(p), k_of(p)) for p in probs}

    d = each(lambda p: jnp.where(causal, (bb_col[p] - bb_row[p]) + ib_row[p], -jnp.inf), side=True)
    r = each(lambda p: jnp.max(d[p], axis=-1, keepdims=True))
    s_loc = each(lambda p: qk[p] * jnp.exp2(d[p] - r[p]), side=True)
    rs = each(lambda p: jnp.sum(s_loc[p], axis=-1, keepdims=True))
    s_bf = each(lambda p: s_loc[p].astype(BF16))
    a = each(lambda p: (b_end[p] - bb_col[p]) + ib_col[p], side=True)
    a_max = each(lambda p: jnp.max(a[p], axis=0, keepdims=True))
    w_loc = each(lambda p: jnp.broadcast_to(jnp.exp2(a[p] - a_max[p]), (L, HEAD_DIM)))
    vw = each(lambda p: (v_of(p).astype(F32) * w_loc[p]).astype(BF16), side=True)
    nu = each(lambda p: jnp.sum(k_of(p).astype(F32) * w_loc[p], axis=0, keepdims=True))

    sv = {p: _dot(s_bf[p], v_of(p)) for p in probs}
    u = {p: lax.dot_general(vw[p], k_of(p), TN_DIMS, preferred_element_type=F32) for p in probs}

    m_in, c_in, n_in = {}, {}, {}
    for hh in heads:
        m_cur = m_ref[hh][:, 0:1]
        c_cur = c_ref[hh]
        n_cur = n_ref[hh]
        for c in chunks:
            p = (c, hh)
            m_in[p], c_in[p], n_in[p] = m_cur, c_cur.astype(BF16), n_cur
            m_new = jnp.maximum(b_end[p] + m_cur, a_max[p])
            decay = jnp.exp2(b_end[p] + m_cur - m_new)
            e_upd = jnp.exp2(a_max[p] - m_new)
            c_cur = decay * c_cur + e_upd * u[p]
            n_cur = decay * n_cur + e_upd * nu[p]
            m_cur = m_new
        c_ref[hh] = c_cur
        n_ref[hh] = n_cur
        m_ref[hh] = jnp.broadcast_to(m_cur, (1, HEAD_DIM))

    q_c = {p: _dot_nt(q_of(p), c_in[p]) for p in probs}

    tile_of = lambda ref, p: ref[rows(p[0]), _head(p[1])].astype(F32)
    g_col = each(lambda p: bb_col[p] + m_in[p])
    m_t = each(lambda p: jnp.maximum(g_col[p], r[p]))
    e_loc = each(lambda p: jnp.exp2(r[p] - m_t[p]))
    inter = each(lambda p: jnp.exp2(g_col[p] - m_t[p]))
    q_n = each(lambda p: jnp.sum(q_of(p).astype(F32) * n_in[p], axis=-1, keepdims=True))
    den = each(lambda p: e_loc[p] * rs[p] + inter[p] * q_n[p])
    scale = each(lambda p: 1.0 / jnp.maximum(jnp.abs(den[p]), jnp.exp2(-m_t[p])))
    w_intra = each(lambda p: e_loc[p] * scale[p])
    w_inter = each(lambda p: inter[p] * scale[p])
    hg = each(lambda p: _sigmoid(tile_of(om_s, p))
              * (w_intra[p] * sv[p] + w_inter[p] * q_c[p]), side=True)
    mu = each(lambda p: jnp.mean(hg[p], axis=-1, keepdims=True))
    cen = each(lambda p: hg[p] - mu[p], side=True)
    var = each(lambda p: jnp.mean(jnp.square(cen[p]), axis=-1, keepdims=True))
    assert not side_pieces and not pending
    for p in probs:
        sl = _head(p[1])
        y = cen[p] * lax.rsqrt(var[p] + NORM_EPS) * mg_ref[:, sl]
        y = (y + sk_ref[:, sl] * tile_of(xcv_s, p)) * _silu(tile_of(zm_s, p))
        ym_ref[0, rows(p[0]), sl] = y.astype(BF16)


def _front_call(x, kv, norm_g, w_in, conv_w, conv_b, wqk, wv, w_if, b_if, mnorm_g, skip_m):
    b, s, _ = x.shape
    tm = FRONT_ROWS
    ns = s // tm
    tok = lambda w: jax.ShapeDtypeStruct((b, s, w), BF16)
    tok_spec = lambda w: pl.BlockSpec((1, tm, w), lambda i, j: (i, j, 0))
    const = lambda shape: pl.BlockSpec(shape, lambda i, j: (0,) * len(shape))
    vdt_tiles = tm // V_TILE
    out_shape = (
        tok(GROUP_WIDTH),
        tok(GROUP_WIDTH), tok(GROUP_WIDTH),
        jax.ShapeDtypeStruct((b, ns * vdt_tiles, N_HEADS * VT_ROWS, V_TILE), BF16),
        tok(GROUP_WIDTH),
        tok(GROUP_WIDTH),
    )
    out_specs = (
        tok_spec(GROUP_WIDTH), tok_spec(GROUP_WIDTH), tok_spec(GROUP_WIDTH),
        pl.BlockSpec((1, vdt_tiles, N_HEADS * VT_ROWS, V_TILE), lambda i, j: (i, j, 0, 0)),
        tok_spec(GROUP_WIDTH), tok_spec(GROUP_WIDTH),
    )
    in_specs = [
        pl.BlockSpec((1, tm, D_MODEL), lambda i, j: (i, j, 0)),
        pl.BlockSpec((1, MEM_LEN, 2 * GROUP_WIDTH), lambda i, j: (i, 0, 0)),
        const((1, D_MODEL)),
        pl.BlockSpec((D_MODEL, IN_WIDTH), lambda i, j: (0, 0),
                     pipeline_mode=pl.Buffered(1)),
        const((CONV_WIDTH, GROUP_WIDTH)),
        const((1, GROUP_WIDTH)),
        const((N_HEADS, HEAD_DIM, 2 * HEAD_DIM)),
        const((N_HEADS, HEAD_DIM, HEAD_DIM)),
        const((3 * GROUP_WIDTH, HEAD_DIM)),
        const((1, HEAD_DIM)),
        const((1, GROUP_WIDTH)),
        const((1, GROUP_WIDTH)),
    ]
    tile_bf16 = pltpu.VMEM((tm, GROUP_WIDTH), BF16)
    return pl.pallas_call(
        _front_kernel,
        grid=(b, ns),
        in_specs=in_specs,
        out_specs=out_specs,
        out_shape=out_shape,
        scratch_shapes=[pltpu.VMEM((tm + CONV_HALO, GROUP_WIDTH), F32),
                        pltpu.VMEM((N_HEADS, HEAD_DIM, HEAD_DIM), F32),
                        pltpu.VMEM((N_HEADS, 1, HEAD_DIM), F32),
                        pltpu.VMEM((N_HEADS, 1, HEAD_DIM), F32),
                        tile_bf16, tile_bf16, tile_bf16,
                        pltpu.VMEM((tm, HEAD_DIM), F32),
                        tile_bf16, tile_bf16, tile_bf16],
        compiler_params=pltpu.CompilerParams(
            dimension_semantics=("arbitrary", "arbitrary"),
            vmem_limit_bytes=VMEM_LIMIT_BYTES,
            allow_input_fusion=[i in (3, 6, 7, 8, 9) for i in range(12)]),
        name="front",
    )(x, kv, norm_g, w_in, conv_w, conv_b, wqk, wv, w_if, b_if, mnorm_g, skip_m)


def _diff_kernel(q_ref, qn_ref, k_ref, vt_ref, zd_ref, g_ref, lq1_ref, lk1_ref, lq2_ref, lk2_ref,
                 x_ref, ym_ref, yc_ref, wo_ref, fg_ref, o_ref,
                 qs_ref, m_ref, l_ref, acc_ref, s0_ref, s1_ref, mx0_ref, mx1_ref, yd_ref, part_ref, qsn_ref):
    tq, tk, tv = DIFF_Q_ROWS, DIFF_KV_ROWS, V_TILE
    i = pl.program_id(1)
    feat = lax.broadcasted_iota(jnp.int32, (HEAD_DIM, tq), 0)
    for hh in range(N_HEADS):
        q_t = q_ref[0, :, _head(hh)].astype(F32).T
        qs_ref[hh, :, 0:tq] = jnp.where(feat < DIFF_QK_DIM, q_t, 0.0).astype(BF16)
        qs_ref[hh, :, tq:2 * tq] = jnp.where(feat >= DIFF_QK_DIM, q_t, 0.0).astype(BF16)
    m_ref[...] = jnp.full_like(m_ref, -jnp.inf)
    l_ref[...] = jnp.zeros_like(l_ref)
    acc_ref[...] = jnp.zeros_like(acc_ref)

    kc = DIFF_KEY_CHUNK
    sub = HEAD_DIM
    tri = (lax.broadcasted_iota(jnp.int32, (sub, sub), 0) <= lax.broadcasted_iota(jnp.int32, (sub, sub), 1))

    def mask_diagonal(s_c, c):
        hidden = jnp.full((sub, sub), -jnp.inf, F32)
        hidden_max = jnp.full((SUBLANES, sub), -jnp.inf, F32)
        row_blocks, maxes = [], None
        for r in range(kc // sub):
            kb = c * (kc // sub) + r
            blocks, row_max = [], []
            for b8 in range(2 * tq // sub):
                b = b8 % (tq // sub)
                if kb > b:
                    blocks.append(hidden)
                    row_max.append(hidden_max)
                    continue
                blk = s_c[r * sub:(r + 1) * sub, b8 * sub:(b8 + 1) * sub]
                if kb == b:
                    blk = jnp.where(tri, blk, -jnp.inf)
                blocks.append(blk)
                row_max.append(_fold_rows(blk, jnp.max))
            row_blocks.append(jnp.concatenate(blocks, axis=1))
            row_max = jnp.concatenate(row_max, axis=1)
            maxes = row_max if maxes is None else jnp.maximum(maxes, row_max)
        return jnp.concatenate(row_blocks, axis=0), maxes

    group_a = (tuple(range(0, N_HEADS // 2)), s0_ref, mx0_ref)
    group_b = (tuple(range(N_HEADS // 2, N_HEADS)), s1_ref, mx1_ref)

    def step(consume=None, produce=None):
        n_slots = N_HEADS // 2
        for slot in range(n_slots):
            if consume is not None:
                t_c, (heads_c, sc_ref, mxc_ref) = consume
                h_c = heads_c[slot]
                m_old = m_ref[h_c]
                m_new = jnp.maximum(m_old, mxc_ref[slot])
                alpha = jnp.exp2(m_old - m_new)
                pv, pb_prev = None, None
            if produce is not None:
                t_p, (heads_p, sp_ref, mxp_ref), diagonal = produce[:3]
                q_src = produce[3] if len(produce) > 3 else None
                h_p = heads_p[slot]
                mx = None
            for c in range(tk // kc):
                chunk = slice(c * kc, (c + 1) * kc)
                if produce is not None:
                    rows = pl.ds(pl.multiple_of(t_p * tk + c * kc, kc), kc)
                    q_mat = qs_ref[h_p] if q_src is None else q_src[slot]
                    s_c = _dot(k_ref[0, rows, _head(h_p)], q_mat)
                if consume is not None:
                    pb_c = jnp.exp2(sc_ref[slot, chunk, :] - m_new).astype(BF16)
                    pb_prev = pb_c if pb_prev is None else jnp.concatenate([pb_prev, pb_c], axis=0)
                    if c == tk // kc - 1:
                        v_rows = slice(h_c * VT_ROWS, (h_c + 1) * VT_ROWS)
                        vt = jnp.concatenate([vt_ref[0, t_c * (tk // tv) + j, v_rows, :]
                                              for j in range(tk // tv)], axis=1)
                        pv = _dot(vt, pb_prev)
                if produce is not None:
                    if diagonal:
                        s_c, c_max = mask_diagonal(s_c, c)
                    else:
                        c_max = _fold_rows(s_c, jnp.max)
                    sp_ref[slot, chunk, :] = s_c
                    mx = c_max if mx is None else jnp.maximum(mx, c_max)
            if consume is not None:
                l_ref[h_c] = alpha * l_ref[h_c] + pv[HEAD_DIM:HEAD_DIM + 1, :]
                acc_ref[h_c] = alpha * acc_ref[h_c] + pv[:HEAD_DIM, :]
                m_ref[h_c] = m_new
            if produce is not None:
                mxp_ref[slot] = jnp.max(mx, axis=0, keepdims=True)

    def tile(t, next_diagonal):
        step(consume=(t, group_a), produce=(t, group_b, False))
        step(consume=(t, group_b), produce=(t + 1, group_a, next_diagonal))

    def body(t, carry):
        tile(t, False)
        return carry

    @pl.when(i == 0)
    def _():
        step(produce=(0, group_a, True))

    @pl.when(i > 0)
    def _():
        lax.fori_loop(0, i - 1, body, 0)
        tile(i - 1, True)

    lam = (jnp.exp(jnp.sum(lq1_ref[...] * lk1_ref[...], axis=-1, keepdims=True))
           - jnp.exp(jnp.sum(lq2_ref[...] * lk2_ref[...], axis=-1, keepdims=True)) + LAM_INIT)

    def finish(heads):
        for hh in heads:
            sl = _head(hh)
            acc = acc_ref[hh]
            inv = 1.0 / l_ref[hh]
            o_t = acc[:, :tq] * inv[:, :tq] - lam * (acc[:, tq:] * inv[:, tq:])
            rstd = lax.rsqrt(jnp.mean(o_t * o_t, axis=0, keepdims=True) + NORM_EPS)
            o = (o_t * rstd * (g_ref[sl, :] * (1.0 - LAM_INIT))).T
            yd_ref[:, sl] = (o * _silu(zd_ref[0, :, sl].astype(F32))).astype(BF16)

    def project_rows(c, src_ref, w_rows, first):
        r_c = slice(c * HEAD_DIM, (c + 1) * HEAD_DIM)
        y = _dot(src_ref[0, r_c, :], wo_ref[w_rows, :])
        part_ref[r_c, :] = (x_ref[0, r_c, :] + y) if first else (part_ref[r_c, :] + y)

    def project_heads(heads):
        cols = slice(heads[0] * HEAD_DIM, (heads[-1] + 1) * HEAD_DIM)
        w_rows = slice(GROUP_WIDTH + cols.start, GROUP_WIDTH + cols.stop)
        return _dot(yd_ref[:, cols], wo_ref[w_rows, :])

    step(consume=(i, group_a), produce=(i, group_b, True))
    finish(group_a[0])

    last = pl.num_programs(1) - 1

    @pl.when(i < last)
    def _():
        for slot, hh in enumerate(group_a[0]):
            q_t = qn_ref[0, :, _head(hh)].astype(F32).T
            qsn_ref[slot, :, 0:tq] = jnp.where(feat < DIFF_QK_DIM, q_t, 0.0).astype(BF16)
            qsn_ref[slot, :, tq:2 * tq] = jnp.where(feat >= DIFF_QK_DIM, q_t, 0.0).astype(BF16)
        step(consume=(i, group_b), produce=(0, group_a, False, qsn_ref))

    @pl.when(i == last)
    def _():
        step(consume=(i, group_b))

    for c in range(tq // HEAD_DIM):
        project_rows(c, ym_ref, slice(0, GROUP_WIDTH), True)
    for c in range(tq // HEAD_DIM):
        project_rows(c, yc_ref, slice(2 * GROUP_WIDTH, 3 * GROUP_WIDTH), False)
    y_a = project_heads(group_a[0])
    finish(group_b[0])
    o_ref[0] = _rms(part_ref[...] + y_a + project_heads(group_b[0]), fg_ref[...])


def _diff_out_call(qd, kd, vdt, zd, dnorm_g, lq1, lk1, lq2, lk2, x, ym, yc, w_out, final_g):
    n_q = qd.shape[1] // DIFF_Q_ROWS
    b, s, _ = qd.shape
    tq = DIFF_Q_ROWS
    nkv = vdt.shape[1]
    const = lambda shape: pl.BlockSpec(shape, lambda i, j: (0,) * len(shape))
    tok_spec = lambda w: pl.BlockSpec((1, tq, w), lambda i, j: (i, j, 0))
    return pl.pallas_call(
        _diff_kernel,
        grid=(b, s // tq),
        in_specs=[
            tok_spec(GROUP_WIDTH),
            pl.BlockSpec((1, tq, GROUP_WIDTH), lambda i, j: (i, jnp.minimum(j + 1, n_q - 1), 0)),
            pl.BlockSpec((1, s, GROUP_WIDTH), lambda i, j: (i, 0, 0)),
            pl.BlockSpec((1, nkv, N_HEADS * VT_ROWS, V_TILE), lambda i, j: (i, 0, 0, 0)),
            tok_spec(GROUP_WIDTH),
            const((GROUP_WIDTH, 1)),
            const((1, DIFF_QK_DIM)), const((1, DIFF_QK_DIM)), const((1, DIFF_QK_DIM)), const((1, DIFF_QK_DIM)),
            tok_spec(D_MODEL), tok_spec(GROUP_WIDTH), tok_spec(GROUP_WIDTH),
            pl.BlockSpec((3 * GROUP_WIDTH, D_MODEL), lambda i, j: (0, 0),
                         pipeline_mode=pl.Buffered(1)),
            const((1, D_MODEL)),
        ],
        out_specs=tok_spec(D_MODEL),
        out_shape=jax.ShapeDtypeStruct((b, s, D_MODEL), F32),
        scratch_shapes=[pltpu.VMEM((N_HEADS, HEAD_DIM, 2 * tq), BF16),
                        pltpu.VMEM((N_HEADS, 1, 2 * tq), F32),
                        pltpu.VMEM((N_HEADS, 1, 2 * tq), F32),
                        pltpu.VMEM((N_HEADS, HEAD_DIM, 2 * tq), F32),
                        pltpu.VMEM((N_HEADS // 2, DIFF_KV_ROWS, 2 * tq), F32),
                        pltpu.VMEM((N_HEADS // 2, DIFF_KV_ROWS, 2 * tq), F32),
                        pltpu.VMEM((N_HEADS // 2, 1, 2 * tq), F32),
                        pltpu.VMEM((N_HEADS // 2, 1, 2 * tq), F32),
                        pltpu.VMEM((tq, GROUP_WIDTH), BF16),
                        pltpu.VMEM((tq, D_MODEL), F32),
                        pltpu.VMEM((N_HEADS // 2, HEAD_DIM, 2 * tq), BF16)],
        compiler_params=pltpu.CompilerParams(
            dimension_semantics=("arbitrary", "arbitrary"),
            vmem_limit_bytes=VMEM_LIMIT_BYTES,
            allow_input_fusion=[i == 13 for i in range(15)]),
        name="diffattn_out",
    )(qd, qd, kd, vdt, zd, dnorm_g, lq1, lk1, lq2, lk2, x, ym, yc, w_out, final_g)


def kernel(x, mem, norm_g, w_in, conv_w, conv_b, wq_m, wk_m, wv_m, w_if, b_if, mnorm_g, skip_m,
           lam_q1, lam_k1, lam_q2, lam_k2, dnorm_g, mem_norm_g, w_mem_kv, w_out, final_g):
    b, s, d = x.shape
    assert (d, s % DIFF_KV_ROWS, DIFF_KV_ROWS % V_TILE, FRONT_ROWS % V_TILE) == (D_MODEL, 0, 0, 0)
    assert (s % FRONT_ROWS, DIFF_KV_ROWS % DIFF_KEY_CHUNK) == (0, 0)
    assert DIFF_Q_ROWS == DIFF_KV_ROWS and DIFF_KEY_CHUNK % HEAD_DIM == 0
    assert norm_g.shape[0] == 1, "single-layer kernel"
    l = 0
    w_if_pad = jnp.pad(w_if[l], ((0, 0), (0, HEAD_DIM - w_if.shape[-1]))).astype(BF16)
    b_if_pad = jnp.pad(b_if[l], (0, HEAD_DIM - b_if.shape[-1]))[None, :]

    kv = _memkv_call(mem, mem_norm_g[l][None, :], w_mem_kv[l].astype(BF16))
    ym, qd, kd, vdt, zd, yc = _front_call(
        x, kv, norm_g[l][None, :], w_in[l].astype(BF16), conv_w[l], conv_b[l][None, :],
        jnp.concatenate([wq_m[l], wk_m[l]], axis=-1).astype(BF16), wv_m[l].astype(BF16), w_if_pad, b_if_pad,
        mnorm_g[l][None, :], skip_m[l][None, :])
    return _diff_out_call(qd, kd, vdt, zd, dnorm_g[l][:, None],
                          lam_q1[l][None, :], lam_k1[l][None, :], lam_q2[l][None, :], lam_k2[l][None, :],
                          x, ym, yc, w_out[l].astype(BF16), final_g[None, :])
```
